```python
import math
import jax, jax.numpy as jnp
from jax import lax
import numpy as np


D_MODEL = 1024
BATCH = 8
SEQ = 4096
DEPTH = 2

EXPAND = 2
D_INNER = EXPAND * D_MODEL
N_MEM = 256
EPS = 1e-6
NEG = -1e30
BIG = 1e30

POOL_WINDOWS = (2, 4, 8, 16)
POOL_WIDTH = 3 * D_INNER // 8
POOL_GROUP = POOL_WIDTH // len(POOL_WINDOWS)

RET_HEADS = 4
RET_DK = 128
RET_DV = 192
RET_CHUNK = 128
ROPE_BASE = 10000.0

MEM_HEADS = 4
MEM_DH = 128
MEM_WIDTH = MEM_HEADS * MEM_DH

NSA_HEADS = 12
NSA_KV = 2
NSA_HPG = NSA_HEADS // NSA_KV
NSA_DH = 128
CMP_BLOCK = 32
CMP_STRIDE = 16
CMP_HIDDEN = 256
SLC_BLOCK = 64
SLC_TOPK = 8
WINDOW = 512
NSA_QBLOCK = 32

REL_BUCKETS = 32
REL_MAX_DIST = 128

EVEN_SPLITS = (POOL_WIDTH, RET_HEADS * RET_DK, RET_HEADS * RET_DK, RET_HEADS * RET_DV, MEM_WIDTH, D_INNER)
ODD_SPLITS = (NSA_HEADS * NSA_DH,) + (NSA_KV * NSA_DH,) * 6 + (3 * NSA_HEADS, MEM_WIDTH, D_INNER)
EVEN_COLS = sum(EVEN_SPLITS)
ODD_COLS = sum(ODD_SPLITS)

kernel_name = 'hybrid_pool_retention_nsa_memory'


def rmsnorm(x, g):
    xf = x.astype(jnp.float32)
    y = xf * lax.rsqrt(jnp.mean(xf * xf, axis=-1, keepdims=True) + EPS)
    return (y * g.astype(jnp.float32)).astype(x.dtype)


def split_cols(z, sizes):
    return jnp.split(z, np.cumsum(sizes)[:-1].tolist(), axis=-1)


def t5_bucket(rel):
    n = jnp.maximum(rel, 0)
    max_exact = REL_BUCKETS // 2
    nf = jnp.maximum(n, 1).astype(jnp.float32)
    large = max_exact + (jnp.log(nf / max_exact) / math.log(REL_MAX_DIST / max_exact)
                         * (REL_BUCKETS - max_exact)).astype(jnp.int32)
    large = jnp.minimum(large, REL_BUCKETS - 1)
    return jnp.where(n < max_exact, n, large)


def pool_mixer(u, w_grp, scale):
    B_, S, W = u.shape
    uf = u.astype(jnp.float32)
    cs = jnp.concatenate([jnp.zeros((B_, 1, W), jnp.float32), jnp.cumsum(uf, axis=1)], axis=1)
    t = jnp.arange(S)
    outs = []
    for gi, w in enumerate(POOL_WINDOWS):
        sl = slice(gi * POOL_GROUP, (gi + 1) * POOL_GROUP)
        lo = jnp.maximum(t + 1 - w, 0)
        cnt = (t + 1 - lo).astype(jnp.float32)[None, :, None]
        outs.append((cs[:, 1:, sl] - cs[:, lo, sl]) / cnt - uf[..., sl])
    pooled = jnp.stack(outs, axis=2)
    y = jnp.einsum('bsgc,gcd->bsgd', pooled, w_grp.astype(jnp.float32)).reshape(B_, S, W)
    return y * scale.astype(jnp.float32)


def rotary(x):
    S, Dh = x.shape[-2], x.shape[-1]
    half = Dh // 2
    inv = ROPE_BASE ** (-jnp.arange(half, dtype=jnp.float32) / half)
    ang = jnp.arange(S, dtype=jnp.float32)[:, None] * inv[None, :]
    cos, sin = jnp.cos(ang), jnp.sin(ang)
    x1 = x[..., :half].astype(jnp.float32)
    x2 = x[..., half:].astype(jnp.float32)
    return jnp.concatenate([x1 * cos - x2 * sin, x1 * sin + x2 * cos], axis=-1)


def retention(q, k, v):
    B_, S, _ = q.shape
    H, C = RET_HEADS, RET_CHUNK
    N = S // C
    qh = rotary(q.reshape(B_, S, H, RET_DK).transpose(0, 2, 1, 3)) * (RET_DK ** -0.5)
    kh = rotary(k.reshape(B_, S, H, RET_DK).transpose(0, 2, 1, 3))
    vh = v.reshape(B_, S, H, RET_DV).transpose(0, 2, 1, 3).astype(jnp.float32)
    log_g = jnp.log(1.0 - jnp.exp2(-5.0 - jnp.arange(H, dtype=jnp.float32)))
    n = jnp.arange(C, dtype=jnp.float32)
    diff = n[:, None] - n[None, :]
    decay_in = jnp.where(diff >= 0, jnp.exp(log_g[:, None, None] * jnp.maximum(diff, 0.0)), 0.0)
    xi = jnp.exp(log_g[:, None] * (n + 1.0))
    zeta = jnp.exp(log_g[:, None] * (C - 1.0 - n))
    g_chunk = jnp.exp(log_g * C)
    qc = qh.reshape(B_, H, N, C, RET_DK)
    kc = kh.reshape(B_, H, N, C, RET_DK)
    vc = vh.reshape(B_, H, N, C, RET_DV)
    att = jnp.einsum('bhncd,bhnmd->bhncm', qc, kc) * decay_in[None, :, None]
    o_inner = jnp.einsum('bhncm,bhnmv->bhncv', att, vc)
    kv = jnp.einsum('bhnmd,bhnmv->bhndv', kc * zeta[None, :, None, :, None], vc)

    def step(R, kv_n):
        return R * g_chunk[None, :, None, None] + kv_n, R

    _, R_prev = lax.scan(step, jnp.zeros((B_, H, RET_DK, RET_DV), jnp.float32), kv.transpose(2, 0, 1, 3, 4))
    R_prev = R_prev.transpose(1, 2, 0, 3, 4)
    o_cross = jnp.einsum('bhncd,bhndv->bhncv', qc * xi[None, :, None, :, None], R_prev)
    o = (o_inner + o_cross).reshape(B_, H, S, RET_DV)
    mu = jnp.mean(o, axis=-1, keepdims=True)
    var = jnp.mean((o - mu) ** 2, axis=-1, keepdims=True)
    o = (o - mu) * lax.rsqrt(var + EPS)
    return o.transpose(0, 2, 1, 3).reshape(B_, S, H * RET_DV)


def mem_attention(xq, mem_n, w_mem_kv):
    B_, S, _ = xq.shape
    M = mem_n.shape[1]
    mk, mv = jnp.split(mem_n @ w_mem_kv, 2, axis=-1)
    mk = mk.reshape(B_, M, MEM_HEADS, MEM_DH)
    mv = mv.reshape(B_, M, MEM_HEADS, MEM_DH)
    qh = xq.reshape(B_, S, MEM_HEADS, MEM_DH).astype(jnp.float32) * (MEM_DH ** -0.5)
    p = jax.nn.softmax(jnp.einsum('bshd,bmhd->bhsm', qh, mk), axis=-1)
    o = jnp.einsum('bhsm,bmhd->bshd', p, mv)
    return o.reshape(B_, S, MEM_WIDTH)


def compress(k, pe, w1, b1, w2):
    B_, S = k.shape[:2]
    n_cmp = (S - CMP_BLOCK) // CMP_STRIDE + 1
    idx = jnp.arange(n_cmp)[:, None] * CMP_STRIDE + jnp.arange(CMP_BLOCK)[None, :]
    blocks = k[:, idx] + pe[:, None, :]
    flat = blocks.transpose(0, 1, 3, 2, 4).reshape(B_, n_cmp, NSA_KV, CMP_BLOCK * NSA_DH)
    return jax.nn.silu(flat @ w1 + b1) @ w2


def cmp_to_slc_matrix(n_cmp, n_slc):
    cst = np.arange(n_cmp)[:, None] * CMP_STRIDE
    sst = np.arange(n_slc)[None, :] * SLC_BLOCK
    ov = np.clip(np.minimum(cst + CMP_BLOCK, sst + SLC_BLOCK) - np.maximum(cst, sst), 0, None)
    return jnp.asarray(ov / CMP_STRIDE, dtype=jnp.float32)


def nsa_attention(q, kc, vc, ks, vs, kw, vw, gate_logits, cmp_pe, cmp_w1, cmp_b1, cmp_w2, rel_bias):
    B_, S = q.shape[:2]
    QB = NSA_QBLOCK
    qh = q.reshape(B_, S, NSA_KV, NSA_HPG, NSA_DH).transpose(0, 2, 3, 1, 4).astype(jnp.float32) * (NSA_DH ** -0.5)

    def kv_heads(t):
        return t.reshape(B_, S, NSA_KV, NSA_DH)

    kcmp = compress(kv_heads(kc), cmp_pe[0], cmp_w1[0], cmp_b1[0], cmp_w2[0]).transpose(0, 2, 1, 3)
    vcmp = compress(kv_heads(vc), cmp_pe[1], cmp_w1[1], cmp_b1[1], cmp_w2[1]).transpose(0, 2, 1, 3)
    n_cmp = kcmp.shape[2]
    n_slc = S // SLC_BLOCK
    kslc = kv_heads(ks).reshape(B_, n_slc, SLC_BLOCK, NSA_KV, NSA_DH).transpose(0, 3, 1, 2, 4)
    vslc = kv_heads(vs).reshape(B_, n_slc, SLC_BLOCK, NSA_KV, NSA_DH).transpose(0, 3, 1, 2, 4)
    pad = ((0, 0), (0, 0), (WINDOW, 0), (0, 0))
    kwin = jnp.pad(kv_heads(kw).transpose(0, 2, 1, 3), pad)
    vwin = jnp.pad(kv_heads(vw).transpose(0, 2, 1, 3), pad)
    gates = jax.nn.sigmoid(gate_logits.astype(jnp.float32)).reshape(B_, S, 3, NSA_KV, NSA_HPG).transpose(0, 2, 3, 4, 1)
    overlap = cmp_to_slc_matrix(n_cmp, n_slc)
    cend = jnp.arange(n_cmp) * CMP_STRIDE + CMP_BLOCK - 1
    table = rel_bias.astype(jnp.float32)
    table_g = table.reshape(REL_BUCKETS, NSA_KV, NSA_HPG)
    k_sel = min(SLC_TOPK, n_slc)
    b_idx = jnp.arange(B_)[:, None, None, None]
    g_idx = jnp.arange(NSA_KV)[None, :, None, None]

    def head_bias(rel):
        return table[t5_bucket(rel)].transpose(2, 0, 1).reshape(NSA_KV, NSA_HPG, rel.shape[0], rel.shape[1])

    def masked_softmax(s, mask):
        return jax.nn.softmax(jnp.where(mask, s, NEG), axis=-1)

    def block(s0):
        tpos = s0 + jnp.arange(QB)
        qb = lax.dynamic_slice_in_dim(qh, s0, QB, axis=3)
        gb = lax.dynamic_slice_in_dim(gates, s0, QB, axis=4)
        rel_c = tpos[:, None] - cend[None, :]
        sc = jnp.einsum('bghqd,bgjd->bghqj', qb, kcmp) + head_bias(rel_c)
        pc = masked_softmax(sc, rel_c >= 0) * (tpos >= CMP_BLOCK - 1).astype(jnp.float32)[:, None]
        oc = jnp.einsum('bghqj,bgjd->bghqd', pc, vcmp)
        imp = jnp.einsum('bghqj,js->bgqs', pc, overlap)
        sblk = jnp.arange(n_slc)[None, :]
        cur = (tpos // SLC_BLOCK)[:, None]
        forced = (sblk == 0) | (sblk == cur) | (sblk == cur - 1)
        future = sblk * SLC_BLOCK > tpos[:, None]
        imp = jnp.where(forced, BIG, jnp.where(future, -BIG, imp))
        _, sel = lax.top_k(imp, k_sel)
        kg = kslc[b_idx, g_idx, sel].reshape(B_, NSA_KV, QB, k_sel * SLC_BLOCK, NSA_DH)
        vg = vslc[b_idx, g_idx, sel].reshape(B_, NSA_KV, QB, k_sel * SLC_BLOCK, NSA_DH)
        kpos_s = (sel[..., None] * SLC_BLOCK + jnp.arange(SLC_BLOCK)).reshape(B_, NSA_KV, QB, k_sel * SLC_BLOCK)
        rel_s = tpos[None, None, :, None] - kpos_s
        bias_s = table_g[t5_bucket(rel_s), g_idx].transpose(0, 1, 4, 2, 3)
        ss = jnp.einsum('bghqd,bgqkd->bghqk', qb, kg) + bias_s
        ps = masked_softmax(ss, (rel_s >= 0)[:, :, None])
        osel = jnp.einsum('bghqk,bgqkd->bghqd', ps, vg)
        kwb = lax.dynamic_slice_in_dim(kwin, s0, WINDOW + QB, axis=2)
        vwb = lax.dynamic_slice_in_dim(vwin, s0, WINDOW + QB, axis=2)
        kpos_w = s0 - WINDOW + jnp.arange(WINDOW + QB)
        rel_w = tpos[:, None] - kpos_w[None, :]
        mask_w = (rel_w >= 0) & (rel_w < WINDOW) & (kpos_w >= 0)[None, :]
        sw = jnp.einsum('bghqd,bgkd->bghqk', qb, kwb) + head_bias(rel_w)
        pw = masked_softmax(sw, mask_w)
        ow = jnp.einsum('bghqk,bgkd->bghqd', pw, vwb)
        return gb[:, 0, ..., None] * oc + gb[:, 1, ..., None] * osel + gb[:, 2, ..., None] * ow

    outs = lax.map(block, jnp.arange(S // QB) * QB)
    return outs.transpose(1, 0, 4, 2, 3, 5).reshape(B_, S, NSA_HEADS * NSA_DH)


def even_layer(h, mem_n, g, w_in, pool_w, pool_scale, w_mem_kv, w_out):
    u = rmsnorm(h, g)
    za, rq, rk, rv, xq, gate = split_cols(u @ w_in, EVEN_SPLITS)
    a = pool_mixer(za, pool_w, pool_scale)
    r = retention(rq, rk, rv)
    m = mem_attention(xq, mem_n, w_mem_kv)
    y = jnp.concatenate([a, r, m], axis=-1) * jax.nn.silu(gate.astype(jnp.float32))
    return h + (y @ w_out).astype(h.dtype)


def odd_layer(h, mem_n, g, w_in, cmp_pe, cmp_w1, cmp_b1, cmp_w2, w_mem_kv, w_out, rel_bias):
    u = rmsnorm(h, g)
    q, kc, vc, ks, vs, kw, vw, gl, xq, gate = split_cols(u @ w_in, ODD_SPLITS)
    c = nsa_attention(q, kc, vc, ks, vs, kw, vw, gl, cmp_pe, cmp_w1, cmp_b1, cmp_w2, rel_bias)
    m = mem_attention(xq, mem_n, w_mem_kv)
    y = jnp.concatenate([c, m], axis=-1) * jax.nn.silu(gate.astype(jnp.float32))
    return h + (y @ w_out).astype(h.dtype)


def setup_inputs(seed: int = 0) -> dict:
    key = jax.random.key(seed)
    ks = jax.random.split(key, 20)
    n_even = (DEPTH + 1) // 2
    n_odd = DEPTH // 2

    def nrm(k, shape, scale):
        return scale * jax.random.normal(k, shape, jnp.float32)

    return {
        'x': nrm(ks[0], (BATCH, SEQ, D_MODEL), 1.0),
        'mem': nrm(ks[1], (BATCH, N_MEM, D_MODEL), 1.0),
        'norm_g': 1.0 + nrm(ks[2], (DEPTH, D_MODEL), 0.05),
        'final_g': 1.0 + nrm(ks[3], (D_MODEL,), 0.05),
        'mem_norm_g': 1.0 + nrm(ks[4], (D_MODEL,), 0.05),
        'rel_bias': nrm(ks[5], (REL_BUCKETS, NSA_HEADS), 0.5),
        'ev_w_in': nrm(ks[6], (n_even, D_MODEL, EVEN_COLS), D_MODEL ** -0.5),
        'ev_pool_w': nrm(ks[7], (n_even, len(POOL_WINDOWS), POOL_GROUP, POOL_GROUP), POOL_GROUP ** -0.5),
        'ev_pool_scale': 1.0 + nrm(ks[8], (n_even, POOL_WIDTH), 0.1),
        'ev_w_mem_kv': nrm(ks[9], (n_even, D_MODEL, 2 * MEM_WIDTH), D_MODEL ** -0.5),
        'ev_w_out': nrm(ks[10], (n_even, D_INNER, D_MODEL), D_INNER ** -0.5),
        'od_w_in': nrm(ks[11], (n_odd, D_MODEL, ODD_COLS), D_MODEL ** -0.5),
        'od_cmp_pe': nrm(ks[12], (n_odd, 2, CMP_BLOCK, NSA_DH), 0.1),
        'od_cmp_w1': nrm(ks[13], (n_odd, 2, CMP_BLOCK * NSA_DH, CMP_HIDDEN), (CMP_BLOCK * NSA_DH) ** -0.5),
        'od_cmp_b1': nrm(ks[14], (n_odd, 2, CMP_HIDDEN), 0.01),
        'od_cmp_w2': nrm(ks[15], (n_odd, 2, CMP_HIDDEN, NSA_DH), CMP_HIDDEN ** -0.5),
        'od_w_mem_kv': nrm(ks[16], (n_odd, D_MODEL, 2 * MEM_WIDTH), D_MODEL ** -0.5),
        'od_w_out': nrm(ks[17], (n_odd, D_INNER, D_MODEL), D_INNER ** -0.5),
    }


def reference(x, mem, norm_g, final_g, mem_norm_g, rel_bias, ev_w_in, ev_pool_w, ev_pool_scale, ev_w_mem_kv, ev_w_out,
              od_w_in, od_cmp_pe, od_cmp_w1, od_cmp_b1, od_cmp_w2, od_w_mem_kv, od_w_out):
    mem_n = rmsnorm(mem, mem_norm_g)
    h = x
    for i in range(DEPTH):
        j = i // 2
        if i % 2 == 0:
            h = even_layer(h, mem_n, norm_g[i], ev_w_in[j], ev_pool_w[j], ev_pool_scale[j], ev_w_mem_kv[j], ev_w_out[j])
        else:
            h = odd_layer(h, mem_n, norm_g[i], od_w_in[j], od_cmp_pe[j], od_cmp_w1[j], od_cmp_b1[j], od_cmp_w2[j],
                          od_w_mem_kv[j], od_w_out[j], rel_bias)
    return rmsnorm(h, final_g)
```

```python
import functools
import math

import numpy as np
import jax
import jax.numpy as jnp
from jax import lax
from jax.experimental import pallas as pl
from jax.experimental.pallas import tpu as pltpu

F32 = jnp.float32
BF16 = jnp.bfloat16

D_MODEL = 1024
D_INNER = 2048
N_MEM = 256
EPS = 1e-6
NEG = -1e30
BIG = 1e30

POOL_WINDOWS = (2, 4, 8, 16)
POOL_WIDTH = 768
POOL_GROUP = 192
POOL_HALO = 16

RET_HEADS = 4
RET_DK = 128
RET_DV = 192
RET_DV_PAD = 256
RET_CHUNK = 128
ROPE_BASE = 10000.0

MEM_HEADS = 4
MEM_DH = 128
MEM_WIDTH = 512

NSA_HEADS = 12
NSA_KV = 2
NSA_HPG = 6
NSA_DH = 128
CMP_BLOCK = 32
CMP_STRIDE = 16
CMP_HIDDEN = 256
SLC_BLOCK = 64
SLC_TOPK = 8
WINDOW = 512
REL_BUCKETS = 32
REL_MAX_DIST = 128

LANES = 128
NSA_TQ = 256
CMP_NEAR = 32
CMP_PAD = 16
LOWEST = -3.0e38
KNOCKED = -3.3e38

VMEM_LIMIT = 56 * 1024 * 1024


def _mm(a, b):
    return jnp.dot(a, b, preferred_element_type=F32)


def _mm_nt(a, b):
    return lax.dot_general(a, b, (((1,), (1,)), ((), ())), preferred_element_type=F32)


def _mm_tn(a, b):
    return lax.dot_general(a, b, (((0,), (0,)), ((), ())), preferred_element_type=F32)


def _rms(x, g):
    return x * lax.rsqrt(jnp.mean(x * x, axis=-1, keepdims=True) + EPS) * g


def _silu(x):
    return x * jax.nn.sigmoid(x)


def _const_spec(shape):
    nd = len(shape)
    return pl.BlockSpec(shape, lambda *_: (0,) * nd, pipeline_mode=pl.Buffered(1))


def _params(sem):
    return pltpu.CompilerParams(dimension_semantics=sem, vmem_limit_bytes=VMEM_LIMIT)


def _bias_kernel(tab_ref, rel_ref, out_ref):
    h = pl.program_id(0)
    rel = rel_ref[0]
    n = jnp.maximum(rel, 0)
    max_exact = REL_BUCKETS // 2
    nf = jnp.maximum(n, 1).astype(F32)
    large = max_exact + (jnp.log(nf / max_exact) / math.log(REL_MAX_DIST / max_exact)
                         * (REL_BUCKETS - max_exact)).astype(jnp.int32)
    large = jnp.minimum(large, REL_BUCKETS - 1)
    bucket = jnp.where(n < max_exact, n, large)
    far = tab_ref[REL_BUCKETS - 1, h]
    val = jnp.zeros(rel.shape, F32)
    for b in range(REL_BUCKETS - 1):
        val = jnp.where(bucket == b, tab_ref[b, h] - far, val)
    out_ref[0, 0] = jnp.where(rel < 0, NEG, val)


def _bias_tiles(rel_bias):
    tq = NSA_TQ
    r = np.arange(tq)[:, None]
    c = np.arange(tq)[None, :]
    d0 = r - c
    d1 = tq + r - c
    dw = np.where(c > r, WINDOW + r - c, -1)
    gc = np.where(c < CMP_NEAR, r - CMP_STRIDE * c + (CMP_STRIDE * CMP_PAD - (CMP_BLOCK - 1)), -1)
    rel = jnp.asarray(np.stack([d0, d1, dw, gc]).astype(np.int32))
    nt = rel.shape[0]
    out = pl.pallas_call(
        _bias_kernel,
        out_shape=jax.ShapeDtypeStruct((NSA_HEADS, nt, tq, tq), F32),
        grid=(NSA_HEADS, nt),
        in_specs=[pl.BlockSpec(memory_space=pltpu.SMEM),
                  pl.BlockSpec((1, tq, tq), lambda h, k: (k, 0, 0))],
        out_specs=pl.BlockSpec((1, 1, tq, tq), lambda h, k: (h, k, 0, 0)),
        compiler_params=_params(("arbitrary", "arbitrary")),
        name="bias_tiles",
    )(rel_bias.astype(F32), rel)
    return out.reshape(NSA_KV, NSA_HPG, nt, tq, tq)


def _memkv_kernel(mem_ref, g_ref, w_ref, out_ref):
    y = _rms(mem_ref[0], g_ref[...]).astype(BF16)
    out_ref[0] = _mm(y, w_ref[...]).astype(BF16)


def _memkv(mem, mem_norm_g, w_all):
    b, m, d = mem.shape
    n = w_all.shape[1]
    return pl.pallas_call(
        _memkv_kernel,
        out_shape=jax.ShapeDtypeStruct((b, m, n), BF16),
        grid=(b,),
        in_specs=[pl.BlockSpec((1, m, d), lambda i: (i, 0, 0)),
                  _const_spec((1, d)),
                  _const_spec((d, n))],
        out_specs=pl.BlockSpec((1, m, n), lambda i: (i, 0, 0)),
        compiler_params=_params(("arbitrary",)),
        name="mem_kv",
    )(mem, mem_norm_g.reshape(1, d), w_all)


def _mem_attention(xq, mk, mv):
    outs = []
    for hd in range(MEM_HEADS):
        sl = slice(hd * MEM_DH, (hd + 1) * MEM_DH)
        qm = (xq[:, sl] * (MEM_DH ** -0.5)).astype(BF16)
        s = _mm_nt(qm, mk[:, sl])
        p = jnp.exp(s - jnp.max(s, axis=-1, keepdims=True))
        l = jnp.sum(p, axis=-1, keepdims=True)
        outs.append(_mm(p.astype(BF16), mv[:, sl]) / l)
    return outs


EV_ZA = 0
EV_RQ = EV_ZA + POOL_WIDTH
EV_RK = EV_RQ + RET_HEADS * RET_DK
EV_RV = EV_RK + RET_HEADS * RET_DK
EV_XQ = EV_RV + RET_HEADS * RET_DV_PAD
EV_GA = EV_XQ + MEM_WIDTH
EV_GR = EV_GA + POOL_WIDTH
EV_GM = EV_GR + RET_HEADS * RET_DV_PAD
EV_COLS = EV_GM + MEM_WIDTH
EV_YA = 0
EV_YR = POOL_WIDTH
EV_YM = EV_YR + RET_HEADS * RET_DV_PAD
EV_YCOLS = EV_YM + MEM_WIDTH


def _even_kernel(gch_ref, h_ref, g_ref, win_ref, wbd_ref, pscale_ref, cos_ref, sin_ref, decay_ref, xi_ref,
                 zeta_ref, mk_ref, mv_ref, wout_ref, o_ref, ext_ref, state_ref, y_ref):
    si = pl.program_id(1)
    ts = h_ref.shape[1]

    @pl.when(si == 0)
    def _():
        ext_ref[0:POOL_HALO, :] = jnp.zeros((POOL_HALO, POOL_WIDTH), F32)
        state_ref[...] = jnp.zeros(state_ref.shape, F32)

    h = h_ref[0]
    u = _rms(h, g_ref[...]).astype(BF16)

    def proj(start, width):
        return _mm(u, win_ref[:, start:start + width])

    ext_ref[POOL_HALO:, :] = proj(EV_ZA, POOL_WIDTH)
    e = ext_ref[...]
    s2 = e + pltpu.roll(e, 1, 0)
    s4 = s2 + pltpu.roll(s2, 2, 0)
    s8 = s4 + pltpu.roll(s4, 4, 0)
    s16 = s8 + pltpu.roll(s8, 8, 0)
    lane = lax.broadcasted_iota(jnp.int32, e.shape, 1)
    row = lax.broadcasted_iota(jnp.int32, e.shape, 0)
    tpos = si * ts + row - POOL_HALO
    g0, g1, g2 = lane < POOL_GROUP, lane < 2 * POOL_GROUP, lane < 3 * POOL_GROUP
    wsum = jnp.where(g0, s2, jnp.where(g1, s4, jnp.where(g2, s8, s16)))
    wlen = jnp.where(g0, POOL_WINDOWS[0], jnp.where(g1, POOL_WINDOWS[1],
                                                    jnp.where(g2, POOL_WINDOWS[2], POOL_WINDOWS[3])))
    cnt = jnp.maximum(jnp.minimum(tpos + 1, wlen), 1).astype(F32)
    pooled = (wsum / cnt - e)[POOL_HALO:]
    ext_ref[0:POOL_HALO, :] = e[ts:ts + POOL_HALO]
    a = _mm(pooled.astype(BF16), wbd_ref[...]) * pscale_ref[...]
    y_ref[:, EV_YA:EV_YA + POOL_WIDTH] = (a * _silu(proj(EV_GA, POOL_WIDTH))).astype(BF16)

    cos = cos_ref[...]
    sin = sin_ref[...]
    vlane = lax.broadcasted_iota(jnp.int32, (RET_CHUNK, RET_DV_PAD), 1) < RET_DV
    for hd in range(RET_HEADS):
        qh = proj(EV_RQ + hd * RET_DK, RET_DK)
        kh = proj(EV_RK + hd * RET_DK, RET_DK)
        q_rot = (qh * cos + pltpu.roll(qh, RET_DK // 2, 1) * sin) * (RET_DK ** -0.5)
        k_rot = kh * cos + pltpu.roll(kh, RET_DK // 2, 1) * sin
        vb = proj(EV_RV + hd * RET_DV_PAD, RET_DV_PAD).astype(BF16)
        gate_r = _silu(proj(EV_GR + hd * RET_DV_PAD, RET_DV_PAD))
        for c in range(ts // RET_CHUNK):
            rows = slice(c * RET_CHUNK, (c + 1) * RET_CHUNK)
            qc, kc, vc = q_rot[rows], k_rot[rows], vb[rows]
            att = _mm_nt(qc.astype(BF16), kc.astype(BF16)) * decay_ref[hd]
            state = state_ref[hd]
            o = _mm(att.astype(BF16), vc) + _mm((qc * xi_ref[hd]).astype(BF16), state.astype(BF16))
            kv = _mm_tn((kc * zeta_ref[hd]).astype(BF16), vc)
            state_ref[hd] = state * gch_ref[hd] + kv
            mu = jnp.sum(o, axis=-1, keepdims=True) * (1.0 / RET_DV)
            dlt = jnp.where(vlane, o - mu, 0.0)
            var = jnp.sum(dlt * dlt, axis=-1, keepdims=True) * (1.0 / RET_DV)
            on = dlt * lax.rsqrt(var + EPS)
            col = EV_YR + hd * RET_DV_PAD
            y_ref[rows, col:col + RET_DV_PAD] = (on * gate_r[rows]).astype(BF16)

    xq = proj(EV_XQ, MEM_WIDTH)
    gm = _silu(proj(EV_GM, MEM_WIDTH))
    for hd, om in enumerate(_mem_attention(xq, mk_ref[0], mv_ref[0])):
        sl = slice(hd * MEM_DH, (hd + 1) * MEM_DH)
        y_ref[:, EV_YM + hd * MEM_DH:EV_YM + (hd + 1) * MEM_DH] = (om * gm[:, sl]).astype(BF16)

    o_ref[0] = h + _mm(y_ref[...], wout_ref[...])


def _pad_heads(w, heads, width, padded, axis):
    shp = w.shape
    w = w.reshape(shp[:axis] + (heads, width) + shp[axis + 1:])
    pad = [(0, 0)] * w.ndim
    pad[axis + 1] = (0, padded - width)
    w = jnp.pad(w, pad)
    return w.reshape(shp[:axis] + (heads * padded,) + shp[axis + 1:])


def _retention_tables(s):
    half = RET_DK // 2
    inv = ROPE_BASE ** (-jnp.arange(half, dtype=F32) / half)
    ang = jnp.arange(s, dtype=F32)[:, None] * inv[None, :]
    cos, sin = jnp.cos(ang), jnp.sin(ang)
    cos_t = jnp.concatenate([cos, cos], axis=-1)
    sin_t = jnp.concatenate([-sin, sin], axis=-1)
    c = RET_CHUNK
    log_g = jnp.log(1.0 - jnp.exp2(-5.0 - jnp.arange(RET_HEADS, dtype=F32)))
    n = jnp.arange(c, dtype=F32)
    diff = n[:, None] - n[None, :]
    decay = jnp.where(diff >= 0, jnp.exp(log_g[:, None, None] * jnp.maximum(diff, 0.0)), 0.0)
    xi = jnp.exp(log_g[:, None] * (n + 1.0))
    zeta = jnp.exp(log_g[:, None] * (c - 1.0 - n))
    g_chunk = jnp.exp(log_g * c)
    xi_t = jnp.broadcast_to(xi[:, :, None], (RET_HEADS, c, RET_DK))
    zeta_t = jnp.broadcast_to(zeta[:, :, None], (RET_HEADS, c, RET_DK))
    return cos_t, sin_t, decay, xi_t, zeta_t, g_chunk


def _even_layer(h, memkv, layer, g, w_in, pool_w, pool_scale, w_out, ts=256):
    b, s, d = h.shape
    za, rq, rk, rv, xq, gate = jnp.split(w_in, np.cumsum(
        [POOL_WIDTH, RET_HEADS * RET_DK, RET_HEADS * RET_DK, RET_HEADS * RET_DV, MEM_WIDTH])[:].tolist(), axis=1)
    ga, gr, gm = jnp.split(gate, [POOL_WIDTH, POOL_WIDTH + RET_HEADS * RET_DV], axis=1)
    win = jnp.concatenate([za, rq, rk, _pad_heads(rv, RET_HEADS, RET_DV, RET_DV_PAD, 1), xq, ga,
                           _pad_heads(gr, RET_HEADS, RET_DV, RET_DV_PAD, 1), gm], axis=1).astype(BF16)
    oa, orr, om = jnp.split(w_out, [POOL_WIDTH, POOL_WIDTH + RET_HEADS * RET_DV], axis=0)
    wout = jnp.concatenate([oa, _pad_heads(orr, RET_HEADS, RET_DV, RET_DV_PAD, 0), om], axis=0).astype(BF16)
    wbd = jnp.zeros((POOL_WIDTH, POOL_WIDTH), F32)
    for gi in range(len(POOL_WINDOWS)):
        sl = slice(gi * POOL_GROUP, (gi + 1) * POOL_GROUP)
        wbd = wbd.at[sl, sl].set(pool_w[gi])
    wbd = wbd.astype(BF16)
    cos_t, sin_t, decay, xi_t, zeta_t, g_chunk = _retention_tables(s)
    kv_blk = 2 * layer
    return pl.pallas_call(
        _even_kernel,
        out_shape=jax.ShapeDtypeStruct((b, s, d), F32),
        grid=(b, s // ts),
        in_specs=[pl.BlockSpec(memory_space=pltpu.SMEM),
                  pl.BlockSpec((1, ts, d), lambda i, j: (i, j, 0)),
                  _const_spec((1, d)),
                  _const_spec((d, EV_COLS)),
                  _const_spec((POOL_WIDTH, POOL_WIDTH)),
                  _const_spec((1, POOL_WIDTH)),
                  pl.BlockSpec((ts, RET_DK), lambda i, j: (j, 0)),
                  pl.BlockSpec((ts, RET_DK), lambda i, j: (j, 0)),
                  _const_spec((RET_HEADS, RET_CHUNK, RET_CHUNK)),
                  _const_spec((RET_HEADS, RET_CHUNK, RET_DK)),
                  _const_spec((RET_HEADS, RET_CHUNK, RET_DK)),
                  pl.BlockSpec((1, N_MEM, MEM_WIDTH), lambda i, j: (i, 0, kv_blk)),
                  pl.BlockSpec((1, N_MEM, MEM_WIDTH), lambda i, j: (i, 0, kv_blk + 1)),
                  _const_spec((EV_YCOLS, d))],
        out_specs=pl.BlockSpec((1, ts, d), lambda i, j: (i, j, 0)),
        scratch_shapes=[pltpu.VMEM((POOL_HALO + ts, POOL_WIDTH), F32),
                        pltpu.VMEM((RET_HEADS, RET_DK, RET_DV_PAD), F32),
                        pltpu.VMEM((ts, EV_YCOLS), BF16)],
        compiler_params=_params(("arbitrary", "arbitrary")),
        name="even_layer",
    )(g_chunk, h, g.reshape(1, d), win, wbd, pool_scale.reshape(1, POOL_WIDTH), cos_t, sin_t, decay, xi_t, zeta_t,
      memkv, memkv, wout)


KV_W = NSA_KV * NSA_DH
OD_Q = 0
OD_KS = OD_Q + NSA_HEADS * NSA_DH
OD_VS = OD_KS + KV_W
OD_KW = OD_VS + KV_W
OD_VW = OD_KW + KV_W
OD_KC = OD_VW + KV_W
OD_VC = OD_KC + KV_W
OD_GL = OD_VC + KV_W
OD_XQ = OD_GL + NSA_KV * LANES
OD_GN = OD_XQ + MEM_WIDTH
OD_GM = OD_GN + NSA_HEADS * NSA_DH
OD_COLS = OD_GM + MEM_WIDTH


def _odd_in_kernel(h_ref, g_ref, win_ref, pe_ref, mk_ref, mv_ref,
                   q_ref, kaug_ref, vs_ref, kw_ref, vw_ref, kca_ref, kcb_ref, vca_ref, vcb_ref, gates_ref, sg_ref,
                   ym_ref):
    si = pl.program_id(1)
    ts = h_ref.shape[1]
    u = _rms(h_ref[0], g_ref[...]).astype(BF16)

    def proj(start, width):
        return _mm(u, win_ref[:, start:start + width])

    zq = proj(OD_Q, NSA_HEADS * NSA_DH) * (NSA_DH ** -0.5)
    zg = _silu(proj(OD_GN, NSA_HEADS * NSA_DH))
    for g in range(NSA_KV):
        for hh in range(NSA_HPG):
            sl = slice((g * NSA_HPG + hh) * NSA_DH, (g * NSA_HPG + hh + 1) * NSA_DH)
            q_ref[0, g, hh] = zq[:, sl].astype(BF16)
            sg_ref[0, g, hh] = zg[:, sl]

    zkv = proj(OD_KS, 6 * KV_W)
    lane = lax.broadcasted_iota(jnp.int32, (ts, LANES), 1)
    blk = (si * ts + lax.broadcasted_iota(jnp.int32, (ts, LANES), 0)) // SLC_BLOCK
    onehot = jnp.where(lane == blk, 1.0, 0.0).astype(BF16)
    zgl = jax.nn.sigmoid(proj(OD_GL, NSA_KV * LANES))
    for g in range(NSA_KV):
        def piece(idx):
            off = idx * KV_W + g * NSA_DH
            return zkv[:, off:off + NSA_DH]
        kaug_ref[0, g, :, 0:NSA_DH] = piece(0).astype(BF16)
        kaug_ref[0, g, :, NSA_DH:2 * NSA_DH] = onehot
        vs_ref[0, g] = piece(1).astype(BF16)
        kw_ref[0, g] = piece(2).astype(BF16)
        vw_ref[0, g] = piece(3).astype(BF16)
        kc, vc = piece(4), piece(5)
        kca_ref[0, g] = (kc + pe_ref[0, 0]).astype(BF16)
        kcb_ref[0, g] = (kc + pe_ref[0, 1]).astype(BF16)
        vca_ref[0, g] = (vc + pe_ref[1, 0]).astype(BF16)
        vcb_ref[0, g] = (vc + pe_ref[1, 1]).astype(BF16)
        gates_ref[0, g] = zgl[:, g * LANES:(g + 1) * LANES]

    xq = proj(OD_XQ, MEM_WIDTH)
    gm = _silu(proj(OD_GM, MEM_WIDTH))
    for hd, om in enumerate(_mem_attention(xq, mk_ref[0], mv_ref[0])):
        sl = slice(hd * MEM_DH, (hd + 1) * MEM_DH)
        ym_ref[0, :, sl] = (om * gm[:, sl]).astype(BF16)


def _odd_in(h, memkv, layer, g, w_in, cmp_pe, ts=256):
    b, s, d = h.shape
    sizes = [NSA_HEADS * NSA_DH] + [KV_W] * 6 + [3 * NSA_HEADS, MEM_WIDTH]
    q, kc, vc, ks, vs, kw, vw, gl, xq, gate = jnp.split(w_in, np.cumsum(sizes).tolist(), axis=1)
    gl = gl.reshape(d, 3, NSA_KV, NSA_HPG).transpose(0, 2, 1, 3).reshape(d, NSA_KV, 3 * NSA_HPG)
    gl = jnp.pad(gl, ((0, 0), (0, 0), (0, LANES - 3 * NSA_HPG))).reshape(d, NSA_KV * LANES)
    gn, gm = jnp.split(gate, [NSA_HEADS * NSA_DH], axis=1)
    win = jnp.concatenate([q, ks, vs, kw, vw, kc, vc, gl, xq, gn, gm], axis=1).astype(BF16)
    reps = ts // CMP_STRIDE
    pe = jnp.stack([jnp.stack([jnp.tile(cmp_pe[kv, :CMP_STRIDE], (reps, 1)),
                               jnp.tile(cmp_pe[kv, CMP_STRIDE:], (reps, 1))]) for kv in range(2)])
    kv_blk = 2 * layer
    head_t = jax.ShapeDtypeStruct((b, NSA_KV, s, NSA_DH), BF16)
    head_spec = pl.BlockSpec((1, NSA_KV, ts, NSA_DH), lambda i, j: (i, 0, j, 0))
    qlike_spec = pl.BlockSpec((1, NSA_KV, NSA_HPG, ts, NSA_DH), lambda i, j: (i, 0, 0, j, 0))
    return pl.pallas_call(
        _odd_in_kernel,
        out_shape=[jax.ShapeDtypeStruct((b, NSA_KV, NSA_HPG, s, NSA_DH), BF16),
                   jax.ShapeDtypeStruct((b, NSA_KV, s, 2 * NSA_DH), BF16),
                   head_t, head_t, head_t, head_t, head_t, head_t, head_t,
                   jax.ShapeDtypeStruct((b, NSA_KV, s, LANES), F32),
                   jax.ShapeDtypeStruct((b, NSA_KV, NSA_HPG, s, NSA_DH), F32),
                   jax.ShapeDtypeStruct((b, s, MEM_WIDTH), BF16)],
        grid=(b, s // ts),
        in_specs=[pl.BlockSpec((1, ts, d), lambda i, j: (i, j, 0)),
                  _const_spec((1, d)),
                  _const_spec((d, OD_COLS)),
                  _const_spec((2, 2, ts, NSA_DH)),
                  pl.BlockSpec((1, N_MEM, MEM_WIDTH), lambda i, j: (i, 0, kv_blk)),
                  pl.BlockSpec((1, N_MEM, MEM_WIDTH), lambda i, j: (i, 0, kv_blk + 1))],
        out_specs=[qlike_spec,
                   pl.BlockSpec((1, NSA_KV, ts, 2 * NSA_DH), lambda i, j: (i, 0, j, 0)),
                   head_spec, head_spec, head_spec, head_spec, head_spec, head_spec, head_spec,
                   pl.BlockSpec((1, NSA_KV, ts, LANES), lambda i, j: (i, 0, j, 0)),
                   qlike_spec,
                   pl.BlockSpec((1, ts, MEM_WIDTH), lambda i, j: (i, j, 0))],
        compiler_params=_params(("arbitrary", "arbitrary")),
        name="odd_in_proj",
    )(h, g.reshape(1, d), win, pe, memkv, memkv)


def _compress_kernel(xka_ref, xkb_ref, xva_ref, xvb_ref, w1a_ref, w1b_ref, b1_ref, w2_ref, kc_ref, vc_ref):
    n = xka_ref.shape[2]
    for kv, (xa, xb, out) in enumerate(((xka_ref, xkb_ref, kc_ref), (xva_ref, xvb_ref, vc_ref))):
        first = _mm(xa[0, 0], w1a_ref[kv])
        second = _mm(xb[0, 0], w1b_ref[kv])
        hid = first + pltpu.roll(second, n - 1, 0) + b1_ref[kv]
        c = _mm(_silu(hid).astype(BF16), w2_ref[kv])
        out[0, 0, 0:CMP_PAD] = jnp.zeros((CMP_PAD, NSA_DH), BF16)
        out[0, 0, CMP_PAD:] = c.astype(BF16)


def _compress(kca, kcb, vca, vcb, w1, b1, w2):
    b, g, s, dh = kca.shape
    n = s // CMP_STRIDE
    half = CMP_STRIDE * dh
    xs = [x.reshape(b, g, n, half) for x in (kca, kcb, vca, vcb)]
    w1 = w1.astype(BF16)
    x_spec = pl.BlockSpec((1, 1, n, half), lambda i, j: (i, j, 0, 0))
    o_spec = pl.BlockSpec((1, 1, n + CMP_PAD, dh), lambda i, j: (i, j, 0, 0))
    o_t = jax.ShapeDtypeStruct((b, g, n + CMP_PAD, dh), BF16)
    return pl.pallas_call(
        _compress_kernel,
        out_shape=[o_t, o_t],
        grid=(b, g),
        in_specs=[x_spec, x_spec, x_spec, x_spec,
                  _const_spec((2, half, CMP_HIDDEN)), _const_spec((2, half, CMP_HIDDEN)),
                  _const_spec((2, 1, CMP_HIDDEN)), _const_spec((2, CMP_HIDDEN, dh))],
        out_specs=[o_spec, o_spec],
        compiler_params=_params(("arbitrary", "arbitrary")),
        name="compress",
    )(*xs, w1[:, :half], w1[:, half:], b1.reshape(2, 1, CMP_HIDDEN), w2.astype(BF16))


def _nsa_kernel(q_ref, kaug_ref, vs_ref, kw_ref, vw_ref, kcmp_ref, vcmp_ref, gates_ref, sg_ref,
                d0_ref, d1_ref, dw_ref, gc_ref, ov_ref, y_ref, qaug_ref, m_ref, l_ref, acc_ref):
    i = pl.program_id(2)
    tq = NSA_TQ
    rows = NSA_HPG * tq
    s_len = vs_ref.shape[2]
    n_cmp = s_len // CMP_STRIDE
    n_slc = s_len // SLC_BLOCK
    t0 = i * tq
    q = q_ref[0, 0].reshape(rows, NSA_DH)

    def per_head(x):
        return x.reshape(NSA_HPG, tq, x.shape[-1])

    j_near = pl.multiple_of(i * (tq // CMP_STRIDE), tq // CMP_STRIDE)
    s_far = _mm_nt(q, kcmp_ref[0, 0, CMP_PAD:CMP_PAD + n_cmp])
    jcol = lax.broadcasted_iota(jnp.int32, s_far.shape, 1)
    s_far = jnp.where(jcol < t0 // CMP_STRIDE - CMP_PAD, s_far, NEG)
    s_near = per_head(_mm_nt(q, kcmp_ref[0, 0, pl.ds(j_near, CMP_NEAR)])) + gc_ref[0, :, 0, :, 0:CMP_NEAR]
    ncol = lax.broadcasted_iota(jnp.int32, s_near.shape, 2)
    s_near = jnp.where(ncol + t0 // CMP_STRIDE - CMP_PAD >= 0, s_near, NEG).reshape(rows, CMP_NEAR)
    mx = jnp.maximum(jnp.max(s_far, axis=-1, keepdims=True), jnp.max(s_near, axis=-1, keepdims=True))
    p_far = jnp.exp(s_far - mx)
    p_near = jnp.exp(s_near - mx)
    denom = jnp.sum(p_far, axis=-1, keepdims=True) + jnp.sum(p_near, axis=-1, keepdims=True)
    trow = t0 + jnp.bitwise_and(lax.broadcasted_iota(jnp.int32, (rows, 1), 0), tq - 1)
    scale = jnp.where(trow >= CMP_BLOCK - 1, 1.0, 0.0) / denom
    p_far = (p_far * scale).astype(BF16)
    p_near = (p_near * scale).astype(BF16)
    o_cmp = _mm(p_far, vcmp_ref[0, 0, CMP_PAD:CMP_PAD + n_cmp]) + _mm(p_near, vcmp_ref[0, 0, pl.ds(j_near, CMP_NEAR)])
    imp = _mm(p_far, ov_ref[CMP_PAD:CMP_PAD + n_cmp]) + _mm(p_near, ov_ref[pl.ds(j_near, CMP_NEAR)])
    imp = jnp.sum(per_head(imp), axis=0)

    lane = lax.broadcasted_iota(jnp.int32, (tq, LANES), 1)
    tq_pos = t0 + lax.broadcasted_iota(jnp.int32, (tq, LANES), 0)
    cur = tq_pos // SLC_BLOCK
    forced = (lane == 0) | (lane == cur) | (lane == cur - 1)
    future = lane * SLC_BLOCK > tq_pos
    imp = jnp.where(forced, BIG, jnp.where(future, -BIG, imp))
    imp = jnp.where(lane < n_slc, imp, LOWEST)
    lane_f = lane.astype(F32)
    sel = jnp.zeros((tq, LANES), jnp.bool_)
    for _ in range(SLC_TOPK):
        top = jnp.max(imp, axis=-1, keepdims=True)
        first = jnp.min(jnp.where(imp == top, lane_f, float(LANES)), axis=-1, keepdims=True)
        hit = lane_f == first
        sel = sel | hit
        imp = jnp.where(hit, KNOCKED, imp)
    sneg = jnp.where(sel, 0.0, NEG).astype(BF16)
    qaug_ref[:, 0:NSA_DH] = q
    for hh in range(NSA_HPG):
        qaug_ref[hh * tq:(hh + 1) * tq, NSA_DH:2 * NSA_DH] = sneg

    def reset():
        m_ref[...] = jnp.full(m_ref.shape, LOWEST, F32)
        l_ref[...] = jnp.zeros(l_ref.shape, F32)
        acc_ref[...] = jnp.zeros(acc_ref.shape, F32)

    def update(s, v):
        m_prev = m_ref[...]
        m_new = jnp.maximum(m_prev, jnp.max(s, axis=-1, keepdims=True))
        alpha = jnp.exp(m_prev - m_new)
        p = jnp.exp(s - m_new)
        l_ref[...] = alpha * l_ref[...] + jnp.sum(p, axis=-1, keepdims=True)
        acc_ref[...] = alpha * acc_ref[...] + _mm(p.astype(BF16), v)
        m_ref[...] = m_new

    def biased(s, bias_ref):
        return (per_head(s) + bias_ref[0, :, 0]).reshape(rows, tq)

    reset()

    def far_tile(j, carry):
        k0 = pl.multiple_of(j * tq, tq)
        update(_mm_nt(qaug_ref[...], kaug_ref[0, 0, pl.ds(k0, tq)]), vs_ref[0, 0, pl.ds(k0, tq)])
        return carry

    lax.fori_loop(0, jnp.maximum(i - 1, 0), far_tile, 0)

    @pl.when(i >= 1)
    def _():
        k0 = pl.multiple_of(t0 - tq, tq)
        update(biased(_mm_nt(qaug_ref[...], kaug_ref[0, 0, pl.ds(k0, tq)]), d1_ref), vs_ref[0, 0, pl.ds(k0, tq)])

    k_diag = pl.multiple_of(t0, tq)
    update(biased(_mm_nt(qaug_ref[...], kaug_ref[0, 0, pl.ds(k_diag, tq)]), d0_ref), vs_ref[0, 0, pl.ds(k_diag, tq)])
    o_slc = acc_ref[...] / l_ref[...]

    reset()

    @pl.when(i >= 2)
    def _():
        k0 = pl.multiple_of(t0 - 2 * tq, tq)
        update(biased(_mm_nt(q, kw_ref[0, 0, pl.ds(k0, tq)]), dw_ref), vw_ref[0, 0, pl.ds(k0, tq)])

    @pl.when(i >= 1)
    def _():
        k0 = pl.multiple_of(t0 - tq, tq)
        update(biased(_mm_nt(q, kw_ref[0, 0, pl.ds(k0, tq)]), d1_ref), vw_ref[0, 0, pl.ds(k0, tq)])

    update(biased(_mm_nt(q, kw_ref[0, 0, pl.ds(k_diag, tq)]), d0_ref), vw_ref[0, 0, pl.ds(k_diag, tq)])
    o_win = acc_ref[...] / l_ref[...]

    gts = gates_ref[0, 0]
    for hh in range(NSA_HPG):
        hs = slice(hh * tq, (hh + 1) * tq)
        o = (gts[:, hh:hh + 1] * o_cmp[hs] + gts[:, NSA_HPG + hh:NSA_HPG + hh + 1] * o_slc[hs]
             + gts[:, 2 * NSA_HPG + hh:2 * NSA_HPG + hh + 1] * o_win[hs])
        y_ref[0, :, hh * NSA_DH:(hh + 1) * NSA_DH] = (o * sg_ref[0, 0, hh]).astype(BF16)


def _overlap_table(s):
    n_cmp = (s - CMP_BLOCK) // CMP_STRIDE + 1
    n_slc = s // SLC_BLOCK
    cst = np.arange(n_cmp)[:, None] * CMP_STRIDE
    sst = np.arange(n_slc)[None, :] * SLC_BLOCK
    ov = np.clip(np.minimum(cst + CMP_BLOCK, sst + SLC_BLOCK) - np.maximum(cst, sst), 0, None) / CMP_STRIDE
    full = np.zeros((CMP_PAD + s // CMP_STRIDE, LANES), np.float32)
    full[CMP_PAD:CMP_PAD + n_cmp, :n_slc] = ov
    return jnp.asarray(full, BF16)


def _nsa(q, kaug, vs, kw, vw, kcmp, vcmp, gates, sg, bias):
    b, g, hpg, s, dh = q.shape
    tq = NSA_TQ
    assert WINDOW == 2 * tq and s % tq == 0 and s // SLC_BLOCK <= LANES and s // SLC_BLOCK >= SLC_TOPK
    n_pad = kcmp.shape[2]
    rows = hpg * tq
    seq_spec = pl.BlockSpec((1, 1, s, dh), lambda i, j, k: (i, j, 0, 0))
    cmp_spec = pl.BlockSpec((1, 1, n_pad, dh), lambda i, j, k: (i, j, 0, 0))
    qlike_spec = pl.BlockSpec((1, 1, hpg, tq, dh), lambda i, j, k: (i, j, 0, k, 0))

    def bias_spec(idx, width):
        return pl.BlockSpec((1, hpg, 1, tq, width), lambda i, j, k: (j, 0, idx, 0, 0))

    return pl.pallas_call(
        _nsa_kernel,
        out_shape=jax.ShapeDtypeStruct((b, s, g * hpg * dh), BF16),
        grid=(b, g, s // tq),
        in_specs=[qlike_spec,
                  pl.BlockSpec((1, 1, s, 2 * dh), lambda i, j, k: (i, j, 0, 0)),
                  seq_spec, seq_spec, seq_spec, cmp_spec, cmp_spec,
                  pl.BlockSpec((1, 1, tq, LANES), lambda i, j, k: (i, j, k, 0)),
                  qlike_spec,
                  bias_spec(0, tq), bias_spec(1, tq), bias_spec(2, tq), bias_spec(3, LANES),
                  _const_spec((n_pad, LANES))],
        out_specs=pl.BlockSpec((1, tq, hpg * dh), lambda i, j, k: (i, k, j)),
        scratch_shapes=[pltpu.VMEM((rows, 2 * dh), BF16),
                        pltpu.VMEM((rows, 1), F32),
                        pltpu.VMEM((rows, 1), F32),
                        pltpu.VMEM((rows, dh), F32)],
        compiler_params=_params(("arbitrary", "arbitrary", "arbitrary")),
        name="nsa_attention",
    )(q, kaug, vs, kw, vw, kcmp, vcmp, gates, sg, bias, bias, bias, bias, _overlap_table(s))


def _odd_out_kernel(h_ref, yn_ref, ym_ref, wn_ref, wm_ref, fg_ref, o_ref, *, final_norm):
    out = h_ref[...] + _mm(yn_ref[...], wn_ref[...]) + _mm(ym_ref[...], wm_ref[...])
    o_ref[...] = _rms(out, fg_ref[...]) if final_norm else out


def _odd_out(h, yn, ym, w_out, final_g, final_norm, tm=512):
    b, s, d = h.shape
    t = b * s
    nw = NSA_HEADS * NSA_DH
    wout = w_out.astype(BF16)
    out = pl.pallas_call(
        functools.partial(_odd_out_kernel, final_norm=final_norm),
        out_shape=jax.ShapeDtypeStruct((t, d), F32),
        grid=(t // tm,),
        in_specs=[pl.BlockSpec((tm, d), lambda i: (i, 0)),
                  pl.BlockSpec((tm, nw), lambda i: (i, 0)),
                  pl.BlockSpec((tm, MEM_WIDTH), lambda i: (i, 0)),
                  _const_spec((nw, d)), _const_spec((MEM_WIDTH, d)), _const_spec((1, d))],
        out_specs=pl.BlockSpec((tm, d), lambda i: (i, 0)),
        compiler_params=_params(("arbitrary",)),
        name="odd_out_proj",
    )(h.reshape(t, d), yn.reshape(t, nw), ym.reshape(t, MEM_WIDTH), wout[:nw], wout[nw:], final_g.reshape(1, d))
    return out.reshape(b, s, d)


def _final_norm_kernel(h_ref, g_ref, o_ref):
    o_ref[...] = _rms(h_ref[...], g_ref[...])


def _final_norm(h, final_g, tm=512):
    b, s, d = h.shape
    t = b * s
    out = pl.pallas_call(
        _final_norm_kernel,
        out_shape=jax.ShapeDtypeStruct((t, d), F32),
        grid=(t // tm,),
        in_specs=[pl.BlockSpec((tm, d), lambda i: (i, 0)), _const_spec((1, d))],
        out_specs=pl.BlockSpec((tm, d), lambda i: (i, 0)),
        compiler_params=_params(("arbitrary",)),
        name="final_norm",
    )(h.reshape(t, d), final_g.reshape(1, d))
    return out.reshape(b, s, d)


def kernel(x, mem, norm_g, final_g, mem_norm_g, rel_bias, ev_w_in, ev_pool_w, ev_pool_scale, ev_w_mem_kv, ev_w_out,
           od_w_in, od_cmp_pe, od_cmp_w1, od_cmp_b1, od_cmp_w2, od_w_mem_kv, od_w_out):
    depth = norm_g.shape[0]
    w_mem = [(ev_w_mem_kv if i % 2 == 0 else od_w_mem_kv)[i // 2] for i in range(depth)]
    memkv = _memkv(mem, mem_norm_g, jnp.concatenate(w_mem, axis=1).astype(BF16))
    bias = _bias_tiles(rel_bias) if depth > 1 else None
    h = x
    for i in range(depth):
        j = i // 2
        last = i == depth - 1
        if i % 2 == 0:
            h = _even_layer(h, memkv, i, norm_g[i], ev_w_in[j], ev_pool_w[j], ev_pool_scale[j], ev_w_out[j])
            if last:
                h = _final_norm(h, final_g)
        else:
            (q, kaug, vs, kw, vw, kca, kcb, vca, vcb, gates, sg, ym) = _odd_in(
                h, memkv, i, norm_g[i], od_w_in[j], od_cmp_pe[j])
            kcmp, vcmp = _compress(kca, kcb, vca, vcb, od_cmp_w1[j], od_cmp_b1[j], od_cmp_w2[j])
            yn = _nsa(q, kaug, vs, kw, vw, kcmp, vcmp, gates, sg, bias)
            h = _odd_out(h, yn, ym, od_w_out[j], final_g, last)
    return h
```

```python
import functools
import math

import numpy as np
import jax
import jax.numpy as jnp
from jax import lax
from jax.experimental import pallas as pl
from jax.experimental.pallas import tpu as pltpu

F32 = jnp.float32
BF16 = jnp.bfloat16

D_MODEL = 1024
D_INNER = 2048
N_MEM = 256
EPS = 1e-6
NEG = -1e30
BIG = 1e30

POOL_WINDOWS = (2, 4, 8, 16)
POOL_WIDTH = 768
POOL_GROUP = 192
POOL_HALO = 16

RET_HEADS = 4
RET_DK = 128
RET_DV = 192
RET_DV_PAD = 256
RET_CHUNK = 128
ROPE_BASE = 10000.0

MEM_HEADS = 4
MEM_DH = 128
MEM_WIDTH = 512

NSA_HEADS = 12
NSA_KV = 2
NSA_HPG = 6
NSA_DH = 128
CMP_BLOCK = 32
CMP_STRIDE = 16
CMP_HIDDEN = 256
SLC_BLOCK = 64
SLC_TOPK = 8
WINDOW = 512
REL_BUCKETS = 32
REL_MAX_DIST = 128

LANES = 128
NSA_TQ = 256
CMP_NEAR = 32
CMP_PAD = 16
LOWEST = -3.0e38
KNOCKED = -3.3e38

VMEM_LIMIT = 56 * 1024 * 1024


def _mm(a, b):
    return jnp.dot(a, b, preferred_element_type=F32)


def _mm_nt(a, b):
    return lax.dot_general(a, b, (((1,), (1,)), ((), ())), preferred_element_type=F32)


def _mm_tn(a, b):
    return lax.dot_general(a, b, (((0,), (0,)), ((), ())), preferred_element_type=F32)


def _rms(x, g):
    return x * lax.rsqrt(jnp.mean(x * x, axis=-1, keepdims=True) + EPS) * g


def _silu(x):
    return x * jax.nn.sigmoid(x)


def _const_spec(shape):
    nd = len(shape)
    return pl.BlockSpec(shape, lambda *_: (0,) * nd, pipeline_mode=pl.Buffered(1))


def _params(sem):
    return pltpu.CompilerParams(dimension_semantics=sem, vmem_limit_bytes=VMEM_LIMIT)


def _bias_kernel(tab_ref, rel_ref, out_ref):
    h = pl.program_id(0)
    rel = rel_ref[0]
    n = jnp.maximum(rel, 0)
    max_exact = REL_BUCKETS // 2
    nf = jnp.maximum(n, 1).astype(F32)
    large = max_exact + (jnp.log(nf / max_exact) / math.log(REL_MAX_DIST / max_exact)
                         * (REL_BUCKETS - max_exact)).astype(jnp.int32)
    large = jnp.minimum(large, REL_BUCKETS - 1)
    bucket = jnp.where(n < max_exact, n, large)
    far = tab_ref[REL_BUCKETS - 1, h]
    val = jnp.zeros(rel.shape, F32)
    for b in range(REL_BUCKETS - 1):
        val = jnp.where(bucket == b, tab_ref[b, h] - far, val)
    out_ref[0, 0] = jnp.where(rel < 0, NEG, val)


def _bias_tiles(rel_bias):
    tq = NSA_TQ
    r = np.arange(tq)[:, None]
    c = np.arange(tq)[None, :]
    d0 = r - c
    d1 = tq + r - c
    dw = np.where(c > r, WINDOW + r - c, -1)
    gc = np.where(c < CMP_NEAR, r - CMP_STRIDE * c + (CMP_STRIDE * CMP_PAD - (CMP_BLOCK - 1)), -1)
    rel = jnp.asarray(np.stack([d0.T, d1.T, dw.T, gc]).astype(np.int32))
    nt = rel.shape[0]
    return pl.pallas_call(
        _bias_kernel,
        out_shape=jax.ShapeDtypeStruct((NSA_KV, nt, tq, NSA_HPG * tq), F32),
        grid=(NSA_HEADS, nt),
        in_specs=[pl.BlockSpec(memory_space=pltpu.SMEM),
                  pl.BlockSpec((1, tq, tq), lambda h, k: (k, 0, 0))],
        out_specs=pl.BlockSpec((1, 1, tq, tq), lambda h, k: (h // NSA_HPG, k, 0, h % NSA_HPG)),
        compiler_params=_params(("arbitrary", "arbitrary")),
        name="bias_tiles",
    )(rel_bias.astype(F32), rel)


def _memkv_kernel(mem_ref, g_ref, w_ref, out_ref):
    y = _rms(mem_ref[0], g_ref[...]).astype(BF16)
    out_ref[0] = _mm(y, w_ref[...]).astype(BF16)


def _memkv(mem, mem_norm_g, w_all):
    b, m, d = mem.shape
    n = w_all.shape[1]
    return pl.pallas_call(
        _memkv_kernel,
        out_shape=jax.ShapeDtypeStruct((b, m, n), BF16),
        grid=(b,),
        in_specs=[pl.BlockSpec((1, m, d), lambda i: (i, 0, 0)),
                  _const_spec((1, d)),
                  _const_spec((d, n))],
        out_specs=pl.BlockSpec((1, m, n), lambda i: (i, 0, 0)),
        compiler_params=_params(("arbitrary",)),
        name="mem_kv",
    )(mem, mem_norm_g.reshape(1, d), w_all)


def _mem_attention(xq, mk, mv):
    outs = []
    for hd in range(MEM_HEADS):
        sl = slice(hd * MEM_DH, (hd + 1) * MEM_DH)
        qm = (xq[:, sl] * (MEM_DH ** -0.5)).astype(BF16)
        s = _mm_nt(qm, mk[:, sl])
        p = jnp.exp(s - jnp.max(s, axis=-1, keepdims=True))
        l = jnp.sum(p, axis=-1, keepdims=True)
        outs.append(_mm(p.astype(BF16), mv[:, sl]) / l)
    return outs


EV_ZA = 0
EV_RQ = EV_ZA + POOL_WIDTH
EV_RK = EV_RQ + RET_HEADS * RET_DK
EV_RV = EV_RK + RET_HEADS * RET_DK
EV_XQ = EV_RV + RET_HEADS * RET_DV_PAD
EV_GA = EV_XQ + MEM_WIDTH
EV_GR = EV_GA + POOL_WIDTH
EV_GM = EV_GR + RET_HEADS * RET_DV_PAD
EV_COLS = EV_GM + MEM_WIDTH
EV_YA = 0
EV_YR = POOL_WIDTH
EV_YM = EV_YR + RET_HEADS * RET_DV_PAD
EV_YCOLS = EV_YM + MEM_WIDTH


def _even_kernel(gch_ref, h_ref, g_ref, win_ref, wbd_ref, pscale_ref, cos_ref, sin_ref, decay_ref, xi_ref,
                 zeta_ref, mk_ref, mv_ref, wout_ref, o_ref, ext_ref, state_ref, y_ref):
    si = pl.program_id(1)
    ts = h_ref.shape[1]

    @pl.when(si == 0)
    def _():
        ext_ref[0:POOL_HALO, :] = jnp.zeros((POOL_HALO, POOL_WIDTH), F32)
        state_ref[...] = jnp.zeros(state_ref.shape, F32)

    h = h_ref[0]
    u = _rms(h, g_ref[...]).astype(BF16)

    def proj(start, width):
        return _mm(u, win_ref[:, start:start + width])

    ext_ref[POOL_HALO:, :] = proj(EV_ZA, POOL_WIDTH)
    e = ext_ref[...]
    s2 = e + pltpu.roll(e, 1, 0)
    s4 = s2 + pltpu.roll(s2, 2, 0)
    s8 = s4 + pltpu.roll(s4, 4, 0)
    s16 = s8 + pltpu.roll(s8, 8, 0)
    lane = lax.broadcasted_iota(jnp.int32, e.shape, 1)
    row = lax.broadcasted_iota(jnp.int32, e.shape, 0)
    tpos = si * ts + row - POOL_HALO
    g0, g1, g2 = lane < POOL_GROUP, lane < 2 * POOL_GROUP, lane < 3 * POOL_GROUP
    wsum = jnp.where(g0, s2, jnp.where(g1, s4, jnp.where(g2, s8, s16)))
    wlen = jnp.where(g0, POOL_WINDOWS[0], jnp.where(g1, POOL_WINDOWS[1],
                                                    jnp.where(g2, POOL_WINDOWS[2], POOL_WINDOWS[3])))
    cnt = jnp.maximum(jnp.minimum(tpos + 1, wlen), 1).astype(F32)
    pooled = (wsum / cnt - e)[POOL_HALO:]
    ext_ref[0:POOL_HALO, :] = e[ts:ts + POOL_HALO]
    a = _mm(pooled.astype(BF16), wbd_ref[...]) * pscale_ref[...]
    y_ref[:, EV_YA:EV_YA + POOL_WIDTH] = (a * _silu(proj(EV_GA, POOL_WIDTH))).astype(BF16)

    cos = cos_ref[...]
    sin = sin_ref[...]
    vlane = lax.broadcasted_iota(jnp.int32, (RET_CHUNK, RET_DV_PAD), 1) < RET_DV
    for hd in range(RET_HEADS):
        qh = proj(EV_RQ + hd * RET_DK, RET_DK)
        kh = proj(EV_RK + hd * RET_DK, RET_DK)
        q_rot = (qh * cos + pltpu.roll(qh, RET_DK // 2, 1) * sin) * (RET_DK ** -0.5)
        k_rot = kh * cos + pltpu.roll(kh, RET_DK // 2, 1) * sin
        vb = proj(EV_RV + hd * RET_DV_PAD, RET_DV_PAD).astype(BF16)
        gate_r = _silu(proj(EV_GR + hd * RET_DV_PAD, RET_DV_PAD))
        for c in range(ts // RET_CHUNK):
            rows = slice(c * RET_CHUNK, (c + 1) * RET_CHUNK)
            qc, kc, vc = q_rot[rows], k_rot[rows], vb[rows]
            att = _mm_nt(qc.astype(BF16), kc.astype(BF16)) * decay_ref[hd]
            state = state_ref[hd]
            o = _mm(att.astype(BF16), vc) + _mm((qc * xi_ref[hd]).astype(BF16), state.astype(BF16))
            kv = _mm_tn((kc * zeta_ref[hd]).astype(BF16), vc)
            state_ref[hd] = state * gch_ref[hd] + kv
            mu = jnp.sum(o, axis=-1, keepdims=True) * (1.0 / RET_DV)
            dlt = jnp.where(vlane, o - mu, 0.0)
            var = jnp.sum(dlt * dlt, axis=-1, keepdims=True) * (1.0 / RET_DV)
            on = dlt * lax.rsqrt(var + EPS)
            col = EV_YR + hd * RET_DV_PAD
            y_ref[rows, col:col + RET_DV_PAD] = (on * gate_r[rows]).astype(BF16)

    xq = proj(EV_XQ, MEM_WIDTH)
    gm = _silu(proj(EV_GM, MEM_WIDTH))
    for hd, om in enumerate(_mem_attention(xq, mk_ref[0], mv_ref[0])):
        sl = slice(hd * MEM_DH, (hd + 1) * MEM_DH)
        y_ref[:, EV_YM + hd * MEM_DH:EV_YM + (hd + 1) * MEM_DH] = (om * gm[:, sl]).astype(BF16)

    o_ref[0] = h + _mm(y_ref[...], wout_ref[...])


def _pad_heads(w, heads, width, padded, axis):
    shp = w.shape
    w = w.reshape(shp[:axis] + (heads, width) + shp[axis + 1:])
    pad = [(0, 0)] * w.ndim
    pad[axis + 1] = (0, padded - width)
    w = jnp.pad(w, pad)
    return w.reshape(shp[:axis] + (heads * padded,) + shp[axis + 1:])


def _retention_tables(s):
    half = RET_DK // 2
    inv = ROPE_BASE ** (-jnp.arange(half, dtype=F32) / half)
    ang = jnp.arange(s, dtype=F32)[:, None] * inv[None, :]
    cos, sin = jnp.cos(ang), jnp.sin(ang)
    cos_t = jnp.concatenate([cos, cos], axis=-1)
    sin_t = jnp.concatenate([-sin, sin], axis=-1)
    c = RET_CHUNK
    log_g = jnp.log(1.0 - jnp.exp2(-5.0 - jnp.arange(RET_HEADS, dtype=F32)))
    n = jnp.arange(c, dtype=F32)
    diff = n[:, None] - n[None, :]
    decay = jnp.where(diff >= 0, jnp.exp(log_g[:, None, None] * jnp.maximum(diff, 0.0)), 0.0)
    xi = jnp.exp(log_g[:, None] * (n + 1.0))
    zeta = jnp.exp(log_g[:, None] * (c - 1.0 - n))
    g_chunk = jnp.exp(log_g * c)
    xi_t = jnp.broadcast_to(xi[:, :, None], (RET_HEADS, c, RET_DK))
    zeta_t = jnp.broadcast_to(zeta[:, :, None], (RET_HEADS, c, RET_DK))
    return cos_t, sin_t, decay, xi_t, zeta_t, g_chunk


def _even_layer(h, memkv, layer, g, w_in, pool_w, pool_scale, w_out, ts=256):
    b, s, d = h.shape
    za, rq, rk, rv, xq, gate = jnp.split(w_in, np.cumsum(
        [POOL_WIDTH, RET_HEADS * RET_DK, RET_HEADS * RET_DK, RET_HEADS * RET_DV, MEM_WIDTH])[:].tolist(), axis=1)
    ga, gr, gm = jnp.split(gate, [POOL_WIDTH, POOL_WIDTH + RET_HEADS * RET_DV], axis=1)
    win = jnp.concatenate([za, rq, rk, _pad_heads(rv, RET_HEADS, RET_DV, RET_DV_PAD, 1), xq, ga,
                           _pad_heads(gr, RET_HEADS, RET_DV, RET_DV_PAD, 1), gm], axis=1).astype(BF16)
    oa, orr, om = jnp.split(w_out, [POOL_WIDTH, POOL_WIDTH + RET_HEADS * RET_DV], axis=0)
    wout = jnp.concatenate([oa, _pad_heads(orr, RET_HEADS, RET_DV, RET_DV_PAD, 0), om], axis=0).astype(BF16)
    wbd = jnp.zeros((POOL_WIDTH, POOL_WIDTH), F32)
    for gi in range(len(POOL_WINDOWS)):
        sl = slice(gi * POOL_GROUP, (gi + 1) * POOL_GROUP)
        wbd = wbd.at[sl, sl].set(pool_w[gi])
    wbd = wbd.astype(BF16)
    cos_t, sin_t, decay, xi_t, zeta_t, g_chunk = _retention_tables(s)
    kv_blk = 2 * layer
    return pl.pallas_call(
        _even_kernel,
        out_shape=jax.ShapeDtypeStruct((b, s, d), F32),
        grid=(b, s // ts),
        in_specs=[pl.BlockSpec(memory_space=pltpu.SMEM),
                  pl.BlockSpec((1, ts, d), lambda i, j: (i, j, 0)),
                  _const_spec((1, d)),
                  _const_spec((d, EV_COLS)),
                  _const_spec((POOL_WIDTH, POOL_WIDTH)),
                  _const_spec((1, POOL_WIDTH)),
                  pl.BlockSpec((ts, RET_DK), lambda i, j: (j, 0)),
                  pl.BlockSpec((ts, RET_DK), lambda i, j: (j, 0)),
                  _const_spec((RET_HEADS, RET_CHUNK, RET_CHUNK)),
                  _const_spec((RET_HEADS, RET_CHUNK, RET_DK)),
                  _const_spec((RET_HEADS, RET_CHUNK, RET_DK)),
                  pl.BlockSpec((1, N_MEM, MEM_WIDTH), lambda i, j: (i, 0, kv_blk)),
                  pl.BlockSpec((1, N_MEM, MEM_WIDTH), lambda i, j: (i, 0, kv_blk + 1)),
                  _const_spec((EV_YCOLS, d))],
        out_specs=pl.BlockSpec((1, ts, d), lambda i, j: (i, j, 0)),
        scratch_shapes=[pltpu.VMEM((POOL_HALO + ts, POOL_WIDTH), F32),
                        pltpu.VMEM((RET_HEADS, RET_DK, RET_DV_PAD), F32),
                        pltpu.VMEM((ts, EV_YCOLS), BF16)],
        compiler_params=_params(("arbitrary", "arbitrary")),
        name="even_layer",
    )(g_chunk, h, g.reshape(1, d), win, wbd, pool_scale.reshape(1, POOL_WIDTH), cos_t, sin_t, decay, xi_t, zeta_t,
      memkv, memkv, wout)


KV_W = NSA_KV * NSA_DH
OD_Q = 0
OD_KS = OD_Q + NSA_HEADS * NSA_DH
OD_VS = OD_KS + KV_W
OD_KW = OD_VS + KV_W
OD_VW = OD_KW + KV_W
OD_KC = OD_VW + KV_W
OD_VC = OD_KC + KV_W
OD_GL = OD_VC + KV_W
OD_XQ = OD_GL + NSA_KV * LANES
OD_GN = OD_XQ + MEM_WIDTH
OD_GM = OD_GN + NSA_HEADS * NSA_DH
OD_COLS = OD_GM + MEM_WIDTH


def _odd_in_kernel(h_ref, g_ref, win_ref, pe_ref, mk_ref, mv_ref,
                   q_ref, kaug_ref, vst_ref, kw_ref, vwt_ref, kca_ref, kcb_ref, vca_ref, vcb_ref, gates_ref, gatest_ref,
                   sg_ref, ym_ref):
    si = pl.program_id(1)
    ts = h_ref.shape[1]
    u = _rms(h_ref[0], g_ref[...]).astype(BF16)

    def proj(start, width):
        return _mm(u, win_ref[:, start:start + width])

    zq = proj(OD_Q, NSA_HEADS * NSA_DH) * (NSA_DH ** -0.5)
    zg = _silu(proj(OD_GN, NSA_HEADS * NSA_DH))
    for g in range(NSA_KV):
        for hh in range(NSA_HPG):
            sl = slice((g * NSA_HPG + hh) * NSA_DH, (g * NSA_HPG + hh + 1) * NSA_DH)
            q_ref[0, g, hh] = zq[:, sl].astype(BF16)
            sg_ref[0, g, hh] = zg[:, sl]

    zkv = proj(OD_KS, 6 * KV_W)
    lane = lax.broadcasted_iota(jnp.int32, (ts, LANES), 1)
    blk = (si * ts + lax.broadcasted_iota(jnp.int32, (ts, LANES), 0)) // SLC_BLOCK
    onehot = jnp.where(lane == blk, 1.0, 0.0).astype(BF16)
    zgl = jax.nn.sigmoid(proj(OD_GL, NSA_KV * LANES))
    for g in range(NSA_KV):
        def piece(idx):
            off = idx * KV_W + g * NSA_DH
            return zkv[:, off:off + NSA_DH]
        kaug_ref[0, g, :, 0:NSA_DH] = piece(0).astype(BF16)
        kaug_ref[0, g, :, NSA_DH:2 * NSA_DH] = onehot
        vst_ref[0, g, 0] = piece(1).T.astype(BF16)
        kw_ref[0, g] = piece(2).astype(BF16)
        vwt_ref[0, g, 0] = piece(3).T.astype(BF16)
        kc, vc = piece(4), piece(5)
        kca_ref[0, g] = (kc + pe_ref[0, 0]).astype(BF16)
        kcb_ref[0, g] = (kc + pe_ref[0, 1]).astype(BF16)
        vca_ref[0, g] = (vc + pe_ref[1, 0]).astype(BF16)
        vcb_ref[0, g] = (vc + pe_ref[1, 1]).astype(BF16)
        gates_ref[0, g] = zgl[:, g * LANES:(g + 1) * LANES]
        gatest_ref[0, g] = zgl[:, g * LANES:(g + 1) * LANES].T

    xq = proj(OD_XQ, MEM_WIDTH)
    gm = _silu(proj(OD_GM, MEM_WIDTH))
    for hd, om in enumerate(_mem_attention(xq, mk_ref[0], mv_ref[0])):
        sl = slice(hd * MEM_DH, (hd + 1) * MEM_DH)
        ym_ref[0, :, sl] = (om * gm[:, sl]).astype(BF16)


def _odd_in(h, memkv, layer, g, w_in, cmp_pe, ts=256):
    b, s, d = h.shape
    sizes = [NSA_HEADS * NSA_DH] + [KV_W] * 6 + [3 * NSA_HEADS, MEM_WIDTH]
    q, kc, vc, ks, vs, kw, vw, gl, xq, gate = jnp.split(w_in, np.cumsum(sizes).tolist(), axis=1)
    gl = gl.reshape(d, 3, NSA_KV, NSA_HPG).transpose(0, 2, 1, 3).reshape(d, NSA_KV, 3 * NSA_HPG)
    gl = jnp.pad(gl, ((0, 0), (0, 0), (0, LANES - 3 * NSA_HPG))).reshape(d, NSA_KV * LANES)
    gn, gm = jnp.split(gate, [NSA_HEADS * NSA_DH], axis=1)
    win = jnp.concatenate([q, ks, vs, kw, vw, kc, vc, gl, xq, gn, gm], axis=1).astype(BF16)
    reps = ts // CMP_STRIDE
    pe = jnp.stack([jnp.stack([jnp.tile(cmp_pe[kv, :CMP_STRIDE], (reps, 1)),
                               jnp.tile(cmp_pe[kv, CMP_STRIDE:], (reps, 1))]) for kv in range(2)])
    kv_blk = 2 * layer
    assert ts == NSA_TQ
    head_t = jax.ShapeDtypeStruct((b, NSA_KV, s, NSA_DH), BF16)
    head_spec = pl.BlockSpec((1, NSA_KV, ts, NSA_DH), lambda i, j: (i, 0, j, 0))
    headt_t = jax.ShapeDtypeStruct((b, NSA_KV, s // ts, NSA_DH, ts), BF16)
    headt_spec = pl.BlockSpec((1, NSA_KV, 1, NSA_DH, ts), lambda i, j: (i, 0, j, 0, 0))
    qlike_spec = pl.BlockSpec((1, NSA_KV, NSA_HPG, ts, NSA_DH), lambda i, j: (i, 0, 0, j, 0))
    return pl.pallas_call(
        _odd_in_kernel,
        out_shape=[jax.ShapeDtypeStruct((b, NSA_KV, NSA_HPG, s, NSA_DH), BF16),
                   jax.ShapeDtypeStruct((b, NSA_KV, s, 2 * NSA_DH), BF16),
                   headt_t, head_t, headt_t, head_t, head_t, head_t, head_t,
                   jax.ShapeDtypeStruct((b, NSA_KV, s, LANES), F32),
                   jax.ShapeDtypeStruct((b, NSA_KV, LANES, s), F32),
                   jax.ShapeDtypeStruct((b, NSA_KV, NSA_HPG, s, NSA_DH), F32),
                   jax.ShapeDtypeStruct((b, s, MEM_WIDTH), BF16)],
        grid=(b, s // ts),
        in_specs=[pl.BlockSpec((1, ts, d), lambda i, j: (i, j, 0)),
                  _const_spec((1, d)),
                  _const_spec((d, OD_COLS)),
                  _const_spec((2, 2, ts, NSA_DH)),
                  pl.BlockSpec((1, N_MEM, MEM_WIDTH), lambda i, j: (i, 0, kv_blk)),
                  pl.BlockSpec((1, N_MEM, MEM_WIDTH), lambda i, j: (i, 0, kv_blk + 1))],
        out_specs=[qlike_spec,
                   pl.BlockSpec((1, NSA_KV, ts, 2 * NSA_DH), lambda i, j: (i, 0, j, 0)),
                   headt_spec, head_spec, headt_spec, head_spec, head_spec, head_spec, head_spec,
                   pl.BlockSpec((1, NSA_KV, ts, LANES), lambda i, j: (i, 0, j, 0)),
                   pl.BlockSpec((1, NSA_KV, LANES, ts), lambda i, j: (i, 0, 0, j)),
                   qlike_spec,
                   pl.BlockSpec((1, ts, MEM_WIDTH), lambda i, j: (i, j, 0))],
        compiler_params=_params(("arbitrary", "arbitrary")),
        name="odd_in_proj",
    )(h, g.reshape(1, d), win, pe, memkv, memkv)


def _compress_kernel(xka_ref, xkb_ref, xva_ref, xvb_ref, w1a_ref, w1b_ref, b1_ref, w2_ref, kc_ref, vc_ref):
    n = xka_ref.shape[2]
    for kv, (xa, xb, out) in enumerate(((xka_ref, xkb_ref, kc_ref), (xva_ref, xvb_ref, vc_ref))):
        first = _mm(xa[0, 0], w1a_ref[kv])
        second = _mm(xb[0, 0], w1b_ref[kv])
        hid = first + pltpu.roll(second, n - 1, 0) + b1_ref[kv]
        c = _mm(_silu(hid).astype(BF16), w2_ref[kv])
        out[0, 0, 0:CMP_PAD] = jnp.zeros((CMP_PAD, NSA_DH), BF16)
        out[0, 0, CMP_PAD:] = c.astype(BF16)


def _compress(kca, kcb, vca, vcb, w1, b1, w2):
    b, g, s, dh = kca.shape
    n = s // CMP_STRIDE
    half = CMP_STRIDE * dh
    xs = [x.reshape(b, g, n, half) for x in (kca, kcb, vca, vcb)]
    w1 = w1.astype(BF16)
    x_spec = pl.BlockSpec((1, 1, n, half), lambda i, j: (i, j, 0, 0))
    o_spec = pl.BlockSpec((1, 1, n + CMP_PAD, dh), lambda i, j: (i, j, 0, 0))
    o_t = jax.ShapeDtypeStruct((b, g, n + CMP_PAD, dh), BF16)
    return pl.pallas_call(
        _compress_kernel,
        out_shape=[o_t, o_t],
        grid=(b, g),
        in_specs=[x_spec, x_spec, x_spec, x_spec,
                  _const_spec((2, half, CMP_HIDDEN)), _const_spec((2, half, CMP_HIDDEN)),
                  _const_spec((2, 1, CMP_HIDDEN)), _const_spec((2, CMP_HIDDEN, dh))],
        out_specs=[o_spec, o_spec],
        compiler_params=_params(("arbitrary", "arbitrary")),
        name="compress",
    )(*xs, w1[:, :half], w1[:, half:], b1.reshape(2, 1, CMP_HIDDEN), w2.astype(BF16))


def _nsa_kernel(q_ref, kaug_ref, vst_ref, kw_ref, vwt_ref, kcmp_ref, vcmp_ref, gates_ref, gatest_ref, sg_ref,
                d0_ref, d1_ref, dw_ref, gc_ref, ov_ref, y_ref, qaug_ref, m_ref, l_ref, acc_ref):
    i = pl.program_id(2)
    tq = NSA_TQ
    rows = NSA_HPG * tq
    s_len = kw_ref.shape[2]
    n_cmp = s_len // CMP_STRIDE
    n_slc = s_len // SLC_BLOCK
    t0 = i * tq
    q = q_ref[0, 0].reshape(rows, NSA_DH)

    def per_head(x):
        return x.reshape(NSA_HPG, tq, x.shape[-1])

    j_near = pl.multiple_of(i * (tq // CMP_STRIDE), tq // CMP_STRIDE)
    s_far = _mm_nt(q, kcmp_ref[0, 0, CMP_PAD:CMP_PAD + n_cmp])
    jcol = lax.broadcasted_iota(jnp.int32, s_far.shape, 1)
    s_far = jnp.where(jcol < t0 // CMP_STRIDE - CMP_PAD, s_far, NEG)
    s_near = _mm_nt(q, kcmp_ref[0, 0, pl.ds(j_near, CMP_NEAR)])
    s_near = s_near + jnp.concatenate([gc_ref[0, 0, :, hh * tq:hh * tq + CMP_NEAR] for hh in range(NSA_HPG)], axis=0)
    ncol = lax.broadcasted_iota(jnp.int32, s_near.shape, 1)
    s_near = jnp.where(ncol + t0 // CMP_STRIDE - CMP_PAD >= 0, s_near, NEG)
    mx = jnp.maximum(jnp.max(s_far, axis=-1, keepdims=True), jnp.max(s_near, axis=-1, keepdims=True))
    p_far = jnp.exp(s_far - mx)
    p_near = jnp.exp(s_near - mx)
    denom = jnp.sum(p_far, axis=-1, keepdims=True) + jnp.sum(p_near, axis=-1, keepdims=True)
    trow = t0 + jnp.bitwise_and(lax.broadcasted_iota(jnp.int32, (rows, 1), 0), tq - 1)
    scale = jnp.where(trow >= CMP_BLOCK - 1, 1.0, 0.0) / denom
    p_far = (p_far * scale).astype(BF16)
    p_near = (p_near * scale).astype(BF16)
    o_cmp = _mm(p_far, vcmp_ref[0, 0, CMP_PAD:CMP_PAD + n_cmp]) + _mm(p_near, vcmp_ref[0, 0, pl.ds(j_near, CMP_NEAR)])
    imp = _mm(p_far, ov_ref[CMP_PAD:CMP_PAD + n_cmp]) + _mm(p_near, ov_ref[pl.ds(j_near, CMP_NEAR)])
    imp = jnp.sum(per_head(imp), axis=0)

    lane = lax.broadcasted_iota(jnp.int32, (tq, LANES), 1)
    tq_pos = t0 + lax.broadcasted_iota(jnp.int32, (tq, LANES), 0)
    cur = tq_pos // SLC_BLOCK
    forced = (lane == 0) | (lane == cur) | (lane == cur - 1)
    future = lane * SLC_BLOCK > tq_pos
    imp = jnp.where(forced, BIG, jnp.where(future, -BIG, imp))
    imp = jnp.where(lane < n_slc, imp, LOWEST)
    lane_f = lane.astype(F32)
    sel = jnp.zeros((tq, LANES), jnp.bool_)
    for _ in range(SLC_TOPK):
        top = jnp.max(imp, axis=-1, keepdims=True)
        first = jnp.min(jnp.where(imp == top, lane_f, float(LANES)), axis=-1, keepdims=True)
        hit = lane_f == first
        sel = sel | hit
        imp = jnp.where(hit, KNOCKED, imp)
    sneg = jnp.where(sel, 0.0, NEG).astype(BF16)
    qaug_ref[:, 0:NSA_DH] = q
    for hh in range(NSA_HPG):
        qaug_ref[hh * tq:(hh + 1) * tq, NSA_DH:2 * NSA_DH] = sneg

    def reset():
        m_ref[...] = jnp.full(m_ref.shape, LOWEST, F32)
        l_ref[...] = jnp.zeros(l_ref.shape, F32)
        acc_ref[...] = jnp.zeros(acc_ref.shape, F32)

    def update(st, vt):
        m_prev = m_ref[...]
        m_new = jnp.maximum(m_prev, jnp.max(st, axis=0, keepdims=True))
        alpha = jnp.exp(m_prev - m_new)
        pt = jnp.exp(st - m_new)
        l_ref[...] = alpha * l_ref[...] + jnp.sum(pt, axis=0, keepdims=True)
        acc_ref[...] = alpha * acc_ref[...] + _mm(vt, pt.astype(BF16))
        m_ref[...] = m_new

    def scores(k, query):
        return _mm_nt(k, query)

    reset()

    def far_tile(j, carry):
        k0 = pl.multiple_of(j * tq, tq)
        update(scores(kaug_ref[0, 0, pl.ds(k0, tq)], qaug_ref[...]), vst_ref[0, 0, j])
        return carry

    lax.fori_loop(0, jnp.maximum(i - 1, 0), far_tile, 0)

    @pl.when(i >= 1)
    def _():
        k0 = pl.multiple_of(t0 - tq, tq)
        update(scores(kaug_ref[0, 0, pl.ds(k0, tq)], qaug_ref[...]) + d1_ref[0, 0], vst_ref[0, 0, i - 1])

    k_diag = pl.multiple_of(t0, tq)
    update(scores(kaug_ref[0, 0, pl.ds(k_diag, tq)], qaug_ref[...]) + d0_ref[0, 0], vst_ref[0, 0, i])
    ot_slc = acc_ref[...] / l_ref[...]

    reset()

    @pl.when(i >= 2)
    def _():
        k0 = pl.multiple_of(t0 - 2 * tq, tq)
        update(scores(kw_ref[0, 0, pl.ds(k0, tq)], q) + dw_ref[0, 0], vwt_ref[0, 0, i - 2])

    @pl.when(i >= 1)
    def _():
        k0 = pl.multiple_of(t0 - tq, tq)
        update(scores(kw_ref[0, 0, pl.ds(k0, tq)], q) + d1_ref[0, 0], vwt_ref[0, 0, i - 1])

    update(scores(kw_ref[0, 0, pl.ds(k_diag, tq)], q) + d0_ref[0, 0], vwt_ref[0, 0, i])
    ot_win = acc_ref[...] / l_ref[...]

    gts = gates_ref[0, 0]
    gtst = gatest_ref[0, 0]
    for hh in range(NSA_HPG):
        hs = slice(hh * tq, (hh + 1) * tq)
        ot = (gtst[NSA_HPG + hh:NSA_HPG + hh + 1] * ot_slc[:, hs]
              + gtst[2 * NSA_HPG + hh:2 * NSA_HPG + hh + 1] * ot_win[:, hs])
        o = gts[:, hh:hh + 1] * o_cmp[hs] + ot.T
        y_ref[0, :, hh * NSA_DH:(hh + 1) * NSA_DH] = (o * sg_ref[0, 0, hh]).astype(BF16)


def _overlap_table(s):
    n_cmp = (s - CMP_BLOCK) // CMP_STRIDE + 1
    n_slc = s // SLC_BLOCK
    cst = np.arange(n_cmp)[:, None] * CMP_STRIDE
    sst = np.arange(n_slc)[None, :] * SLC_BLOCK
    ov = np.clip(np.minimum(cst + CMP_BLOCK, sst + SLC_BLOCK) - np.maximum(cst, sst), 0, None) / CMP_STRIDE
    full = np.zeros((CMP_PAD + s // CMP_STRIDE, LANES), np.float32)
    full[CMP_PAD:CMP_PAD + n_cmp, :n_slc] = ov
    return jnp.asarray(full, BF16)


def _nsa(q, kaug, vst, kw, vwt, kcmp, vcmp, gates, gatest, sg, bias):
    b, g, hpg, s, dh = q.shape
    tq = NSA_TQ
    assert WINDOW == 2 * tq and s % tq == 0 and s // SLC_BLOCK <= LANES and s // SLC_BLOCK >= SLC_TOPK
    n_pad = kcmp.shape[2]
    rows = hpg * tq
    seq_spec = pl.BlockSpec((1, 1, s, dh), lambda i, j, k: (i, j, 0, 0))
    seqt_spec = pl.BlockSpec((1, 1, s // tq, dh, tq), lambda i, j, k: (i, j, 0, 0, 0))
    cmp_spec = pl.BlockSpec((1, 1, n_pad, dh), lambda i, j, k: (i, j, 0, 0))
    qlike_spec = pl.BlockSpec((1, 1, hpg, tq, dh), lambda i, j, k: (i, j, 0, k, 0))

    def bias_spec(idx):
        return pl.BlockSpec((1, 1, tq, rows), lambda i, j, k: (j, idx, 0, 0))

    return pl.pallas_call(
        _nsa_kernel,
        out_shape=jax.ShapeDtypeStruct((b, s, g * hpg * dh), BF16),
        grid=(b, g, s // tq),
        in_specs=[qlike_spec,
                  pl.BlockSpec((1, 1, s, 2 * dh), lambda i, j, k: (i, j, 0, 0)),
                  seqt_spec, seq_spec, seqt_spec, cmp_spec, cmp_spec,
                  pl.BlockSpec((1, 1, tq, LANES), lambda i, j, k: (i, j, k, 0)),
                  pl.BlockSpec((1, 1, LANES, tq), lambda i, j, k: (i, j, 0, k)),
                  qlike_spec,
                  bias_spec(0), bias_spec(1), bias_spec(2), bias_spec(3),
                  _const_spec((n_pad, LANES))],
        out_specs=pl.BlockSpec((1, tq, hpg * dh), lambda i, j, k: (i, k, j)),
        scratch_shapes=[pltpu.VMEM((rows, 2 * dh), BF16),
                        pltpu.VMEM((1, rows), F32),
                        pltpu.VMEM((1, rows), F32),
                        pltpu.VMEM((dh, rows), F32)],
        compiler_params=_params(("arbitrary", "arbitrary", "arbitrary")),
        name="nsa_attention",
    )(q, kaug, vst, kw, vwt, kcmp, vcmp, gates, gatest, sg, bias, bias, bias, bias, _overlap_table(s))


def _odd_out_kernel(h_ref, yn_ref, ym_ref, wn_ref, wm_ref, fg_ref, o_ref, *, final_norm):
    out = h_ref[...] + _mm(yn_ref[...], wn_ref[...]) + _mm(ym_ref[...], wm_ref[...])
    o_ref[...] = _rms(out, fg_ref[...]) if final_norm else out


def _odd_out(h, yn, ym, w_out, final_g, final_norm, tm=512):
    b, s, d = h.shape
    t = b * s
    nw = NSA_HEADS * NSA_DH
    wout = w_out.astype(BF16)
    out = pl.pallas_call(
        functools.partial(_odd_out_kernel, final_norm=final_norm),
        out_shape=jax.ShapeDtypeStruct((t, d), F32),
        grid=(t // tm,),
        in_specs=[pl.BlockSpec((tm, d), lambda i: (i, 0)),
                  pl.BlockSpec((tm, nw), lambda i: (i, 0)),
                  pl.BlockSpec((tm, MEM_WIDTH), lambda i: (i, 0)),
                  _const_spec((nw, d)), _const_spec((MEM_WIDTH, d)), _const_spec((1, d))],
        out_specs=pl.BlockSpec((tm, d), lambda i: (i, 0)),
        compiler_params=_params(("arbitrary",)),
        name="odd_out_proj",
    )(h.reshape(t, d), yn.reshape(t, nw), ym.reshape(t, MEM_WIDTH), wout[:nw], wout[nw:], final_g.reshape(1, d))
    return out.reshape(b, s, d)


def _final_norm_kernel(h_ref, g_ref, o_ref):
    o_ref[...] = _rms(h_ref[...], g_ref[...])


def _final_norm(h, final_g, tm=512):
    b, s, d = h.shape
    t = b * s
    out = pl.pallas_call(
        _final_norm_kernel,
        out_shape=jax.ShapeDtypeStruct((t, d), F32),
        grid=(t // tm,),
        in_specs=[pl.BlockSpec((tm, d), lambda i: (i, 0)), _const_spec((1, d))],
        out_specs=pl.BlockSpec((tm, d), lambda i: (i, 0)),
        compiler_params=_params(("arbitrary",)),
        name="final_norm",
    )(h.reshape(t, d), final_g.reshape(1, d))
    return out.reshape(b, s, d)


def kernel(x, mem, norm_g, final_g, mem_norm_g, rel_bias, ev_w_in, ev_pool_w, ev_pool_scale, ev_w_mem_kv, ev_w_out,
           od_w_in, od_cmp_pe, od_cmp_w1, od_cmp_b1, od_cmp_w2, od_w_mem_kv, od_w_out):
    depth = norm_g.shape[0]
    w_mem = [(ev_w_mem_kv if i % 2 == 0 else od_w_mem_kv)[i // 2] for i in range(depth)]
    memkv = _memkv(mem, mem_norm_g, jnp.concatenate(w_mem, axis=1).astype(BF16))
    bias = _bias_tiles(rel_bias) if depth > 1 else None
    h = x
    for i in range(depth):
        j = i // 2
        last = i == depth - 1
        if i % 2 == 0:
            h = _even_layer(h, memkv, i, norm_g[i], ev_w_in[j], ev_pool_w[j], ev_pool_scale[j], ev_w_out[j])
            if last:
                h = _final_norm(h, final_g)
        else:
            (q, kaug, vst, kw, vwt, kca, kcb, vca, vcb, gates, gatest, sg, ym) = _odd_in(
                h, memkv, i, norm_g[i], od_w_in[j], od_cmp_pe[j])
            kcmp, vcmp = _compress(kca, kcb, vca, vcb, od_cmp_w1[j], od_cmp_b1[j], od_cmp_w2[j])
            yn = _nsa(q, kaug, vst, kw, vwt, kcmp, vcmp, gates, gatest, sg, bias)
            h = _odd_out(h, yn, ym, od_w_out[j], final_g, last)
    return h
```

```python
import functools
import math

import numpy as np
import jax
import jax.numpy as jnp
from jax import lax
from jax.experimental import pallas as pl
from jax.experimental.pallas import tpu as pltpu

F32 = jnp.float32
BF16 = jnp.bfloat16

D_MODEL = 1024
D_INNER = 2048
N_MEM = 256
EPS = 1e-6
NEG = -1e30
BIG = 1e30

POOL_WINDOWS = (2, 4, 8, 16)
POOL_WIDTH = 768
POOL_GROUP = 192
POOL_HALO = 16

RET_HEADS = 4
RET_DK = 128
RET_DV = 192
RET_DV_PAD = 256
RET_CHUNK = 128
ROPE_BASE = 10000.0

MEM_HEADS = 4
MEM_DH = 128
MEM_WIDTH = 512

NSA_HEADS = 12
NSA_KV = 2
NSA_HPG = 6
NSA_DH = 128
CMP_BLOCK = 32
CMP_STRIDE = 16
CMP_HIDDEN = 256
SLC_BLOCK = 64
SLC_TOPK = 8
WINDOW = 512
REL_BUCKETS = 32
REL_MAX_DIST = 128

LANES = 128
NSA_TQ = 256
CMP_NEAR = 32
CMP_PAD = 16
LOG2E = math.log2(math.e)
V_ROWS = NSA_DH + 16
LOWEST = -3.0e38
KNOCKED = -3.3e38

VMEM_LIMIT = 56 * 1024 * 1024


def _mm(a, b):
    return jnp.dot(a, b, preferred_element_type=F32)


def _mm_nt(a, b):
    return lax.dot_general(a, b, (((1,), (1,)), ((), ())), preferred_element_type=F32)


def _mm_tn(a, b):
    return lax.dot_general(a, b, (((0,), (0,)), ((), ())), preferred_element_type=F32)


def _rms(x, g):
    return x * lax.rsqrt(jnp.mean(x * x, axis=-1, keepdims=True) + EPS) * g


def _silu(x):
    return x * jax.nn.sigmoid(x)


def _const_spec(shape):
    nd = len(shape)
    return pl.BlockSpec(shape, lambda *_: (0,) * nd, pipeline_mode=pl.Buffered(1))


def _params(sem):
    return pltpu.CompilerParams(dimension_semantics=sem, vmem_limit_bytes=VMEM_LIMIT)


def _bias_kernel(tab_ref, rel_ref, out_ref):
    h = pl.program_id(0)
    rel = rel_ref[0]
    n = jnp.maximum(rel, 0)
    max_exact = REL_BUCKETS // 2
    nf = jnp.maximum(n, 1).astype(F32)
    large = max_exact + (jnp.log(nf / max_exact) / math.log(REL_MAX_DIST / max_exact)
                         * (REL_BUCKETS - max_exact)).astype(jnp.int32)
    large = jnp.minimum(large, REL_BUCKETS - 1)
    bucket = jnp.where(n < max_exact, n, large)
    far = tab_ref[REL_BUCKETS - 1, h]
    val = jnp.zeros(rel.shape, F32)
    for b in range(REL_BUCKETS - 1):
        val = jnp.where(bucket == b, tab_ref[b, h] - far, val)
    out_ref[0, 0] = jnp.where(rel < 0, NEG, val * LOG2E)


def _bias_tiles(rel_bias):
    tq = NSA_TQ
    r = np.arange(tq)[:, None]
    c = np.arange(tq)[None, :]
    d0 = r - c
    d1 = tq + r - c
    dw = np.where(c > r, WINDOW + r - c, -1)
    gc = np.where(c < CMP_NEAR, r - CMP_STRIDE * c + (CMP_STRIDE * CMP_PAD - (CMP_BLOCK - 1)), -1)
    rel = jnp.asarray(np.stack([d0.T, d1.T, dw.T, gc.T]).astype(np.int32))
    nt = rel.shape[0]
    return pl.pallas_call(
        _bias_kernel,
        out_shape=jax.ShapeDtypeStruct((NSA_KV, nt, tq, NSA_HPG * tq), F32),
        grid=(NSA_HEADS, nt),
        in_specs=[pl.BlockSpec(memory_space=pltpu.SMEM),
                  pl.BlockSpec((1, tq, tq), lambda h, k: (k, 0, 0))],
        out_specs=pl.BlockSpec((1, 1, tq, tq), lambda h, k: (h // NSA_HPG, k, 0, h % NSA_HPG)),
        compiler_params=_params(("arbitrary", "arbitrary")),
        name="bias_tiles",
    )(rel_bias.astype(F32), rel)


def _memkv_kernel(mem_ref, g_ref, w_ref, out_ref):
    y = _rms(mem_ref[0], g_ref[...]).astype(BF16)
    out_ref[0] = _mm(y, w_ref[...]).astype(BF16)


def _memkv(mem, mem_norm_g, w_all):
    b, m, d = mem.shape
    n = w_all.shape[1]
    return pl.pallas_call(
        _memkv_kernel,
        out_shape=jax.ShapeDtypeStruct((b, m, n), BF16),
        grid=(b,),
        in_specs=[pl.BlockSpec((1, m, d), lambda i: (i, 0, 0)),
                  _const_spec((1, d)),
                  _const_spec((d, n))],
        out_specs=pl.BlockSpec((1, m, n), lambda i: (i, 0, 0)),
        compiler_params=_params(("arbitrary",)),
        name="mem_kv",
    )(mem, mem_norm_g.reshape(1, d), w_all)


def _mem_attention(xq, mk, mv):
    outs = []
    for hd in range(MEM_HEADS):
        sl = slice(hd * MEM_DH, (hd + 1) * MEM_DH)
        qm = (xq[:, sl] * (MEM_DH ** -0.5)).astype(BF16)
        s = _mm_nt(qm, mk[:, sl])
        p = jnp.exp(s - jnp.max(s, axis=-1, keepdims=True))
        l = jnp.sum(p, axis=-1, keepdims=True)
        outs.append(_mm(p.astype(BF16), mv[:, sl]) / l)
    return outs


EV_ZA = 0
EV_RQ = EV_ZA + POOL_WIDTH
EV_RK = EV_RQ + RET_HEADS * RET_DK
EV_RV = EV_RK + RET_HEADS * RET_DK
EV_XQ = EV_RV + RET_HEADS * RET_DV_PAD
EV_GA = EV_XQ + MEM_WIDTH
EV_GR = EV_GA + POOL_WIDTH
EV_GM = EV_GR + RET_HEADS * RET_DV_PAD
EV_COLS = EV_GM + MEM_WIDTH
EV_YA = 0
EV_YR = POOL_WIDTH
EV_YM = EV_YR + RET_HEADS * RET_DV_PAD
EV_YCOLS = EV_YM + MEM_WIDTH


def _even_kernel(gch_ref, h_ref, g_ref, win_ref, wbd_ref, pscale_ref, cos_ref, sin_ref, decay_ref, xi_ref,
                 zeta_ref, mk_ref, mv_ref, wout_ref, o_ref, ext_ref, state_ref, y_ref):
    si = pl.program_id(1)
    ts = h_ref.shape[1]

    @pl.when(si == 0)
    def _():
        ext_ref[0:POOL_HALO, :] = jnp.zeros((POOL_HALO, POOL_WIDTH), F32)
        state_ref[...] = jnp.zeros(state_ref.shape, F32)

    h = h_ref[0]
    u = _rms(h, g_ref[...]).astype(BF16)

    def proj(start, width):
        return _mm(u, win_ref[:, start:start + width])

    ext_ref[POOL_HALO:, :] = proj(EV_ZA, POOL_WIDTH)
    e = ext_ref[...]
    s2 = e + pltpu.roll(e, 1, 0)
    s4 = s2 + pltpu.roll(s2, 2, 0)
    s8 = s4 + pltpu.roll(s4, 4, 0)
    s16 = s8 + pltpu.roll(s8, 8, 0)
    lane = lax.broadcasted_iota(jnp.int32, e.shape, 1)
    row = lax.broadcasted_iota(jnp.int32, e.shape, 0)
    tpos = si * ts + row - POOL_HALO
    g0, g1, g2 = lane < POOL_GROUP, lane < 2 * POOL_GROUP, lane < 3 * POOL_GROUP
    wsum = jnp.where(g0, s2, jnp.where(g1, s4, jnp.where(g2, s8, s16)))
    wlen = jnp.where(g0, POOL_WINDOWS[0], jnp.where(g1, POOL_WINDOWS[1],
                                                    jnp.where(g2, POOL_WINDOWS[2], POOL_WINDOWS[3])))
    cnt = jnp.maximum(jnp.minimum(tpos + 1, wlen), 1).astype(F32)
    pooled = (wsum / cnt - e)[POOL_HALO:]
    ext_ref[0:POOL_HALO, :] = e[ts:ts + POOL_HALO]
    a = _mm(pooled.astype(BF16), wbd_ref[...]) * pscale_ref[...]
    y_ref[:, EV_YA:EV_YA + POOL_WIDTH] = (a * _silu(proj(EV_GA, POOL_WIDTH))).astype(BF16)

    cos = cos_ref[...]
    sin = sin_ref[...]
    vlane = lax.broadcasted_iota(jnp.int32, (RET_CHUNK, RET_DV_PAD), 1) < RET_DV
    for hd in range(RET_HEADS):
        qh = proj(EV_RQ + hd * RET_DK, RET_DK)
        kh = proj(EV_RK + hd * RET_DK, RET_DK)
        q_rot = (qh * cos + pltpu.roll(qh, RET_DK // 2, 1) * sin) * (RET_DK ** -0.5)
        k_rot = kh * cos + pltpu.roll(kh, RET_DK // 2, 1) * sin
        vb = proj(EV_RV + hd * RET_DV_PAD, RET_DV_PAD).astype(BF16)
        gate_r = _silu(proj(EV_GR + hd * RET_DV_PAD, RET_DV_PAD))
        for c in range(ts // RET_CHUNK):
            rows = slice(c * RET_CHUNK, (c + 1) * RET_CHUNK)
            qc, kc, vc = q_rot[rows], k_rot[rows], vb[rows]
            att = _mm_nt(qc.astype(BF16), kc.astype(BF16)) * decay_ref[hd]
            state = state_ref[hd]
            o = _mm(att.astype(BF16), vc) + _mm((qc * xi_ref[hd]).astype(BF16), state.astype(BF16))
            kv = _mm_tn((kc * zeta_ref[hd]).astype(BF16), vc)
            state_ref[hd] = state * gch_ref[hd] + kv
            mu = jnp.sum(o, axis=-1, keepdims=True) * (1.0 / RET_DV)
            dlt = jnp.where(vlane, o - mu, 0.0)
            var = jnp.sum(dlt * dlt, axis=-1, keepdims=True) * (1.0 / RET_DV)
            on = dlt * lax.rsqrt(var + EPS)
            col = EV_YR + hd * RET_DV_PAD
            y_ref[rows, col:col + RET_DV_PAD] = (on * gate_r[rows]).astype(BF16)

    xq = proj(EV_XQ, MEM_WIDTH)
    gm = _silu(proj(EV_GM, MEM_WIDTH))
    for hd, om in enumerate(_mem_attention(xq, mk_ref[0], mv_ref[0])):
        sl = slice(hd * MEM_DH, (hd + 1) * MEM_DH)
        y_ref[:, EV_YM + hd * MEM_DH:EV_YM + (hd + 1) * MEM_DH] = (om * gm[:, sl]).astype(BF16)

    o_ref[0] = h + _mm(y_ref[...], wout_ref[...])


def _pad_heads(w, heads, width, padded, axis):
    shp = w.shape
    w = w.reshape(shp[:axis] + (heads, width) + shp[axis + 1:])
    pad = [(0, 0)] * w.ndim
    pad[axis + 1] = (0, padded - width)
    w = jnp.pad(w, pad)
    return w.reshape(shp[:axis] + (heads * padded,) + shp[axis + 1:])


def _retention_tables(s):
    half = RET_DK // 2
    inv = ROPE_BASE ** (-jnp.arange(half, dtype=F32) / half)
    ang = jnp.arange(s, dtype=F32)[:, None] * inv[None, :]
    cos, sin = jnp.cos(ang), jnp.sin(ang)
    cos_t = jnp.concatenate([cos, cos], axis=-1)
    sin_t = jnp.concatenate([-sin, sin], axis=-1)
    c = RET_CHUNK
    log_g = jnp.log(1.0 - jnp.exp2(-5.0 - jnp.arange(RET_HEADS, dtype=F32)))
    n = jnp.arange(c, dtype=F32)
    diff = n[:, None] - n[None, :]
    decay = jnp.where(diff >= 0, jnp.exp(log_g[:, None, None] * jnp.maximum(diff, 0.0)), 0.0)
    xi = jnp.exp(log_g[:, None] * (n + 1.0))
    zeta = jnp.exp(log_g[:, None] * (c - 1.0 - n))
    g_chunk = jnp.exp(log_g * c)
    xi_t = jnp.broadcast_to(xi[:, :, None], (RET_HEADS, c, RET_DK))
    zeta_t = jnp.broadcast_to(zeta[:, :, None], (RET_HEADS, c, RET_DK))
    return cos_t, sin_t, decay, xi_t, zeta_t, g_chunk


def _even_layer(h, memkv, layer, g, w_in, pool_w, pool_scale, w_out, ts=256):
    b, s, d = h.shape
    za, rq, rk, rv, xq, gate = jnp.split(w_in, np.cumsum(
        [POOL_WIDTH, RET_HEADS * RET_DK, RET_HEADS * RET_DK, RET_HEADS * RET_DV, MEM_WIDTH])[:].tolist(), axis=1)
    ga, gr, gm = jnp.split(gate, [POOL_WIDTH, POOL_WIDTH + RET_HEADS * RET_DV], axis=1)
    win = jnp.concatenate([za, rq, rk, _pad_heads(rv, RET_HEADS, RET_DV, RET_DV_PAD, 1), xq, ga,
                           _pad_heads(gr, RET_HEADS, RET_DV, RET_DV_PAD, 1), gm], axis=1).astype(BF16)
    oa, orr, om = jnp.split(w_out, [POOL_WIDTH, POOL_WIDTH + RET_HEADS * RET_DV], axis=0)
    wout = jnp.concatenate([oa, _pad_heads(orr, RET_HEADS, RET_DV, RET_DV_PAD, 0), om], axis=0).astype(BF16)
    wbd = jnp.zeros((POOL_WIDTH, POOL_WIDTH), F32)
    for gi in range(len(POOL_WINDOWS)):
        sl = slice(gi * POOL_GROUP, (gi + 1) * POOL_GROUP)
        wbd = wbd.at[sl, sl].set(pool_w[gi])
    wbd = wbd.astype(BF16)
    cos_t, sin_t, decay, xi_t, zeta_t, g_chunk = _retention_tables(s)
    kv_blk = 2 * layer
    return pl.pallas_call(
        _even_kernel,
        out_shape=jax.ShapeDtypeStruct((b, s, d), F32),
        grid=(b, s // ts),
        in_specs=[pl.BlockSpec(memory_space=pltpu.SMEM),
                  pl.BlockSpec((1, ts, d), lambda i, j: (i, j, 0)),
                  _const_spec((1, d)),
                  _const_spec((d, EV_COLS)),
                  _const_spec((POOL_WIDTH, POOL_WIDTH)),
                  _const_spec((1, POOL_WIDTH)),
                  pl.BlockSpec((ts, RET_DK), lambda i, j: (j, 0)),
                  pl.BlockSpec((ts, RET_DK), lambda i, j: (j, 0)),
                  _const_spec((RET_HEADS, RET_CHUNK, RET_CHUNK)),
                  _const_spec((RET_HEADS, RET_CHUNK, RET_DK)),
                  _const_spec((RET_HEADS, RET_CHUNK, RET_DK)),
                  pl.BlockSpec((1, N_MEM, MEM_WIDTH), lambda i, j: (i, 0, kv_blk)),
                  pl.BlockSpec((1, N_MEM, MEM_WIDTH), lambda i, j: (i, 0, kv_blk + 1)),
                  _const_spec((EV_YCOLS, d))],
        out_specs=pl.BlockSpec((1, ts, d), lambda i, j: (i, j, 0)),
        scratch_shapes=[pltpu.VMEM((POOL_HALO + ts, POOL_WIDTH), F32),
                        pltpu.VMEM((RET_HEADS, RET_DK, RET_DV_PAD), F32),
                        pltpu.VMEM((ts, EV_YCOLS), BF16)],
        compiler_params=_params(("arbitrary", "arbitrary")),
        name="even_layer",
    )(g_chunk, h, g.reshape(1, d), win, wbd, pool_scale.reshape(1, POOL_WIDTH), cos_t, sin_t, decay, xi_t, zeta_t,
      memkv, memkv, wout)


KV_W = NSA_KV * NSA_DH
OD_Q = 0
OD_KS = OD_Q + NSA_HEADS * NSA_DH
OD_VS = OD_KS + KV_W
OD_KW = OD_VS + KV_W
OD_VW = OD_KW + KV_W
OD_KC = OD_VW + KV_W
OD_VC = OD_KC + KV_W
OD_GL = OD_VC + KV_W
OD_XQ = OD_GL + NSA_KV * LANES
OD_GN = OD_XQ + MEM_WIDTH
OD_GM = OD_GN + NSA_HEADS * NSA_DH
OD_COLS = OD_GM + MEM_WIDTH


def _odd_in_kernel(h_ref, g_ref, win_ref, pe_ref, mk_ref, mv_ref,
                   q_ref, kaug_ref, vst_ref, kw_ref, vwt_ref, kca_ref, kcb_ref, vca_ref, vcb_ref, gatest_ref,
                   sg_ref, ym_ref):
    si = pl.program_id(1)
    ts = h_ref.shape[1]
    u = _rms(h_ref[0], g_ref[...]).astype(BF16)

    def proj(start, width):
        return _mm(u, win_ref[:, start:start + width])

    zq = proj(OD_Q, NSA_HEADS * NSA_DH) * (NSA_DH ** -0.5 * LOG2E)
    zg = _silu(proj(OD_GN, NSA_HEADS * NSA_DH))
    for g in range(NSA_KV):
        for hh in range(NSA_HPG):
            sl = slice((g * NSA_HPG + hh) * NSA_DH, (g * NSA_HPG + hh + 1) * NSA_DH)
            q_ref[0, g, hh] = zq[:, sl].astype(BF16)
            sg_ref[0, g, hh] = zg[:, sl]

    zkv = proj(OD_KS, 6 * KV_W)
    lane = lax.broadcasted_iota(jnp.int32, (ts, LANES), 1)
    blk = (si * ts + lax.broadcasted_iota(jnp.int32, (ts, LANES), 0)) // SLC_BLOCK
    onehot = jnp.where(lane == blk, 1.0, 0.0).astype(BF16)
    ones_rows = jnp.where(lax.broadcasted_iota(jnp.int32, (V_ROWS - NSA_DH, ts), 0) == 0, 1.0, 0.0).astype(BF16)
    zgl = jax.nn.sigmoid(proj(OD_GL, NSA_KV * LANES))
    for g in range(NSA_KV):
        def piece(idx):
            off = idx * KV_W + g * NSA_DH
            return zkv[:, off:off + NSA_DH]
        kaug_ref[0, g, :, 0:NSA_DH] = piece(0).astype(BF16)
        kaug_ref[0, g, :, NSA_DH:2 * NSA_DH] = onehot
        vst_ref[0, g, 0, 0:NSA_DH] = piece(1).T.astype(BF16)
        vst_ref[0, g, 0, NSA_DH:V_ROWS] = ones_rows
        kw_ref[0, g] = piece(2).astype(BF16)
        vwt_ref[0, g, 0, 0:NSA_DH] = piece(3).T.astype(BF16)
        vwt_ref[0, g, 0, NSA_DH:V_ROWS] = ones_rows
        kc, vc = piece(4), piece(5)
        kca_ref[0, g] = (kc + pe_ref[0, 0]).astype(BF16)
        kcb_ref[0, g] = (kc + pe_ref[0, 1]).astype(BF16)
        vca_ref[0, g] = (vc + pe_ref[1, 0]).astype(BF16)
        vcb_ref[0, g] = (vc + pe_ref[1, 1]).astype(BF16)
        gatest_ref[0, g] = zgl[:, g * LANES:(g + 1) * LANES].T

    xq = proj(OD_XQ, MEM_WIDTH)
    gm = _silu(proj(OD_GM, MEM_WIDTH))
    for hd, om in enumerate(_mem_attention(xq, mk_ref[0], mv_ref[0])):
        sl = slice(hd * MEM_DH, (hd + 1) * MEM_DH)
        ym_ref[0, :, sl] = (om * gm[:, sl]).astype(BF16)


def _odd_in(h, memkv, layer, g, w_in, cmp_pe, ts=256):
    b, s, d = h.shape
    sizes = [NSA_HEADS * NSA_DH] + [KV_W] * 6 + [3 * NSA_HEADS, MEM_WIDTH]
    q, kc, vc, ks, vs, kw, vw, gl, xq, gate = jnp.split(w_in, np.cumsum(sizes).tolist(), axis=1)
    gl = gl.reshape(d, 3, NSA_KV, NSA_HPG).transpose(0, 2, 1, 3).reshape(d, NSA_KV, 3 * NSA_HPG)
    gl = jnp.pad(gl, ((0, 0), (0, 0), (0, LANES - 3 * NSA_HPG))).reshape(d, NSA_KV * LANES)
    gn, gm = jnp.split(gate, [NSA_HEADS * NSA_DH], axis=1)
    win = jnp.concatenate([q, ks, vs, kw, vw, kc, vc, gl, xq, gn, gm], axis=1).astype(BF16)
    reps = ts // CMP_STRIDE
    pe = jnp.stack([jnp.stack([jnp.tile(cmp_pe[kv, :CMP_STRIDE], (reps, 1)),
                               jnp.tile(cmp_pe[kv, CMP_STRIDE:], (reps, 1))]) for kv in range(2)])
    kv_blk = 2 * layer
    assert ts == NSA_TQ
    head_t = jax.ShapeDtypeStruct((b, NSA_KV, s, NSA_DH), BF16)
    head_spec = pl.BlockSpec((1, NSA_KV, ts, NSA_DH), lambda i, j: (i, 0, j, 0))
    headt_t = jax.ShapeDtypeStruct((b, NSA_KV, s // ts, V_ROWS, ts), BF16)
    headt_spec = pl.BlockSpec((1, NSA_KV, 1, V_ROWS, ts), lambda i, j: (i, 0, j, 0, 0))
    qlike_spec = pl.BlockSpec((1, NSA_KV, NSA_HPG, ts, NSA_DH), lambda i, j: (i, 0, 0, j, 0))
    return pl.pallas_call(
        _odd_in_kernel,
        out_shape=[jax.ShapeDtypeStruct((b, NSA_KV, NSA_HPG, s, NSA_DH), BF16),
                   jax.ShapeDtypeStruct((b, NSA_KV, s, 2 * NSA_DH), BF16),
                   headt_t, head_t, headt_t, head_t, head_t, head_t, head_t,
                   jax.ShapeDtypeStruct((b, NSA_KV, LANES, s), F32),
                   jax.ShapeDtypeStruct((b, NSA_KV, NSA_HPG, s, NSA_DH), F32),
                   jax.ShapeDtypeStruct((b, s, MEM_WIDTH), BF16)],
        grid=(b, s // ts),
        in_specs=[pl.BlockSpec((1, ts, d), lambda i, j: (i, j, 0)),
                  _const_spec((1, d)),
                  _const_spec((d, OD_COLS)),
                  _const_spec((2, 2, ts, NSA_DH)),
                  pl.BlockSpec((1, N_MEM, MEM_WIDTH), lambda i, j: (i, 0, kv_blk)),
                  pl.BlockSpec((1, N_MEM, MEM_WIDTH), lambda i, j: (i, 0, kv_blk + 1))],
        out_specs=[qlike_spec,
                   pl.BlockSpec((1, NSA_KV, ts, 2 * NSA_DH), lambda i, j: (i, 0, j, 0)),
                   headt_spec, head_spec, headt_spec, head_spec, head_spec, head_spec, head_spec,
                   pl.BlockSpec((1, NSA_KV, LANES, ts), lambda i, j: (i, 0, 0, j)),
                   qlike_spec,
                   pl.BlockSpec((1, ts, MEM_WIDTH), lambda i, j: (i, j, 0))],
        compiler_params=_params(("arbitrary", "arbitrary")),
        name="odd_in_proj",
    )(h, g.reshape(1, d), win, pe, memkv, memkv)


def _compress_kernel(xka_ref, xkb_ref, xva_ref, xvb_ref, w1a_ref, w1b_ref, b1_ref, w2_ref, ovt_ref, kc_ref, vct_ref):
    n = xka_ref.shape[2]
    k_cols = vct_ref.shape[3]

    def block_mlp(kv, xa, xb):
        first = _mm(xa[0, 0], w1a_ref[kv])
        second = _mm(xb[0, 0], w1b_ref[kv])
        hid = first + pltpu.roll(second, n - 1, 0) + b1_ref[kv]
        return _mm(_silu(hid).astype(BF16), w2_ref[kv])

    kc_ref[0, 0, 0:CMP_PAD] = jnp.zeros((CMP_PAD, NSA_DH), BF16)
    kc_ref[0, 0, CMP_PAD:] = block_mlp(0, xka_ref, xkb_ref).astype(BF16)
    vc = jnp.concatenate([jnp.zeros((CMP_PAD, NSA_DH), F32), block_mlp(1, xva_ref, xvb_ref),
                          jnp.zeros((k_cols - CMP_PAD - n, NSA_DH), F32)], axis=0)
    vct_ref[0, 0, 0:NSA_DH] = vc.T.astype(BF16)
    vct_ref[0, 0, NSA_DH:] = ovt_ref[...]


def _compress(kca, kcb, vca, vcb, w1, b1, w2):
    b, g, s, dh = kca.shape
    n = s // CMP_STRIDE
    half = CMP_STRIDE * dh
    xs = [x.reshape(b, g, n, half) for x in (kca, kcb, vca, vcb)]
    w1 = w1.astype(BF16)
    ovt = _overlap_table(s)
    k_cols = ovt.shape[1]
    x_spec = pl.BlockSpec((1, 1, n, half), lambda i, j: (i, j, 0, 0))
    return pl.pallas_call(
        _compress_kernel,
        out_shape=[jax.ShapeDtypeStruct((b, g, n + CMP_PAD, dh), BF16),
                   jax.ShapeDtypeStruct((b, g, dh + LANES, k_cols), BF16)],
        grid=(b, g),
        in_specs=[x_spec, x_spec, x_spec, x_spec,
                  _const_spec((2, half, CMP_HIDDEN)), _const_spec((2, half, CMP_HIDDEN)),
                  _const_spec((2, 1, CMP_HIDDEN)), _const_spec((2, CMP_HIDDEN, dh)),
                  _const_spec((LANES, k_cols))],
        out_specs=[pl.BlockSpec((1, 1, n + CMP_PAD, dh), lambda i, j: (i, j, 0, 0)),
                   pl.BlockSpec((1, 1, dh + LANES, k_cols), lambda i, j: (i, j, 0, 0))],
        compiler_params=_params(("arbitrary", "arbitrary")),
        name="compress",
    )(*xs, w1[:, :half], w1[:, half:], b1.reshape(2, 1, CMP_HIDDEN), w2.astype(BF16), ovt)


def _nsa_kernel(q_ref, kaug_ref, vst_ref, kw_ref, vwt_ref, kcmp_ref, vcmpt_ref, gatest_ref, sg_ref,
                d0_ref, d1_ref, dw_ref, gc_ref, y_ref, qaug_ref, m_ref, acc_ref, s_ref, sc_ref, mw_ref, accw_ref, sw_ref,
                mix_ref):
    i = pl.program_id(2)
    tq = NSA_TQ
    rows = NSA_HPG * tq
    s_len = kw_ref.shape[2]
    n_cmp = s_len // CMP_STRIDE
    n_slc = s_len // SLC_BLOCK
    t0 = i * tq
    q = q_ref[0, 0].reshape(rows, NSA_DH)


    def stream(mx_ref, ac_ref, sc2_ref):
        def reset():
            mx_ref[...] = jnp.full(mx_ref.shape, LOWEST, F32)
            ac_ref[...] = jnp.zeros(ac_ref.shape, F32)

        def s_to(slot, k_ref, query, tile, bias=None):
            k0 = pl.multiple_of(tile * tq, tq)
            st = _mm_nt(k_ref[0, 0, pl.ds(k0, tq)], query)
            sc2_ref[slot] = st if bias is None else st + bias

        def pv_from(slot, vt):
            st = sc2_ref[slot]
            m_prev = mx_ref[...]
            m_new = jnp.maximum(m_prev, jnp.max(st, axis=0, keepdims=True))
            pt = jnp.exp2(st - m_new).astype(BF16)
            ac_ref[...] = jnp.exp2(m_prev - m_new) * ac_ref[...] + _mm(vt, pt)
            mx_ref[...] = m_new

        def finish(cols=slice(None)):
            return ac_ref[0:NSA_DH, cols] / ac_ref[NSA_DH:NSA_DH + 1, cols]

        return reset, s_to, pv_from, finish

    def off_unless(cond):
        return jnp.where(cond, 0.0, NEG)

    prev1 = jnp.maximum(i - 1, 0)
    prev2 = jnp.maximum(i - 2, 0)
    w_reset, w_s_to, w_pv_from, w_finish = stream(mw_ref, accw_ref, sw_ref)


    n_pad = kcmp_ref.shape[2]
    j_near = pl.multiple_of(i * (tq // CMP_STRIDE), tq // CMP_STRIDE)
    sc_ref[...] = _mm_nt(kcmp_ref[0, 0], q)
    w_reset()
    w_s_to(0, kw_ref, q, prev2, dw_ref[0, 0] + off_unless(i >= 2))
    w_s_to(1, kw_ref, q, prev1, d1_ref[0, 0] + off_unless(i >= 1))
    sc_ref[pl.ds(j_near, CMP_NEAR)] = sc_ref[pl.ds(j_near, CMP_NEAR)] + gc_ref[0, 0, 0:CMP_NEAR, :]
    s_cmp = sc_ref[...]
    krow = lax.broadcasted_iota(jnp.int32, s_cmp.shape, 0)
    s_cmp = jnp.where((krow >= CMP_PAD) & (krow < j_near + CMP_NEAR), s_cmp, NEG)
    p_cmp = jnp.exp2(s_cmp - jnp.max(s_cmp, axis=0, keepdims=True))
    tcol = t0 + jnp.bitwise_and(lax.broadcasted_iota(jnp.int32, (1, rows), 1), tq - 1)
    scale = jnp.where(tcol >= CMP_BLOCK - 1, 1.0, 0.0) / jnp.sum(p_cmp, axis=0, keepdims=True)
    p_cmp = (p_cmp * scale).astype(BF16)
    k_cols = vcmpt_ref.shape[3]
    if k_cols > n_pad:
        p_cmp = jnp.concatenate([p_cmp, jnp.zeros((k_cols - n_pad, rows), BF16)], axis=0)
    w_pv_from(0, vwt_ref[0, 0, prev2])
    both = _mm(vcmpt_ref[0, 0], p_cmp)
    w_s_to(0, kw_ref, q, i, d0_ref[0, 0])
    w_pv_from(1, vwt_ref[0, 0, prev1])
    gtst = gatest_ref[0, 0]
    for hh in range(NSA_HPG):
        hs = slice(hh * tq, (hh + 1) * tq)
        mix_ref[:, hs] = gtst[hh:hh + 1] * both[0:NSA_DH, hs]
    n_blk = -(-n_slc // 8) * 8
    imp = both[NSA_DH:NSA_DH + n_blk, 0:tq]
    for hh in range(1, NSA_HPG):
        imp = imp + both[NSA_DH:NSA_DH + n_blk, hh * tq:(hh + 1) * tq]

    blk = lax.broadcasted_iota(jnp.int32, (n_blk, tq), 0)
    tq_pos = t0 + lax.broadcasted_iota(jnp.int32, (n_blk, tq), 1)
    cur = tq_pos // SLC_BLOCK
    forced = (blk == 0) | (blk == cur) | (blk == cur - 1)
    future = blk * SLC_BLOCK > tq_pos
    imp = jnp.where(forced, BIG, jnp.where(future, -BIG, imp))
    imp = jnp.where(blk < n_slc, imp, LOWEST)
    blk_f = blk.astype(F32)
    sel = jnp.zeros((n_blk, tq), jnp.bool_)
    for _ in range(SLC_TOPK):
        top = jnp.max(imp, axis=0, keepdims=True)
        first = jnp.min(jnp.where(imp == top, blk_f, float(LANES)), axis=0, keepdims=True)
        hit = blk_f == first
        sel = sel | hit
        imp = jnp.where(hit, KNOCKED, imp)
    sneg_t = jnp.where(sel, 0.0, NEG)
    if n_blk < LANES:
        sneg_t = jnp.concatenate([sneg_t, jnp.zeros((LANES - n_blk, tq), F32)], axis=0)
    sneg = sneg_t.T.astype(BF16)
    qaug_ref[:, 0:NSA_DH] = q
    for hh in range(NSA_HPG):
        qaug_ref[hh * tq:(hh + 1) * tq, NSA_DH:2 * NSA_DH] = sneg
    w_pv_from(0, vwt_ref[0, 0, i])
    for hh in range(NSA_HPG):
        hs = slice(hh * tq, (hh + 1) * tq)
        mix_ref[:, hs] = mix_ref[:, hs] + gtst[2 * NSA_HPG + hh:2 * NSA_HPG + hh + 1] * w_finish(hs)

    reset, s_to, pv_from, finish = stream(m_ref, acc_ref, s_ref)
    reset()
    n_far = prev1
    n_pairs = jnp.maximum(n_far - 1, 0) // 2
    s_to(0, kaug_ref, qaug_ref[...], 0, off_unless(n_far >= 1))

    def far_pair(t2, carry):
        s_to(1, kaug_ref, qaug_ref[...], 2 * t2 + 1)
        pv_from(0, vst_ref[0, 0, 2 * t2])
        s_to(0, kaug_ref, qaug_ref[...], 2 * t2 + 2)
        pv_from(1, vst_ref[0, 0, 2 * t2 + 1])
        return carry

    lax.fori_loop(0, n_pairs, far_pair, 0)
    c0 = 2 * n_pairs
    two_left = n_far - c0 == 2

    @pl.when(two_left)
    def _():
        s_to(1, kaug_ref, qaug_ref[...], c0 + 1)
        pv_from(0, vst_ref[0, 0, c0])
        s_to(0, kaug_ref, qaug_ref[...], prev1, d1_ref[0, 0])
        pv_from(1, vst_ref[0, 0, c0 + 1])
        s_to(1, kaug_ref, qaug_ref[...], i, d0_ref[0, 0])
        pv_from(0, vst_ref[0, 0, prev1])
        pv_from(1, vst_ref[0, 0, i])

    @pl.when(jnp.logical_not(two_left))
    def _():
        s_to(1, kaug_ref, qaug_ref[...], prev1, d1_ref[0, 0] + off_unless(i >= 1))
        pv_from(0, vst_ref[0, 0, c0])
        s_to(0, kaug_ref, qaug_ref[...], i, d0_ref[0, 0])
        pv_from(1, vst_ref[0, 0, prev1])
        pv_from(0, vst_ref[0, 0, i])

    for hh in range(NSA_HPG):
        hs = slice(hh * tq, (hh + 1) * tq)
        ot = mix_ref[:, hs] + gatest_ref[0, 0, NSA_HPG + hh:NSA_HPG + hh + 1] * finish(hs)
        y_ref[0, :, hh * NSA_DH:(hh + 1) * NSA_DH] = (ot.T * sg_ref[0, 0, hh]).astype(BF16)


def _overlap_table(s):
    n_cmp = (s - CMP_BLOCK) // CMP_STRIDE + 1
    n_slc = s // SLC_BLOCK
    cst = np.arange(n_cmp)[:, None] * CMP_STRIDE
    sst = np.arange(n_slc)[None, :] * SLC_BLOCK
    ov = np.clip(np.minimum(cst + CMP_BLOCK, sst + SLC_BLOCK) - np.maximum(cst, sst), 0, None) / CMP_STRIDE
    k_cols = -(-(CMP_PAD + s // CMP_STRIDE) // LANES) * LANES
    full = np.zeros((LANES, k_cols), np.float32)
    full[:n_slc, CMP_PAD:CMP_PAD + n_cmp] = ov.T
    return jnp.asarray(full, BF16)


def _nsa(q, kaug, vst, kw, vwt, kcmp, vcmpt, gatest, sg, bias):
    b, g, hpg, s, dh = q.shape
    tq = NSA_TQ
    assert WINDOW == 2 * tq and s % tq == 0 and s // SLC_BLOCK <= LANES and s // SLC_BLOCK >= SLC_TOPK
    n_pad = kcmp.shape[2]
    rows = hpg * tq
    seq_spec = pl.BlockSpec((1, 1, s, dh), lambda i, j, k: (i, j, 0, 0))
    seqt_spec = pl.BlockSpec((1, 1, s // tq, V_ROWS, tq), lambda i, j, k: (i, j, 0, 0, 0))
    qlike_spec = pl.BlockSpec((1, 1, hpg, tq, dh), lambda i, j, k: (i, j, 0, k, 0))

    def bias_spec(idx):
        return pl.BlockSpec((1, 1, tq, rows), lambda i, j, k: (j, idx, 0, 0))

    return pl.pallas_call(
        _nsa_kernel,
        out_shape=jax.ShapeDtypeStruct((b, s, g * hpg * dh), BF16),
        grid=(b, g, s // tq),
        in_specs=[qlike_spec,
                  pl.BlockSpec((1, 1, s, 2 * dh), lambda i, j, k: (i, j, 0, 0)),
                  seqt_spec, seq_spec, seqt_spec,
                  pl.BlockSpec((1, 1, n_pad, dh), lambda i, j, k: (i, j, 0, 0)),
                  pl.BlockSpec((1, 1, dh + LANES, vcmpt.shape[3]), lambda i, j, k: (i, j, 0, 0)),
                  pl.BlockSpec((1, 1, LANES, tq), lambda i, j, k: (i, j, 0, k)),
                  qlike_spec,
                  bias_spec(0), bias_spec(1), bias_spec(2), bias_spec(3)],
        out_specs=pl.BlockSpec((1, tq, hpg * dh), lambda i, j, k: (i, k, j)),
        scratch_shapes=[pltpu.VMEM((rows, 2 * dh), BF16),
                        pltpu.VMEM((1, rows), F32),
                        pltpu.VMEM((V_ROWS, rows), F32),
                        pltpu.VMEM((2, tq, rows), F32),
                        pltpu.VMEM((n_pad, rows), F32),
                        pltpu.VMEM((1, rows), F32),
                        pltpu.VMEM((V_ROWS, rows), F32),
                        pltpu.VMEM((2, tq, rows), F32),
                        pltpu.VMEM((dh, rows), F32)],
        compiler_params=_params(("arbitrary", "arbitrary", "arbitrary")),
        name="nsa_attention",
    )(q, kaug, vst, kw, vwt, kcmp, vcmpt, gatest, sg, bias, bias, bias, bias)


def _odd_out_kernel(h_ref, yn_ref, ym_ref, wn_ref, wm_ref, fg_ref, o_ref, *, final_norm):
    out = h_ref[...] + _mm(yn_ref[...], wn_ref[...]) + _mm(ym_ref[...], wm_ref[...])
    o_ref[...] = _rms(out, fg_ref[...]) if final_norm else out


def _odd_out(h, yn, ym, w_out, final_g, final_norm, tm=512):
    b, s, d = h.shape
    t = b * s
    nw = NSA_HEADS * NSA_DH
    wout = w_out.astype(BF16)
    out = pl.pallas_call(
        functools.partial(_odd_out_kernel, final_norm=final_norm),
        out_shape=jax.ShapeDtypeStruct((t, d), F32),
        grid=(t // tm,),
        in_specs=[pl.BlockSpec((tm, d), lambda i: (i, 0)),
                  pl.BlockSpec((tm, nw), lambda i: (i, 0)),
                  pl.BlockSpec((tm, MEM_WIDTH), lambda i: (i, 0)),
                  _const_spec((nw, d)), _const_spec((MEM_WIDTH, d)), _const_spec((1, d))],
        out_specs=pl.BlockSpec((tm, d), lambda i: (i, 0)),
        compiler_params=_params(("arbitrary",)),
        name="odd_out_proj",
    )(h.reshape(t, d), yn.reshape(t, nw), ym.reshape(t, MEM_WIDTH), wout[:nw], wout[nw:], final_g.reshape(1, d))
    return out.reshape(b, s, d)


def _final_norm_kernel(h_ref, g_ref, o_ref):
    o_ref[...] = _rms(h_ref[...], g_ref[...])


def _final_norm(h, final_g, tm=512):
    b, s, d = h.shape
    t = b * s
    out = pl.pallas_call(
        _final_norm_kernel,
        out_shape=jax.ShapeDtypeStruct((t, d), F32),
        grid=(t // tm,),
        in_specs=[pl.BlockSpec((tm, d), lambda i: (i, 0)), _const_spec((1, d))],
        out_specs=pl.BlockSpec((tm, d), lambda i: (i, 0)),
        compiler_params=_params(("arbitrary",)),
        name="final_norm",
    )(h.reshape(t, d), final_g.reshape(1, d))
    return out.reshape(b, s, d)


def kernel(x, mem, norm_g, final_g, mem_norm_g, rel_bias, ev_w_in, ev_pool_w, ev_pool_scale, ev_w_mem_kv, ev_w_out,
           od_w_in, od_cmp_pe, od_cmp_w1, od_cmp_b1, od_cmp_w2, od_w_mem_kv, od_w_out):
    depth = norm_g.shape[0]
    w_mem = [(ev_w_mem_kv if i % 2 == 0 else od_w_mem_kv)[i // 2] for i in range(depth)]
    memkv = _memkv(mem, mem_norm_g, jnp.concatenate(w_mem, axis=1).astype(BF16))
    bias = _bias_tiles(rel_bias) if depth > 1 else None
    h = x
    for i in range(depth):
        j = i // 2
        last = i == depth - 1
        if i % 2 == 0:
            h = _even_layer(h, memkv, i, norm_g[i], ev_w_in[j], ev_pool_w[j], ev_pool_scale[j], ev_w_out[j])
            if last:
                h = _final_norm(h, final_g)
        else:
            (q, kaug, vst, kw, vwt, kca, kcb, vca, vcb, gatest, sg, ym) = _odd_in(
                h, memkv, i, norm_g[i], od_w_in[j], od_cmp_pe[j])
            kcmp, vcmp = _compress(kca, kcb, vca, vcb, od_cmp_w1[j], od_cmp_b1[j], od_cmp_w2[j])
            yn = _nsa(q, kaug, vst, kw, vwt, kcmp, vcmp, gatest, sg, bias)
            h = _odd_out(h, yn, ym, od_w_out[j], final_g, last)
    return h
```

```python
import functools
import math

import numpy as np
import jax
import jax.numpy as jnp
from jax import lax
from jax.experimental import pallas as pl
from jax.experimental.pallas import tpu as pltpu

F32 = jnp.float32
BF16 = jnp.bfloat16

D_MODEL = 1024
D_INNER = 2048
N_MEM = 256
EPS = 1e-6
NEG = -1e30
BIG = 1e30

POOL_WINDOWS = (2, 4, 8, 16)
POOL_WIDTH = 768
POOL_GROUP = 192
POOL_HALO = 16

RET_HEADS = 4
RET_DK = 128
RET_DV = 192
RET_DV_PAD = 256
RET_CHUNK = 128
ROPE_BASE = 10000.0

MEM_HEADS = 4
MEM_DH = 128
MEM_WIDTH = 512

NSA_HEADS = 12
NSA_KV = 2
NSA_HPG = 6
NSA_DH = 128
CMP_BLOCK = 32
CMP_STRIDE = 16
CMP_HIDDEN = 256
SLC_BLOCK = 64
SLC_TOPK = 8
WINDOW = 512
REL_BUCKETS = 32
REL_MAX_DIST = 128

LANES = 128
NSA_TQ = 256
CMP_NEAR = 32
CMP_PAD = 16
LOG2E = math.log2(math.e)
V_ROWS = NSA_DH + 16
LOWEST = -3.0e38
KNOCKED = -3.3e38

VMEM_LIMIT = 56 * 1024 * 1024


def _mm(a, b):
    return jnp.dot(a, b, preferred_element_type=F32)


def _mm_nt(a, b):
    return lax.dot_general(a, b, (((1,), (1,)), ((), ())), preferred_element_type=F32)


def _mm_tn(a, b):
    return lax.dot_general(a, b, (((0,), (0,)), ((), ())), preferred_element_type=F32)


def _rms(x, g):
    return x * lax.rsqrt(jnp.mean(x * x, axis=-1, keepdims=True) + EPS) * g


def _silu(x):
    return x * jax.nn.sigmoid(x)


def _const_spec(shape):
    nd = len(shape)
    return pl.BlockSpec(shape, lambda *_: (0,) * nd, pipeline_mode=pl.Buffered(1))


def _params(sem):
    return pltpu.CompilerParams(dimension_semantics=sem, vmem_limit_bytes=VMEM_LIMIT)


def _bias_kernel(tab_ref, rel_ref, out_ref):
    h = pl.program_id(0)
    rel = rel_ref[0]
    n = jnp.maximum(rel, 0)
    max_exact = REL_BUCKETS // 2
    nf = jnp.maximum(n, 1).astype(F32)
    large = max_exact + (jnp.log(nf / max_exact) / math.log(REL_MAX_DIST / max_exact)
                         * (REL_BUCKETS - max_exact)).astype(jnp.int32)
    large = jnp.minimum(large, REL_BUCKETS - 1)
    bucket = jnp.where(n < max_exact, n, large)
    far = tab_ref[REL_BUCKETS - 1, h]
    val = jnp.zeros(rel.shape, F32)
    for b in range(REL_BUCKETS - 1):
        val = jnp.where(bucket == b, tab_ref[b, h] - far, val)
    out_ref[0, 0] = jnp.where(rel < 0, NEG, val * LOG2E)


def _bias_tiles(rel_bias):
    tq = NSA_TQ
    r = np.arange(tq)[:, None]
    c = np.arange(tq)[None, :]
    d0 = r - c
    d1 = tq + r - c
    dw = np.where(c > r, WINDOW + r - c, -1)
    gc = np.where(c < CMP_NEAR, r - CMP_STRIDE * c + (CMP_STRIDE * CMP_PAD - (CMP_BLOCK - 1)), -1)
    rel = jnp.asarray(np.stack([d0.T, d1.T, dw.T, gc.T]).astype(np.int32))
    nt = rel.shape[0]
    return pl.pallas_call(
        _bias_kernel,
        out_shape=jax.ShapeDtypeStruct((NSA_KV, nt, tq, NSA_HPG * tq), F32),
        grid=(NSA_HEADS, nt),
        in_specs=[pl.BlockSpec(memory_space=pltpu.SMEM),
                  pl.BlockSpec((1, tq, tq), lambda h, k: (k, 0, 0))],
        out_specs=pl.BlockSpec((1, 1, tq, tq), lambda h, k: (h // NSA_HPG, k, 0, h % NSA_HPG)),
        compiler_params=_params(("arbitrary", "arbitrary")),
        name="bias_tiles",
    )(rel_bias.astype(F32), rel)


def _memkv_kernel(mem_ref, g_ref, w_ref, out_ref):
    y = _rms(mem_ref[0], g_ref[...]).astype(BF16)
    out_ref[0] = _mm(y, w_ref[...]).astype(BF16)


def _memkv(mem, mem_norm_g, w_all):
    b, m, d = mem.shape
    n = w_all.shape[1]
    return pl.pallas_call(
        _memkv_kernel,
        out_shape=jax.ShapeDtypeStruct((b, m, n), BF16),
        grid=(b,),
        in_specs=[pl.BlockSpec((1, m, d), lambda i: (i, 0, 0)),
                  _const_spec((1, d)),
                  _const_spec((d, n))],
        out_specs=pl.BlockSpec((1, m, n), lambda i: (i, 0, 0)),
        compiler_params=_params(("arbitrary",)),
        name="mem_kv",
    )(mem, mem_norm_g.reshape(1, d), w_all)


def _mem_attention(xq, mk, mv):
    outs = []
    for hd in range(MEM_HEADS):
        sl = slice(hd * MEM_DH, (hd + 1) * MEM_DH)
        qm = (xq[:, sl] * (MEM_DH ** -0.5)).astype(BF16)
        s = _mm_nt(qm, mk[:, sl])
        p = jnp.exp(s - jnp.max(s, axis=-1, keepdims=True))
        l = jnp.sum(p, axis=-1, keepdims=True)
        outs.append(_mm(p.astype(BF16), mv[:, sl]) / l)
    return outs


EV_ZA = 0
EV_RQ = EV_ZA + POOL_WIDTH
EV_RK = EV_RQ + RET_HEADS * RET_DK
EV_RV = EV_RK + RET_HEADS * RET_DK
EV_XQ = EV_RV + RET_HEADS * RET_DV_PAD
EV_GA = EV_XQ + MEM_WIDTH
EV_GR = EV_GA + POOL_WIDTH
EV_GM = EV_GR + RET_HEADS * RET_DV_PAD
EV_COLS = EV_GM + MEM_WIDTH
EV_YA = 0
EV_YR = POOL_WIDTH
EV_YM = EV_YR + RET_HEADS * RET_DV_PAD
EV_YCOLS = EV_YM + MEM_WIDTH


def _even_kernel(gch_ref, h_ref, g_ref, win_ref, wbd_ref, pscale_ref, cos_ref, sin_ref, decay_ref, xi_ref,
                 zeta_ref, mk_ref, mv_ref, wout_ref, o_ref, ext_ref, state_ref, y_ref):
    si = pl.program_id(1)
    ts = h_ref.shape[1]

    @pl.when(si == 0)
    def _():
        ext_ref[0:POOL_HALO, :] = jnp.zeros((POOL_HALO, POOL_WIDTH), F32)
        state_ref[...] = jnp.zeros(state_ref.shape, F32)

    h = h_ref[0]
    u = _rms(h, g_ref[...]).astype(BF16)

    def proj(start, width):
        return _mm(u, win_ref[:, start:start + width])

    ext_ref[POOL_HALO:, :] = proj(EV_ZA, POOL_WIDTH)
    e = ext_ref[...]
    s2 = e + pltpu.roll(e, 1, 0)
    s4 = s2 + pltpu.roll(s2, 2, 0)
    s8 = s4 + pltpu.roll(s4, 4, 0)
    s16 = s8 + pltpu.roll(s8, 8, 0)
    lane = lax.broadcasted_iota(jnp.int32, e.shape, 1)
    row = lax.broadcasted_iota(jnp.int32, e.shape, 0)
    tpos = si * ts + row - POOL_HALO
    g0, g1, g2 = lane < POOL_GROUP, lane < 2 * POOL_GROUP, lane < 3 * POOL_GROUP
    wsum = jnp.where(g0, s2, jnp.where(g1, s4, jnp.where(g2, s8, s16)))
    wlen = jnp.where(g0, POOL_WINDOWS[0], jnp.where(g1, POOL_WINDOWS[1],
                                                    jnp.where(g2, POOL_WINDOWS[2], POOL_WINDOWS[3])))
    cnt = jnp.maximum(jnp.minimum(tpos + 1, wlen), 1).astype(F32)
    pooled = (wsum / cnt - e)[POOL_HALO:]
    ext_ref[0:POOL_HALO, :] = e[ts:ts + POOL_HALO]
    a = _mm(pooled.astype(BF16), wbd_ref[...]) * pscale_ref[...]
    y_ref[:, EV_YA:EV_YA + POOL_WIDTH] = (a * _silu(proj(EV_GA, POOL_WIDTH))).astype(BF16)

    cos = cos_ref[...]
    sin = sin_ref[...]
    vlane = lax.broadcasted_iota(jnp.int32, (RET_CHUNK, RET_DV_PAD), 1) < RET_DV
    for hd in range(RET_HEADS):
        qh = proj(EV_RQ + hd * RET_DK, RET_DK)
        kh = proj(EV_RK + hd * RET_DK, RET_DK)
        q_rot = (qh * cos + pltpu.roll(qh, RET_DK // 2, 1) * sin) * (RET_DK ** -0.5)
        k_rot = kh * cos + pltpu.roll(kh, RET_DK // 2, 1) * sin
        vb = proj(EV_RV + hd * RET_DV_PAD, RET_DV_PAD).astype(BF16)
        gate_r = _silu(proj(EV_GR + hd * RET_DV_PAD, RET_DV_PAD))
        for c in range(ts // RET_CHUNK):
            rows = slice(c * RET_CHUNK, (c + 1) * RET_CHUNK)
            qc, kc, vc = q_rot[rows], k_rot[rows], vb[rows]
            att = _mm_nt(qc.astype(BF16), kc.astype(BF16)) * decay_ref[hd]
            state = state_ref[hd]
            o = _mm(att.astype(BF16), vc) + _mm((qc * xi_ref[hd]).astype(BF16), state.astype(BF16))
            kv = _mm_tn((kc * zeta_ref[hd]).astype(BF16), vc)
            state_ref[hd] = state * gch_ref[hd] + kv
            mu = jnp.sum(o, axis=-1, keepdims=True) * (1.0 / RET_DV)
            dlt = jnp.where(vlane, o - mu, 0.0)
            var = jnp.sum(dlt * dlt, axis=-1, keepdims=True) * (1.0 / RET_DV)
            on = dlt * lax.rsqrt(var + EPS)
            col = EV_YR + hd * RET_DV_PAD
            y_ref[rows, col:col + RET_DV_PAD] = (on * gate_r[rows]).astype(BF16)

    xq = proj(EV_XQ, MEM_WIDTH)
    gm = _silu(proj(EV_GM, MEM_WIDTH))
    for hd, om in enumerate(_mem_attention(xq, mk_ref[0], mv_ref[0])):
        sl = slice(hd * MEM_DH, (hd + 1) * MEM_DH)
        y_ref[:, EV_YM + hd * MEM_DH:EV_YM + (hd + 1) * MEM_DH] = (om * gm[:, sl]).astype(BF16)

    o_ref[0] = h + _mm(y_ref[...], wout_ref[...])


def _pad_heads(w, heads, width, padded, axis):
    shp = w.shape
    w = w.reshape(shp[:axis] + (heads, width) + shp[axis + 1:])
    pad = [(0, 0)] * w.ndim
    pad[axis + 1] = (0, padded - width)
    w = jnp.pad(w, pad)
    return w.reshape(shp[:axis] + (heads * padded,) + shp[axis + 1:])


def _retention_tables(s):
    half = RET_DK // 2
    inv = ROPE_BASE ** (-jnp.arange(half, dtype=F32) / half)
    ang = jnp.arange(s, dtype=F32)[:, None] * inv[None, :]
    cos, sin = jnp.cos(ang), jnp.sin(ang)
    cos_t = jnp.concatenate([cos, cos], axis=-1)
    sin_t = jnp.concatenate([-sin, sin], axis=-1)
    c = RET_CHUNK
    log_g = jnp.log(1.0 - jnp.exp2(-5.0 - jnp.arange(RET_HEADS, dtype=F32)))
    n = jnp.arange(c, dtype=F32)
    diff = n[:, None] - n[None, :]
    decay = jnp.where(diff >= 0, jnp.exp(log_g[:, None, None] * jnp.maximum(diff, 0.0)), 0.0)
    xi = jnp.exp(log_g[:, None] * (n + 1.0))
    zeta = jnp.exp(log_g[:, None] * (c - 1.0 - n))
    g_chunk = jnp.exp(log_g * c)
    xi_t = jnp.broadcast_to(xi[:, :, None], (RET_HEADS, c, RET_DK))
    zeta_t = jnp.broadcast_to(zeta[:, :, None], (RET_HEADS, c, RET_DK))
    return cos_t, sin_t, decay, xi_t, zeta_t, g_chunk


def _even_layer(h, memkv, layer, g, w_in, pool_w, pool_scale, w_out, ts=256):
    b, s, d = h.shape
    za, rq, rk, rv, xq, gate = jnp.split(w_in, np.cumsum(
        [POOL_WIDTH, RET_HEADS * RET_DK, RET_HEADS * RET_DK, RET_HEADS * RET_DV, MEM_WIDTH])[:].tolist(), axis=1)
    ga, gr, gm = jnp.split(gate, [POOL_WIDTH, POOL_WIDTH + RET_HEADS * RET_DV], axis=1)
    win = jnp.concatenate([za, rq, rk, _pad_heads(rv, RET_HEADS, RET_DV, RET_DV_PAD, 1), xq, ga,
                           _pad_heads(gr, RET_HEADS, RET_DV, RET_DV_PAD, 1), gm], axis=1).astype(BF16)
    oa, orr, om = jnp.split(w_out, [POOL_WIDTH, POOL_WIDTH + RET_HEADS * RET_DV], axis=0)
    wout = jnp.concatenate([oa, _pad_heads(orr, RET_HEADS, RET_DV, RET_DV_PAD, 0), om], axis=0).astype(BF16)
    wbd = jnp.zeros((POOL_WIDTH, POOL_WIDTH), F32)
    for gi in range(len(POOL_WINDOWS)):
        sl = slice(gi * POOL_GROUP, (gi + 1) * POOL_GROUP)
        wbd = wbd.at[sl, sl].set(pool_w[gi])
    wbd = wbd.astype(BF16)
    cos_t, sin_t, decay, xi_t, zeta_t, g_chunk = _retention_tables(s)
    kv_blk = 2 * layer
    return pl.pallas_call(
        _even_kernel,
        out_shape=jax.ShapeDtypeStruct((b, s, d), F32),
        grid=(b, s // ts),
        in_specs=[pl.BlockSpec(memory_space=pltpu.SMEM),
                  pl.BlockSpec((1, ts, d), lambda i, j: (i, j, 0)),
                  _const_spec((1, d)),
                  _const_spec((d, EV_COLS)),
                  _const_spec((POOL_WIDTH, POOL_WIDTH)),
                  _const_spec((1, POOL_WIDTH)),
                  pl.BlockSpec((ts, RET_DK), lambda i, j: (j, 0)),
                  pl.BlockSpec((ts, RET_DK), lambda i, j: (j, 0)),
                  _const_spec((RET_HEADS, RET_CHUNK, RET_CHUNK)),
                  _const_spec((RET_HEADS, RET_CHUNK, RET_DK)),
                  _const_spec((RET_HEADS, RET_CHUNK, RET_DK)),
                  pl.BlockSpec((1, N_MEM, MEM_WIDTH), lambda i, j: (i, 0, kv_blk)),
                  pl.BlockSpec((1, N_MEM, MEM_WIDTH), lambda i, j: (i, 0, kv_blk + 1)),
                  _const_spec((EV_YCOLS, d))],
        out_specs=pl.BlockSpec((1, ts, d), lambda i, j: (i, j, 0)),
        scratch_shapes=[pltpu.VMEM((POOL_HALO + ts, POOL_WIDTH), F32),
                        pltpu.VMEM((RET_HEADS, RET_DK, RET_DV_PAD), F32),
                        pltpu.VMEM((ts, EV_YCOLS), BF16)],
        compiler_params=_params(("arbitrary", "arbitrary")),
        name="even_layer",
    )(g_chunk, h, g.reshape(1, d), win, wbd, pool_scale.reshape(1, POOL_WIDTH), cos_t, sin_t, decay, xi_t, zeta_t,
      memkv, memkv, wout)


KV_W = NSA_KV * NSA_DH
OD_Q = 0
OD_KS = OD_Q + NSA_HEADS * NSA_DH
OD_VS = OD_KS + KV_W
OD_KW = OD_VS + KV_W
OD_VW = OD_KW + KV_W
OD_KC = OD_VW + KV_W
OD_VC = OD_KC + KV_W
OD_GL = OD_VC + KV_W
OD_XQ = OD_GL + NSA_KV * LANES
OD_GN = OD_XQ + MEM_WIDTH
OD_GM = OD_GN + NSA_HEADS * NSA_DH
OD_COLS = OD_GM + MEM_WIDTH


def _odd_in_kernel(h_ref, g_ref, win_ref, pe_ref, perm_ref, mk_ref, mv_ref,
                   q_ref, kaug_ref, vst_ref, kw_ref, vwt_ref, kca_ref, kcb_ref, vca_ref, vcb_ref, gatest_ref,
                   sg_ref, ym_ref):
    si = pl.program_id(1)
    ts = h_ref.shape[1]
    u = _rms(h_ref[0], g_ref[...]).astype(BF16)

    def proj(start, width):
        return _mm(u, win_ref[:, start:start + width])

    zq = proj(OD_Q, NSA_HEADS * NSA_DH) * (NSA_DH ** -0.5 * LOG2E)
    zg = _silu(proj(OD_GN, NSA_HEADS * NSA_DH))
    for g in range(NSA_KV):
        for hh in range(NSA_HPG):
            sl = slice((g * NSA_HPG + hh) * NSA_DH, (g * NSA_HPG + hh + 1) * NSA_DH)
            q_ref[0, g, hh] = zq[:, sl].astype(BF16)
            sg_ref[0, g, hh] = zg[:, sl]

    zkv = proj(OD_KS, 6 * KV_W)
    lane = lax.broadcasted_iota(jnp.int32, (ts, LANES), 1)
    blk = (si * ts + lax.broadcasted_iota(jnp.int32, (ts, LANES), 0)) // SLC_BLOCK
    onehot = jnp.where(lane == blk, 1.0, 0.0).astype(BF16)
    ones_rows = jnp.where(lax.broadcasted_iota(jnp.int32, (V_ROWS - NSA_DH, ts), 0) == 0, 1.0, 0.0).astype(BF16)
    zgl = jax.nn.sigmoid(proj(OD_GL, NSA_KV * LANES))
    flat_in = []
    for g in range(NSA_KV):
        def piece(idx):
            off = idx * KV_W + g * NSA_DH
            return zkv[:, off:off + NSA_DH]
        kaug_ref[0, g, :, 0:NSA_DH] = piece(0).astype(BF16)
        kaug_ref[0, g, :, NSA_DH:2 * NSA_DH] = onehot
        vst_ref[0, g, 0, 0:NSA_DH] = piece(1).T.astype(BF16)
        vst_ref[0, g, 0, NSA_DH:V_ROWS] = ones_rows
        kw_ref[0, g] = piece(2).astype(BF16)
        vwt_ref[0, g, 0, 0:NSA_DH] = piece(3).T.astype(BF16)
        vwt_ref[0, g, 0, NSA_DH:V_ROWS] = ones_rows
        kc, vc = piece(4), piece(5)
        flat_in += [(kc + pe_ref[0, 0]).astype(BF16), (kc + pe_ref[0, 1]).astype(BF16),
                    (vc + pe_ref[1, 0]).astype(BF16), (vc + pe_ref[1, 1]).astype(BF16)]
        gatest_ref[0, g] = zgl[:, g * LANES:(g + 1) * LANES].T

    nj = ts // CMP_STRIDE
    perm = _mm(perm_ref[...], jnp.concatenate(flat_in, axis=1)).astype(BF16)
    for c, out_ref in enumerate((kca_ref, kcb_ref, vca_ref, vcb_ref) * NSA_KV):
        for l in range(CMP_STRIDE):
            out_ref[0, c // 4, :, l * NSA_DH:(l + 1) * NSA_DH] = perm[l * nj:(l + 1) * nj, c * NSA_DH:(c + 1) * NSA_DH]

    xq = proj(OD_XQ, MEM_WIDTH)
    gm = _silu(proj(OD_GM, MEM_WIDTH))
    for hd, om in enumerate(_mem_attention(xq, mk_ref[0], mv_ref[0])):
        sl = slice(hd * MEM_DH, (hd + 1) * MEM_DH)
        ym_ref[0, :, sl] = (om * gm[:, sl]).astype(BF16)


def _odd_in(h, memkv, layer, g, w_in, cmp_pe, ts=256):
    b, s, d = h.shape
    sizes = [NSA_HEADS * NSA_DH] + [KV_W] * 6 + [3 * NSA_HEADS, MEM_WIDTH]
    q, kc, vc, ks, vs, kw, vw, gl, xq, gate = jnp.split(w_in, np.cumsum(sizes).tolist(), axis=1)
    gl = gl.reshape(d, 3, NSA_KV, NSA_HPG).transpose(0, 2, 1, 3).reshape(d, NSA_KV, 3 * NSA_HPG)
    gl = jnp.pad(gl, ((0, 0), (0, 0), (0, LANES - 3 * NSA_HPG))).reshape(d, NSA_KV * LANES)
    gn, gm = jnp.split(gate, [NSA_HEADS * NSA_DH], axis=1)
    win = jnp.concatenate([w.astype(BF16) for w in (q, ks, vs, kw, vw, kc, vc, gl, xq, gn, gm)], axis=1)
    reps = ts // CMP_STRIDE
    pe = jnp.stack([jnp.stack([jnp.tile(cmp_pe[kv, :CMP_STRIDE], (reps, 1)),
                               jnp.tile(cmp_pe[kv, CMP_STRIDE:], (reps, 1))]) for kv in range(2)])
    kv_blk = 2 * layer
    assert ts == NSA_TQ
    nj = ts // CMP_STRIDE
    perm = np.zeros((ts, ts), np.float32)
    perm[np.arange(ts), (np.arange(ts) % nj) * CMP_STRIDE + np.arange(ts) // nj] = 1.0
    head_t = jax.ShapeDtypeStruct((b, NSA_KV, s, NSA_DH), BF16)
    head_spec = pl.BlockSpec((1, NSA_KV, ts, NSA_DH), lambda i, j: (i, 0, j, 0))
    headt_t = jax.ShapeDtypeStruct((b, NSA_KV, s // ts, V_ROWS, ts), BF16)
    headt_spec = pl.BlockSpec((1, NSA_KV, 1, V_ROWS, ts), lambda i, j: (i, 0, j, 0, 0))
    flat_t = jax.ShapeDtypeStruct((b, NSA_KV, s // CMP_STRIDE, CMP_STRIDE * NSA_DH), BF16)
    flat_spec = pl.BlockSpec((1, NSA_KV, ts // CMP_STRIDE, CMP_STRIDE * NSA_DH), lambda i, j: (i, 0, j, 0))
    qlike_spec = pl.BlockSpec((1, NSA_KV, NSA_HPG, ts, NSA_DH), lambda i, j: (i, 0, 0, j, 0))
    return pl.pallas_call(
        _odd_in_kernel,
        out_shape=[jax.ShapeDtypeStruct((b, NSA_KV, NSA_HPG, s, NSA_DH), BF16),
                   jax.ShapeDtypeStruct((b, NSA_KV, s, 2 * NSA_DH), BF16),
                   headt_t, head_t, headt_t, flat_t, flat_t, flat_t, flat_t,
                   jax.ShapeDtypeStruct((b, NSA_KV, LANES, s), F32),
                   jax.ShapeDtypeStruct((b, NSA_KV, NSA_HPG, s, NSA_DH), F32),
                   jax.ShapeDtypeStruct((b, s, MEM_WIDTH), BF16)],
        grid=(b, s // ts),
        in_specs=[pl.BlockSpec((1, ts, d), lambda i, j: (i, j, 0)),
                  _const_spec((1, d)),
                  _const_spec((d, OD_COLS)),
                  _const_spec((2, 2, ts, NSA_DH)),
                  _const_spec((ts, ts)),
                  pl.BlockSpec((1, N_MEM, MEM_WIDTH), lambda i, j: (i, 0, kv_blk)),
                  pl.BlockSpec((1, N_MEM, MEM_WIDTH), lambda i, j: (i, 0, kv_blk + 1))],
        out_specs=[qlike_spec,
                   pl.BlockSpec((1, NSA_KV, ts, 2 * NSA_DH), lambda i, j: (i, 0, j, 0)),
                   headt_spec, head_spec, headt_spec, flat_spec, flat_spec, flat_spec, flat_spec,
                   pl.BlockSpec((1, NSA_KV, LANES, ts), lambda i, j: (i, 0, 0, j)),
                   qlike_spec,
                   pl.BlockSpec((1, ts, MEM_WIDTH), lambda i, j: (i, j, 0))],
        compiler_params=_params(("arbitrary", "arbitrary")),
        name="odd_in_proj",
    )(h, g.reshape(1, d), win, pe, jnp.asarray(perm, BF16), memkv, memkv)


def _compress_kernel(xka_ref, xkb_ref, xva_ref, xvb_ref, w1a_ref, w1b_ref, b1_ref, w2_ref, ovt_ref, kc_ref, vct_ref):
    n = xka_ref.shape[2]
    k_cols = vct_ref.shape[3]

    def block_mlp(kv, xa, xb):
        first = _mm(xa[0, 0], w1a_ref[kv])
        second = _mm(xb[0, 0], w1b_ref[kv])
        hid = first + pltpu.roll(second, n - 1, 0) + b1_ref[kv]
        return _mm(_silu(hid).astype(BF16), w2_ref[kv])

    kc_ref[0, 0, 0:CMP_PAD] = jnp.zeros((CMP_PAD, NSA_DH), BF16)
    kc_ref[0, 0, CMP_PAD:] = block_mlp(0, xka_ref, xkb_ref).astype(BF16)
    vc = jnp.concatenate([jnp.zeros((CMP_PAD, NSA_DH), F32), block_mlp(1, xva_ref, xvb_ref),
                          jnp.zeros((k_cols - CMP_PAD - n, NSA_DH), F32)], axis=0)
    vct_ref[0, 0, 0:NSA_DH] = vc.T.astype(BF16)
    vct_ref[0, 0, NSA_DH:] = ovt_ref[...]


def _compress(kca, kcb, vca, vcb, w1, b1, w2):
    b, g, n, half = kca.shape
    dh = half // CMP_STRIDE
    s = n * CMP_STRIDE
    xs = (kca, kcb, vca, vcb)
    w1 = w1.astype(BF16)
    ovt = _overlap_table(s)
    k_cols = ovt.shape[1]
    x_spec = pl.BlockSpec((1, 1, n, half), lambda i, j: (i, j, 0, 0))
    return pl.pallas_call(
        _compress_kernel,
        out_shape=[jax.ShapeDtypeStruct((b, g, n + CMP_PAD, dh), BF16),
                   jax.ShapeDtypeStruct((b, g, dh + LANES, k_cols), BF16)],
        grid=(b, g),
        in_specs=[x_spec, x_spec, x_spec, x_spec,
                  _const_spec((2, half, CMP_HIDDEN)), _const_spec((2, half, CMP_HIDDEN)),
                  _const_spec((2, 1, CMP_HIDDEN)), _const_spec((2, CMP_HIDDEN, dh)),
                  _const_spec((LANES, k_cols))],
        out_specs=[pl.BlockSpec((1, 1, n + CMP_PAD, dh), lambda i, j: (i, j, 0, 0)),
                   pl.BlockSpec((1, 1, dh + LANES, k_cols), lambda i, j: (i, j, 0, 0))],
        compiler_params=_params(("arbitrary", "arbitrary")),
        name="compress",
    )(*xs, w1[:, :half], w1[:, half:], b1.reshape(2, 1, CMP_HIDDEN), w2.astype(BF16), ovt)


def _nsa_kernel(q_ref, kaug_ref, vst_ref, kw_ref, vwt_ref, kcmp_ref, vcmpt_ref, gatest_ref, sg_ref,
                d0_ref, d1_ref, dw_ref, gc_ref, y_ref, qaug_ref, m_ref, acc_ref, s_ref, sc_ref, mw_ref, accw_ref, sw_ref,
                mix_ref):
    i = pl.program_id(2)
    tq = NSA_TQ
    rows = NSA_HPG * tq
    s_len = kw_ref.shape[2]
    n_cmp = s_len // CMP_STRIDE
    n_slc = s_len // SLC_BLOCK
    t0 = i * tq
    q = q_ref[0, 0].reshape(rows, NSA_DH)


    def stream(mx_ref, ac_ref, sc2_ref):
        def reset():
            mx_ref[...] = jnp.full(mx_ref.shape, LOWEST, F32)
            ac_ref[...] = jnp.zeros(ac_ref.shape, F32)

        def s_to(slot, k_ref, query, tile, bias=None):
            k0 = pl.multiple_of(tile * tq, tq)
            st = _mm_nt(k_ref[0, 0, pl.ds(k0, tq)], query)
            sc2_ref[slot] = st if bias is None else st + bias

        def pv_from(slot, vt):
            st = sc2_ref[slot]
            m_prev = mx_ref[...]
            m_new = jnp.maximum(m_prev, jnp.max(st, axis=0, keepdims=True))
            pt = jnp.exp2(st - m_new).astype(BF16)
            ac_ref[...] = jnp.exp2(m_prev - m_new) * ac_ref[...] + _mm(vt, pt)
            mx_ref[...] = m_new

        def finish(cols=slice(None)):
            return ac_ref[0:NSA_DH, cols] / ac_ref[NSA_DH:NSA_DH + 1, cols]

        return reset, s_to, pv_from, finish

    def off_unless(cond):
        return jnp.where(cond, 0.0, NEG)

    prev1 = jnp.maximum(i - 1, 0)
    prev2 = jnp.maximum(i - 2, 0)
    w_reset, w_s_to, w_pv_from, w_finish = stream(mw_ref, accw_ref, sw_ref)


    n_pad = kcmp_ref.shape[2]
    j_near = pl.multiple_of(i * (tq // CMP_STRIDE), tq // CMP_STRIDE)
    sc_ref[...] = _mm_nt(kcmp_ref[0, 0], q)
    w_reset()
    w_s_to(0, kw_ref, q, prev2, dw_ref[0, 0] + off_unless(i >= 2))
    w_s_to(1, kw_ref, q, prev1, d1_ref[0, 0] + off_unless(i >= 1))
    sc_ref[pl.ds(j_near, CMP_NEAR)] = sc_ref[pl.ds(j_near, CMP_NEAR)] + gc_ref[0, 0, 0:CMP_NEAR, :]
    s_cmp = sc_ref[...]
    krow = lax.broadcasted_iota(jnp.int32, s_cmp.shape, 0)
    s_cmp = jnp.where((krow >= CMP_PAD) & (krow < j_near + CMP_NEAR), s_cmp, NEG)
    p_cmp = jnp.exp2(s_cmp - jnp.max(s_cmp, axis=0, keepdims=True))
    tcol = t0 + jnp.bitwise_and(lax.broadcasted_iota(jnp.int32, (1, rows), 1), tq - 1)
    scale = jnp.where(tcol >= CMP_BLOCK - 1, 1.0, 0.0) / jnp.sum(p_cmp, axis=0, keepdims=True)
    p_cmp = p_cmp * scale
    k_cols = vcmpt_ref.shape[3]

    def key_pad(x):
        return jnp.concatenate([x, jnp.zeros((k_cols - n_pad, x.shape[1]), x.dtype)], axis=0) if k_cols > n_pad else x

    p_sum = p_cmp[:, 0:tq]
    for hh in range(1, NSA_HPG):
        p_sum = p_sum + p_cmp[:, hh * tq:(hh + 1) * tq]
    p_hi = p_sum.astype(BF16)
    p_lo = (p_sum - p_hi.astype(F32)).astype(BF16)
    n_blk = -(-n_slc // 8) * 8
    imp = (_mm(vcmpt_ref[0, 0, NSA_DH:], key_pad(p_hi)) + _mm(vcmpt_ref[0, 0, NSA_DH:], key_pad(p_lo)))[0:n_blk]
    w_pv_from(0, vwt_ref[0, 0, prev2])
    ot_cmp = _mm(vcmpt_ref[0, 0, 0:NSA_DH], key_pad(p_cmp.astype(BF16)))
    w_s_to(0, kw_ref, q, i, d0_ref[0, 0])
    w_pv_from(1, vwt_ref[0, 0, prev1])
    gtst = gatest_ref[0, 0]
    for hh in range(NSA_HPG):
        hs = slice(hh * tq, (hh + 1) * tq)
        mix_ref[:, hs] = gtst[hh:hh + 1] * ot_cmp[:, hs]

    blk = lax.broadcasted_iota(jnp.int32, (n_blk, tq), 0)
    tq_pos = t0 + lax.broadcasted_iota(jnp.int32, (n_blk, tq), 1)
    cur = tq_pos // SLC_BLOCK
    forced = (blk == 0) | (blk == cur) | (blk == cur - 1)
    future = blk * SLC_BLOCK > tq_pos
    imp = jnp.where(forced, BIG, jnp.where(future, -BIG, imp))
    imp = jnp.where(blk < n_slc, imp, LOWEST)
    blk_f = blk.astype(F32)
    sel = jnp.zeros((n_blk, tq), jnp.bool_)
    for _ in range(SLC_TOPK):
        top = jnp.max(imp, axis=0, keepdims=True)
        first = jnp.min(jnp.where(imp == top, blk_f, float(LANES)), axis=0, keepdims=True)
        hit = blk_f == first
        sel = sel | hit
        imp = jnp.where(hit, KNOCKED, imp)
    sneg_t = jnp.where(sel, 0.0, NEG)
    if n_blk < LANES:
        sneg_t = jnp.concatenate([sneg_t, jnp.zeros((LANES - n_blk, tq), F32)], axis=0)
    sneg = sneg_t.T.astype(BF16)
    qaug_ref[:, 0:NSA_DH] = q
    for hh in range(NSA_HPG):
        qaug_ref[hh * tq:(hh + 1) * tq, NSA_DH:2 * NSA_DH] = sneg
    w_pv_from(0, vwt_ref[0, 0, i])
    for hh in range(NSA_HPG):
        hs = slice(hh * tq, (hh + 1) * tq)
        mix_ref[:, hs] = mix_ref[:, hs] + gtst[2 * NSA_HPG + hh:2 * NSA_HPG + hh + 1] * w_finish(hs)

    reset, s_to, pv_from, finish = stream(m_ref, acc_ref, s_ref)
    reset()
    n_far = prev1
    n_pairs = jnp.maximum(n_far - 1, 0) // 2
    s_to(0, kaug_ref, qaug_ref[...], 0, off_unless(n_far >= 1))

    def far_pair(t2, carry):
        s_to(1, kaug_ref, qaug_ref[...], 2 * t2 + 1)
        pv_from(0, vst_ref[0, 0, 2 * t2])
        s_to(0, kaug_ref, qaug_ref[...], 2 * t2 + 2)
        pv_from(1, vst_ref[0, 0, 2 * t2 + 1])
        return carry

    lax.fori_loop(0, n_pairs, far_pair, 0)
    c0 = 2 * n_pairs
    two_left = n_far - c0 == 2

    @pl.when(two_left)
    def _():
        s_to(1, kaug_ref, qaug_ref[...], c0 + 1)
        pv_from(0, vst_ref[0, 0, c0])
        s_to(0, kaug_ref, qaug_ref[...], prev1, d1_ref[0, 0])
        pv_from(1, vst_ref[0, 0, c0 + 1])
        s_to(1, kaug_ref, qaug_ref[...], i, d0_ref[0, 0])
        pv_from(0, vst_ref[0, 0, prev1])
        pv_from(1, vst_ref[0, 0, i])

    @pl.when(jnp.logical_not(two_left))
    def _():
        s_to(1, kaug_ref, qaug_ref[...], prev1, d1_ref[0, 0] + off_unless(i >= 1))
        pv_from(0, vst_ref[0, 0, c0])
        s_to(0, kaug_ref, qaug_ref[...], i, d0_ref[0, 0])
        pv_from(1, vst_ref[0, 0, prev1])
        pv_from(0, vst_ref[0, 0, i])

    for hh in range(NSA_HPG):
        hs = slice(hh * tq, (hh + 1) * tq)
        ot = mix_ref[:, hs] + gatest_ref[0, 0, NSA_HPG + hh:NSA_HPG + hh + 1] * finish(hs)
        y_ref[0, :, hh * NSA_DH:(hh + 1) * NSA_DH] = (ot.T * sg_ref[0, 0, hh]).astype(BF16)


def _overlap_table(s):
    n_cmp = (s - CMP_BLOCK) // CMP_STRIDE + 1
    n_slc = s // SLC_BLOCK
    cst = np.arange(n_cmp)[:, None] * CMP_STRIDE
    sst = np.arange(n_slc)[None, :] * SLC_BLOCK
    ov = np.clip(np.minimum(cst + CMP_BLOCK, sst + SLC_BLOCK) - np.maximum(cst, sst), 0, None) / CMP_STRIDE
    k_cols = -(-(CMP_PAD + s // CMP_STRIDE) // LANES) * LANES
    full = np.zeros((LANES, k_cols), np.float32)
    full[:n_slc, CMP_PAD:CMP_PAD + n_cmp] = ov.T
    return jnp.asarray(full, BF16)


def _nsa(q, kaug, vst, kw, vwt, kcmp, vcmpt, gatest, sg, bias):
    b, g, hpg, s, dh = q.shape
    tq = NSA_TQ
    assert WINDOW == 2 * tq and s % tq == 0 and s // SLC_BLOCK <= LANES and s // SLC_BLOCK >= SLC_TOPK
    n_pad = kcmp.shape[2]
    rows = hpg * tq
    seq_spec = pl.BlockSpec((1, 1, s, dh), lambda i, j, k: (i, j, 0, 0))
    seqt_spec = pl.BlockSpec((1, 1, s // tq, V_ROWS, tq), lambda i, j, k: (i, j, 0, 0, 0))
    qlike_spec = pl.BlockSpec((1, 1, hpg, tq, dh), lambda i, j, k: (i, j, 0, k, 0))

    def bias_spec(idx):
        return pl.BlockSpec((1, 1, tq, rows), lambda i, j, k: (j, idx, 0, 0))

    return pl.pallas_call(
        _nsa_kernel,
        out_shape=jax.ShapeDtypeStruct((b, s, g * hpg * dh), BF16),
        grid=(b, g, s // tq),
        in_specs=[qlike_spec,
                  pl.BlockSpec((1, 1, s, 2 * dh), lambda i, j, k: (i, j, 0, 0)),
                  seqt_spec, seq_spec, seqt_spec,
                  pl.BlockSpec((1, 1, n_pad, dh), lambda i, j, k: (i, j, 0, 0)),
                  pl.BlockSpec((1, 1, dh + LANES, vcmpt.shape[3]), lambda i, j, k: (i, j, 0, 0)),
                  pl.BlockSpec((1, 1, LANES, tq), lambda i, j, k: (i, j, 0, k)),
                  qlike_spec,
                  bias_spec(0), bias_spec(1), bias_spec(2), bias_spec(3)],
        out_specs=pl.BlockSpec((1, tq, hpg * dh), lambda i, j, k: (i, k, j)),
        scratch_shapes=[pltpu.VMEM((rows, 2 * dh), BF16),
                        pltpu.VMEM((1, rows), F32),
                        pltpu.VMEM((V_ROWS, rows), F32),
                        pltpu.VMEM((2, tq, rows), F32),
                        pltpu.VMEM((n_pad, rows), F32),
                        pltpu.VMEM((1, rows), F32),
                        pltpu.VMEM((V_ROWS, rows), F32),
                        pltpu.VMEM((2, tq, rows), F32),
                        pltpu.VMEM((dh, rows), F32)],
        compiler_params=_params(("arbitrary", "arbitrary", "arbitrary")),
        name="nsa_attention",
    )(q, kaug, vst, kw, vwt, kcmp, vcmpt, gatest, sg, bias, bias, bias, bias)


def _odd_out_kernel(h_ref, yn_ref, ym_ref, wn_ref, wm_ref, fg_ref, o_ref, *, final_norm):
    out = h_ref[...] + _mm(yn_ref[...], wn_ref[...]) + _mm(ym_ref[...], wm_ref[...])
    o_ref[...] = _rms(out, fg_ref[...]) if final_norm else out


def _odd_out(h, yn, ym, w_out, final_g, final_norm, tm=512):
    b, s, d = h.shape
    t = b * s
    nw = NSA_HEADS * NSA_DH
    wout = w_out.astype(BF16)
    out = pl.pallas_call(
        functools.partial(_odd_out_kernel, final_norm=final_norm),
        out_shape=jax.ShapeDtypeStruct((t, d), F32),
        grid=(t // tm,),
        in_specs=[pl.BlockSpec((tm, d), lambda i: (i, 0)),
                  pl.BlockSpec((tm, nw), lambda i: (i, 0)),
                  pl.BlockSpec((tm, MEM_WIDTH), lambda i: (i, 0)),
                  _const_spec((nw, d)), _const_spec((MEM_WIDTH, d)), _const_spec((1, d))],
        out_specs=pl.BlockSpec((tm, d), lambda i: (i, 0)),
        compiler_params=_params(("arbitrary",)),
        name="odd_out_proj",
    )(h.reshape(t, d), yn.reshape(t, nw), ym.reshape(t, MEM_WIDTH), wout[:nw], wout[nw:], final_g.reshape(1, d))
    return out.reshape(b, s, d)


def _final_norm_kernel(h_ref, g_ref, o_ref):
    o_ref[...] = _rms(h_ref[...], g_ref[...])


def _final_norm(h, final_g, tm=512):
    b, s, d = h.shape
    t = b * s
    out = pl.pallas_call(
        _final_norm_kernel,
        out_shape=jax.ShapeDtypeStruct((t, d), F32),
        grid=(t // tm,),
        in_specs=[pl.BlockSpec((tm, d), lambda i: (i, 0)), _const_spec((1, d))],
        out_specs=pl.BlockSpec((tm, d), lambda i: (i, 0)),
        compiler_params=_params(("arbitrary",)),
        name="final_norm",
    )(h.reshape(t, d), final_g.reshape(1, d))
    return out.reshape(b, s, d)


def kernel(x, mem, norm_g, final_g, mem_norm_g, rel_bias, ev_w_in, ev_pool_w, ev_pool_scale, ev_w_mem_kv, ev_w_out,
           od_w_in, od_cmp_pe, od_cmp_w1, od_cmp_b1, od_cmp_w2, od_w_mem_kv, od_w_out):
    depth = norm_g.shape[0]
    w_mem = [(ev_w_mem_kv if i % 2 == 0 else od_w_mem_kv)[i // 2] for i in range(depth)]
    memkv = _memkv(mem, mem_norm_g, jnp.concatenate(w_mem, axis=1).astype(BF16))
    bias = _bias_tiles(rel_bias) if depth > 1 else None
    h = x
    for i in range(depth):
        j = i // 2
        last = i == depth - 1
        if i % 2 == 0:
            h = _even_layer(h, memkv, i, norm_g[i], ev_w_in[j], ev_pool_w[j], ev_pool_scale[j], ev_w_out[j])
            if last:
                h = _final_norm(h, final_g)
        else:
            (q, kaug, vst, kw, vwt, kca, kcb, vca, vcb, gatest, sg, ym) = _odd_in(
                h, memkv, i, norm_g[i], od_w_in[j], od_cmp_pe[j])
            kcmp, vcmp = _compress(kca, kcb, vca, vcb, od_cmp_w1[j], od_cmp_b1[j], od_cmp_w2[j])
            yn = _nsa(q, kaug, vst, kw, vwt, kcmp, vcmp, gatest, sg, bias)
            h = _odd_out(h, yn, ym, od_w_out[j], final_g, last)
    return h
```

```python
import functools
import math

import numpy as np
import jax
import jax.numpy as jnp
from jax import lax
from jax.experimental import pallas as pl
from jax.experimental.pallas import tpu as pltpu

F32 = jnp.float32
BF16 = jnp.bfloat16

D_MODEL = 1024
D_INNER = 2048
N_MEM = 256
EPS = 1e-6
NEG = -1e30
BIG = 1e30

POOL_WINDOWS = (2, 4, 8, 16)
POOL_WIDTH = 768
POOL_GROUP = 192
POOL_HALO = 16

RET_HEADS = 4
RET_DK = 128
RET_DV = 192
RET_DV_PAD = 256
RET_CHUNK = 128
ROPE_BASE = 10000.0

MEM_HEADS = 4
MEM_DH = 128
MEM_WIDTH = 512

NSA_HEADS = 12
NSA_KV = 2
NSA_HPG = 6
NSA_DH = 128
CMP_BLOCK = 32
CMP_STRIDE = 16
CMP_HIDDEN = 256
SLC_BLOCK = 64
SLC_TOPK = 8
WINDOW = 512
REL_BUCKETS = 32
REL_MAX_DIST = 128

LANES = 128
NSA_TQ = 256
CMP_NEAR = 32
CMP_PAD = 16
LOG2E = math.log2(math.e)
V_ROWS = NSA_DH + 16
LOWEST = -3.0e38
KNOCKED = -3.3e38
BIAS_MASKED = 4

VMEM_LIMIT = 56 * 1024 * 1024


def _mm(a, b):
    return jnp.dot(a, b, preferred_element_type=F32)


def _mm_nt(a, b):
    return lax.dot_general(a, b, (((1,), (1,)), ((), ())), preferred_element_type=F32)


def _mm_tn(a, b):
    return lax.dot_general(a, b, (((0,), (0,)), ((), ())), preferred_element_type=F32)


def _rms(x, g):
    return x * lax.rsqrt(jnp.mean(x * x, axis=-1, keepdims=True) + EPS) * g


def _silu(x):
    return x * jax.nn.sigmoid(x)


def _const_spec(shape):
    nd = len(shape)
    return pl.BlockSpec(shape, lambda *_: (0,) * nd, pipeline_mode=pl.Buffered(1))


def _params(sem):
    return pltpu.CompilerParams(dimension_semantics=sem, vmem_limit_bytes=VMEM_LIMIT)


def _bias_kernel(tab_ref, rel_ref, out_ref):
    h = pl.program_id(0)
    rel = rel_ref[0]
    n = jnp.maximum(rel, 0)
    max_exact = REL_BUCKETS // 2
    nf = jnp.maximum(n, 1).astype(F32)
    large = max_exact + (jnp.log(nf / max_exact) / math.log(REL_MAX_DIST / max_exact)
                         * (REL_BUCKETS - max_exact)).astype(jnp.int32)
    large = jnp.minimum(large, REL_BUCKETS - 1)
    bucket = jnp.where(n < max_exact, n, large)
    far = tab_ref[REL_BUCKETS - 1, h]
    val = jnp.zeros(rel.shape, F32)
    for b in range(REL_BUCKETS - 1):
        val = jnp.where(bucket == b, tab_ref[b, h] - far, val)
    out_ref[0, 0] = jnp.where(rel < 0, NEG, val * LOG2E)


def _bias_tiles(rel_bias):
    tq = NSA_TQ
    r = np.arange(tq)[:, None]
    c = np.arange(tq)[None, :]
    d0 = r - c
    d1 = tq + r - c
    dw = np.where(c > r, WINDOW + r - c, -1)
    gc = np.where(c < CMP_NEAR, r - CMP_STRIDE * c + (CMP_STRIDE * CMP_PAD - (CMP_BLOCK - 1)), -1)
    rel = jnp.asarray(np.stack([d0.T, d1.T, dw.T, gc.T, np.full((tq, tq), -1)]).astype(np.int32))
    nt = rel.shape[0]
    return pl.pallas_call(
        _bias_kernel,
        out_shape=jax.ShapeDtypeStruct((NSA_KV, nt, tq, NSA_HPG * tq), F32),
        grid=(NSA_HEADS, nt),
        in_specs=[pl.BlockSpec(memory_space=pltpu.SMEM),
                  pl.BlockSpec((1, tq, tq), lambda h, k: (k, 0, 0))],
        out_specs=pl.BlockSpec((1, 1, tq, tq), lambda h, k: (h // NSA_HPG, k, 0, h % NSA_HPG)),
        compiler_params=_params(("arbitrary", "arbitrary")),
        name="bias_tiles",
    )(rel_bias.astype(F32), rel)


def _memkv_kernel(mem_ref, g_ref, w_ref, out_ref):
    y = _rms(mem_ref[0], g_ref[...]).astype(BF16)
    out_ref[0] = _mm(y, w_ref[...]).astype(BF16)


def _memkv(mem, mem_norm_g, w_all):
    b, m, d = mem.shape
    n = w_all.shape[1]
    return pl.pallas_call(
        _memkv_kernel,
        out_shape=jax.ShapeDtypeStruct((b, m, n), BF16),
        grid=(b,),
        in_specs=[pl.BlockSpec((1, m, d), lambda i: (i, 0, 0)),
                  _const_spec((1, d)),
                  _const_spec((d, n))],
        out_specs=pl.BlockSpec((1, m, n), lambda i: (i, 0, 0)),
        compiler_params=_params(("arbitrary",)),
        name="mem_kv",
    )(mem, mem_norm_g.reshape(1, d), w_all)


def _mem_attention(xq, mk, mv):
    outs = []
    for hd in range(MEM_HEADS):
        sl = slice(hd * MEM_DH, (hd + 1) * MEM_DH)
        qm = (xq[:, sl] * (MEM_DH ** -0.5)).astype(BF16)
        s = _mm_nt(qm, mk[:, sl])
        p = jnp.exp(s - jnp.max(s, axis=-1, keepdims=True))
        l = jnp.sum(p, axis=-1, keepdims=True)
        outs.append(_mm(p.astype(BF16), mv[:, sl]) / l)
    return outs


EV_ZA = 0
EV_RQ = EV_ZA + POOL_WIDTH
EV_RK = EV_RQ + RET_HEADS * RET_DK
EV_RV = EV_RK + RET_HEADS * RET_DK
EV_XQ = EV_RV + RET_HEADS * RET_DV_PAD
EV_GA = EV_XQ + MEM_WIDTH
EV_GR = EV_GA + POOL_WIDTH
EV_GM = EV_GR + RET_HEADS * RET_DV_PAD
EV_COLS = EV_GM + MEM_WIDTH
EV_YA = 0
EV_YR = POOL_WIDTH
EV_YM = EV_YR + RET_HEADS * RET_DV_PAD
EV_YCOLS = EV_YM + MEM_WIDTH


def _even_kernel(gch_ref, h_ref, g_ref, win_ref, wbd_ref, pscale_ref, cos_ref, sin_ref, decay_ref, xi_ref,
                 zeta_ref, mk_ref, mv_ref, wout_ref, o_ref, ext_ref, state_ref, y_ref):
    si = pl.program_id(1)
    ts = h_ref.shape[1]

    @pl.when(si == 0)
    def _():
        ext_ref[0:POOL_HALO, :] = jnp.zeros((POOL_HALO, POOL_WIDTH), F32)
        state_ref[...] = jnp.zeros(state_ref.shape, F32)

    h = h_ref[0]
    u = _rms(h, g_ref[...]).astype(BF16)

    def proj(start, width):
        return _mm(u, win_ref[:, start:start + width])

    ext_ref[POOL_HALO:, :] = proj(EV_ZA, POOL_WIDTH)
    e = ext_ref[...]
    s2 = e + pltpu.roll(e, 1, 0)
    s4 = s2 + pltpu.roll(s2, 2, 0)
    s8 = s4 + pltpu.roll(s4, 4, 0)
    s16 = s8 + pltpu.roll(s8, 8, 0)
    lane = lax.broadcasted_iota(jnp.int32, e.shape, 1)
    row = lax.broadcasted_iota(jnp.int32, e.shape, 0)
    tpos = si * ts + row - POOL_HALO
    g0, g1, g2 = lane < POOL_GROUP, lane < 2 * POOL_GROUP, lane < 3 * POOL_GROUP
    wsum = jnp.where(g0, s2, jnp.where(g1, s4, jnp.where(g2, s8, s16)))
    wlen = jnp.where(g0, POOL_WINDOWS[0], jnp.where(g1, POOL_WINDOWS[1],
                                                    jnp.where(g2, POOL_WINDOWS[2], POOL_WINDOWS[3])))
    cnt = jnp.maximum(jnp.minimum(tpos + 1, wlen), 1).astype(F32)
    pooled = (wsum / cnt - e)[POOL_HALO:]
    ext_ref[0:POOL_HALO, :] = e[ts:ts + POOL_HALO]
    a = _mm(pooled.astype(BF16), wbd_ref[...]) * pscale_ref[...]
    y_ref[:, EV_YA:EV_YA + POOL_WIDTH] = (a * _silu(proj(EV_GA, POOL_WIDTH))).astype(BF16)

    cos = cos_ref[...]
    sin = sin_ref[...]
    vlane = lax.broadcasted_iota(jnp.int32, (RET_CHUNK, RET_DV_PAD), 1) < RET_DV
    for hd in range(RET_HEADS):
        qh = proj(EV_RQ + hd * RET_DK, RET_DK)
        kh = proj(EV_RK + hd * RET_DK, RET_DK)
        q_rot = (qh * cos + pltpu.roll(qh, RET_DK // 2, 1) * sin) * (RET_DK ** -0.5)
        k_rot = kh * cos + pltpu.roll(kh, RET_DK // 2, 1) * sin
        vb = proj(EV_RV + hd * RET_DV_PAD, RET_DV_PAD).astype(BF16)
        gate_r = _silu(proj(EV_GR + hd * RET_DV_PAD, RET_DV_PAD))
        for c in range(ts // RET_CHUNK):
            rows = slice(c * RET_CHUNK, (c + 1) * RET_CHUNK)
            qc, kc, vc = q_rot[rows], k_rot[rows], vb[rows]
            att = _mm_nt(qc.astype(BF16), kc.astype(BF16)) * decay_ref[hd]
            state = state_ref[hd]
            o = _mm(att.astype(BF16), vc) + _mm((qc * xi_ref[hd]).astype(BF16), state.astype(BF16))
            kv = _mm_tn((kc * zeta_ref[hd]).astype(BF16), vc)
            state_ref[hd] = state * gch_ref[hd] + kv
            mu = jnp.sum(o, axis=-1, keepdims=True) * (1.0 / RET_DV)
            dlt = jnp.where(vlane, o - mu, 0.0)
            var = jnp.sum(dlt * dlt, axis=-1, keepdims=True) * (1.0 / RET_DV)
            on = dlt * lax.rsqrt(var + EPS)
            col = EV_YR + hd * RET_DV_PAD
            y_ref[rows, col:col + RET_DV_PAD] = (on * gate_r[rows]).astype(BF16)

    xq = proj(EV_XQ, MEM_WIDTH)
    gm = _silu(proj(EV_GM, MEM_WIDTH))
    for hd, om in enumerate(_mem_attention(xq, mk_ref[0], mv_ref[0])):
        sl = slice(hd * MEM_DH, (hd + 1) * MEM_DH)
        y_ref[:, EV_YM + hd * MEM_DH:EV_YM + (hd + 1) * MEM_DH] = (om * gm[:, sl]).astype(BF16)

    o_ref[0] = h + _mm(y_ref[...], wout_ref[...])


def _pad_heads(w, heads, width, padded, axis):
    shp = w.shape
    w = w.reshape(shp[:axis] + (heads, width) + shp[axis + 1:])
    pad = [(0, 0)] * w.ndim
    pad[axis + 1] = (0, padded - width)
    w = jnp.pad(w, pad)
    return w.reshape(shp[:axis] + (heads * padded,) + shp[axis + 1:])


def _retention_tables(s):
    half = RET_DK // 2
    inv = ROPE_BASE ** (-jnp.arange(half, dtype=F32) / half)
    ang = jnp.arange(s, dtype=F32)[:, None] * inv[None, :]
    cos, sin = jnp.cos(ang), jnp.sin(ang)
    cos_t = jnp.concatenate([cos, cos], axis=-1)
    sin_t = jnp.concatenate([-sin, sin], axis=-1)
    c = RET_CHUNK
    log_g = jnp.log(1.0 - jnp.exp2(-5.0 - jnp.arange(RET_HEADS, dtype=F32)))
    n = jnp.arange(c, dtype=F32)
    diff = n[:, None] - n[None, :]
    decay = jnp.where(diff >= 0, jnp.exp(log_g[:, None, None] * jnp.maximum(diff, 0.0)), 0.0)
    xi = jnp.exp(log_g[:, None] * (n + 1.0))
    zeta = jnp.exp(log_g[:, None] * (c - 1.0 - n))
    g_chunk = jnp.exp(log_g * c)
    xi_t = jnp.broadcast_to(xi[:, :, None], (RET_HEADS, c, RET_DK))
    zeta_t = jnp.broadcast_to(zeta[:, :, None], (RET_HEADS, c, RET_DK))
    return cos_t, sin_t, decay, xi_t, zeta_t, g_chunk


def _even_layer(h, memkv, layer, g, w_in, pool_w, pool_scale, w_out, ts=256):
    b, s, d = h.shape
    za, rq, rk, rv, xq, gate = jnp.split(w_in, np.cumsum(
        [POOL_WIDTH, RET_HEADS * RET_DK, RET_HEADS * RET_DK, RET_HEADS * RET_DV, MEM_WIDTH])[:].tolist(), axis=1)
    ga, gr, gm = jnp.split(gate, [POOL_WIDTH, POOL_WIDTH + RET_HEADS * RET_DV], axis=1)
    win = jnp.concatenate([za, rq, rk, _pad_heads(rv, RET_HEADS, RET_DV, RET_DV_PAD, 1), xq, ga,
                           _pad_heads(gr, RET_HEADS, RET_DV, RET_DV_PAD, 1), gm], axis=1).astype(BF16)
    oa, orr, om = jnp.split(w_out, [POOL_WIDTH, POOL_WIDTH + RET_HEADS * RET_DV], axis=0)
    wout = jnp.concatenate([oa, _pad_heads(orr, RET_HEADS, RET_DV, RET_DV_PAD, 0), om], axis=0).astype(BF16)
    wbd = jnp.zeros((POOL_WIDTH, POOL_WIDTH), F32)
    for gi in range(len(POOL_WINDOWS)):
        sl = slice(gi * POOL_GROUP, (gi + 1) * POOL_GROUP)
        wbd = wbd.at[sl, sl].set(pool_w[gi])
    wbd = wbd.astype(BF16)
    cos_t, sin_t, decay, xi_t, zeta_t, g_chunk = _retention_tables(s)
    kv_blk = 2 * layer
    return pl.pallas_call(
        _even_kernel,
        out_shape=jax.ShapeDtypeStruct((b, s, d), F32),
        grid=(b, s // ts),
        in_specs=[pl.BlockSpec(memory_space=pltpu.SMEM),
                  pl.BlockSpec((1, ts, d), lambda i, j: (i, j, 0)),
                  _const_spec((1, d)),
                  _const_spec((d, EV_COLS)),
                  _const_spec((POOL_WIDTH, POOL_WIDTH)),
                  _const_spec((1, POOL_WIDTH)),
                  pl.BlockSpec((ts, RET_DK), lambda i, j: (j, 0)),
                  pl.BlockSpec((ts, RET_DK), lambda i, j: (j, 0)),
                  _const_spec((RET_HEADS, RET_CHUNK, RET_CHUNK)),
                  _const_spec((RET_HEADS, RET_CHUNK, RET_DK)),
                  _const_spec((RET_HEADS, RET_CHUNK, RET_DK)),
                  pl.BlockSpec((1, N_MEM, MEM_WIDTH), lambda i, j: (i, 0, kv_blk)),
                  pl.BlockSpec((1, N_MEM, MEM_WIDTH), lambda i, j: (i, 0, kv_blk + 1)),
                  _const_spec((EV_YCOLS, d))],
        out_specs=pl.BlockSpec((1, ts, d), lambda i, j: (i, j, 0)),
        scratch_shapes=[pltpu.VMEM((POOL_HALO + ts, POOL_WIDTH), F32),
                        pltpu.VMEM((RET_HEADS, RET_DK, RET_DV_PAD), F32),
                        pltpu.VMEM((ts, EV_YCOLS), BF16)],
        compiler_params=_params(("arbitrary", "arbitrary")),
        name="even_layer",
    )(g_chunk, h, g.reshape(1, d), win, wbd, pool_scale.reshape(1, POOL_WIDTH), cos_t, sin_t, decay, xi_t, zeta_t,
      memkv, memkv, wout)


KV_W = NSA_KV * NSA_DH
OD_Q = 0
OD_KS = OD_Q + NSA_HEADS * NSA_DH
OD_VS = OD_KS + KV_W
OD_KW = OD_VS + KV_W
OD_VW = OD_KW + KV_W
OD_KC = OD_VW + KV_W
OD_VC = OD_KC + KV_W
OD_GL = OD_VC + KV_W
OD_XQ = OD_GL + NSA_KV * LANES
OD_GN = OD_XQ + MEM_WIDTH
OD_GM = OD_GN + NSA_HEADS * NSA_DH
OD_COLS = OD_GM + MEM_WIDTH


def _odd_in_kernel(h_ref, g_ref, win_ref, pe_ref, perm_ref, mk_ref, mv_ref,
                   q_ref, kaug_ref, vst_ref, kw_ref, vwt_ref, kca_ref, kcb_ref, vca_ref, vcb_ref, gatest_ref,
                   sg_ref, ym_ref):
    si = pl.program_id(1)
    ts = h_ref.shape[1]
    u = _rms(h_ref[0], g_ref[...]).astype(BF16)

    def proj(start, width):
        return _mm(u, win_ref[:, start:start + width])

    zq = proj(OD_Q, NSA_HEADS * NSA_DH) * (NSA_DH ** -0.5 * LOG2E)
    zg = _silu(proj(OD_GN, NSA_HEADS * NSA_DH))
    for g in range(NSA_KV):
        for hh in range(NSA_HPG):
            sl = slice((g * NSA_HPG + hh) * NSA_DH, (g * NSA_HPG + hh + 1) * NSA_DH)
            q_ref[0, g, hh] = zq[:, sl].astype(BF16)
            sg_ref[0, g, hh] = zg[:, sl]

    zkv = proj(OD_KS, 6 * KV_W)
    lane = lax.broadcasted_iota(jnp.int32, (ts, LANES), 1)
    blk = (si * ts + lax.broadcasted_iota(jnp.int32, (ts, LANES), 0)) // SLC_BLOCK
    onehot = jnp.where(lane == blk, 1.0, 0.0).astype(BF16)
    ones_rows = jnp.where(lax.broadcasted_iota(jnp.int32, (V_ROWS - NSA_DH, ts), 0) == 0, 1.0, 0.0).astype(BF16)
    zgl = jax.nn.sigmoid(proj(OD_GL, NSA_KV * LANES))
    flat_in = []
    for g in range(NSA_KV):
        def piece(idx):
            off = idx * KV_W + g * NSA_DH
            return zkv[:, off:off + NSA_DH]
        kaug_ref[0, g, :, 0:NSA_DH] = piece(0).astype(BF16)
        kaug_ref[0, g, :, NSA_DH:2 * NSA_DH] = onehot
        vst_ref[0, g, 0, 0:NSA_DH] = piece(1).T.astype(BF16)
        vst_ref[0, g, 0, NSA_DH:V_ROWS] = ones_rows
        kw_ref[0, g] = piece(2).astype(BF16)
        vwt_ref[0, g, 0, 0:NSA_DH] = piece(3).T.astype(BF16)
        vwt_ref[0, g, 0, NSA_DH:V_ROWS] = ones_rows
        kc, vc = piece(4), piece(5)
        flat_in += [(kc + pe_ref[0, 0]).astype(BF16), (kc + pe_ref[0, 1]).astype(BF16),
                    (vc + pe_ref[1, 0]).astype(BF16), (vc + pe_ref[1, 1]).astype(BF16)]
        gatest_ref[0, g] = zgl[:, g * LANES:(g + 1) * LANES].T

    nj = ts // CMP_STRIDE
    perm = _mm(perm_ref[...], jnp.concatenate(flat_in, axis=1)).astype(BF16)
    for c, out_ref in enumerate((kca_ref, kcb_ref, vca_ref, vcb_ref) * NSA_KV):
        for l in range(CMP_STRIDE):
            out_ref[0, c // 4, :, l * NSA_DH:(l + 1) * NSA_DH] = perm[l * nj:(l + 1) * nj, c * NSA_DH:(c + 1) * NSA_DH]

    xq = proj(OD_XQ, MEM_WIDTH)
    gm = _silu(proj(OD_GM, MEM_WIDTH))
    for hd, om in enumerate(_mem_attention(xq, mk_ref[0], mv_ref[0])):
        sl = slice(hd * MEM_DH, (hd + 1) * MEM_DH)
        ym_ref[0, :, sl] = (om * gm[:, sl]).astype(BF16)


def _odd_in(h, memkv, layer, g, w_in, cmp_pe, ts=256):
    b, s, d = h.shape
    sizes = [NSA_HEADS * NSA_DH] + [KV_W] * 6 + [3 * NSA_HEADS, MEM_WIDTH]
    q, kc, vc, ks, vs, kw, vw, gl, xq, gate = jnp.split(w_in, np.cumsum(sizes).tolist(), axis=1)
    gl = gl.reshape(d, 3, NSA_KV, NSA_HPG).transpose(0, 2, 1, 3).reshape(d, NSA_KV, 3 * NSA_HPG)
    gl = jnp.pad(gl, ((0, 0), (0, 0), (0, LANES - 3 * NSA_HPG))).reshape(d, NSA_KV * LANES)
    gn, gm = jnp.split(gate, [NSA_HEADS * NSA_DH], axis=1)
    win = jnp.concatenate([w.astype(BF16) for w in (q, ks, vs, kw, vw, kc, vc, gl, xq, gn, gm)], axis=1)
    reps = ts // CMP_STRIDE
    pe = jnp.stack([jnp.stack([jnp.tile(cmp_pe[kv, :CMP_STRIDE], (reps, 1)),
                               jnp.tile(cmp_pe[kv, CMP_STRIDE:], (reps, 1))]) for kv in range(2)])
    kv_blk = 2 * layer
    assert ts == NSA_TQ
    nj = ts // CMP_STRIDE
    perm = np.zeros((ts, ts), np.float32)
    perm[np.arange(ts), (np.arange(ts) % nj) * CMP_STRIDE + np.arange(ts) // nj] = 1.0
    head_t = jax.ShapeDtypeStruct((b, NSA_KV, s, NSA_DH), BF16)
    head_spec = pl.BlockSpec((1, NSA_KV, ts, NSA_DH), lambda i, j: (i, 0, j, 0))
    headt_t = jax.ShapeDtypeStruct((b, NSA_KV, s // ts, V_ROWS, ts), BF16)
    headt_spec = pl.BlockSpec((1, NSA_KV, 1, V_ROWS, ts), lambda i, j: (i, 0, j, 0, 0))
    flat_t = jax.ShapeDtypeStruct((b, NSA_KV, s // CMP_STRIDE, CMP_STRIDE * NSA_DH), BF16)
    flat_spec = pl.BlockSpec((1, NSA_KV, ts // CMP_STRIDE, CMP_STRIDE * NSA_DH), lambda i, j: (i, 0, j, 0))
    qlike_spec = pl.BlockSpec((1, NSA_KV, NSA_HPG, ts, NSA_DH), lambda i, j: (i, 0, 0, j, 0))
    return pl.pallas_call(
        _odd_in_kernel,
        out_shape=[jax.ShapeDtypeStruct((b, NSA_KV, NSA_HPG, s, NSA_DH), BF16),
                   jax.ShapeDtypeStruct((b, NSA_KV, s, 2 * NSA_DH), BF16),
                   headt_t, head_t, headt_t, flat_t, flat_t, flat_t, flat_t,
                   jax.ShapeDtypeStruct((b, NSA_KV, LANES, s), F32),
                   jax.ShapeDtypeStruct((b, NSA_KV, NSA_HPG, s, NSA_DH), F32),
                   jax.ShapeDtypeStruct((b, s, MEM_WIDTH), BF16)],
        grid=(b, s // ts),
        in_specs=[pl.BlockSpec((1, ts, d), lambda i, j: (i, j, 0)),
                  _const_spec((1, d)),
                  _const_spec((d, OD_COLS)),
                  _const_spec((2, 2, ts, NSA_DH)),
                  _const_spec((ts, ts)),
                  pl.BlockSpec((1, N_MEM, MEM_WIDTH), lambda i, j: (i, 0, kv_blk)),
                  pl.BlockSpec((1, N_MEM, MEM_WIDTH), lambda i, j: (i, 0, kv_blk + 1))],
        out_specs=[qlike_spec,
                   pl.BlockSpec((1, NSA_KV, ts, 2 * NSA_DH), lambda i, j: (i, 0, j, 0)),
                   headt_spec, head_spec, headt_spec, flat_spec, flat_spec, flat_spec, flat_spec,
                   pl.BlockSpec((1, NSA_KV, LANES, ts), lambda i, j: (i, 0, 0, j)),
                   qlike_spec,
                   pl.BlockSpec((1, ts, MEM_WIDTH), lambda i, j: (i, j, 0))],
        compiler_params=_params(("arbitrary", "arbitrary")),
        name="odd_in_proj",
    )(h, g.reshape(1, d), win, pe, jnp.asarray(perm, BF16), memkv, memkv)


def _compress_kernel(xka_ref, xkb_ref, xva_ref, xvb_ref, w1a_ref, w1b_ref, b1_ref, w2_ref, ovt_ref, kc_ref, vct_ref):
    n = xka_ref.shape[2]
    k_cols = vct_ref.shape[3]

    def block_mlp(kv, xa, xb):
        first = _mm(xa[0, 0], w1a_ref[kv])
        second = _mm(xb[0, 0], w1b_ref[kv])
        hid = first + pltpu.roll(second, n - 1, 0) + b1_ref[kv]
        return _mm(_silu(hid).astype(BF16), w2_ref[kv])

    kc_ref[0, 0, 0:CMP_PAD] = jnp.zeros((CMP_PAD, NSA_DH), BF16)
    kc_ref[0, 0, CMP_PAD:] = block_mlp(0, xka_ref, xkb_ref).astype(BF16)
    vc = jnp.concatenate([jnp.zeros((CMP_PAD, NSA_DH), F32), block_mlp(1, xva_ref, xvb_ref),
                          jnp.zeros((k_cols - CMP_PAD - n, NSA_DH), F32)], axis=0)
    vct_ref[0, 0, 0:NSA_DH] = vc.T.astype(BF16)
    vct_ref[0, 0, NSA_DH:] = ovt_ref[...]


def _compress(kca, kcb, vca, vcb, w1, b1, w2):
    b, g, n, half = kca.shape
    dh = half // CMP_STRIDE
    s = n * CMP_STRIDE
    xs = (kca, kcb, vca, vcb)
    w1 = w1.astype(BF16)
    ovt = _overlap_table(s)
    k_cols = ovt.shape[1]
    x_spec = pl.BlockSpec((1, 1, n, half), lambda i, j: (i, j, 0, 0))
    return pl.pallas_call(
        _compress_kernel,
        out_shape=[jax.ShapeDtypeStruct((b, g, n + CMP_PAD, dh), BF16),
                   jax.ShapeDtypeStruct((b, g, dh + LANES, k_cols), BF16)],
        grid=(b, g),
        in_specs=[x_spec, x_spec, x_spec, x_spec,
                  _const_spec((2, half, CMP_HIDDEN)), _const_spec((2, half, CMP_HIDDEN)),
                  _const_spec((2, 1, CMP_HIDDEN)), _const_spec((2, CMP_HIDDEN, dh)),
                  _const_spec((LANES, k_cols))],
        out_specs=[pl.BlockSpec((1, 1, n + CMP_PAD, dh), lambda i, j: (i, j, 0, 0)),
                   pl.BlockSpec((1, 1, dh + LANES, k_cols), lambda i, j: (i, j, 0, 0))],
        compiler_params=_params(("arbitrary", "arbitrary")),
        name="compress",
    )(*xs, w1[:, :half], w1[:, half:], b1.reshape(2, 1, CMP_HIDDEN), w2.astype(BF16), ovt)


def _nsa_kernel(q_ref, kaug_ref, vst_ref, kw_ref, vwt_ref, kcmp_ref, vcmpt_ref, gatest_ref, sg_ref,
                d0_ref, d1_ref, dw_ref, gc_ref, y_ref, qaug_ref, m_ref, acc_ref, s_ref, sc_ref, mw_ref, accw_ref, sw_ref,
                mix_ref, tmax_ref, tmaxw_ref):
    i = pl.program_id(2)
    tq = NSA_TQ
    rows = NSA_HPG * tq
    s_len = kw_ref.shape[2]
    n_cmp = s_len // CMP_STRIDE
    n_slc = s_len // SLC_BLOCK
    t0 = i * tq
    q = q_ref[0, 0].reshape(rows, NSA_DH)


    def stream(mx_ref, ac_ref, sc2_ref, tmax_ref):
        def reset():
            mx_ref[...] = jnp.full(mx_ref.shape, LOWEST, F32)
            ac_ref[...] = jnp.zeros(ac_ref.shape, F32)

        def s_to(slot, k_ref, query, tile, bias=None):
            k0 = pl.multiple_of(tile * tq, tq)
            st = _mm_nt(k_ref[0, 0, pl.ds(k0, tq)], query)
            if bias is not None:
                st = st + bias
            sc2_ref[slot] = st
            tmax_ref[slot] = jnp.max(st, axis=0, keepdims=True)

        def pv_from(slot, vt):
            st = sc2_ref[slot]
            m_prev = mx_ref[...]
            m_new = jnp.maximum(m_prev, tmax_ref[slot])
            pt = jnp.exp2(st - m_new).astype(BF16)
            ac_ref[...] = jnp.exp2(m_prev - m_new) * ac_ref[...] + _mm(vt, pt)
            mx_ref[...] = m_new

        def finish(cols=slice(None)):
            return ac_ref[0:NSA_DH, cols] / ac_ref[NSA_DH:NSA_DH + 1, cols]

        return reset, s_to, pv_from, finish

    def off_unless(cond):
        return jnp.where(cond, 0.0, NEG)

    prev1 = jnp.maximum(i - 1, 0)
    prev2 = jnp.maximum(i - 2, 0)
    w_reset, w_s_to, w_pv_from, w_finish = stream(mw_ref, accw_ref, sw_ref, tmaxw_ref)


    n_pad = kcmp_ref.shape[2]
    j_near = pl.multiple_of(i * (tq // CMP_STRIDE), tq // CMP_STRIDE)
    qaug_ref[:, 0:NSA_DH] = q
    qaug_ref[:, NSA_DH:] = jnp.where(lax.broadcasted_iota(jnp.int32, (rows, NSA_DH), 1) == 0, 1.0, 0.0).astype(BF16)
    krow = lax.broadcasted_iota(jnp.int32, (n_pad, NSA_DH), 0)
    key_bias = jnp.where((krow >= CMP_PAD) & (krow < j_near + CMP_NEAR), 0.0, NEG).astype(BF16)
    sc_ref[...] = _mm_nt(jnp.concatenate([kcmp_ref[0, 0], key_bias], axis=1), qaug_ref[...])
    w_reset()
    w_s_to(0, kw_ref, q, prev2, dw_ref[0, 0])
    w_s_to(1, kw_ref, q, prev1, d1_ref[0, 0])
    sc_ref[pl.ds(j_near, CMP_NEAR)] = sc_ref[pl.ds(j_near, CMP_NEAR)] + gc_ref[0, 0, 0:CMP_NEAR, :]
    s_cmp = sc_ref[...]
    p_cmp = jnp.exp2(s_cmp - jnp.max(s_cmp, axis=0, keepdims=True))
    tcol = t0 + jnp.bitwise_and(lax.broadcasted_iota(jnp.int32, (1, rows), 1), tq - 1)
    scale = jnp.where(tcol >= CMP_BLOCK - 1, 1.0, 0.0) / jnp.sum(p_cmp, axis=0, keepdims=True)
    p_cmp = p_cmp.astype(BF16)
    k_cols = vcmpt_ref.shape[3]
    if k_cols > n_pad:
        p_cmp = jnp.concatenate([p_cmp, jnp.zeros((k_cols - n_pad, rows), BF16)], axis=0)
    w_pv_from(0, vwt_ref[0, 0, prev2])
    both = _mm(vcmpt_ref[0, 0], p_cmp) * scale
    w_s_to(0, kw_ref, q, i, d0_ref[0, 0])
    w_pv_from(1, vwt_ref[0, 0, prev1])
    gtst = gatest_ref[0, 0]
    n_blk = -(-n_slc // 8) * 8
    imp = both[NSA_DH:NSA_DH + n_blk, 0:tq]
    for hh in range(NSA_HPG):
        hs = slice(hh * tq, (hh + 1) * tq)
        mix_ref[:, hs] = gtst[hh:hh + 1] * both[0:NSA_DH, hs]
        if hh:
            imp = imp + both[NSA_DH:NSA_DH + n_blk, hs]

    blk = lax.broadcasted_iota(jnp.int32, (n_blk, tq), 0)
    tq_pos = t0 + lax.broadcasted_iota(jnp.int32, (n_blk, tq), 1)
    cur = tq_pos // SLC_BLOCK
    forced = (blk == 0) | (blk == cur) | (blk == cur - 1)
    future = blk * SLC_BLOCK > tq_pos
    imp = jnp.where(forced, BIG, jnp.where(future, -BIG, imp))
    imp = jnp.where(blk < n_slc, imp, LOWEST)
    blk_f = blk.astype(F32)
    sel = jnp.zeros((n_blk, tq), jnp.bool_)
    for _ in range(SLC_TOPK):
        top = jnp.max(imp, axis=0, keepdims=True)
        first = jnp.min(jnp.where(imp == top, blk_f, float(LANES)), axis=0, keepdims=True)
        hit = blk_f == first
        sel = sel | hit
        imp = jnp.where(hit, KNOCKED, imp)
    sneg_t = jnp.where(sel, 0.0, NEG)
    if n_blk < LANES:
        sneg_t = jnp.concatenate([sneg_t, jnp.zeros((LANES - n_blk, tq), F32)], axis=0)
    sneg = sneg_t.T.astype(BF16)
    for hh in range(NSA_HPG):
        qaug_ref[hh * tq:(hh + 1) * tq, NSA_DH:2 * NSA_DH] = sneg
    w_pv_from(0, vwt_ref[0, 0, i])
    for hh in range(NSA_HPG):
        hs = slice(hh * tq, (hh + 1) * tq)
        mix_ref[:, hs] = mix_ref[:, hs] + gtst[2 * NSA_HPG + hh:2 * NSA_HPG + hh + 1] * w_finish(hs)

    reset, s_to, pv_from, finish = stream(m_ref, acc_ref, s_ref, tmax_ref)
    reset()
    n_far = prev1
    n_pairs = jnp.maximum(n_far - 1, 0) // 2
    s_to(0, kaug_ref, qaug_ref[...], 0, off_unless(n_far >= 1))

    def far_pair(t2, carry):
        s_to(1, kaug_ref, qaug_ref[...], 2 * t2 + 1)
        pv_from(0, vst_ref[0, 0, 2 * t2])
        s_to(0, kaug_ref, qaug_ref[...], 2 * t2 + 2)
        pv_from(1, vst_ref[0, 0, 2 * t2 + 1])
        return carry

    lax.fori_loop(0, n_pairs, far_pair, 0)
    c0 = 2 * n_pairs
    two_left = n_far - c0 == 2

    @pl.when(two_left)
    def _():
        s_to(1, kaug_ref, qaug_ref[...], c0 + 1)
        pv_from(0, vst_ref[0, 0, c0])
        s_to(0, kaug_ref, qaug_ref[...], prev1, d1_ref[0, 0])
        pv_from(1, vst_ref[0, 0, c0 + 1])
        s_to(1, kaug_ref, qaug_ref[...], i, d0_ref[0, 0])
        pv_from(0, vst_ref[0, 0, prev1])
        pv_from(1, vst_ref[0, 0, i])

    @pl.when(jnp.logical_not(two_left))
    def _():
        s_to(1, kaug_ref, qaug_ref[...], prev1, d1_ref[0, 0])
        pv_from(0, vst_ref[0, 0, c0])
        s_to(0, kaug_ref, qaug_ref[...], i, d0_ref[0, 0])
        pv_from(1, vst_ref[0, 0, prev1])
        pv_from(0, vst_ref[0, 0, i])

    for hh in range(NSA_HPG):
        hs = slice(hh * tq, (hh + 1) * tq)
        ot = mix_ref[:, hs] + gatest_ref[0, 0, NSA_HPG + hh:NSA_HPG + hh + 1] * finish(hs)
        y_ref[0, :, hh * NSA_DH:(hh + 1) * NSA_DH] = (ot.T * sg_ref[0, 0, hh]).astype(BF16)


def _overlap_table(s):
    n_cmp = (s - CMP_BLOCK) // CMP_STRIDE + 1
    n_slc = s // SLC_BLOCK
    cst = np.arange(n_cmp)[:, None] * CMP_STRIDE
    sst = np.arange(n_slc)[None, :] * SLC_BLOCK
    ov = np.clip(np.minimum(cst + CMP_BLOCK, sst + SLC_BLOCK) - np.maximum(cst, sst), 0, None) / CMP_STRIDE
    k_cols = -(-(CMP_PAD + s // CMP_STRIDE) // LANES) * LANES
    full = np.zeros((LANES, k_cols), np.float32)
    full[:n_slc, CMP_PAD:CMP_PAD + n_cmp] = ov.T
    return jnp.asarray(full, BF16)


def _nsa(q, kaug, vst, kw, vwt, kcmp, vcmpt, gatest, sg, bias):
    b, g, hpg, s, dh = q.shape
    tq = NSA_TQ
    assert WINDOW == 2 * tq and s % tq == 0 and s // SLC_BLOCK <= LANES and s // SLC_BLOCK >= SLC_TOPK
    n_pad = kcmp.shape[2]
    rows = hpg * tq
    seq_spec = pl.BlockSpec((1, 1, s, dh), lambda i, j, k: (i, j, 0, 0))
    seqt_spec = pl.BlockSpec((1, 1, s // tq, V_ROWS, tq), lambda i, j, k: (i, j, 0, 0, 0))
    qlike_spec = pl.BlockSpec((1, 1, hpg, tq, dh), lambda i, j, k: (i, j, 0, k, 0))

    def bias_spec(idx, needs_tiles_before=0):
        return pl.BlockSpec((1, 1, tq, rows),
                            lambda i, j, k: (j, jnp.where(k >= needs_tiles_before, idx, BIAS_MASKED), 0, 0))

    return pl.pallas_call(
        _nsa_kernel,
        out_shape=jax.ShapeDtypeStruct((b, s, g * hpg * dh), BF16),
        grid=(b, g, s // tq),
        in_specs=[qlike_spec,
                  pl.BlockSpec((1, 1, s, 2 * dh), lambda i, j, k: (i, j, 0, 0)),
                  seqt_spec, seq_spec, seqt_spec,
                  pl.BlockSpec((1, 1, n_pad, dh), lambda i, j, k: (i, j, 0, 0)),
                  pl.BlockSpec((1, 1, dh + LANES, vcmpt.shape[3]), lambda i, j, k: (i, j, 0, 0)),
                  pl.BlockSpec((1, 1, LANES, tq), lambda i, j, k: (i, j, 0, k)),
                  qlike_spec,
                  bias_spec(0), bias_spec(1, 1), bias_spec(2, 2), bias_spec(3)],
        out_specs=pl.BlockSpec((1, tq, hpg * dh), lambda i, j, k: (i, k, j)),
        scratch_shapes=[pltpu.VMEM((rows, 2 * dh), BF16),
                        pltpu.VMEM((1, rows), F32),
                        pltpu.VMEM((V_ROWS, rows), F32),
                        pltpu.VMEM((2, tq, rows), F32),
                        pltpu.VMEM((n_pad, rows), F32),
                        pltpu.VMEM((1, rows), F32),
                        pltpu.VMEM((V_ROWS, rows), F32),
                        pltpu.VMEM((2, tq, rows), F32),
                        pltpu.VMEM((dh, rows), F32),
                        pltpu.VMEM((2, 1, rows), F32),
                        pltpu.VMEM((2, 1, rows), F32)],
        compiler_params=_params(("arbitrary", "arbitrary", "arbitrary")),
        name="nsa_attention",
    )(q, kaug, vst, kw, vwt, kcmp, vcmpt, gatest, sg, bias, bias, bias, bias)


def _odd_out_kernel(h_ref, yn_ref, ym_ref, wn_ref, wm_ref, fg_ref, o_ref, *, final_norm):
    out = h_ref[...] + _mm(yn_ref[...], wn_ref[...]) + _mm(ym_ref[...], wm_ref[...])
    o_ref[...] = _rms(out, fg_ref[...]) if final_norm else out


def _odd_out(h, yn, ym, w_out, final_g, final_norm, tm=512):
    b, s, d = h.shape
    t = b * s
    nw = NSA_HEADS * NSA_DH
    wout = w_out.astype(BF16)
    out = pl.pallas_call(
        functools.partial(_odd_out_kernel, final_norm=final_norm),
        out_shape=jax.ShapeDtypeStruct((t, d), F32),
        grid=(t // tm,),
        in_specs=[pl.BlockSpec((tm, d), lambda i: (i, 0)),
                  pl.BlockSpec((tm, nw), lambda i: (i, 0)),
                  pl.BlockSpec((tm, MEM_WIDTH), lambda i: (i, 0)),
                  _const_spec((nw, d)), _const_spec((MEM_WIDTH, d)), _const_spec((1, d))],
        out_specs=pl.BlockSpec((tm, d), lambda i: (i, 0)),
        compiler_params=_params(("arbitrary",)),
        name="odd_out_proj",
    )(h.reshape(t, d), yn.reshape(t, nw), ym.reshape(t, MEM_WIDTH), wout[:nw], wout[nw:], final_g.reshape(1, d))
    return out.reshape(b, s, d)


def _final_norm_kernel(h_ref, g_ref, o_ref):
    o_ref[...] = _rms(h_ref[...], g_ref[...])


def _final_norm(h, final_g, tm=512):
    b, s, d = h.shape
    t = b * s
    out = pl.pallas_call(
        _final_norm_kernel,
        out_shape=jax.ShapeDtypeStruct((t, d), F32),
        grid=(t // tm,),
        in_specs=[pl.BlockSpec((tm, d), lambda i: (i, 0)), _const_spec((1, d))],
        out_specs=pl.BlockSpec((tm, d), lambda i: (i, 0)),
        compiler_params=_params(("arbitrary",)),
        name="final_norm",
    )(h.reshape(t, d), final_g.reshape(1, d))
    return out.reshape(b, s, d)


def kernel(x, mem, norm_g, final_g, mem_norm_g, rel_bias, ev_w_in, ev_pool_w, ev_pool_scale, ev_w_mem_kv, ev_w_out,
           od_w_in, od_cmp_pe, od_cmp_w1, od_cmp_b1, od_cmp_w2, od_w_mem_kv, od_w_out):
    depth = norm_g.shape[0]
    w_mem = [(ev_w_mem_kv if i % 2 == 0 else od_w_mem_kv)[i // 2] for i in range(depth)]
    memkv = _memkv(mem, mem_norm_g, jnp.concatenate(w_mem, axis=1).astype(BF16))
    bias = _bias_tiles(rel_bias) if depth > 1 else None
    h = x
    for i in range(depth):
        j = i // 2
        last = i == depth - 1
        if i % 2 == 0:
            h = _even_layer(h, memkv, i, norm_g[i], ev_w_in[j], ev_pool_w[j], ev_pool_scale[j], ev_w_out[j])
            if last:
                h = _final_norm(h, final_g)
        else:
            (q, kaug, vst, kw, vwt, kca, kcb, vca, vcb, gatest, sg, ym) = _odd_in(
                h, memkv, i, norm_g[i], od_w_in[j], od_cmp_pe[j])
            kcmp, vcmp = _compress(kca, kcb, vca, vcb, od_cmp_w1[j], od_cmp_b1[j], od_cmp_w2[j])
            yn = _nsa(q, kaug, vst, kw, vwt, kcmp, vcmp, gatest, sg, bias)
            h = _odd_out(h, yn, ym, od_w_out[j], final_g, last)
    return h
```

```python
import functools
import math

import numpy as np
import jax
import jax.numpy as jnp
from jax import lax
from jax.experimental import pallas as pl
from jax.experimental.pallas import tpu as pltpu

F32 = jnp.float32
BF16 = jnp.bfloat16

D_MODEL = 1024
D_INNER = 2048
N_MEM = 256
EPS = 1e-6
NEG = -1e30
BIG = 1e30

POOL_WINDOWS = (2, 4, 8, 16)
POOL_WIDTH = 768
POOL_GROUP = 192
POOL_HALO = 16

RET_HEADS = 4
RET_DK = 128
RET_DV = 192
RET_DV_PAD = 256
RET_CHUNK = 128
ROPE_BASE = 10000.0

MEM_HEADS = 4
MEM_DH = 128
MEM_WIDTH = 512

NSA_HEADS = 12
NSA_KV = 2
NSA_HPG = 6
NSA_DH = 128
CMP_BLOCK = 32
CMP_STRIDE = 16
CMP_HIDDEN = 256
SLC_BLOCK = 64
SLC_TOPK = 8
WINDOW = 512
REL_BUCKETS = 32
REL_MAX_DIST = 128

LANES = 128
NSA_TQ = 256
CMP_NEAR = 32
CMP_PAD = 16
LOG2E = math.log2(math.e)
V_ROWS = NSA_DH + 16
LOWEST = -3.0e38
KNOCKED = -3.3e38
BIAS_MASKED = 4

VMEM_LIMIT = 56 * 1024 * 1024


def _mm(a, b):
    return jnp.dot(a, b, preferred_element_type=F32)


def _mm_nt(a, b):
    return lax.dot_general(a, b, (((1,), (1,)), ((), ())), preferred_element_type=F32)


def _mm_tn(a, b):
    return lax.dot_general(a, b, (((0,), (0,)), ((), ())), preferred_element_type=F32)


def _rms(x, g):
    return x * lax.rsqrt(jnp.mean(x * x, axis=-1, keepdims=True) + EPS) * g


def _silu(x):
    return x * jax.nn.sigmoid(x)


def _const_spec(shape):
    nd = len(shape)
    return pl.BlockSpec(shape, lambda *_: (0,) * nd, pipeline_mode=pl.Buffered(1))


def _params(sem):
    return pltpu.CompilerParams(dimension_semantics=sem, vmem_limit_bytes=VMEM_LIMIT)


def _bias_kernel(tab_ref, rel_ref, out_ref, *, boxes):
    h = pl.program_id(0)
    k = pl.program_id(1)

    def lookup(rel):
        n = jnp.maximum(rel, 0)
        max_exact = REL_BUCKETS // 2
        nf = jnp.maximum(n, 1).astype(F32)
        large = max_exact + (jnp.log(nf / max_exact) / math.log(REL_MAX_DIST / max_exact)
                             * (REL_BUCKETS - max_exact)).astype(jnp.int32)
        large = jnp.minimum(large, REL_BUCKETS - 1)
        bucket = jnp.where(n < max_exact, n, large)
        far = tab_ref[REL_BUCKETS - 1, h]
        val = jnp.zeros(rel.shape, F32)
        for b in range(REL_BUCKETS - 1):
            val = jnp.where(bucket == b, tab_ref[b, h] - far, val)
        return jnp.where(rel < 0, NEG, val * LOG2E)

    out_ref[0, 0] = jnp.where(rel_ref[0] < 0, NEG, 0.0)
    for kind, box in enumerate(boxes):
        if box is not None:
            r0, r1, c0, c1 = box

            @pl.when(k == kind)
            def _():
                out_ref[0, 0, r0:r1, c0:c1] = lookup(rel_ref[0, r0:r1, c0:c1])


def _bias_tiles(rel_bias):
    tq = NSA_TQ
    r = np.arange(tq)[:, None]
    c = np.arange(tq)[None, :]
    d0 = r - c
    d1 = tq + r - c
    dw = np.where(c > r, WINDOW + r - c, -1)
    gc = np.where(c < CMP_NEAR, r - CMP_STRIDE * c + (CMP_STRIDE * CMP_PAD - (CMP_BLOCK - 1)), -1)
    rel_np = np.stack([d0.T, d1.T, dw.T, gc.T, np.full((tq, tq), -1)]).astype(np.int32)
    boxes = []
    for tile in rel_np:
        rr, cc = np.nonzero((tile >= 0) & (tile < REL_MAX_DIST))
        boxes.append(None if rr.size == 0 else tuple(int(v) for v in (
            rr.min() // 8 * 8, -(-(rr.max() + 1) // 8) * 8, cc.min() // LANES * LANES, -(-(cc.max() + 1) // LANES) * LANES)))
    rel = jnp.asarray(rel_np)
    nt = rel.shape[0]
    return pl.pallas_call(
        functools.partial(_bias_kernel, boxes=tuple(boxes)),
        out_shape=jax.ShapeDtypeStruct((NSA_KV, nt, tq, NSA_HPG * tq), F32),
        grid=(NSA_HEADS, nt),
        in_specs=[pl.BlockSpec(memory_space=pltpu.SMEM),
                  pl.BlockSpec((1, tq, tq), lambda h, k: (k, 0, 0))],
        out_specs=pl.BlockSpec((1, 1, tq, tq), lambda h, k: (h // NSA_HPG, k, 0, h % NSA_HPG)),
        compiler_params=_params(("arbitrary", "arbitrary")),
        name="bias_tiles",
    )(rel_bias.astype(F32), rel)


def _memkv_kernel(mem_ref, g_ref, w_ref, out_ref):
    y = _rms(mem_ref[0], g_ref[...]).astype(BF16)
    out_ref[0] = _mm(y, w_ref[...]).astype(BF16)


def _memkv(mem, mem_norm_g, w_all):
    b, m, d = mem.shape
    n = w_all.shape[1]
    return pl.pallas_call(
        _memkv_kernel,
        out_shape=jax.ShapeDtypeStruct((b, m, n), BF16),
        grid=(b,),
        in_specs=[pl.BlockSpec((1, m, d), lambda i: (i, 0, 0)),
                  _const_spec((1, d)),
                  _const_spec((d, n))],
        out_specs=pl.BlockSpec((1, m, n), lambda i: (i, 0, 0)),
        compiler_params=_params(("arbitrary",)),
        name="mem_kv",
    )(mem, mem_norm_g.reshape(1, d), w_all)


def _mem_attention(xq, mk, mv):
    outs = []
    for hd in range(MEM_HEADS):
        sl = slice(hd * MEM_DH, (hd + 1) * MEM_DH)
        qm = (xq[:, sl] * (MEM_DH ** -0.5)).astype(BF16)
        s = _mm_nt(qm, mk[:, sl])
        p = jnp.exp(s - jnp.max(s, axis=-1, keepdims=True))
        l = jnp.sum(p, axis=-1, keepdims=True)
        outs.append(_mm(p.astype(BF16), mv[:, sl]) / l)
    return outs


EV_ZA = 0
EV_RQ = EV_ZA + POOL_WIDTH
EV_RK = EV_RQ + RET_HEADS * RET_DK
EV_RV = EV_RK + RET_HEADS * RET_DK
EV_XQ = EV_RV + RET_HEADS * RET_DV_PAD
EV_GA = EV_XQ + MEM_WIDTH
EV_GR = EV_GA + POOL_WIDTH
EV_GM = EV_GR + RET_HEADS * RET_DV_PAD
EV_COLS = EV_GM + MEM_WIDTH
EV_YA = 0
EV_YR = POOL_WIDTH
EV_YM = EV_YR + RET_HEADS * RET_DV_PAD
EV_YCOLS = EV_YM + MEM_WIDTH
EV_SUB = 256


def _even_kernel(gch_ref, h_ref, g_ref, win_ref, wbd_ref, pscale_ref, cos_ref, sin_ref, decay_ref, xi_ref,
                 zeta_ref, mk_ref, mv_ref, wout_ref, o_ref, ext_ref, state_ref, y_ref):
    si = pl.program_id(1)
    ts = h_ref.shape[1]

    @pl.when(si == 0)
    def _():
        ext_ref[0:POOL_HALO, :] = jnp.zeros((POOL_HALO, POOL_WIDTH), F32)
        state_ref[...] = jnp.zeros(state_ref.shape, F32)

    for r0 in range(0, ts, EV_SUB):
        _even_subtile(si * ts + r0, slice(r0, r0 + EV_SUB), gch_ref, h_ref, g_ref, win_ref, wbd_ref, pscale_ref,
                      cos_ref, sin_ref, decay_ref, xi_ref, zeta_ref, mk_ref, mv_ref, wout_ref, o_ref, ext_ref,
                      state_ref, y_ref)


def _even_subtile(t0, tile, gch_ref, h_ref, g_ref, win_ref, wbd_ref, pscale_ref, cos_ref, sin_ref, decay_ref, xi_ref,
                  zeta_ref, mk_ref, mv_ref, wout_ref, o_ref, ext_ref, state_ref, y_ref):
    ts = EV_SUB
    r0 = tile.start
    h = h_ref[0, tile]
    u = _rms(h, g_ref[...]).astype(BF16)

    def proj(start, width):
        return _mm(u, win_ref[:, start:start + width])

    ext_ref[POOL_HALO:, :] = proj(EV_ZA, POOL_WIDTH)
    e = ext_ref[...]
    s2 = e + pltpu.roll(e, 1, 0)
    s4 = s2 + pltpu.roll(s2, 2, 0)
    s8 = s4 + pltpu.roll(s4, 4, 0)
    s16 = s8 + pltpu.roll(s8, 8, 0)
    lane = lax.broadcasted_iota(jnp.int32, e.shape, 1)
    row = lax.broadcasted_iota(jnp.int32, e.shape, 0)
    tpos = t0 + row - POOL_HALO
    g0, g1, g2 = lane < POOL_GROUP, lane < 2 * POOL_GROUP, lane < 3 * POOL_GROUP
    wsum = jnp.where(g0, s2, jnp.where(g1, s4, jnp.where(g2, s8, s16)))
    wlen = jnp.where(g0, POOL_WINDOWS[0], jnp.where(g1, POOL_WINDOWS[1],
                                                    jnp.where(g2, POOL_WINDOWS[2], POOL_WINDOWS[3])))
    cnt = jnp.maximum(jnp.minimum(tpos + 1, wlen), 1).astype(F32)
    pooled = (wsum / cnt - e)[POOL_HALO:]
    ext_ref[0:POOL_HALO, :] = e[ts:ts + POOL_HALO]
    a = _mm(pooled.astype(BF16), wbd_ref[...]) * pscale_ref[...]
    y_ref[tile, EV_YA:EV_YA + POOL_WIDTH] = (a * _silu(proj(EV_GA, POOL_WIDTH))).astype(BF16)

    cos = cos_ref[tile]
    sin = sin_ref[tile]
    vlane = lax.broadcasted_iota(jnp.int32, (RET_CHUNK, RET_DV_PAD), 1) < RET_DV
    for hd in range(RET_HEADS):
        qh = proj(EV_RQ + hd * RET_DK, RET_DK)
        kh = proj(EV_RK + hd * RET_DK, RET_DK)
        q_rot = (qh * cos + pltpu.roll(qh, RET_DK // 2, 1) * sin) * (RET_DK ** -0.5)
        k_rot = kh * cos + pltpu.roll(kh, RET_DK // 2, 1) * sin
        vb = proj(EV_RV + hd * RET_DV_PAD, RET_DV_PAD).astype(BF16)
        gate_r = _silu(proj(EV_GR + hd * RET_DV_PAD, RET_DV_PAD))
        for c in range(ts // RET_CHUNK):
            rows = slice(c * RET_CHUNK, (c + 1) * RET_CHUNK)
            qc, kc, vc = q_rot[rows], k_rot[rows], vb[rows]
            att = _mm_nt(qc.astype(BF16), kc.astype(BF16)) * decay_ref[hd]
            state = state_ref[hd]
            o = _mm(att.astype(BF16), vc) + _mm((qc * xi_ref[hd]).astype(BF16), state.astype(BF16))
            kv = _mm_tn((kc * zeta_ref[hd]).astype(BF16), vc)
            state_ref[hd] = state * gch_ref[hd] + kv
            mu = jnp.sum(o, axis=-1, keepdims=True) * (1.0 / RET_DV)
            dlt = jnp.where(vlane, o - mu, 0.0)
            var = jnp.sum(dlt * dlt, axis=-1, keepdims=True) * (1.0 / RET_DV)
            on = dlt * lax.rsqrt(var + EPS)
            col = EV_YR + hd * RET_DV_PAD
            y_ref[r0 + c * RET_CHUNK:r0 + (c + 1) * RET_CHUNK, col:col + RET_DV_PAD] = (on * gate_r[rows]).astype(BF16)

    xq = proj(EV_XQ, MEM_WIDTH)
    gm = _silu(proj(EV_GM, MEM_WIDTH))
    for hd, om in enumerate(_mem_attention(xq, mk_ref[0], mv_ref[0])):
        sl = slice(hd * MEM_DH, (hd + 1) * MEM_DH)
        y_ref[tile, EV_YM + hd * MEM_DH:EV_YM + (hd + 1) * MEM_DH] = (om * gm[:, sl]).astype(BF16)

    o_ref[0, tile] = h + _mm(y_ref[tile], wout_ref[...])


def _pad_heads(w, heads, width, padded, axis):
    shp = w.shape
    w = w.reshape(shp[:axis] + (heads, width) + shp[axis + 1:])
    pad = [(0, 0)] * w.ndim
    pad[axis + 1] = (0, padded - width)
    w = jnp.pad(w, pad)
    return w.reshape(shp[:axis] + (heads * padded,) + shp[axis + 1:])


def _retention_tables(s):
    half = RET_DK // 2
    inv = ROPE_BASE ** (-jnp.arange(half, dtype=F32) / half)
    ang = jnp.arange(s, dtype=F32)[:, None] * inv[None, :]
    cos, sin = jnp.cos(ang), jnp.sin(ang)
    cos_t = jnp.concatenate([cos, cos], axis=-1)
    sin_t = jnp.concatenate([-sin, sin], axis=-1)
    c = RET_CHUNK
    log_g = jnp.log(1.0 - jnp.exp2(-5.0 - jnp.arange(RET_HEADS, dtype=F32)))
    n = jnp.arange(c, dtype=F32)
    diff = n[:, None] - n[None, :]
    decay = jnp.where(diff >= 0, jnp.exp(log_g[:, None, None] * jnp.maximum(diff, 0.0)), 0.0)
    xi = jnp.exp(log_g[:, None] * (n + 1.0))
    zeta = jnp.exp(log_g[:, None] * (c - 1.0 - n))
    g_chunk = jnp.exp(log_g * c)
    xi_t = jnp.broadcast_to(xi[:, :, None], (RET_HEADS, c, RET_DK))
    zeta_t = jnp.broadcast_to(zeta[:, :, None], (RET_HEADS, c, RET_DK))
    return cos_t, sin_t, decay, xi_t, zeta_t, g_chunk


def _even_layer(h, memkv, layer, g, w_in, pool_w, pool_scale, w_out, ts=2 * EV_SUB):
    b, s, d = h.shape
    za, rq, rk, rv, xq, gate = jnp.split(w_in, np.cumsum(
        [POOL_WIDTH, RET_HEADS * RET_DK, RET_HEADS * RET_DK, RET_HEADS * RET_DV, MEM_WIDTH])[:].tolist(), axis=1)
    ga, gr, gm = jnp.split(gate, [POOL_WIDTH, POOL_WIDTH + RET_HEADS * RET_DV], axis=1)
    win = jnp.concatenate([za, rq, rk, _pad_heads(rv, RET_HEADS, RET_DV, RET_DV_PAD, 1), xq, ga,
                           _pad_heads(gr, RET_HEADS, RET_DV, RET_DV_PAD, 1), gm], axis=1).astype(BF16)
    oa, orr, om = jnp.split(w_out, [POOL_WIDTH, POOL_WIDTH + RET_HEADS * RET_DV], axis=0)
    wout = jnp.concatenate([oa, _pad_heads(orr, RET_HEADS, RET_DV, RET_DV_PAD, 0), om], axis=0).astype(BF16)
    wbd = jnp.zeros((POOL_WIDTH, POOL_WIDTH), F32)
    for gi in range(len(POOL_WINDOWS)):
        sl = slice(gi * POOL_GROUP, (gi + 1) * POOL_GROUP)
        wbd = wbd.at[sl, sl].set(pool_w[gi])
    wbd = wbd.astype(BF16)
    cos_t, sin_t, decay, xi_t, zeta_t, g_chunk = _retention_tables(s)
    kv_blk = 2 * layer
    return pl.pallas_call(
        _even_kernel,
        out_shape=jax.ShapeDtypeStruct((b, s, d), F32),
        grid=(b, s // ts),
        in_specs=[pl.BlockSpec(memory_space=pltpu.SMEM),
                  pl.BlockSpec((1, ts, d), lambda i, j: (i, j, 0)),
                  _const_spec((1, d)),
                  _const_spec((d, EV_COLS)),
                  _const_spec((POOL_WIDTH, POOL_WIDTH)),
                  _const_spec((1, POOL_WIDTH)),
                  pl.BlockSpec((ts, RET_DK), lambda i, j: (j, 0)),
                  pl.BlockSpec((ts, RET_DK), lambda i, j: (j, 0)),
                  _const_spec((RET_HEADS, RET_CHUNK, RET_CHUNK)),
                  _const_spec((RET_HEADS, RET_CHUNK, RET_DK)),
                  _const_spec((RET_HEADS, RET_CHUNK, RET_DK)),
                  pl.BlockSpec((1, N_MEM, MEM_WIDTH), lambda i, j: (i, 0, kv_blk)),
                  pl.BlockSpec((1, N_MEM, MEM_WIDTH), lambda i, j: (i, 0, kv_blk + 1)),
                  _const_spec((EV_YCOLS, d))],
        out_specs=pl.BlockSpec((1, ts, d), lambda i, j: (i, j, 0)),
        scratch_shapes=[pltpu.VMEM((POOL_HALO + EV_SUB, POOL_WIDTH), F32),
                        pltpu.VMEM((RET_HEADS, RET_DK, RET_DV_PAD), F32),
                        pltpu.VMEM((ts, EV_YCOLS), BF16)],
        compiler_params=_params(("arbitrary", "arbitrary")),
        name="even_layer",
    )(g_chunk, h, g.reshape(1, d), win, wbd, pool_scale.reshape(1, POOL_WIDTH), cos_t, sin_t, decay, xi_t, zeta_t,
      memkv, memkv, wout)


KV_W = NSA_KV * NSA_DH
OD_Q = 0
OD_KS = OD_Q + NSA_HEADS * NSA_DH
OD_VS = OD_KS + KV_W
OD_KW = OD_VS + KV_W
OD_VW = OD_KW + KV_W
OD_KC = OD_VW + KV_W
OD_VC = OD_KC + KV_W
OD_GL = OD_VC + KV_W
OD_XQ = OD_GL + NSA_KV * LANES
OD_GN = OD_XQ + MEM_WIDTH
OD_GM = OD_GN + NSA_HEADS * NSA_DH
OD_COLS = OD_GM + MEM_WIDTH


def _odd_in_kernel(h_ref, g_ref, win_ref, pe_ref, perm_ref, mk_ref, mv_ref,
                   q_ref, kaug_ref, vst_ref, kw_ref, vwt_ref, kca_ref, kcb_ref, vca_ref, vcb_ref, gatest_ref,
                   sg_ref, ym_ref):
    si = pl.program_id(1)
    n_sub = h_ref.shape[1] // NSA_TQ
    for sub in range(n_sub):
        _odd_in_subtile(si * n_sub + sub, sub, h_ref, g_ref, win_ref, pe_ref, perm_ref, mk_ref, mv_ref,
                        q_ref, kaug_ref, vst_ref, kw_ref, vwt_ref, kca_ref, kcb_ref, vca_ref, vcb_ref, gatest_ref,
                        sg_ref, ym_ref)


def _odd_in_subtile(tile_idx, sub, h_ref, g_ref, win_ref, pe_ref, perm_ref, mk_ref, mv_ref,
                    q_ref, kaug_ref, vst_ref, kw_ref, vwt_ref, kca_ref, kcb_ref, vca_ref, vcb_ref, gatest_ref,
                    sg_ref, ym_ref):
    ts = NSA_TQ
    tile = slice(sub * ts, (sub + 1) * ts)
    u = _rms(h_ref[0, tile], g_ref[...]).astype(BF16)

    def proj(start, width):
        return _mm(u, win_ref[:, start:start + width])

    zq = proj(OD_Q, NSA_HEADS * NSA_DH) * (NSA_DH ** -0.5 * LOG2E)
    zg = _silu(proj(OD_GN, NSA_HEADS * NSA_DH))
    for g in range(NSA_KV):
        for hh in range(NSA_HPG):
            sl = slice((g * NSA_HPG + hh) * NSA_DH, (g * NSA_HPG + hh + 1) * NSA_DH)
            q_ref[0, g, hh, tile] = zq[:, sl].astype(BF16)
            sg_ref[0, g, hh, tile] = zg[:, sl]

    zkv = proj(OD_KS, 6 * KV_W)
    lane = lax.broadcasted_iota(jnp.int32, (ts, LANES), 1)
    blk = (tile_idx * ts + lax.broadcasted_iota(jnp.int32, (ts, LANES), 0)) // SLC_BLOCK
    onehot = jnp.where(lane == blk, 1.0, 0.0).astype(BF16)
    ones_rows = jnp.where(lax.broadcasted_iota(jnp.int32, (V_ROWS - NSA_DH, ts), 0) == 0, 1.0, 0.0).astype(BF16)
    zgl = jax.nn.sigmoid(proj(OD_GL, NSA_KV * LANES))
    flat_in = []
    for g in range(NSA_KV):
        def piece(idx):
            off = idx * KV_W + g * NSA_DH
            return zkv[:, off:off + NSA_DH]
        kaug_ref[0, g, tile, 0:NSA_DH] = piece(0).astype(BF16)
        kaug_ref[0, g, tile, NSA_DH:2 * NSA_DH] = onehot
        vst_ref[0, g, sub, 0:NSA_DH] = piece(1).T.astype(BF16)
        vst_ref[0, g, sub, NSA_DH:V_ROWS] = ones_rows
        kw_ref[0, g, tile] = piece(2).astype(BF16)
        vwt_ref[0, g, sub, 0:NSA_DH] = piece(3).T.astype(BF16)
        vwt_ref[0, g, sub, NSA_DH:V_ROWS] = ones_rows
        kc, vc = piece(4), piece(5)
        flat_in += [(kc + pe_ref[0, 0]).astype(BF16), (kc + pe_ref[0, 1]).astype(BF16),
                    (vc + pe_ref[1, 0]).astype(BF16), (vc + pe_ref[1, 1]).astype(BF16)]
        gatest_ref[0, g, :, tile] = zgl[:, g * LANES:(g + 1) * LANES].T

    nj = ts // CMP_STRIDE
    perm = _mm(perm_ref[...], jnp.concatenate(flat_in, axis=1)).astype(BF16)
    for c, out_ref in enumerate((kca_ref, kcb_ref, vca_ref, vcb_ref) * NSA_KV):
        for l in range(CMP_STRIDE):
            out_ref[0, c // 4, sub * nj:(sub + 1) * nj, l * NSA_DH:(l + 1) * NSA_DH] = (
                perm[l * nj:(l + 1) * nj, c * NSA_DH:(c + 1) * NSA_DH])

    xq = proj(OD_XQ, MEM_WIDTH)
    gm = _silu(proj(OD_GM, MEM_WIDTH))
    for hd, om in enumerate(_mem_attention(xq, mk_ref[0], mv_ref[0])):
        sl = slice(hd * MEM_DH, (hd + 1) * MEM_DH)
        ym_ref[0, tile, sl] = (om * gm[:, sl]).astype(BF16)


def _odd_in(h, memkv, layer, g, w_in, cmp_pe, ts=2 * NSA_TQ):
    b, s, d = h.shape
    sizes = [NSA_HEADS * NSA_DH] + [KV_W] * 6 + [3 * NSA_HEADS, MEM_WIDTH]
    q, kc, vc, ks, vs, kw, vw, gl, xq, gate = jnp.split(w_in, np.cumsum(sizes).tolist(), axis=1)
    gl = gl.reshape(d, 3, NSA_KV, NSA_HPG).transpose(0, 2, 1, 3).reshape(d, NSA_KV, 3 * NSA_HPG)
    gl = jnp.pad(gl, ((0, 0), (0, 0), (0, LANES - 3 * NSA_HPG))).reshape(d, NSA_KV * LANES)
    gn, gm = jnp.split(gate, [NSA_HEADS * NSA_DH], axis=1)
    win = jnp.concatenate([w.astype(BF16) for w in (q, ks, vs, kw, vw, kc, vc, gl, xq, gn, gm)], axis=1)
    sub = NSA_TQ
    reps = sub // CMP_STRIDE
    pe = jnp.stack([jnp.stack([jnp.tile(cmp_pe[kv, :CMP_STRIDE], (reps, 1)),
                               jnp.tile(cmp_pe[kv, CMP_STRIDE:], (reps, 1))]) for kv in range(2)])
    kv_blk = 2 * layer
    assert ts % sub == 0
    nj = sub // CMP_STRIDE
    perm = np.zeros((sub, sub), np.float32)
    perm[np.arange(sub), (np.arange(sub) % nj) * CMP_STRIDE + np.arange(sub) // nj] = 1.0
    head_t = jax.ShapeDtypeStruct((b, NSA_KV, s, NSA_DH), BF16)
    head_spec = pl.BlockSpec((1, NSA_KV, ts, NSA_DH), lambda i, j: (i, 0, j, 0))
    headt_t = jax.ShapeDtypeStruct((b, NSA_KV, s // sub, V_ROWS, sub), BF16)
    headt_spec = pl.BlockSpec((1, NSA_KV, ts // sub, V_ROWS, sub), lambda i, j: (i, 0, j, 0, 0))
    flat_t = jax.ShapeDtypeStruct((b, NSA_KV, s // CMP_STRIDE, CMP_STRIDE * NSA_DH), BF16)
    flat_spec = pl.BlockSpec((1, NSA_KV, ts // CMP_STRIDE, CMP_STRIDE * NSA_DH), lambda i, j: (i, 0, j, 0))
    qlike_spec = pl.BlockSpec((1, NSA_KV, NSA_HPG, ts, NSA_DH), lambda i, j: (i, 0, 0, j, 0))
    return pl.pallas_call(
        _odd_in_kernel,
        out_shape=[jax.ShapeDtypeStruct((b, NSA_KV, NSA_HPG, s, NSA_DH), BF16),
                   jax.ShapeDtypeStruct((b, NSA_KV, s, 2 * NSA_DH), BF16),
                   headt_t, head_t, headt_t, flat_t, flat_t, flat_t, flat_t,
                   jax.ShapeDtypeStruct((b, NSA_KV, LANES, s), F32),
                   jax.ShapeDtypeStruct((b, NSA_KV, NSA_HPG, s, NSA_DH), F32),
                   jax.ShapeDtypeStruct((b, s, MEM_WIDTH), BF16)],
        grid=(b, s // ts),
        in_specs=[pl.BlockSpec((1, ts, d), lambda i, j: (i, j, 0)),
                  _const_spec((1, d)),
                  _const_spec((d, OD_COLS)),
                  _const_spec((2, 2, sub, NSA_DH)),
                  _const_spec((sub, sub)),
                  pl.BlockSpec((1, N_MEM, MEM_WIDTH), lambda i, j: (i, 0, kv_blk)),
                  pl.BlockSpec((1, N_MEM, MEM_WIDTH), lambda i, j: (i, 0, kv_blk + 1))],
        out_specs=[qlike_spec,
                   pl.BlockSpec((1, NSA_KV, ts, 2 * NSA_DH), lambda i, j: (i, 0, j, 0)),
                   headt_spec, head_spec, headt_spec, flat_spec, flat_spec, flat_spec, flat_spec,
                   pl.BlockSpec((1, NSA_KV, LANES, ts), lambda i, j: (i, 0, 0, j)),
                   qlike_spec,
                   pl.BlockSpec((1, ts, MEM_WIDTH), lambda i, j: (i, j, 0))],
        compiler_params=_params(("arbitrary", "arbitrary")),
        name="odd_in_proj",
    )(h, g.reshape(1, d), win, pe, jnp.asarray(perm, BF16), memkv, memkv)


def _compress_kernel(xka_ref, xkb_ref, xva_ref, xvb_ref, w1a_ref, w1b_ref, b1_ref, w2_ref, ovt_ref, kc_ref, vct_ref):
    n = xka_ref.shape[2]
    k_cols = vct_ref.shape[3]

    def block_mlp(kv, xa, xb):
        first = _mm(xa[0, 0], w1a_ref[kv])
        second = _mm(xb[0, 0], w1b_ref[kv])
        hid = first + pltpu.roll(second, n - 1, 0) + b1_ref[kv]
        return _mm(_silu(hid).astype(BF16), w2_ref[kv])

    kc_ref[0, 0, 0:CMP_PAD] = jnp.zeros((CMP_PAD, NSA_DH), BF16)
    kc_ref[0, 0, CMP_PAD:] = block_mlp(0, xka_ref, xkb_ref).astype(BF16)
    vc = jnp.concatenate([jnp.zeros((CMP_PAD, NSA_DH), F32), block_mlp(1, xva_ref, xvb_ref),
                          jnp.zeros((k_cols - CMP_PAD - n, NSA_DH), F32)], axis=0)
    vct_ref[0, 0, 0:NSA_DH] = vc.T.astype(BF16)
    vct_ref[0, 0, NSA_DH:] = ovt_ref[...]


def _compress(kca, kcb, vca, vcb, w1, b1, w2):
    b, g, n, half = kca.shape
    dh = half // CMP_STRIDE
    s = n * CMP_STRIDE
    xs = (kca, kcb, vca, vcb)
    w1 = w1.astype(BF16)
    ovt = _overlap_table(s)
    k_cols = ovt.shape[1]
    x_spec = pl.BlockSpec((1, 1, n, half), lambda i, j: (i, j, 0, 0))
    return pl.pallas_call(
        _compress_kernel,
        out_shape=[jax.ShapeDtypeStruct((b, g, n + CMP_PAD, dh), BF16),
                   jax.ShapeDtypeStruct((b, g, dh + LANES, k_cols), BF16)],
        grid=(b, g),
        in_specs=[x_spec, x_spec, x_spec, x_spec,
                  _const_spec((2, half, CMP_HIDDEN)), _const_spec((2, half, CMP_HIDDEN)),
                  _const_spec((2, 1, CMP_HIDDEN)), _const_spec((2, CMP_HIDDEN, dh)),
                  _const_spec((LANES, k_cols))],
        out_specs=[pl.BlockSpec((1, 1, n + CMP_PAD, dh), lambda i, j: (i, j, 0, 0)),
                   pl.BlockSpec((1, 1, dh + LANES, k_cols), lambda i, j: (i, j, 0, 0))],
        compiler_params=_params(("arbitrary", "arbitrary")),
        name="compress",
    )(*xs, w1[:, :half], w1[:, half:], b1.reshape(2, 1, CMP_HIDDEN), w2.astype(BF16), ovt)


def _nsa_kernel(q_ref, kaug_ref, vst_ref, kw_ref, vwt_ref, kcmp_ref, vcmpt_ref, gatest_ref, sg_ref,
                d0_ref, d1_ref, dw_ref, gc_ref, y_ref, qaug_ref, m_ref, acc_ref, s_ref, sc_ref, mw_ref, accw_ref, sw_ref,
                mix_ref, tmax_ref, tmaxw_ref):
    i = pl.program_id(2)
    tq = NSA_TQ
    rows = NSA_HPG * tq
    s_len = kw_ref.shape[2]
    n_cmp = s_len // CMP_STRIDE
    n_slc = s_len // SLC_BLOCK
    t0 = i * tq
    q = q_ref[0, 0].reshape(rows, NSA_DH)


    def stream(mx_ref, ac_ref, sc2_ref, tmax_ref):
        def reset():
            mx_ref[...] = jnp.full(mx_ref.shape, LOWEST, F32)
            ac_ref[...] = jnp.zeros(ac_ref.shape, F32)

        def s_to(slot, k_ref, query, tile, bias=None):
            k0 = pl.multiple_of(tile * tq, tq)
            st = _mm_nt(k_ref[0, 0, pl.ds(k0, tq)], query)
            if bias is not None:
                st = st + bias
            sc2_ref[slot] = st
            tmax_ref[slot] = jnp.max(st, axis=0, keepdims=True)

        def pv_from(slot, vt):
            st = sc2_ref[slot]
            m_prev = mx_ref[...]
            m_new = jnp.maximum(m_prev, tmax_ref[slot])
            pt = jnp.exp2(st - m_new).astype(BF16)
            ac_ref[...] = jnp.exp2(m_prev - m_new) * ac_ref[...] + _mm(vt, pt)
            mx_ref[...] = m_new

        def finish(cols=slice(None)):
            return ac_ref[0:NSA_DH, cols] / ac_ref[NSA_DH:NSA_DH + 1, cols]

        return reset, s_to, pv_from, finish

    def off_unless(cond):
        return jnp.where(cond, 0.0, NEG)

    prev1 = jnp.maximum(i - 1, 0)
    prev2 = jnp.maximum(i - 2, 0)
    w_reset, w_s_to, w_pv_from, w_finish = stream(mw_ref, accw_ref, sw_ref, tmaxw_ref)


    n_pad = kcmp_ref.shape[2]
    j_near = pl.multiple_of(i * (tq // CMP_STRIDE), tq // CMP_STRIDE)
    qaug_ref[:, 0:NSA_DH] = q
    qaug_ref[:, NSA_DH:] = jnp.where(lax.broadcasted_iota(jnp.int32, (rows, NSA_DH), 1) == 0, 1.0, 0.0).astype(BF16)
    krow = lax.broadcasted_iota(jnp.int32, (n_pad, NSA_DH), 0)
    key_bias = jnp.where((krow >= CMP_PAD) & (krow < j_near + CMP_NEAR), 0.0, NEG).astype(BF16)
    sc_ref[...] = _mm_nt(jnp.concatenate([kcmp_ref[0, 0], key_bias], axis=1), qaug_ref[...])
    w_reset()
    w_s_to(0, kw_ref, q, prev2, dw_ref[0, 0])
    w_s_to(1, kw_ref, q, prev1, d1_ref[0, 0])
    sc_ref[pl.ds(j_near, CMP_NEAR)] = sc_ref[pl.ds(j_near, CMP_NEAR)] + gc_ref[0, 0, 0:CMP_NEAR, :]
    s_cmp = sc_ref[...]
    p_cmp = jnp.exp2(s_cmp - jnp.max(s_cmp, axis=0, keepdims=True))
    tcol = t0 + jnp.bitwise_and(lax.broadcasted_iota(jnp.int32, (1, rows), 1), tq - 1)
    scale = jnp.where(tcol >= CMP_BLOCK - 1, 1.0, 0.0) / jnp.sum(p_cmp, axis=0, keepdims=True)
    p_cmp = p_cmp.astype(BF16)
    k_cols = vcmpt_ref.shape[3]
    if k_cols > n_pad:
        p_cmp = jnp.concatenate([p_cmp, jnp.zeros((k_cols - n_pad, rows), BF16)], axis=0)
    w_pv_from(0, vwt_ref[0, 0, prev2])
    both = _mm(vcmpt_ref[0, 0], p_cmp) * scale
    w_s_to(0, kw_ref, q, i, d0_ref[0, 0])
    w_pv_from(1, vwt_ref[0, 0, prev1])
    gtst = gatest_ref[0, 0]
    n_blk = -(-n_slc // 8) * 8
    imp = both[NSA_DH:NSA_DH + n_blk, 0:tq]
    for hh in range(NSA_HPG):
        hs = slice(hh * tq, (hh + 1) * tq)
        mix_ref[:, hs] = gtst[hh:hh + 1] * both[0:NSA_DH, hs]
        if hh:
            imp = imp + both[NSA_DH:NSA_DH + n_blk, hs]

    blk = lax.broadcasted_iota(jnp.int32, (n_blk, tq), 0)
    tq_pos = t0 + lax.broadcasted_iota(jnp.int32, (n_blk, tq), 1)
    cur = tq_pos // SLC_BLOCK
    forced = (blk == 0) | (blk == cur) | (blk == cur - 1)
    future = blk * SLC_BLOCK > tq_pos
    imp = jnp.where(forced, BIG, jnp.where(future, -BIG, imp))
    imp = jnp.where(blk < n_slc, imp, LOWEST)
    blk_f = blk.astype(F32)
    sel = jnp.zeros((n_blk, tq), jnp.bool_)
    for _ in range(SLC_TOPK):
        top = jnp.max(imp, axis=0, keepdims=True)
        first = jnp.min(jnp.where(imp == top, blk_f, float(LANES)), axis=0, keepdims=True)
        hit = blk_f == first
        sel = sel | hit
        imp = jnp.where(hit, KNOCKED, imp)
    sneg_t = jnp.where(sel, 0.0, NEG)
    if n_blk < LANES:
        sneg_t = jnp.concatenate([sneg_t, jnp.zeros((LANES - n_blk, tq), F32)], axis=0)
    sneg = sneg_t.T.astype(BF16)
    for hh in range(NSA_HPG):
        qaug_ref[hh * tq:(hh + 1) * tq, NSA_DH:2 * NSA_DH] = sneg
    w_pv_from(0, vwt_ref[0, 0, i])
    for hh in range(NSA_HPG):
        hs = slice(hh * tq, (hh + 1) * tq)
        mix_ref[:, hs] = mix_ref[:, hs] + gtst[2 * NSA_HPG + hh:2 * NSA_HPG + hh + 1] * w_finish(hs)

    reset, s_to, pv_from, finish = stream(m_ref, acc_ref, s_ref, tmax_ref)
    reset()
    n_far = prev1
    n_pairs = jnp.maximum(n_far - 1, 0) // 2
    s_to(0, kaug_ref, qaug_ref[...], 0, off_unless(n_far >= 1))

    def far_pair(t2, carry):
        s_to(1, kaug_ref, qaug_ref[...], 2 * t2 + 1)
        pv_from(0, vst_ref[0, 0, 2 * t2])
        s_to(0, kaug_ref, qaug_ref[...], 2 * t2 + 2)
        pv_from(1, vst_ref[0, 0, 2 * t2 + 1])
        return carry

    lax.fori_loop(0, n_pairs, far_pair, 0)
    c0 = 2 * n_pairs
    two_left = n_far - c0 == 2

    @pl.when(two_left)
    def _():
        s_to(1, kaug_ref, qaug_ref[...], c0 + 1)
        pv_from(0, vst_ref[0, 0, c0])
        s_to(0, kaug_ref, qaug_ref[...], prev1, d1_ref[0, 0])
        pv_from(1, vst_ref[0, 0, c0 + 1])
        s_to(1, kaug_ref, qaug_ref[...], i, d0_ref[0, 0])
        pv_from(0, vst_ref[0, 0, prev1])
        pv_from(1, vst_ref[0, 0, i])

    @pl.when(jnp.logical_not(two_left))
    def _():
        s_to(1, kaug_ref, qaug_ref[...], prev1, d1_ref[0, 0])
        pv_from(0, vst_ref[0, 0, c0])
        s_to(0, kaug_ref, qaug_ref[...], i, d0_ref[0, 0])
        pv_from(1, vst_ref[0, 0, prev1])
        pv_from(0, vst_ref[0, 0, i])

    for hh in range(NSA_HPG):
        hs = slice(hh * tq, (hh + 1) * tq)
        ot = mix_ref[:, hs] + gatest_ref[0, 0, NSA_HPG + hh:NSA_HPG + hh + 1] * finish(hs)
        y_ref[0, :, hh * NSA_DH:(hh + 1) * NSA_DH] = (ot.T * sg_ref[0, 0, hh]).astype(BF16)


def _overlap_table(s):
    n_cmp = (s - CMP_BLOCK) // CMP_STRIDE + 1
    n_slc = s // SLC_BLOCK
    cst = np.arange(n_cmp)[:, None] * CMP_STRIDE
    sst = np.arange(n_slc)[None, :] * SLC_BLOCK
    ov = np.clip(np.minimum(cst + CMP_BLOCK, sst + SLC_BLOCK) - np.maximum(cst, sst), 0, None) / CMP_STRIDE
    k_cols = -(-(CMP_PAD + s // CMP_STRIDE) // LANES) * LANES
    full = np.zeros((LANES, k_cols), np.float32)
    full[:n_slc, CMP_PAD:CMP_PAD + n_cmp] = ov.T
    return jnp.asarray(full, BF16)


def _nsa(q, kaug, vst, kw, vwt, kcmp, vcmpt, gatest, sg, bias):
    b, g, hpg, s, dh = q.shape
    tq = NSA_TQ
    assert WINDOW == 2 * tq and s % tq == 0 and s // SLC_BLOCK <= LANES and s // SLC_BLOCK >= SLC_TOPK
    n_pad = kcmp.shape[2]
    rows = hpg * tq
    seq_spec = pl.BlockSpec((1, 1, s, dh), lambda i, j, k: (i, j, 0, 0))
    seqt_spec = pl.BlockSpec((1, 1, s // tq, V_ROWS, tq), lambda i, j, k: (i, j, 0, 0, 0))
    qlike_spec = pl.BlockSpec((1, 1, hpg, tq, dh), lambda i, j, k: (i, j, 0, k, 0))

    def bias_spec(idx, needs_tiles_before=0):
        return pl.BlockSpec((1, 1, tq, rows),
                            lambda i, j, k: (j, jnp.where(k >= needs_tiles_before, idx, BIAS_MASKED), 0, 0))

    return pl.pallas_call(
        _nsa_kernel,
        out_shape=jax.ShapeDtypeStruct((b, s, g * hpg * dh), BF16),
        grid=(b, g, s // tq),
        in_specs=[qlike_spec,
                  pl.BlockSpec((1, 1, s, 2 * dh), lambda i, j, k: (i, j, 0, 0)),
                  seqt_spec, seq_spec, seqt_spec,
                  pl.BlockSpec((1, 1, n_pad, dh), lambda i, j, k: (i, j, 0, 0)),
                  pl.BlockSpec((1, 1, dh + LANES, vcmpt.shape[3]), lambda i, j, k: (i, j, 0, 0)),
                  pl.BlockSpec((1, 1, LANES, tq), lambda i, j, k: (i, j, 0, k)),
                  qlike_spec,
                  bias_spec(0), bias_spec(1, 1), bias_spec(2, 2), bias_spec(3)],
        out_specs=pl.BlockSpec((1, tq, hpg * dh), lambda i, j, k: (i, k, j)),
        scratch_shapes=[pltpu.VMEM((rows, 2 * dh), BF16),
                        pltpu.VMEM((1, rows), F32),
                        pltpu.VMEM((V_ROWS, rows), F32),
                        pltpu.VMEM((2, tq, rows), F32),
                        pltpu.VMEM((n_pad, rows), F32),
                        pltpu.VMEM((1, rows), F32),
                        pltpu.VMEM((V_ROWS, rows), F32),
                        pltpu.VMEM((2, tq, rows), F32),
                        pltpu.VMEM((dh, rows), F32),
                        pltpu.VMEM((2, 1, rows), F32),
                        pltpu.VMEM((2, 1, rows), F32)],
        compiler_params=_params(("arbitrary", "arbitrary", "arbitrary")),
        name="nsa_attention",
    )(q, kaug, vst, kw, vwt, kcmp, vcmpt, gatest, sg, bias, bias, bias, bias)


def _odd_out_kernel(h_ref, yn_ref, ym_ref, wn_ref, wm_ref, fg_ref, o_ref, *, final_norm):
    out = h_ref[...] + _mm(yn_ref[...], wn_ref[...]) + _mm(ym_ref[...], wm_ref[...])
    o_ref[...] = _rms(out, fg_ref[...]) if final_norm else out


def _odd_out(h, yn, ym, w_out, final_g, final_norm, tm=512):
    b, s, d = h.shape
    t = b * s
    nw = NSA_HEADS * NSA_DH
    wout = w_out.astype(BF16)
    out = pl.pallas_call(
        functools.partial(_odd_out_kernel, final_norm=final_norm),
        out_shape=jax.ShapeDtypeStruct((t, d), F32),
        grid=(t // tm,),
        in_specs=[pl.BlockSpec((tm, d), lambda i: (i, 0)),
                  pl.BlockSpec((tm, nw), lambda i: (i, 0)),
                  pl.BlockSpec((tm, MEM_WIDTH), lambda i: (i, 0)),
                  _const_spec((nw, d)), _const_spec((MEM_WIDTH, d)), _const_spec((1, d))],
        out_specs=pl.BlockSpec((tm, d), lambda i: (i, 0)),
        compiler_params=_params(("arbitrary",)),
        name="odd_out_proj",
    )(h.reshape(t, d), yn.reshape(t, nw), ym.reshape(t, MEM_WIDTH), wout[:nw], wout[nw:], final_g.reshape(1, d))
    return out.reshape(b, s, d)


def _final_norm_kernel(h_ref, g_ref, o_ref):
    o_ref[...] = _rms(h_ref[...], g_ref[...])


def _final_norm(h, final_g, tm=512):
    b, s, d = h.shape
    t = b * s
    out = pl.pallas_call(
        _final_norm_kernel,
        out_shape=jax.ShapeDtypeStruct((t, d), F32),
        grid=(t // tm,),
        in_specs=[pl.BlockSpec((tm, d), lambda i: (i, 0)), _const_spec((1, d))],
        out_specs=pl.BlockSpec((tm, d), lambda i: (i, 0)),
        compiler_params=_params(("arbitrary",)),
        name="final_norm",
    )(h.reshape(t, d), final_g.reshape(1, d))
    return out.reshape(b, s, d)


def kernel(x, mem, norm_g, final_g, mem_norm_g, rel_bias, ev_w_in, ev_pool_w, ev_pool_scale, ev_w_mem_kv, ev_w_out,
           od_w_in, od_cmp_pe, od_cmp_w1, od_cmp_b1, od_cmp_w2, od_w_mem_kv, od_w_out):
    depth = norm_g.shape[0]
    w_mem = [(ev_w_mem_kv if i % 2 == 0 else od_w_mem_kv)[i // 2] for i in range(depth)]
    memkv = _memkv(mem, mem_norm_g, jnp.concatenate(w_mem, axis=1).astype(BF16))
    bias = _bias_tiles(rel_bias) if depth > 1 else None
    h = x
    for i in range(depth):
        j = i // 2
        last = i == depth - 1
        if i % 2 == 0:
            h = _even_layer(h, memkv, i, norm_g[i], ev_w_in[j], ev_pool_w[j], ev_pool_scale[j], ev_w_out[j])
            if last:
                h = _final_norm(h, final_g)
        else:
            (q, kaug, vst, kw, vwt, kca, kcb, vca, vcb, gatest, sg, ym) = _odd_in(
                h, memkv, i, norm_g[i], od_w_in[j], od_cmp_pe[j])
            kcmp, vcmp = _compress(kca, kcb, vca, vcb, od_cmp_w1[j], od_cmp_b1[j], od_cmp_w2[j])
            yn = _nsa(q, kaug, vst, kw, vwt, kcmp, vcmp, gatest, sg, bias)
            h = _odd_out(h, yn, ym, od_w_out[j], final_g, last)
    return h
```

```python
import functools
import math

import numpy as np
import jax
import jax.numpy as jnp
from jax import lax
from jax.experimental import pallas as pl
from jax.experimental.pallas import tpu as pltpu

F32 = jnp.float32
BF16 = jnp.bfloat16

D_MODEL = 1024
D_INNER = 2048
N_MEM = 256
EPS = 1e-6
NEG = -1e30
BIG = 1e30

POOL_WINDOWS = (2, 4, 8, 16)
POOL_WIDTH = 768
POOL_GROUP = 192
POOL_HALO = 16

RET_HEADS = 4
RET_DK = 128
RET_DV = 192
RET_CHUNK = 128
ROPE_BASE = 10000.0

MEM_HEADS = 4
MEM_DH = 128
MEM_WIDTH = 512

NSA_HEADS = 12
NSA_KV = 2
NSA_HPG = 6
NSA_DH = 128
CMP_BLOCK = 32
CMP_STRIDE = 16
CMP_HIDDEN = 256
SLC_BLOCK = 64
SLC_TOPK = 8
WINDOW = 512
REL_BUCKETS = 32
REL_MAX_DIST = 128

LANES = 128
NSA_TQ = 256
CMP_NEAR = 32
CMP_PAD = 16
LOG2E = math.log2(math.e)
V_ROWS = NSA_DH + 16
LOWEST = -3.0e38
KNOCKED = -3.3e38

VMEM_LIMIT = 56 * 1024 * 1024


def _mm(a, b):
    return jnp.dot(a, b, preferred_element_type=F32)


def _mm_nt(a, b):
    return lax.dot_general(a, b, (((1,), (1,)), ((), ())), preferred_element_type=F32)


def _mm_tn(a, b):
    return lax.dot_general(a, b, (((0,), (0,)), ((), ())), preferred_element_type=F32)


def _rms(x, g):
    return x * lax.rsqrt(jnp.mean(x * x, axis=-1, keepdims=True) + EPS) * g


def _silu(x):
    return x * jax.nn.sigmoid(x)


def _const_spec(shape):
    nd = len(shape)
    return pl.BlockSpec(shape, lambda *_: (0,) * nd, pipeline_mode=pl.Buffered(1))


def _params(sem):
    return pltpu.CompilerParams(dimension_semantics=sem, vmem_limit_bytes=VMEM_LIMIT)


def _bucket_starts():
    n = np.arange(REL_MAX_DIST + 1)
    max_exact = REL_BUCKETS // 2
    nf = np.maximum(n, 1).astype(np.float32)
    large = max_exact + (np.log(nf / np.float32(max_exact)) / np.float32(math.log(REL_MAX_DIST / max_exact))
                         * np.float32(REL_BUCKETS - max_exact)).astype(np.int32)
    bucket = np.where(n < max_exact, n, np.minimum(large, REL_BUCKETS - 1))
    assert np.all(np.diff(bucket) >= 0) and bucket[-1] == REL_BUCKETS - 1
    return tuple(int(np.argmax(bucket >= b)) for b in range(REL_BUCKETS))


def _bias_kernel(tab_ref, rel_ref, out_ref, *, boxes, starts):
    h = pl.program_id(0)

    def lookup(rel):
        far = tab_ref[REL_BUCKETS - 1, h]
        val = jnp.full(rel.shape, tab_ref[0, h] - far, F32)
        for b in range(1, REL_BUCKETS - 1):
            val = jnp.where(rel >= starts[b], tab_ref[b, h] - far, val)
        val = jnp.where(rel >= starts[REL_BUCKETS - 1], 0.0, val)
        return jnp.where(rel < 0, NEG, val * LOG2E)

    for kind, box in enumerate(boxes):
        out_ref[0, kind] = jnp.where(rel_ref[kind] < 0, NEG, 0.0)
        if box is not None:
            r0, r1, c0, c1 = box
            out_ref[0, kind, r0:r1, c0:c1] = lookup(rel_ref[kind, r0:r1, c0:c1])


def _bias_tiles(rel_bias):
    tq = NSA_TQ
    r = np.arange(tq)[:, None]
    c = np.arange(tq)[None, :]
    d0 = r - c
    d1 = tq + r - c
    dw = np.where(c > r, WINDOW + r - c, -1)
    gc = np.where(c < CMP_NEAR, r - CMP_STRIDE * c + (CMP_STRIDE * CMP_PAD - (CMP_BLOCK - 1)), -1)
    rel_np = np.stack([d0.T, d1.T, dw.T, gc.T]).astype(np.int32)
    boxes = []
    for tile in rel_np:
        rr, cc = np.nonzero((tile >= 0) & (tile < REL_MAX_DIST))
        boxes.append(None if rr.size == 0 else tuple(int(v) for v in (
            rr.min() // 8 * 8, -(-(rr.max() + 1) // 8) * 8, cc.min() // LANES * LANES, -(-(cc.max() + 1) // LANES) * LANES)))
    rel = jnp.asarray(rel_np)
    nt = rel.shape[0]
    return pl.pallas_call(
        functools.partial(_bias_kernel, boxes=tuple(boxes), starts=_bucket_starts()),
        out_shape=jax.ShapeDtypeStruct((NSA_KV, nt, tq, NSA_HPG * tq), F32),
        grid=(NSA_HEADS,),
        in_specs=[pl.BlockSpec(memory_space=pltpu.SMEM),
                  _const_spec((nt, tq, tq))],
        out_specs=pl.BlockSpec((1, nt, tq, tq), lambda h: (h // NSA_HPG, 0, 0, h % NSA_HPG)),
        compiler_params=_params(("arbitrary",)),
        name="bias_tiles",
    )(rel_bias.astype(F32), rel)


def _memkv_kernel(mem_ref, g_ref, w_ref, out_ref):
    y = _rms(mem_ref[0], g_ref[...]).astype(BF16)
    out_ref[0] = _mm(y, w_ref[...]).astype(BF16)


def _memkv(mem, mem_norm_g, w_all):
    b, m, d = mem.shape
    n = w_all.shape[1]
    return pl.pallas_call(
        _memkv_kernel,
        out_shape=jax.ShapeDtypeStruct((b, m, n), BF16),
        grid=(b,),
        in_specs=[pl.BlockSpec((1, m, d), lambda i: (i, 0, 0)),
                  _const_spec((1, d)),
                  _const_spec((d, n))],
        out_specs=pl.BlockSpec((1, m, n), lambda i: (i, 0, 0)),
        compiler_params=_params(("arbitrary",)),
        name="mem_kv",
    )(mem, mem_norm_g.reshape(1, d), w_all)


def _mem_attention(xq, mk, mv):
    outs = []
    for hd in range(MEM_HEADS):
        sl = slice(hd * MEM_DH, (hd + 1) * MEM_DH)
        qm = (xq[:, sl] * (MEM_DH ** -0.5)).astype(BF16)
        s = _mm_nt(qm, mk[:, sl])
        p = jnp.exp(s - jnp.max(s, axis=-1, keepdims=True))
        l = jnp.sum(p, axis=-1, keepdims=True)
        outs.append(_mm(p.astype(BF16), mv[:, sl]) / l)
    return outs


EV_ZA = 0
EV_RQ = EV_ZA + POOL_WIDTH
EV_RK = EV_RQ + RET_HEADS * RET_DK
EV_RV = EV_RK + RET_HEADS * RET_DK
EV_XQ = EV_RV + RET_HEADS * RET_DV
EV_GA = EV_XQ + MEM_WIDTH
EV_GR = EV_GA + POOL_WIDTH
EV_GM = EV_GR + RET_HEADS * RET_DV
EV_COLS = EV_GM + MEM_WIDTH
EV_YA = 0
EV_YR = POOL_WIDTH
EV_YM = EV_YR + RET_HEADS * RET_DV
EV_YCOLS = EV_YM + MEM_WIDTH
EV_SUB = 256


def _even_kernel(gch_ref, h_ref, g_ref, win_ref, wbd_ref, pscale_ref, cos_ref, sin_ref, decay_ref, xi_ref,
                 zeta_ref, mk_ref, mv_ref, wout_ref, o_ref, ext_ref, state_ref, y_ref):
    si = pl.program_id(1)
    ts = h_ref.shape[1]

    @pl.when(si == 0)
    def _():
        ext_ref[0:POOL_HALO, :] = jnp.zeros((POOL_HALO, POOL_WIDTH), F32)
        state_ref[...] = jnp.zeros(state_ref.shape, F32)

    for r0 in range(0, ts, EV_SUB):
        _even_subtile(si * ts + r0, slice(r0, r0 + EV_SUB), gch_ref, h_ref, g_ref, win_ref, wbd_ref, pscale_ref,
                      cos_ref, sin_ref, decay_ref, xi_ref, zeta_ref, mk_ref, mv_ref, wout_ref, o_ref, ext_ref,
                      state_ref, y_ref)


def _even_subtile(t0, tile, gch_ref, h_ref, g_ref, win_ref, wbd_ref, pscale_ref, cos_ref, sin_ref, decay_ref, xi_ref,
                  zeta_ref, mk_ref, mv_ref, wout_ref, o_ref, ext_ref, state_ref, y_ref):
    ts = EV_SUB
    r0 = tile.start
    h = h_ref[0, tile]
    u = _rms(h, g_ref[...]).astype(BF16)

    def proj(start, width):
        return _mm(u, win_ref[:, start:start + width])

    ext_ref[POOL_HALO:, :] = proj(EV_ZA, POOL_WIDTH)
    e = ext_ref[...]
    s2 = e + pltpu.roll(e, 1, 0)
    s4 = s2 + pltpu.roll(s2, 2, 0)
    s8 = s4 + pltpu.roll(s4, 4, 0)
    s16 = s8 + pltpu.roll(s8, 8, 0)
    lane = lax.broadcasted_iota(jnp.int32, e.shape, 1)
    row = lax.broadcasted_iota(jnp.int32, e.shape, 0)
    tpos = t0 + row - POOL_HALO
    g0, g1, g2 = lane < POOL_GROUP, lane < 2 * POOL_GROUP, lane < 3 * POOL_GROUP
    wsum = jnp.where(g0, s2, jnp.where(g1, s4, jnp.where(g2, s8, s16)))
    wlen = jnp.where(g0, POOL_WINDOWS[0], jnp.where(g1, POOL_WINDOWS[1],
                                                    jnp.where(g2, POOL_WINDOWS[2], POOL_WINDOWS[3])))
    cnt = jnp.maximum(jnp.minimum(tpos + 1, wlen), 1).astype(F32)
    pooled = (wsum / cnt - e)[POOL_HALO:]
    ext_ref[0:POOL_HALO, :] = e[ts:ts + POOL_HALO]
    a = _mm(pooled.astype(BF16), wbd_ref[...]) * pscale_ref[...]
    y_ref[tile, EV_YA:EV_YA + POOL_WIDTH] = (a * _silu(proj(EV_GA, POOL_WIDTH))).astype(BF16)

    cos = cos_ref[tile]
    sin = sin_ref[tile]
    zv = proj(EV_RV, RET_HEADS * RET_DV).astype(BF16)
    gate_r = _silu(proj(EV_GR, RET_HEADS * RET_DV))
    q_rot, k_rot = [], []
    for hd in range(RET_HEADS):
        qh = proj(EV_RQ + hd * RET_DK, RET_DK)
        kh = proj(EV_RK + hd * RET_DK, RET_DK)
        q_rot.append((qh * cos + pltpu.roll(qh, RET_DK // 2, 1) * sin) * (RET_DK ** -0.5))
        k_rot.append(kh * cos + pltpu.roll(kh, RET_DK // 2, 1) * sin)
    for c in range(ts // RET_CHUNK):
        rows = slice(c * RET_CHUNK, (c + 1) * RET_CHUNK)
        normed = []
        for hd in range(RET_HEADS):
            qc, kc = q_rot[hd][rows], k_rot[hd][rows]
            vc = zv[rows, hd * RET_DV:(hd + 1) * RET_DV]
            att = _mm_nt(qc.astype(BF16), kc.astype(BF16)) * decay_ref[hd]
            state = state_ref[hd]
            o = _mm(att.astype(BF16), vc) + _mm((qc * xi_ref[hd]).astype(BF16), state.astype(BF16))
            kv = _mm_tn((kc * zeta_ref[hd]).astype(BF16), vc)
            state_ref[hd] = state * gch_ref[hd] + kv
            dlt = o - jnp.mean(o, axis=-1, keepdims=True)
            normed.append(dlt * lax.rsqrt(jnp.mean(dlt * dlt, axis=-1, keepdims=True) + EPS))
        y_ref[r0 + c * RET_CHUNK:r0 + (c + 1) * RET_CHUNK, EV_YR:EV_YM] = (
            jnp.concatenate(normed, axis=-1) * gate_r[rows]).astype(BF16)

    xq = proj(EV_XQ, MEM_WIDTH)
    gm = _silu(proj(EV_GM, MEM_WIDTH))
    for hd, om in enumerate(_mem_attention(xq, mk_ref[0], mv_ref[0])):
        sl = slice(hd * MEM_DH, (hd + 1) * MEM_DH)
        y_ref[tile, EV_YM + hd * MEM_DH:EV_YM + (hd + 1) * MEM_DH] = (om * gm[:, sl]).astype(BF16)

    o_ref[0, tile] = h + _mm(y_ref[tile], wout_ref[...])


def _retention_tables(s):
    half = RET_DK // 2
    inv = ROPE_BASE ** (-jnp.arange(half, dtype=F32) / half)
    ang = jnp.arange(s, dtype=F32)[:, None] * inv[None, :]
    cos, sin = jnp.cos(ang), jnp.sin(ang)
    cos_t = jnp.concatenate([cos, cos], axis=-1)
    sin_t = jnp.concatenate([-sin, sin], axis=-1)
    c = RET_CHUNK
    log_g = jnp.log(1.0 - jnp.exp2(-5.0 - jnp.arange(RET_HEADS, dtype=F32)))
    n = jnp.arange(c, dtype=F32)
    diff = n[:, None] - n[None, :]
    decay = jnp.where(diff >= 0, jnp.exp(log_g[:, None, None] * jnp.maximum(diff, 0.0)), 0.0)
    xi = jnp.exp(log_g[:, None] * (n + 1.0))
    zeta = jnp.exp(log_g[:, None] * (c - 1.0 - n))
    g_chunk = jnp.exp(log_g * c)
    xi_t = jnp.broadcast_to(xi[:, :, None], (RET_HEADS, c, RET_DK))
    zeta_t = jnp.broadcast_to(zeta[:, :, None], (RET_HEADS, c, RET_DK))
    return cos_t, sin_t, decay, xi_t, zeta_t, g_chunk


def _even_layer(h, memkv, layer, g, w_in, pool_w, pool_scale, w_out, ts=2 * EV_SUB):
    b, s, d = h.shape
    assert w_in.shape == (d, EV_COLS) and w_out.shape == (EV_YCOLS, d)
    win = w_in.astype(BF16)
    wout = w_out.astype(BF16)
    wbd = jnp.zeros((POOL_WIDTH, POOL_WIDTH), F32)
    for gi in range(len(POOL_WINDOWS)):
        sl = slice(gi * POOL_GROUP, (gi + 1) * POOL_GROUP)
        wbd = wbd.at[sl, sl].set(pool_w[gi])
    wbd = wbd.astype(BF16)
    cos_t, sin_t, decay, xi_t, zeta_t, g_chunk = _retention_tables(s)
    kv_blk = 2 * layer
    return pl.pallas_call(
        _even_kernel,
        out_shape=jax.ShapeDtypeStruct((b, s, d), F32),
        grid=(b, s // ts),
        in_specs=[pl.BlockSpec(memory_space=pltpu.SMEM),
                  pl.BlockSpec((1, ts, d), lambda i, j: (i, j, 0)),
                  _const_spec((1, d)),
                  _const_spec((d, EV_COLS)),
                  _const_spec((POOL_WIDTH, POOL_WIDTH)),
                  _const_spec((1, POOL_WIDTH)),
                  pl.BlockSpec((ts, RET_DK), lambda i, j: (j, 0)),
                  pl.BlockSpec((ts, RET_DK), lambda i, j: (j, 0)),
                  _const_spec((RET_HEADS, RET_CHUNK, RET_CHUNK)),
                  _const_spec((RET_HEADS, RET_CHUNK, RET_DK)),
                  _const_spec((RET_HEADS, RET_CHUNK, RET_DK)),
                  pl.BlockSpec((1, N_MEM, MEM_WIDTH), lambda i, j: (i, 0, kv_blk)),
                  pl.BlockSpec((1, N_MEM, MEM_WIDTH), lambda i, j: (i, 0, kv_blk + 1)),
                  _const_spec((EV_YCOLS, d))],
        out_specs=pl.BlockSpec((1, ts, d), lambda i, j: (i, j, 0)),
        scratch_shapes=[pltpu.VMEM((POOL_HALO + EV_SUB, POOL_WIDTH), F32),
                        pltpu.VMEM((RET_HEADS, RET_DK, RET_DV), F32),
                        pltpu.VMEM((ts, EV_YCOLS), BF16)],
        compiler_params=_params(("arbitrary", "arbitrary")),
        name="even_layer",
    )(g_chunk, h, g.reshape(1, d), win, wbd, pool_scale.reshape(1, POOL_WIDTH), cos_t, sin_t, decay, xi_t, zeta_t,
      memkv, memkv, wout)


KV_W = NSA_KV * NSA_DH
OD_Q = 0
OD_KS = OD_Q + NSA_HEADS * NSA_DH
OD_VS = OD_KS + KV_W
OD_KW = OD_VS + KV_W
OD_VW = OD_KW + KV_W
OD_KC = OD_VW + KV_W
OD_VC = OD_KC + KV_W
OD_GL = OD_VC + KV_W
OD_XQ = OD_GL + NSA_KV * LANES
OD_GN = OD_XQ + MEM_WIDTH
OD_GM = OD_GN + NSA_HEADS * NSA_DH
OD_COLS = OD_GM + MEM_WIDTH


def _odd_in_kernel(h_ref, g_ref, win_ref, pe_ref, perm_ref, mk_ref, mv_ref,
                   q_ref, kaug_ref, vst_ref, kw_ref, vwt_ref, kca_ref, kcb_ref, vca_ref, vcb_ref, gatest_ref,
                   sg_ref, ym_ref):
    si = pl.program_id(1)
    n_sub = h_ref.shape[1] // NSA_TQ
    for sub in range(n_sub):
        _odd_in_subtile(si * n_sub + sub, sub, h_ref, g_ref, win_ref, pe_ref, perm_ref, mk_ref, mv_ref,
                        q_ref, kaug_ref, vst_ref, kw_ref, vwt_ref, kca_ref, kcb_ref, vca_ref, vcb_ref, gatest_ref,
                        sg_ref, ym_ref)


def _odd_in_subtile(tile_idx, sub, h_ref, g_ref, win_ref, pe_ref, perm_ref, mk_ref, mv_ref,
                    q_ref, kaug_ref, vst_ref, kw_ref, vwt_ref, kca_ref, kcb_ref, vca_ref, vcb_ref, gatest_ref,
                    sg_ref, ym_ref):
    ts = NSA_TQ
    tile = slice(sub * ts, (sub + 1) * ts)
    u = _rms(h_ref[0, tile], g_ref[...]).astype(BF16)

    def proj(start, width):
        return _mm(u, win_ref[:, start:start + width])

    zq = proj(OD_Q, NSA_HEADS * NSA_DH) * (NSA_DH ** -0.5 * LOG2E)
    zg = _silu(proj(OD_GN, NSA_HEADS * NSA_DH))
    for g in range(NSA_KV):
        for hh in range(NSA_HPG):
            sl = slice((g * NSA_HPG + hh) * NSA_DH, (g * NSA_HPG + hh + 1) * NSA_DH)
            q_ref[0, g, hh, tile] = zq[:, sl].astype(BF16)
            sg_ref[0, g, hh, tile] = zg[:, sl]

    zkv = proj(OD_KS, 6 * KV_W)
    lane = lax.broadcasted_iota(jnp.int32, (ts, LANES), 1)
    blk = (tile_idx * ts + lax.broadcasted_iota(jnp.int32, (ts, LANES), 0)) // SLC_BLOCK
    onehot = jnp.where(lane == blk, 1.0, 0.0).astype(BF16)
    ones_rows = jnp.where(lax.broadcasted_iota(jnp.int32, (V_ROWS - NSA_DH, ts), 0) == 0, 1.0, 0.0).astype(BF16)
    zgl = jax.nn.sigmoid(proj(OD_GL, NSA_KV * LANES))
    flat_in = []
    for g in range(NSA_KV):
        def piece(idx):
            off = idx * KV_W + g * NSA_DH
            return zkv[:, off:off + NSA_DH]
        kaug_ref[0, g, tile, 0:NSA_DH] = piece(0).astype(BF16)
        kaug_ref[0, g, tile, NSA_DH:2 * NSA_DH] = onehot
        vst_ref[0, g, sub, 0:NSA_DH] = piece(1).T.astype(BF16)
        vst_ref[0, g, sub, NSA_DH:V_ROWS] = ones_rows
        kw_ref[0, g, tile] = piece(2).astype(BF16)
        vwt_ref[0, g, sub, 0:NSA_DH] = piece(3).T.astype(BF16)
        vwt_ref[0, g, sub, NSA_DH:V_ROWS] = ones_rows
        kc, vc = piece(4), piece(5)
        flat_in += [(kc + pe_ref[0, 0]).astype(BF16), (kc + pe_ref[0, 1]).astype(BF16),
                    (vc + pe_ref[1, 0]).astype(BF16), (vc + pe_ref[1, 1]).astype(BF16)]
        gatest_ref[0, g, :, tile] = zgl[:, g * LANES:(g + 1) * LANES].T

    nj = ts // CMP_STRIDE
    perm = _mm(perm_ref[...], jnp.concatenate(flat_in, axis=1)).astype(BF16)
    for c, out_ref in enumerate((kca_ref, kcb_ref, vca_ref, vcb_ref) * NSA_KV):
        for l in range(CMP_STRIDE):
            out_ref[0, c // 4, sub * nj:(sub + 1) * nj, l * NSA_DH:(l + 1) * NSA_DH] = (
                perm[l * nj:(l + 1) * nj, c * NSA_DH:(c + 1) * NSA_DH])

    xq = proj(OD_XQ, MEM_WIDTH)
    gm = _silu(proj(OD_GM, MEM_WIDTH))
    for hd, om in enumerate(_mem_attention(xq, mk_ref[0], mv_ref[0])):
        sl = slice(hd * MEM_DH, (hd + 1) * MEM_DH)
        ym_ref[0, tile, sl] = (om * gm[:, sl]).astype(BF16)


def _odd_in(h, memkv, layer, g, w_in, cmp_pe, ts=2 * NSA_TQ):
    b, s, d = h.shape
    sizes = [NSA_HEADS * NSA_DH] + [KV_W] * 6 + [3 * NSA_HEADS, MEM_WIDTH]
    q, kc, vc, ks, vs, kw, vw, gl, xq, gate = jnp.split(w_in, np.cumsum(sizes).tolist(), axis=1)
    gl = gl.reshape(d, 3, NSA_KV, NSA_HPG).transpose(0, 2, 1, 3).reshape(d, NSA_KV, 3 * NSA_HPG)
    gl = jnp.pad(gl, ((0, 0), (0, 0), (0, LANES - 3 * NSA_HPG))).reshape(d, NSA_KV * LANES)
    gn, gm = jnp.split(gate, [NSA_HEADS * NSA_DH], axis=1)
    win = jnp.concatenate([w.astype(BF16) for w in (q, ks, vs, kw, vw, kc, vc, gl, xq, gn, gm)], axis=1)
    sub = NSA_TQ
    reps = sub // CMP_STRIDE
    pe = jnp.stack([jnp.stack([jnp.tile(cmp_pe[kv, :CMP_STRIDE], (reps, 1)),
                               jnp.tile(cmp_pe[kv, CMP_STRIDE:], (reps, 1))]) for kv in range(2)])
    kv_blk = 2 * layer
    assert ts % sub == 0
    nj = sub // CMP_STRIDE
    perm = np.zeros((sub, sub), np.float32)
    perm[np.arange(sub), (np.arange(sub) % nj) * CMP_STRIDE + np.arange(sub) // nj] = 1.0
    head_t = jax.ShapeDtypeStruct((b, NSA_KV, s, NSA_DH), BF16)
    head_spec = pl.BlockSpec((1, NSA_KV, ts, NSA_DH), lambda i, j: (i, 0, j, 0))
    headt_t = jax.ShapeDtypeStruct((b, NSA_KV, s // sub, V_ROWS, sub), BF16)
    headt_spec = pl.BlockSpec((1, NSA_KV, ts // sub, V_ROWS, sub), lambda i, j: (i, 0, j, 0, 0))
    flat_t = jax.ShapeDtypeStruct((b, NSA_KV, s // CMP_STRIDE, CMP_STRIDE * NSA_DH), BF16)
    flat_spec = pl.BlockSpec((1, NSA_KV, ts // CMP_STRIDE, CMP_STRIDE * NSA_DH), lambda i, j: (i, 0, j, 0))
    qlike_spec = pl.BlockSpec((1, NSA_KV, NSA_HPG, ts, NSA_DH), lambda i, j: (i, 0, 0, j, 0))
    return pl.pallas_call(
        _odd_in_kernel,
        out_shape=[jax.ShapeDtypeStruct((b, NSA_KV, NSA_HPG, s, NSA_DH), BF16),
                   jax.ShapeDtypeStruct((b, NSA_KV, s, 2 * NSA_DH), BF16),
                   headt_t, head_t, headt_t, flat_t, flat_t, flat_t, flat_t,
                   jax.ShapeDtypeStruct((b, NSA_KV, LANES, s), F32),
                   jax.ShapeDtypeStruct((b, NSA_KV, NSA_HPG, s, NSA_DH), F32),
                   jax.ShapeDtypeStruct((b, s, MEM_WIDTH), BF16)],
        grid=(b, s // ts),
        in_specs=[pl.BlockSpec((1, ts, d), lambda i, j: (i, j, 0)),
                  _const_spec((1, d)),
                  _const_spec((d, OD_COLS)),
                  _const_spec((2, 2, sub, NSA_DH)),
                  _const_spec((sub, sub)),
                  pl.BlockSpec((1, N_MEM, MEM_WIDTH), lambda i, j: (i, 0, kv_blk)),
                  pl.BlockSpec((1, N_MEM, MEM_WIDTH), lambda i, j: (i, 0, kv_blk + 1))],
        out_specs=[qlike_spec,
                   pl.BlockSpec((1, NSA_KV, ts, 2 * NSA_DH), lambda i, j: (i, 0, j, 0)),
                   headt_spec, head_spec, headt_spec, flat_spec, flat_spec, flat_spec, flat_spec,
                   pl.BlockSpec((1, NSA_KV, LANES, ts), lambda i, j: (i, 0, 0, j)),
                   qlike_spec,
                   pl.BlockSpec((1, ts, MEM_WIDTH), lambda i, j: (i, j, 0))],
        compiler_params=_params(("arbitrary", "arbitrary")),
        name="odd_in_proj",
    )(h, g.reshape(1, d), win, pe, jnp.asarray(perm, BF16), memkv, memkv)


def _compress_kernel(xka_ref, xkb_ref, xva_ref, xvb_ref, w1a_ref, w1b_ref, b1_ref, w2_ref, ovt_ref, kc_ref, vct_ref):
    n = xka_ref.shape[2]
    k_cols = vct_ref.shape[3]

    def block_mlp(kv, xa, xb):
        first = _mm(xa[0, 0], w1a_ref[kv])
        second = _mm(xb[0, 0], w1b_ref[kv])
        hid = first + pltpu.roll(second, n - 1, 0) + b1_ref[kv]
        return _mm(_silu(hid).astype(BF16), w2_ref[kv])

    kc_ref[0, 0, 0:CMP_PAD] = jnp.zeros((CMP_PAD, NSA_DH), BF16)
    kc_ref[0, 0, CMP_PAD:] = block_mlp(0, xka_ref, xkb_ref).astype(BF16)
    vc = jnp.concatenate([jnp.zeros((CMP_PAD, NSA_DH), F32), block_mlp(1, xva_ref, xvb_ref),
                          jnp.zeros((k_cols - CMP_PAD - n, NSA_DH), F32)], axis=0)
    vct_ref[0, 0, 0:NSA_DH] = vc.T.astype(BF16)
    vct_ref[0, 0, NSA_DH:] = ovt_ref[...]


def _compress(kca, kcb, vca, vcb, w1, b1, w2):
    b, g, n, half = kca.shape
    dh = half // CMP_STRIDE
    s = n * CMP_STRIDE
    xs = (kca, kcb, vca, vcb)
    w1 = w1.astype(BF16)
    ovt = _overlap_table(s)
    k_cols = ovt.shape[1]
    x_spec = pl.BlockSpec((1, 1, n, half), lambda i, j: (i, j, 0, 0))
    return pl.pallas_call(
        _compress_kernel,
        out_shape=[jax.ShapeDtypeStruct((b, g, n + CMP_PAD, dh), BF16),
                   jax.ShapeDtypeStruct((b, g, dh + LANES, k_cols), BF16)],
        grid=(b, g),
        in_specs=[x_spec, x_spec, x_spec, x_spec,
                  _const_spec((2, half, CMP_HIDDEN)), _const_spec((2, half, CMP_HIDDEN)),
                  _const_spec((2, 1, CMP_HIDDEN)), _const_spec((2, CMP_HIDDEN, dh)),
                  _const_spec((LANES, k_cols))],
        out_specs=[pl.BlockSpec((1, 1, n + CMP_PAD, dh), lambda i, j: (i, j, 0, 0)),
                   pl.BlockSpec((1, 1, dh + LANES, k_cols), lambda i, j: (i, j, 0, 0))],
        compiler_params=_params(("arbitrary", "arbitrary")),
        name="compress",
    )(*xs, w1[:, :half], w1[:, half:], b1.reshape(2, 1, CMP_HIDDEN), w2.astype(BF16), ovt)


def _nsa_kernel(q_ref, kaug_ref, vst_ref, kw_ref, vwt_ref, kcmp_ref, vcmpt_ref, gatest_ref, sg_ref,
                d0_ref, d1_ref, dw_ref, gc_ref, y_ref, qaug_ref, m_ref, acc_ref, s_ref, sc_ref, mw_ref, accw_ref, sw_ref,
                mix_ref, tmax_ref, tmaxw_ref):
    i = pl.program_id(2)
    tq = NSA_TQ
    rows = NSA_HPG * tq
    s_len = kw_ref.shape[2]
    n_cmp = s_len // CMP_STRIDE
    n_slc = s_len // SLC_BLOCK
    t0 = i * tq
    q = q_ref[0, 0].reshape(rows, NSA_DH)


    def stream(mx_ref, ac_ref, sc2_ref, tmax_ref):
        def reset():
            mx_ref[...] = jnp.full(mx_ref.shape, LOWEST, F32)
            ac_ref[...] = jnp.zeros(ac_ref.shape, F32)

        def s_to(slot, k_ref, query, tile, bias=None):
            k0 = pl.multiple_of(tile * tq, tq)
            st = _mm_nt(k_ref[0, 0, pl.ds(k0, tq)], query)
            if bias is not None:
                st = st + bias
            sc2_ref[slot] = st
            tmax_ref[slot] = jnp.max(st, axis=0, keepdims=True)

        def pv_from(slot, vt):
            st = sc2_ref[slot]
            m_prev = mx_ref[...]
            m_new = jnp.maximum(m_prev, tmax_ref[slot])
            pt = jnp.exp2(st - m_new).astype(BF16)
            ac_ref[...] = jnp.exp2(m_prev - m_new) * ac_ref[...] + _mm(vt, pt)
            mx_ref[...] = m_new

        def finish(cols=slice(None)):
            return ac_ref[0:NSA_DH, cols] / ac_ref[NSA_DH:NSA_DH + 1, cols]

        return reset, s_to, pv_from, finish

    def off_unless(cond):
        return jnp.where(cond, 0.0, NEG)

    prev1 = jnp.maximum(i - 1, 0)
    prev2 = jnp.maximum(i - 2, 0)
    w_reset, w_s_to, w_pv_from, w_finish = stream(mw_ref, accw_ref, sw_ref, tmaxw_ref)


    n_pad = kcmp_ref.shape[2]
    j_near = pl.multiple_of(i * (tq // CMP_STRIDE), tq // CMP_STRIDE)
    qaug_ref[:, 0:NSA_DH] = q
    qaug_ref[:, NSA_DH:] = jnp.where(lax.broadcasted_iota(jnp.int32, (rows, NSA_DH), 1) == 0, 1.0, 0.0).astype(BF16)
    krow = lax.broadcasted_iota(jnp.int32, (n_pad, NSA_DH), 0)
    key_bias = jnp.where((krow >= CMP_PAD) & (krow < j_near + CMP_NEAR), 0.0, NEG).astype(BF16)
    sc_ref[...] = _mm_nt(jnp.concatenate([kcmp_ref[0, 0], key_bias], axis=1), qaug_ref[...])
    w_reset()
    w_s_to(0, kw_ref, q, prev2, dw_ref[0, 0] + off_unless(i >= 2))
    w_s_to(1, kw_ref, q, prev1, d1_ref[0, 0] + off_unless(i >= 1))
    sc_ref[pl.ds(j_near, CMP_NEAR)] = sc_ref[pl.ds(j_near, CMP_NEAR)] + gc_ref[0, 0, 0:CMP_NEAR, :]
    s_cmp = sc_ref[...]
    p_cmp = jnp.exp2(s_cmp - jnp.max(s_cmp, axis=0, keepdims=True))
    tcol = t0 + jnp.bitwise_and(lax.broadcasted_iota(jnp.int32, (1, rows), 1), tq - 1)
    scale = jnp.where(tcol >= CMP_BLOCK - 1, 1.0, 0.0) / jnp.sum(p_cmp, axis=0, keepdims=True)
    p_cmp = p_cmp.astype(BF16)
    k_cols = vcmpt_ref.shape[3]
    if k_cols > n_pad:
        p_cmp = jnp.concatenate([p_cmp, jnp.zeros((k_cols - n_pad, rows), BF16)], axis=0)
    w_pv_from(0, vwt_ref[0, 0, prev2])
    both = _mm(vcmpt_ref[0, 0], p_cmp) * scale
    w_s_to(0, kw_ref, q, i, d0_ref[0, 0])
    w_pv_from(1, vwt_ref[0, 0, prev1])
    gtst = gatest_ref[0, 0]
    n_blk = -(-n_slc // 8) * 8
    imp = both[NSA_DH:NSA_DH + n_blk, 0:tq]
    for hh in range(NSA_HPG):
        hs = slice(hh * tq, (hh + 1) * tq)
        mix_ref[:, hs] = gtst[hh:hh + 1] * both[0:NSA_DH, hs]
        if hh:
            imp = imp + both[NSA_DH:NSA_DH + n_blk, hs]

    blk = lax.broadcasted_iota(jnp.int32, (n_blk, tq), 0)
    tq_pos = t0 + lax.broadcasted_iota(jnp.int32, (n_blk, tq), 1)
    cur = tq_pos // SLC_BLOCK
    forced = (blk == 0) | (blk == cur) | (blk == cur - 1)
    future = blk * SLC_BLOCK > tq_pos
    imp = jnp.where(forced, BIG, jnp.where(future, -BIG, imp))
    imp = jnp.where(blk < n_slc, imp, LOWEST)
    blk_f = blk.astype(F32)
    sel = jnp.zeros((n_blk, tq), jnp.bool_)
    for _ in range(SLC_TOPK):
        top = jnp.max(imp, axis=0, keepdims=True)
        first = jnp.min(jnp.where(imp == top, blk_f, float(LANES)), axis=0, keepdims=True)
        hit = blk_f == first
        sel = sel | hit
        imp = jnp.where(hit, KNOCKED, imp)
    sneg_t = jnp.where(sel, 0.0, NEG)
    if n_blk < LANES:
        sneg_t = jnp.concatenate([sneg_t, jnp.zeros((LANES - n_blk, tq), F32)], axis=0)
    sneg = sneg_t.T.astype(BF16)
    for hh in range(NSA_HPG):
        qaug_ref[hh * tq:(hh + 1) * tq, NSA_DH:2 * NSA_DH] = sneg
    w_pv_from(0, vwt_ref[0, 0, i])
    for hh in range(NSA_HPG):
        hs = slice(hh * tq, (hh + 1) * tq)
        mix_ref[:, hs] = mix_ref[:, hs] + gtst[2 * NSA_HPG + hh:2 * NSA_HPG + hh + 1] * w_finish(hs)

    reset, s_to, pv_from, finish = stream(m_ref, acc_ref, s_ref, tmax_ref)
    reset()
    n_far = prev1
    n_pairs = jnp.maximum(n_far - 1, 0) // 2
    s_to(0, kaug_ref, qaug_ref[...], 0, off_unless(n_far >= 1))

    def far_pair(t2, carry):
        s_to(1, kaug_ref, qaug_ref[...], 2 * t2 + 1)
        pv_from(0, vst_ref[0, 0, 2 * t2])
        s_to(0, kaug_ref, qaug_ref[...], 2 * t2 + 2)
        pv_from(1, vst_ref[0, 0, 2 * t2 + 1])
        return carry

    lax.fori_loop(0, n_pairs, far_pair, 0)
    c0 = 2 * n_pairs
    two_left = n_far - c0 == 2

    @pl.when(two_left)
    def _():
        s_to(1, kaug_ref, qaug_ref[...], c0 + 1)
        pv_from(0, vst_ref[0, 0, c0])
        s_to(0, kaug_ref, qaug_ref[...], prev1, d1_ref[0, 0])
        pv_from(1, vst_ref[0, 0, c0 + 1])
        s_to(1, kaug_ref, qaug_ref[...], i, d0_ref[0, 0])
        pv_from(0, vst_ref[0, 0, prev1])
        pv_from(1, vst_ref[0, 0, i])

    @pl.when(jnp.logical_not(two_left))
    def _():
        s_to(1, kaug_ref, qaug_ref[...], prev1, d1_ref[0, 0] + off_unless(i >= 1))
        pv_from(0, vst_ref[0, 0, c0])
        s_to(0, kaug_ref, qaug_ref[...], i, d0_ref[0, 0])
        pv_from(1, vst_ref[0, 0, prev1])
        pv_from(0, vst_ref[0, 0, i])

    for hh in range(NSA_HPG):
        hs = slice(hh * tq, (hh + 1) * tq)
        ot = mix_ref[:, hs] + gatest_ref[0, 0, NSA_HPG + hh:NSA_HPG + hh + 1] * finish(hs)
        y_ref[0, :, hh * NSA_DH:(hh + 1) * NSA_DH] = (ot.T * sg_ref[0, 0, hh]).astype(BF16)


def _overlap_table(s):
    n_cmp = (s - CMP_BLOCK) // CMP_STRIDE + 1
    n_slc = s // SLC_BLOCK
    cst = np.arange(n_cmp)[:, None] * CMP_STRIDE
    sst = np.arange(n_slc)[None, :] * SLC_BLOCK
    ov = np.clip(np.minimum(cst + CMP_BLOCK, sst + SLC_BLOCK) - np.maximum(cst, sst), 0, None) / CMP_STRIDE
    k_cols = -(-(CMP_PAD + s // CMP_STRIDE) // LANES) * LANES
    full = np.zeros((LANES, k_cols), np.float32)
    full[:n_slc, CMP_PAD:CMP_PAD + n_cmp] = ov.T
    return jnp.asarray(full, BF16)


def _nsa(q, kaug, vst, kw, vwt, kcmp, vcmpt, gatest, sg, bias):
    b, g, hpg, s, dh = q.shape
    tq = NSA_TQ
    assert WINDOW == 2 * tq and s % tq == 0 and s // SLC_BLOCK <= LANES and s // SLC_BLOCK >= SLC_TOPK
    n_pad = kcmp.shape[2]
    rows = hpg * tq
    seq_spec = pl.BlockSpec((1, 1, s, dh), lambda i, j, k: (i, j, 0, 0))
    seqt_spec = pl.BlockSpec((1, 1, s // tq, V_ROWS, tq), lambda i, j, k: (i, j, 0, 0, 0))
    qlike_spec = pl.BlockSpec((1, 1, hpg, tq, dh), lambda i, j, k: (i, j, 0, k, 0))

    def bias_spec(idx):
        return pl.BlockSpec((1, 1, tq, rows), lambda i, j, k: (j, idx, 0, 0))

    return pl.pallas_call(
        _nsa_kernel,
        out_shape=jax.ShapeDtypeStruct((b, s, g * hpg * dh), BF16),
        grid=(b, g, s // tq),
        in_specs=[qlike_spec,
                  pl.BlockSpec((1, 1, s, 2 * dh), lambda i, j, k: (i, j, 0, 0)),
                  seqt_spec, seq_spec, seqt_spec,
                  pl.BlockSpec((1, 1, n_pad, dh), lambda i, j, k: (i, j, 0, 0)),
                  pl.BlockSpec((1, 1, dh + LANES, vcmpt.shape[3]), lambda i, j, k: (i, j, 0, 0)),
                  pl.BlockSpec((1, 1, LANES, tq), lambda i, j, k: (i, j, 0, k)),
                  qlike_spec,
                  bias_spec(0), bias_spec(1), bias_spec(2), bias_spec(3)],
        out_specs=pl.BlockSpec((1, tq, hpg * dh), lambda i, j, k: (i, k, j)),
        scratch_shapes=[pltpu.VMEM((rows, 2 * dh), BF16),
                        pltpu.VMEM((1, rows), F32),
                        pltpu.VMEM((V_ROWS, rows), F32),
                        pltpu.VMEM((2, tq, rows), F32),
                        pltpu.VMEM((n_pad, rows), F32),
                        pltpu.VMEM((1, rows), F32),
                        pltpu.VMEM((V_ROWS, rows), F32),
                        pltpu.VMEM((2, tq, rows), F32),
                        pltpu.VMEM((dh, rows), F32),
                        pltpu.VMEM((2, 1, rows), F32),
                        pltpu.VMEM((2, 1, rows), F32)],
        compiler_params=_params(("arbitrary", "arbitrary", "arbitrary")),
        name="nsa_attention",
    )(q, kaug, vst, kw, vwt, kcmp, vcmpt, gatest, sg, bias, bias, bias, bias)


def _odd_out_kernel(h_ref, yn_ref, ym_ref, wn_ref, wm_ref, fg_ref, o_ref, *, final_norm, sub):
    for r0 in range(0, h_ref.shape[0], sub):
        rows = slice(r0, r0 + sub)
        out = h_ref[rows] + _mm(yn_ref[rows], wn_ref[...]) + _mm(ym_ref[rows], wm_ref[...])
        o_ref[rows] = _rms(out, fg_ref[...]) if final_norm else out


def _odd_out(h, yn, ym, w_out, final_g, final_norm, tm=1024, sub=256):
    b, s, d = h.shape
    t = b * s
    nw = NSA_HEADS * NSA_DH
    wout = w_out.astype(BF16)
    out = pl.pallas_call(
        functools.partial(_odd_out_kernel, final_norm=final_norm, sub=sub),
        out_shape=jax.ShapeDtypeStruct((t, d), F32),
        grid=(t // tm,),
        in_specs=[pl.BlockSpec((tm, d), lambda i: (i, 0)),
                  pl.BlockSpec((tm, nw), lambda i: (i, 0)),
                  pl.BlockSpec((tm, MEM_WIDTH), lambda i: (i, 0)),
                  _const_spec((nw, d)), _const_spec((MEM_WIDTH, d)), _const_spec((1, d))],
        out_specs=pl.BlockSpec((tm, d), lambda i: (i, 0)),
        compiler_params=_params(("arbitrary",)),
        name="odd_out_proj",
    )(h.reshape(t, d), yn.reshape(t, nw), ym.reshape(t, MEM_WIDTH), wout[:nw], wout[nw:], final_g.reshape(1, d))
    return out.reshape(b, s, d)


def _final_norm_kernel(h_ref, g_ref, o_ref):
    o_ref[...] = _rms(h_ref[...], g_ref[...])


def _final_norm(h, final_g, tm=512):
    b, s, d = h.shape
    t = b * s
    out = pl.pallas_call(
        _final_norm_kernel,
        out_shape=jax.ShapeDtypeStruct((t, d), F32),
        grid=(t // tm,),
        in_specs=[pl.BlockSpec((tm, d), lambda i: (i, 0)), _const_spec((1, d))],
        out_specs=pl.BlockSpec((tm, d), lambda i: (i, 0)),
        compiler_params=_params(("arbitrary",)),
        name="final_norm",
    )(h.reshape(t, d), final_g.reshape(1, d))
    return out.reshape(b, s, d)


def kernel(x, mem, norm_g, final_g, mem_norm_g, rel_bias, ev_w_in, ev_pool_w, ev_pool_scale, ev_w_mem_kv, ev_w_out,
           od_w_in, od_cmp_pe, od_cmp_w1, od_cmp_b1, od_cmp_w2, od_w_mem_kv, od_w_out):
    depth = norm_g.shape[0]
    w_mem = [(ev_w_mem_kv if i % 2 == 0 else od_w_mem_kv)[i // 2] for i in range(depth)]
    memkv = _memkv(mem, mem_norm_g, jnp.concatenate(w_mem, axis=1).astype(BF16))
    bias = _bias_tiles(rel_bias) if depth > 1 else None
    h = x
    for i in range(depth):
        j = i // 2
        last = i == depth - 1
        if i % 2 == 0:
            h = _even_layer(h, memkv, i, norm_g[i], ev_w_in[j], ev_pool_w[j], ev_pool_scale[j], ev_w_out[j])
            if last:
                h = _final_norm(h, final_g)
        else:
            (q, kaug, vst, kw, vwt, kca, kcb, vca, vcb, gatest, sg, ym) = _odd_in(
                h, memkv, i, norm_g[i], od_w_in[j], od_cmp_pe[j])
            kcmp, vcmp = _compress(kca, kcb, vca, vcb, od_cmp_w1[j], od_cmp_b1[j], od_cmp_w2[j])
            yn = _nsa(q, kaug, vst, kw, vwt, kcmp, vcmp, gatest, sg, bias)
            h = _odd_out(h, yn, ym, od_w_out[j], final_g, last)
    return h
```

```python
import functools
import math

import numpy as np
import jax
import jax.numpy as jnp
from jax import lax
from jax.experimental import pallas as pl
from jax.experimental.pallas import tpu as pltpu

F32 = jnp.float32
BF16 = jnp.bfloat16

D_MODEL = 1024
D_INNER = 2048
N_MEM = 256
EPS = 1e-6
NEG = -1e30
BIG = 1e30

POOL_WINDOWS = (2, 4, 8, 16)
POOL_WIDTH = 768
POOL_GROUP = 192
POOL_HALO = 16

RET_HEADS = 4
RET_DK = 128
RET_DV = 192
RET_CHUNK = 128
ROPE_BASE = 10000.0

MEM_HEADS = 4
MEM_DH = 128
MEM_WIDTH = 512

NSA_HEADS = 12
NSA_KV = 2
NSA_HPG = 6
NSA_DH = 128
CMP_BLOCK = 32
CMP_STRIDE = 16
CMP_HIDDEN = 256
SLC_BLOCK = 64
SLC_TOPK = 8
WINDOW = 512
REL_BUCKETS = 32
REL_MAX_DIST = 128

LANES = 128
NSA_TQ = 256
CMP_NEAR = 32
CMP_PAD = 16
LOG2E = math.log2(math.e)
V_ROWS = NSA_DH + 16
LOWEST = -3.0e38
KNOCKED = -3.3e38
BIAS_MASKED = 4

VMEM_LIMIT = 56 * 1024 * 1024


def _mm(a, b):
    return jnp.dot(a, b, preferred_element_type=F32)


def _mm_nt(a, b):
    return lax.dot_general(a, b, (((1,), (1,)), ((), ())), preferred_element_type=F32)


def _mm_tn(a, b):
    return lax.dot_general(a, b, (((0,), (0,)), ((), ())), preferred_element_type=F32)


def _rms(x, g):
    return x * lax.rsqrt(jnp.mean(x * x, axis=-1, keepdims=True) + EPS) * g


def _silu(x):
    return x * jax.nn.sigmoid(x)


def _const_spec(shape):
    nd = len(shape)
    return pl.BlockSpec(shape, lambda *_: (0,) * nd, pipeline_mode=pl.Buffered(1))


def _params(sem):
    return pltpu.CompilerParams(dimension_semantics=sem, vmem_limit_bytes=VMEM_LIMIT)


def _bucket_starts():
    n = np.arange(REL_MAX_DIST + 1)
    max_exact = REL_BUCKETS // 2
    nf = np.maximum(n, 1).astype(np.float32)
    large = max_exact + (np.log(nf / np.float32(max_exact)) / np.float32(math.log(REL_MAX_DIST / max_exact))
                         * np.float32(REL_BUCKETS - max_exact)).astype(np.int32)
    bucket = np.where(n < max_exact, n, np.minimum(large, REL_BUCKETS - 1))
    assert np.all(np.diff(bucket) >= 0) and bucket[-1] == REL_BUCKETS - 1
    return tuple(int(np.argmax(bucket >= b)) for b in range(REL_BUCKETS))


def _bias_kernel(tab_ref, rel_ref, out_ref, *, boxes, starts):
    h = pl.program_id(0)

    def lookup(rel):
        far = tab_ref[REL_BUCKETS - 1, h]
        val = jnp.full(rel.shape, tab_ref[0, h] - far, F32)
        for b in range(1, REL_BUCKETS - 1):
            val = jnp.where(rel >= starts[b], tab_ref[b, h] - far, val)
        val = jnp.where(rel >= starts[REL_BUCKETS - 1], 0.0, val)
        return jnp.where(rel < 0, NEG, val * LOG2E)

    for kind, box in enumerate(boxes):
        out_ref[0, kind] = jnp.where(rel_ref[kind] < 0, NEG, 0.0)
        if box is not None:
            r0, r1, c0, c1 = box
            out_ref[0, kind, r0:r1, c0:c1] = lookup(rel_ref[kind, r0:r1, c0:c1])


def _bias_tiles(rel_bias):
    tq = NSA_TQ
    r = np.arange(tq)[:, None]
    c = np.arange(tq)[None, :]
    d0 = r - c
    d1 = tq + r - c
    dw = np.where(c > r, WINDOW + r - c, -1)
    gc = np.where(c < CMP_NEAR, r - CMP_STRIDE * c + (CMP_STRIDE * CMP_PAD - (CMP_BLOCK - 1)), -1)
    rel_np = np.stack([d0.T, d1.T, dw.T, gc.T, np.full((tq, tq), -1)]).astype(np.int32)
    boxes = []
    for tile in rel_np:
        rr, cc = np.nonzero((tile >= 0) & (tile < REL_MAX_DIST))
        boxes.append(None if rr.size == 0 else tuple(int(v) for v in (
            rr.min() // 8 * 8, -(-(rr.max() + 1) // 8) * 8, cc.min() // LANES * LANES, -(-(cc.max() + 1) // LANES) * LANES)))
    rel = jnp.asarray(rel_np)
    nt = rel.shape[0]
    return pl.pallas_call(
        functools.partial(_bias_kernel, boxes=tuple(boxes), starts=_bucket_starts()),
        out_shape=jax.ShapeDtypeStruct((NSA_KV, nt, tq, NSA_HPG * tq), F32),
        grid=(NSA_HEADS,),
        in_specs=[pl.BlockSpec(memory_space=pltpu.SMEM),
                  _const_spec((nt, tq, tq))],
        out_specs=pl.BlockSpec((1, nt, tq, tq), lambda h: (h // NSA_HPG, 0, 0, h % NSA_HPG)),
        compiler_params=_params(("arbitrary",)),
        name="bias_tiles",
    )(rel_bias.astype(F32), rel)


def _memkv_kernel(mem_ref, g_ref, w_ref, out_ref):
    y = _rms(mem_ref[0], g_ref[...]).astype(BF16)
    out_ref[0] = _mm(y, w_ref[...]).astype(BF16)


def _memkv(mem, mem_norm_g, w_all):
    b, m, d = mem.shape
    n = w_all.shape[1]
    return pl.pallas_call(
        _memkv_kernel,
        out_shape=jax.ShapeDtypeStruct((b, m, n), BF16),
        grid=(b,),
        in_specs=[pl.BlockSpec((1, m, d), lambda i: (i, 0, 0)),
                  _const_spec((1, d)),
                  _const_spec((d, n))],
        out_specs=pl.BlockSpec((1, m, n), lambda i: (i, 0, 0)),
        compiler_params=_params(("arbitrary",)),
        name="mem_kv",
    )(mem, mem_norm_g.reshape(1, d), w_all)


def _mem_attention(xq, mk, mv):
    outs = []
    for hd in range(MEM_HEADS):
        sl = slice(hd * MEM_DH, (hd + 1) * MEM_DH)
        qm = (xq[:, sl] * (MEM_DH ** -0.5)).astype(BF16)
        s = _mm_nt(qm, mk[:, sl])
        p = jnp.exp(s - jnp.max(s, axis=-1, keepdims=True))
        l = jnp.sum(p, axis=-1, keepdims=True)
        outs.append(_mm(p.astype(BF16), mv[:, sl]) / l)
    return outs


EV_ZA = 0
EV_RQ = EV_ZA + POOL_WIDTH
EV_RK = EV_RQ + RET_HEADS * RET_DK
EV_RV = EV_RK + RET_HEADS * RET_DK
EV_XQ = EV_RV + RET_HEADS * RET_DV
EV_GA = EV_XQ + MEM_WIDTH
EV_GR = EV_GA + POOL_WIDTH
EV_GM = EV_GR + RET_HEADS * RET_DV
EV_COLS = EV_GM + MEM_WIDTH
EV_YA = 0
EV_YR = POOL_WIDTH
EV_YM = EV_YR + RET_HEADS * RET_DV
EV_YCOLS = EV_YM + MEM_WIDTH
EV_SUB = 256


def _even_kernel(gch_ref, h_ref, g_ref, win_ref, wbd_ref, pscale_ref, cos_ref, sin_ref, decay_ref, xi_ref,
                 zeta_ref, mk_ref, mv_ref, wout_ref, o_ref, ext_ref, state_ref, y_ref):
    si = pl.program_id(1)
    ts = h_ref.shape[1]

    @pl.when(si == 0)
    def _():
        ext_ref[0:POOL_HALO, :] = jnp.zeros((POOL_HALO, POOL_WIDTH), F32)
        state_ref[...] = jnp.zeros(state_ref.shape, F32)

    for r0 in range(0, ts, EV_SUB):
        _even_subtile(si * ts + r0, slice(r0, r0 + EV_SUB), gch_ref, h_ref, g_ref, win_ref, wbd_ref, pscale_ref,
                      cos_ref, sin_ref, decay_ref, xi_ref, zeta_ref, mk_ref, mv_ref, wout_ref, o_ref, ext_ref,
                      state_ref, y_ref)


def _even_subtile(t0, tile, gch_ref, h_ref, g_ref, win_ref, wbd_ref, pscale_ref, cos_ref, sin_ref, decay_ref, xi_ref,
                  zeta_ref, mk_ref, mv_ref, wout_ref, o_ref, ext_ref, state_ref, y_ref):
    ts = EV_SUB
    r0 = tile.start
    h = h_ref[0, tile]
    u = _rms(h, g_ref[...]).astype(BF16)

    def proj(start, width):
        return _mm(u, win_ref[:, start:start + width])

    ext_ref[POOL_HALO:, :] = proj(EV_ZA, POOL_WIDTH)
    e = ext_ref[...]
    s2 = e + pltpu.roll(e, 1, 0)
    s4 = s2 + pltpu.roll(s2, 2, 0)
    s8 = s4 + pltpu.roll(s4, 4, 0)
    s16 = s8 + pltpu.roll(s8, 8, 0)
    lane = lax.broadcasted_iota(jnp.int32, e.shape, 1)
    row = lax.broadcasted_iota(jnp.int32, e.shape, 0)
    tpos = t0 + row - POOL_HALO
    g0, g1, g2 = lane < POOL_GROUP, lane < 2 * POOL_GROUP, lane < 3 * POOL_GROUP
    wsum = jnp.where(g0, s2, jnp.where(g1, s4, jnp.where(g2, s8, s16)))
    wlen = jnp.where(g0, POOL_WINDOWS[0], jnp.where(g1, POOL_WINDOWS[1],
                                                    jnp.where(g2, POOL_WINDOWS[2], POOL_WINDOWS[3])))
    cnt = jnp.maximum(jnp.minimum(tpos + 1, wlen), 1).astype(F32)
    pooled = (wsum / cnt - e)[POOL_HALO:]
    ext_ref[0:POOL_HALO, :] = e[ts:ts + POOL_HALO]
    a = _mm(pooled.astype(BF16), wbd_ref[...]) * pscale_ref[...]
    y_ref[tile, EV_YA:EV_YA + POOL_WIDTH] = (a * _silu(proj(EV_GA, POOL_WIDTH))).astype(BF16)

    cos = cos_ref[tile]
    sin = sin_ref[tile]
    zv = proj(EV_RV, RET_HEADS * RET_DV).astype(BF16)
    gate_r = _silu(proj(EV_GR, RET_HEADS * RET_DV))
    q_rot, k_rot = [], []
    for hd in range(RET_HEADS):
        qh = proj(EV_RQ + hd * RET_DK, RET_DK)
        kh = proj(EV_RK + hd * RET_DK, RET_DK)
        q_rot.append((qh * cos + pltpu.roll(qh, RET_DK // 2, 1) * sin) * (RET_DK ** -0.5))
        k_rot.append(kh * cos + pltpu.roll(kh, RET_DK // 2, 1) * sin)
    for c in range(ts // RET_CHUNK):
        rows = slice(c * RET_CHUNK, (c + 1) * RET_CHUNK)
        normed = []
        for hd in range(RET_HEADS):
            qc, kc = q_rot[hd][rows], k_rot[hd][rows]
            vc = zv[rows, hd * RET_DV:(hd + 1) * RET_DV]
            att = _mm_nt(qc.astype(BF16), kc.astype(BF16)) * decay_ref[hd]
            state = state_ref[hd]
            o = _mm(att.astype(BF16), vc) + _mm((qc * xi_ref[hd]).astype(BF16), state.astype(BF16))
            kv = _mm_tn((kc * zeta_ref[hd]).astype(BF16), vc)
            state_ref[hd] = state * gch_ref[hd] + kv
            dlt = o - jnp.mean(o, axis=-1, keepdims=True)
            normed.append(dlt * lax.rsqrt(jnp.mean(dlt * dlt, axis=-1, keepdims=True) + EPS))
        y_ref[r0 + c * RET_CHUNK:r0 + (c + 1) * RET_CHUNK, EV_YR:EV_YM] = (
            jnp.concatenate(normed, axis=-1) * gate_r[rows]).astype(BF16)

    xq = proj(EV_XQ, MEM_WIDTH)
    gm = _silu(proj(EV_GM, MEM_WIDTH))
    for hd, om in enumerate(_mem_attention(xq, mk_ref[0], mv_ref[0])):
        sl = slice(hd * MEM_DH, (hd + 1) * MEM_DH)
        y_ref[tile, EV_YM + hd * MEM_DH:EV_YM + (hd + 1) * MEM_DH] = (om * gm[:, sl]).astype(BF16)

    o_ref[0, tile] = h + _mm(y_ref[tile], wout_ref[...])


def _retention_tables(s):
    half = RET_DK // 2
    inv = ROPE_BASE ** (-jnp.arange(half, dtype=F32) / half)
    ang = jnp.arange(s, dtype=F32)[:, None] * inv[None, :]
    cos, sin = jnp.cos(ang), jnp.sin(ang)
    cos_t = jnp.concatenate([cos, cos], axis=-1)
    sin_t = jnp.concatenate([-sin, sin], axis=-1)
    c = RET_CHUNK
    log_g = jnp.log(1.0 - jnp.exp2(-5.0 - jnp.arange(RET_HEADS, dtype=F32)))
    n = jnp.arange(c, dtype=F32)
    diff = n[:, None] - n[None, :]
    decay = jnp.where(diff >= 0, jnp.exp(log_g[:, None, None] * jnp.maximum(diff, 0.0)), 0.0)
    xi = jnp.exp(log_g[:, None] * (n + 1.0))
    zeta = jnp.exp(log_g[:, None] * (c - 1.0 - n))
    g_chunk = jnp.exp(log_g * c)
    xi_t = jnp.broadcast_to(xi[:, :, None], (RET_HEADS, c, RET_DK))
    zeta_t = jnp.broadcast_to(zeta[:, :, None], (RET_HEADS, c, RET_DK))
    return cos_t, sin_t, decay, xi_t, zeta_t, g_chunk


def _even_layer(h, memkv, layer, g, w_in, pool_w, pool_scale, w_out, ts=2 * EV_SUB):
    b, s, d = h.shape
    assert w_in.shape == (d, EV_COLS) and w_out.shape == (EV_YCOLS, d)
    win = w_in.astype(BF16)
    wout = w_out.astype(BF16)
    wbd = jnp.zeros((POOL_WIDTH, POOL_WIDTH), F32)
    for gi in range(len(POOL_WINDOWS)):
        sl = slice(gi * POOL_GROUP, (gi + 1) * POOL_GROUP)
        wbd = wbd.at[sl, sl].set(pool_w[gi])
    wbd = wbd.astype(BF16)
    cos_t, sin_t, decay, xi_t, zeta_t, g_chunk = _retention_tables(s)
    kv_blk = 2 * layer
    return pl.pallas_call(
        _even_kernel,
        out_shape=jax.ShapeDtypeStruct((b, s, d), F32),
        grid=(b, s // ts),
        in_specs=[pl.BlockSpec(memory_space=pltpu.SMEM),
                  pl.BlockSpec((1, ts, d), lambda i, j: (i, j, 0)),
                  _const_spec((1, d)),
                  _const_spec((d, EV_COLS)),
                  _const_spec((POOL_WIDTH, POOL_WIDTH)),
                  _const_spec((1, POOL_WIDTH)),
                  pl.BlockSpec((ts, RET_DK), lambda i, j: (j, 0)),
                  pl.BlockSpec((ts, RET_DK), lambda i, j: (j, 0)),
                  _const_spec((RET_HEADS, RET_CHUNK, RET_CHUNK)),
                  _const_spec((RET_HEADS, RET_CHUNK, RET_DK)),
                  _const_spec((RET_HEADS, RET_CHUNK, RET_DK)),
                  pl.BlockSpec((1, N_MEM, MEM_WIDTH), lambda i, j: (i, 0, kv_blk)),
                  pl.BlockSpec((1, N_MEM, MEM_WIDTH), lambda i, j: (i, 0, kv_blk + 1)),
                  _const_spec((EV_YCOLS, d))],
        out_specs=pl.BlockSpec((1, ts, d), lambda i, j: (i, j, 0)),
        scratch_shapes=[pltpu.VMEM((POOL_HALO + EV_SUB, POOL_WIDTH), F32),
                        pltpu.VMEM((RET_HEADS, RET_DK, RET_DV), F32),
                        pltpu.VMEM((ts, EV_YCOLS), BF16)],
        compiler_params=_params(("arbitrary", "arbitrary")),
        name="even_layer",
    )(g_chunk, h, g.reshape(1, d), win, wbd, pool_scale.reshape(1, POOL_WIDTH), cos_t, sin_t, decay, xi_t, zeta_t,
      memkv, memkv, wout)


KV_W = NSA_KV * NSA_DH
OD_Q = 0
OD_KV = OD_Q + NSA_HEADS * NSA_DH
OD_A = OD_KV + 6 * KV_W
PIECE_KC, PIECE_VC, PIECE_KS, PIECE_VS, PIECE_KW, PIECE_VW = range(6)
OD_GL = OD_A
OD_XQ = OD_GL + NSA_KV * LANES
OD_GN = OD_XQ + MEM_WIDTH
OD_GM = OD_GN + NSA_HEADS * NSA_DH
OD_COLS = OD_GM + MEM_WIDTH


def _odd_in_kernel(h_ref, g_ref, wraw_ref, wb_ref, pe_ref, perm_ref, mk_ref, mv_ref,
                   q_ref, kaug_ref, vst_ref, kw_ref, vwt_ref, kca_ref, kcb_ref, vca_ref, vcb_ref, gatest_ref,
                   sg_ref, ym_ref, wa_ref):
    @pl.when((pl.program_id(0) == 0) & (pl.program_id(1) == 0))
    def _():
        for c0 in range(0, OD_A, 4 * LANES):
            wa_ref[:, c0:c0 + 4 * LANES] = wraw_ref[:, c0:c0 + 4 * LANES].astype(BF16)

    si = pl.program_id(1)
    n_sub = h_ref.shape[1] // NSA_TQ
    for sub in range(n_sub):
        _odd_in_subtile(si * n_sub + sub, sub, h_ref, g_ref, wa_ref, wb_ref, pe_ref, perm_ref, mk_ref, mv_ref,
                        q_ref, kaug_ref, vst_ref, kw_ref, vwt_ref, kca_ref, kcb_ref, vca_ref, vcb_ref, gatest_ref,
                        sg_ref, ym_ref)


def _odd_in_subtile(tile_idx, sub, h_ref, g_ref, wa_ref, wb_ref, pe_ref, perm_ref, mk_ref, mv_ref,
                    q_ref, kaug_ref, vst_ref, kw_ref, vwt_ref, kca_ref, kcb_ref, vca_ref, vcb_ref, gatest_ref,
                    sg_ref, ym_ref):
    ts = NSA_TQ
    tile = slice(sub * ts, (sub + 1) * ts)
    u = _rms(h_ref[0, tile], g_ref[...]).astype(BF16)

    def proj(start, width):
        if start < OD_A:
            return _mm(u, wa_ref[:, start:start + width])
        return _mm(u, wb_ref[:, start - OD_A:start - OD_A + width])

    zq = proj(OD_Q, NSA_HEADS * NSA_DH) * (NSA_DH ** -0.5 * LOG2E)
    zg = _silu(proj(OD_GN, NSA_HEADS * NSA_DH))
    for g in range(NSA_KV):
        for hh in range(NSA_HPG):
            sl = slice((g * NSA_HPG + hh) * NSA_DH, (g * NSA_HPG + hh + 1) * NSA_DH)
            q_ref[0, g, hh, tile] = zq[:, sl].astype(BF16)
            sg_ref[0, g, hh, tile] = zg[:, sl]

    zkv = proj(OD_KV, 6 * KV_W)
    lane = lax.broadcasted_iota(jnp.int32, (ts, LANES), 1)
    blk = (tile_idx * ts + lax.broadcasted_iota(jnp.int32, (ts, LANES), 0)) // SLC_BLOCK
    onehot = jnp.where(lane == blk, 1.0, 0.0).astype(BF16)
    ones_rows = jnp.where(lax.broadcasted_iota(jnp.int32, (V_ROWS - NSA_DH, ts), 0) == 0, 1.0, 0.0).astype(BF16)
    zgl = jax.nn.sigmoid(proj(OD_GL, NSA_KV * LANES))
    flat_in = []
    for g in range(NSA_KV):
        def piece(idx):
            off = idx * KV_W + g * NSA_DH
            return zkv[:, off:off + NSA_DH]
        kaug_ref[0, g, tile, 0:NSA_DH] = piece(PIECE_KS).astype(BF16)
        kaug_ref[0, g, tile, NSA_DH:2 * NSA_DH] = onehot
        vst_ref[0, g, sub, 0:NSA_DH] = piece(PIECE_VS).T.astype(BF16)
        vst_ref[0, g, sub, NSA_DH:V_ROWS] = ones_rows
        kw_ref[0, g, tile] = piece(PIECE_KW).astype(BF16)
        vwt_ref[0, g, sub, 0:NSA_DH] = piece(PIECE_VW).T.astype(BF16)
        vwt_ref[0, g, sub, NSA_DH:V_ROWS] = ones_rows
        kc, vc = piece(PIECE_KC), piece(PIECE_VC)
        flat_in += [(kc + pe_ref[0, 0]).astype(BF16), (kc + pe_ref[0, 1]).astype(BF16),
                    (vc + pe_ref[1, 0]).astype(BF16), (vc + pe_ref[1, 1]).astype(BF16)]
        gatest_ref[0, g, :, tile] = zgl[:, g * LANES:(g + 1) * LANES].T

    nj = ts // CMP_STRIDE
    perm = _mm(perm_ref[...], jnp.concatenate(flat_in, axis=1)).astype(BF16)
    for c, out_ref in enumerate((kca_ref, kcb_ref, vca_ref, vcb_ref) * NSA_KV):
        for l in range(CMP_STRIDE):
            out_ref[0, c // 4, sub * nj:(sub + 1) * nj, l * NSA_DH:(l + 1) * NSA_DH] = (
                perm[l * nj:(l + 1) * nj, c * NSA_DH:(c + 1) * NSA_DH])

    xq = proj(OD_XQ, MEM_WIDTH)
    gm = _silu(proj(OD_GM, MEM_WIDTH))
    for hd, om in enumerate(_mem_attention(xq, mk_ref[0], mv_ref[0])):
        sl = slice(hd * MEM_DH, (hd + 1) * MEM_DH)
        ym_ref[0, tile, sl] = (om * gm[:, sl]).astype(BF16)


def _odd_in(h, memkv, layer, g, w_in, cmp_pe, ts=2 * NSA_TQ):
    b, s, d = h.shape
    assert w_in.shape[1] == OD_A + 3 * NSA_HEADS + MEM_WIDTH + D_INNER
    gl, xq, gate = jnp.split(w_in[:, OD_A:], [3 * NSA_HEADS, 3 * NSA_HEADS + MEM_WIDTH], axis=1)
    gl = gl.reshape(d, 3, NSA_KV, NSA_HPG).transpose(0, 2, 1, 3).reshape(d, NSA_KV, 3 * NSA_HPG)
    gl = jnp.pad(gl, ((0, 0), (0, 0), (0, LANES - 3 * NSA_HPG))).reshape(d, NSA_KV * LANES)
    wb = jnp.concatenate([w.astype(BF16) for w in (gl, xq, gate)], axis=1)
    sub = NSA_TQ
    reps = sub // CMP_STRIDE
    pe = jnp.stack([jnp.stack([jnp.tile(cmp_pe[kv, :CMP_STRIDE], (reps, 1)),
                               jnp.tile(cmp_pe[kv, CMP_STRIDE:], (reps, 1))]) for kv in range(2)])
    kv_blk = 2 * layer
    assert ts % sub == 0
    nj = sub // CMP_STRIDE
    perm = np.zeros((sub, sub), np.float32)
    perm[np.arange(sub), (np.arange(sub) % nj) * CMP_STRIDE + np.arange(sub) // nj] = 1.0
    head_t = jax.ShapeDtypeStruct((b, NSA_KV, s, NSA_DH), BF16)
    head_spec = pl.BlockSpec((1, NSA_KV, ts, NSA_DH), lambda i, j: (i, 0, j, 0))
    headt_t = jax.ShapeDtypeStruct((b, NSA_KV, s // sub, V_ROWS, sub), BF16)
    headt_spec = pl.BlockSpec((1, NSA_KV, ts // sub, V_ROWS, sub), lambda i, j: (i, 0, j, 0, 0))
    flat_t = jax.ShapeDtypeStruct((b, NSA_KV, s // CMP_STRIDE, CMP_STRIDE * NSA_DH), BF16)
    flat_spec = pl.BlockSpec((1, NSA_KV, ts // CMP_STRIDE, CMP_STRIDE * NSA_DH), lambda i, j: (i, 0, j, 0))
    qlike_spec = pl.BlockSpec((1, NSA_KV, NSA_HPG, ts, NSA_DH), lambda i, j: (i, 0, 0, j, 0))
    return pl.pallas_call(
        _odd_in_kernel,
        out_shape=[jax.ShapeDtypeStruct((b, NSA_KV, NSA_HPG, s, NSA_DH), BF16),
                   jax.ShapeDtypeStruct((b, NSA_KV, s, 2 * NSA_DH), BF16),
                   headt_t, head_t, headt_t, flat_t, flat_t, flat_t, flat_t,
                   jax.ShapeDtypeStruct((b, NSA_KV, LANES, s), F32),
                   jax.ShapeDtypeStruct((b, NSA_KV, NSA_HPG, s, NSA_DH), F32),
                   jax.ShapeDtypeStruct((b, s, MEM_WIDTH), BF16)],
        grid=(b, s // ts),
        in_specs=[pl.BlockSpec((1, ts, d), lambda i, j: (i, j, 0)),
                  _const_spec((1, d)),
                  _const_spec((d, OD_A)),
                  _const_spec((d, OD_COLS - OD_A)),
                  _const_spec((2, 2, sub, NSA_DH)),
                  _const_spec((sub, sub)),
                  pl.BlockSpec((1, N_MEM, MEM_WIDTH), lambda i, j: (i, 0, kv_blk)),
                  pl.BlockSpec((1, N_MEM, MEM_WIDTH), lambda i, j: (i, 0, kv_blk + 1))],
        out_specs=[qlike_spec,
                   pl.BlockSpec((1, NSA_KV, ts, 2 * NSA_DH), lambda i, j: (i, 0, j, 0)),
                   headt_spec, head_spec, headt_spec, flat_spec, flat_spec, flat_spec, flat_spec,
                   pl.BlockSpec((1, NSA_KV, LANES, ts), lambda i, j: (i, 0, 0, j)),
                   qlike_spec,
                   pl.BlockSpec((1, ts, MEM_WIDTH), lambda i, j: (i, j, 0))],
        scratch_shapes=[pltpu.VMEM((d, OD_A), BF16)],
        compiler_params=_params(("arbitrary", "arbitrary")),
        name="odd_in_proj",
    )(h, g.reshape(1, d), w_in, wb, pe, jnp.asarray(perm, BF16), memkv, memkv)


def _compress_kernel(xka_ref, xkb_ref, xva_ref, xvb_ref, w1a_ref, w1b_ref, b1_ref, w2_ref, ovt_ref, kc_ref, vct_ref):
    n = xka_ref.shape[2]
    k_cols = vct_ref.shape[3]

    def block_mlp(kv, xa, xb):
        first = _mm(xa[0, 0], w1a_ref[kv])
        second = _mm(xb[0, 0], w1b_ref[kv])
        hid = first + pltpu.roll(second, n - 1, 0) + b1_ref[kv]
        return _mm(_silu(hid).astype(BF16), w2_ref[kv])

    kc_ref[0, 0, 0:CMP_PAD] = jnp.zeros((CMP_PAD, NSA_DH), BF16)
    kc_ref[0, 0, CMP_PAD:] = block_mlp(0, xka_ref, xkb_ref).astype(BF16)
    vc = jnp.concatenate([jnp.zeros((CMP_PAD, NSA_DH), F32), block_mlp(1, xva_ref, xvb_ref),
                          jnp.zeros((k_cols - CMP_PAD - n, NSA_DH), F32)], axis=0)
    vct_ref[0, 0, 0:NSA_DH] = vc.T.astype(BF16)
    vct_ref[0, 0, NSA_DH:] = ovt_ref[...]


def _compress(kca, kcb, vca, vcb, w1, b1, w2):
    b, g, n, half = kca.shape
    dh = half // CMP_STRIDE
    s = n * CMP_STRIDE
    xs = (kca, kcb, vca, vcb)
    w1 = w1.astype(BF16)
    ovt = _overlap_table(s)
    k_cols = ovt.shape[1]
    x_spec = pl.BlockSpec((1, 1, n, half), lambda i, j: (i, j, 0, 0))
    return pl.pallas_call(
        _compress_kernel,
        out_shape=[jax.ShapeDtypeStruct((b, g, n + CMP_PAD, dh), BF16),
                   jax.ShapeDtypeStruct((b, g, dh + LANES, k_cols), BF16)],
        grid=(b, g),
        in_specs=[x_spec, x_spec, x_spec, x_spec,
                  _const_spec((2, half, CMP_HIDDEN)), _const_spec((2, half, CMP_HIDDEN)),
                  _const_spec((2, 1, CMP_HIDDEN)), _const_spec((2, CMP_HIDDEN, dh)),
                  _const_spec((LANES, k_cols))],
        out_specs=[pl.BlockSpec((1, 1, n + CMP_PAD, dh), lambda i, j: (i, j, 0, 0)),
                   pl.BlockSpec((1, 1, dh + LANES, k_cols), lambda i, j: (i, j, 0, 0))],
        compiler_params=_params(("arbitrary", "arbitrary")),
        name="compress",
    )(*xs, w1[:, :half], w1[:, half:], b1.reshape(2, 1, CMP_HIDDEN), w2.astype(BF16), ovt)


def _nsa_kernel(q_ref, kaug_ref, vst_ref, kw_ref, vwt_ref, kcmp_ref, vcmpt_ref, gatest_ref, sg_ref,
                d0_ref, d1_ref, dw_ref, gc_ref, y_ref, qaug_ref, m_ref, acc_ref, s_ref, sc_ref, mw_ref, accw_ref, sw_ref,
                mix_ref, tmax_ref, tmaxw_ref):
    i = pl.program_id(2)
    tq = NSA_TQ
    rows = NSA_HPG * tq
    s_len = kw_ref.shape[2]
    n_cmp = s_len // CMP_STRIDE
    n_slc = s_len // SLC_BLOCK
    t0 = i * tq
    q = q_ref[0, 0].reshape(rows, NSA_DH)


    def stream(mx_ref, ac_ref, sc2_ref, tmax_ref):
        def reset():
            mx_ref[...] = jnp.full(mx_ref.shape, LOWEST, F32)
            ac_ref[...] = jnp.zeros(ac_ref.shape, F32)

        def s_to(slot, k_ref, query, tile, bias=None):
            k0 = pl.multiple_of(tile * tq, tq)
            st = _mm_nt(k_ref[0, 0, pl.ds(k0, tq)], query)
            if bias is not None:
                st = st + bias
            sc2_ref[slot] = st
            tmax_ref[slot] = jnp.max(st, axis=0, keepdims=True)

        def pv_from(slot, vt):
            st = sc2_ref[slot]
            m_prev = mx_ref[...]
            m_new = jnp.maximum(m_prev, tmax_ref[slot])
            pt = jnp.exp2(st - m_new).astype(BF16)
            ac_ref[...] = jnp.exp2(m_prev - m_new) * ac_ref[...] + _mm(vt, pt)
            mx_ref[...] = m_new

        def finish(cols=slice(None)):
            return ac_ref[0:NSA_DH, cols] / ac_ref[NSA_DH:NSA_DH + 1, cols]

        return reset, s_to, pv_from, finish

    def off_unless(cond):
        return jnp.where(cond, 0.0, NEG)

    prev1 = jnp.maximum(i - 1, 0)
    prev2 = jnp.maximum(i - 2, 0)
    w_reset, w_s_to, w_pv_from, w_finish = stream(mw_ref, accw_ref, sw_ref, tmaxw_ref)


    n_pad = kcmp_ref.shape[2]
    j_near = pl.multiple_of(i * (tq // CMP_STRIDE), tq // CMP_STRIDE)
    qaug_ref[:, 0:NSA_DH] = q
    qaug_ref[:, NSA_DH:] = jnp.where(lax.broadcasted_iota(jnp.int32, (rows, NSA_DH), 1) == 0, 1.0, 0.0).astype(BF16)
    krow = lax.broadcasted_iota(jnp.int32, (n_pad, NSA_DH), 0)
    key_bias = jnp.where((krow >= CMP_PAD) & (krow < j_near + CMP_NEAR), 0.0, NEG).astype(BF16)
    sc_ref[...] = _mm_nt(jnp.concatenate([kcmp_ref[0, 0], key_bias], axis=1), qaug_ref[...])
    w_reset()
    w_s_to(0, kw_ref, q, prev2, dw_ref[0, 0])
    w_s_to(1, kw_ref, q, prev1, d1_ref[0, 0])
    sc_ref[pl.ds(j_near, CMP_NEAR)] = sc_ref[pl.ds(j_near, CMP_NEAR)] + gc_ref[0, 0, 0:CMP_NEAR, :]
    s_cmp = sc_ref[...]
    p_cmp = jnp.exp2(s_cmp - jnp.max(s_cmp, axis=0, keepdims=True))
    tcol = t0 + jnp.bitwise_and(lax.broadcasted_iota(jnp.int32, (1, rows), 1), tq - 1)
    scale = jnp.where(tcol >= CMP_BLOCK - 1, 1.0, 0.0) / jnp.sum(p_cmp, axis=0, keepdims=True)
    p_cmp = p_cmp.astype(BF16)
    k_cols = vcmpt_ref.shape[3]
    if k_cols > n_pad:
        p_cmp = jnp.concatenate([p_cmp, jnp.zeros((k_cols - n_pad, rows), BF16)], axis=0)
    w_pv_from(0, vwt_ref[0, 0, prev2])
    both = _mm(vcmpt_ref[0, 0], p_cmp) * scale
    w_s_to(0, kw_ref, q, i, d0_ref[0, 0])
    w_pv_from(1, vwt_ref[0, 0, prev1])
    gtst = gatest_ref[0, 0]
    n_blk = -(-n_slc // 8) * 8
    imp = both[NSA_DH:NSA_DH + n_blk, 0:tq]
    for hh in range(NSA_HPG):
        hs = slice(hh * tq, (hh + 1) * tq)
        mix_ref[:, hs] = gtst[hh:hh + 1] * both[0:NSA_DH, hs]
        if hh:
            imp = imp + both[NSA_DH:NSA_DH + n_blk, hs]

    blk = lax.broadcasted_iota(jnp.int32, (n_blk, tq), 0)
    tq_pos = t0 + lax.broadcasted_iota(jnp.int32, (n_blk, tq), 1)
    cur = tq_pos // SLC_BLOCK
    forced = (blk == 0) | (blk == cur) | (blk == cur - 1)
    future = blk * SLC_BLOCK > tq_pos
    imp = jnp.where(forced, BIG, jnp.where(future, -BIG, imp))
    imp = jnp.where(blk < n_slc, imp, LOWEST)
    blk_f = blk.astype(F32)
    sel = jnp.zeros((n_blk, tq), jnp.bool_)
    for _ in range(SLC_TOPK):
        top = jnp.max(imp, axis=0, keepdims=True)
        first = jnp.min(jnp.where(imp == top, blk_f, float(LANES)), axis=0, keepdims=True)
        hit = blk_f == first
        sel = sel | hit
        imp = jnp.where(hit, KNOCKED, imp)
    sneg_t = jnp.where(sel, 0.0, NEG)
    if n_blk < LANES:
        sneg_t = jnp.concatenate([sneg_t, jnp.zeros((LANES - n_blk, tq), F32)], axis=0)
    sneg = sneg_t.T.astype(BF16)
    for hh in range(NSA_HPG):
        qaug_ref[hh * tq:(hh + 1) * tq, NSA_DH:2 * NSA_DH] = sneg
    w_pv_from(0, vwt_ref[0, 0, i])
    for hh in range(NSA_HPG):
        hs = slice(hh * tq, (hh + 1) * tq)
        mix_ref[:, hs] = mix_ref[:, hs] + gtst[2 * NSA_HPG + hh:2 * NSA_HPG + hh + 1] * w_finish(hs)

    reset, s_to, pv_from, finish = stream(m_ref, acc_ref, s_ref, tmax_ref)
    reset()
    n_far = prev1
    n_pairs = jnp.maximum(n_far - 1, 0) // 2
    s_to(0, kaug_ref, qaug_ref[...], 0, off_unless(n_far >= 1))

    def far_pair(t2, carry):
        s_to(1, kaug_ref, qaug_ref[...], 2 * t2 + 1)
        pv_from(0, vst_ref[0, 0, 2 * t2])
        s_to(0, kaug_ref, qaug_ref[...], 2 * t2 + 2)
        pv_from(1, vst_ref[0, 0, 2 * t2 + 1])
        return carry

    lax.fori_loop(0, n_pairs, far_pair, 0)
    c0 = 2 * n_pairs
    two_left = n_far - c0 == 2

    @pl.when(two_left)
    def _():
        s_to(1, kaug_ref, qaug_ref[...], c0 + 1)
        pv_from(0, vst_ref[0, 0, c0])
        s_to(0, kaug_ref, qaug_ref[...], prev1, d1_ref[0, 0])
        pv_from(1, vst_ref[0, 0, c0 + 1])
        s_to(1, kaug_ref, qaug_ref[...], i, d0_ref[0, 0])
        pv_from(0, vst_ref[0, 0, prev1])
        pv_from(1, vst_ref[0, 0, i])

    @pl.when(jnp.logical_not(two_left))
    def _():
        s_to(1, kaug_ref, qaug_ref[...], prev1, d1_ref[0, 0])
        pv_from(0, vst_ref[0, 0, c0])
        s_to(0, kaug_ref, qaug_ref[...], i, d0_ref[0, 0])
        pv_from(1, vst_ref[0, 0, prev1])
        pv_from(0, vst_ref[0, 0, i])

    for hh in range(NSA_HPG):
        hs = slice(hh * tq, (hh + 1) * tq)
        ot = mix_ref[:, hs] + gatest_ref[0, 0, NSA_HPG + hh:NSA_HPG + hh + 1] * finish(hs)
        y_ref[0, :, hh * NSA_DH:(hh + 1) * NSA_DH] = (ot.T * sg_ref[0, 0, hh]).astype(BF16)


def _overlap_table(s):
    n_cmp = (s - CMP_BLOCK) // CMP_STRIDE + 1
    n_slc = s // SLC_BLOCK
    cst = np.arange(n_cmp)[:, None] * CMP_STRIDE
    sst = np.arange(n_slc)[None, :] * SLC_BLOCK
    ov = np.clip(np.minimum(cst + CMP_BLOCK, sst + SLC_BLOCK) - np.maximum(cst, sst), 0, None) / CMP_STRIDE
    k_cols = -(-(CMP_PAD + s // CMP_STRIDE) // LANES) * LANES
    full = np.zeros((LANES, k_cols), np.float32)
    full[:n_slc, CMP_PAD:CMP_PAD + n_cmp] = ov.T
    return jnp.asarray(full, BF16)


def _nsa(q, kaug, vst, kw, vwt, kcmp, vcmpt, gatest, sg, bias):
    b, g, hpg, s, dh = q.shape
    tq = NSA_TQ
    assert WINDOW == 2 * tq and s % tq == 0 and s // SLC_BLOCK <= LANES and s // SLC_BLOCK >= SLC_TOPK
    n_pad = kcmp.shape[2]
    rows = hpg * tq
    seq_spec = pl.BlockSpec((1, 1, s, dh), lambda i, j, k: (i, j, 0, 0))
    seqt_spec = pl.BlockSpec((1, 1, s // tq, V_ROWS, tq), lambda i, j, k: (i, j, 0, 0, 0))
    qlike_spec = pl.BlockSpec((1, 1, hpg, tq, dh), lambda i, j, k: (i, j, 0, k, 0))

    def bias_spec(idx, needs_tiles_before=0):
        return pl.BlockSpec((1, 1, tq, rows),
                            lambda i, j, k: (j, jnp.where(k >= needs_tiles_before, idx, BIAS_MASKED), 0, 0))

    return pl.pallas_call(
        _nsa_kernel,
        out_shape=jax.ShapeDtypeStruct((b, s, g * hpg * dh), BF16),
        grid=(b, g, s // tq),
        in_specs=[qlike_spec,
                  pl.BlockSpec((1, 1, s, 2 * dh), lambda i, j, k: (i, j, 0, 0)),
                  seqt_spec, seq_spec, seqt_spec,
                  pl.BlockSpec((1, 1, n_pad, dh), lambda i, j, k: (i, j, 0, 0)),
                  pl.BlockSpec((1, 1, dh + LANES, vcmpt.shape[3]), lambda i, j, k: (i, j, 0, 0)),
                  pl.BlockSpec((1, 1, LANES, tq), lambda i, j, k: (i, j, 0, k)),
                  qlike_spec,
                  bias_spec(0), bias_spec(1, 1), bias_spec(2, 2), bias_spec(3)],
        out_specs=pl.BlockSpec((1, tq, hpg * dh), lambda i, j, k: (i, k, j)),
        scratch_shapes=[pltpu.VMEM((rows, 2 * dh), BF16),
                        pltpu.VMEM((1, rows), F32),
                        pltpu.VMEM((V_ROWS, rows), F32),
                        pltpu.VMEM((2, tq, rows), F32),
                        pltpu.VMEM((n_pad, rows), F32),
                        pltpu.VMEM((1, rows), F32),
                        pltpu.VMEM((V_ROWS, rows), F32),
                        pltpu.VMEM((2, tq, rows), F32),
                        pltpu.VMEM((dh, rows), F32),
                        pltpu.VMEM((2, 1, rows), F32),
                        pltpu.VMEM((2, 1, rows), F32)],
        compiler_params=_params(("arbitrary", "arbitrary", "arbitrary")),
        name="nsa_attention",
    )(q, kaug, vst, kw, vwt, kcmp, vcmpt, gatest, sg, bias, bias, bias, bias)


def _odd_out_kernel(h_ref, yn_ref, ym_ref, wn_ref, wm_ref, fg_ref, o_ref, *, final_norm, sub):
    for r0 in range(0, h_ref.shape[0], sub):
        rows = slice(r0, r0 + sub)
        out = h_ref[rows] + _mm(yn_ref[rows], wn_ref[...]) + _mm(ym_ref[rows], wm_ref[...])
        o_ref[rows] = _rms(out, fg_ref[...]) if final_norm else out


def _odd_out(h, yn, ym, w_out, final_g, final_norm, tm=1024, sub=256):
    b, s, d = h.shape
    t = b * s
    nw = NSA_HEADS * NSA_DH
    wout = w_out.astype(BF16)
    out = pl.pallas_call(
        functools.partial(_odd_out_kernel, final_norm=final_norm, sub=sub),
        out_shape=jax.ShapeDtypeStruct((t, d), F32),
        grid=(t // tm,),
        in_specs=[pl.BlockSpec((tm, d), lambda i: (i, 0)),
                  pl.BlockSpec((tm, nw), lambda i: (i, 0)),
                  pl.BlockSpec((tm, MEM_WIDTH), lambda i: (i, 0)),
                  _const_spec((nw, d)), _const_spec((MEM_WIDTH, d)), _const_spec((1, d))],
        out_specs=pl.BlockSpec((tm, d), lambda i: (i, 0)),
        compiler_params=_params(("arbitrary",)),
        name="odd_out_proj",
    )(h.reshape(t, d), yn.reshape(t, nw), ym.reshape(t, MEM_WIDTH), wout[:nw], wout[nw:], final_g.reshape(1, d))
    return out.reshape(b, s, d)


def _final_norm_kernel(h_ref, g_ref, o_ref):
    o_ref[...] = _rms(h_ref[...], g_ref[...])


def _final_norm(h, final_g, tm=512):
    b, s, d = h.shape
    t = b * s
    out = pl.pallas_call(
        _final_norm_kernel,
        out_shape=jax.ShapeDtypeStruct((t, d), F32),
        grid=(t // tm,),
        in_specs=[pl.BlockSpec((tm, d), lambda i: (i, 0)), _const_spec((1, d))],
        out_specs=pl.BlockSpec((tm, d), lambda i: (i, 0)),
        compiler_params=_params(("arbitrary",)),
        name="final_norm",
    )(h.reshape(t, d), final_g.reshape(1, d))
    return out.reshape(b, s, d)


def kernel(x, mem, norm_g, final_g, mem_norm_g, rel_bias, ev_w_in, ev_pool_w, ev_pool_scale, ev_w_mem_kv, ev_w_out,
           od_w_in, od_cmp_pe, od_cmp_w1, od_cmp_b1, od_cmp_w2, od_w_mem_kv, od_w_out):
    depth = norm_g.shape[0]
    w_mem = [(ev_w_mem_kv if i % 2 == 0 else od_w_mem_kv)[i // 2] for i in range(depth)]
    memkv = _memkv(mem, mem_norm_g, jnp.concatenate(w_mem, axis=1).astype(BF16))
    bias = _bias_tiles(rel_bias) if depth > 1 else None
    h = x
    for i in range(depth):
        j = i // 2
        last = i == depth - 1
        if i % 2 == 0:
            h = _even_layer(h, memkv, i, norm_g[i], ev_w_in[j], ev_pool_w[j], ev_pool_scale[j], ev_w_out[j])
            if last:
                h = _final_norm(h, final_g)
        else:
            (q, kaug, vst, kw, vwt, kca, kcb, vca, vcb, gatest, sg, ym) = _odd_in(
                h, memkv, i, norm_g[i], od_w_in[j], od_cmp_pe[j])
            kcmp, vcmp = _compress(kca, kcb, vca, vcb, od_cmp_w1[j], od_cmp_b1[j], od_cmp_w2[j])
            yn = _nsa(q, kaug, vst, kw, vwt, kcmp, vcmp, gatest, sg, bias)
            h = _odd_out(h, yn, ym, od_w_out[j], final_g, last)
    return h
```

```python
import functools
import math

import numpy as np
import jax
import jax.numpy as jnp
from jax import lax
from jax.experimental import pallas as pl
from jax.experimental.pallas import tpu as pltpu

F32 = jnp.float32
BF16 = jnp.bfloat16

D_MODEL = 1024
D_INNER = 2048
N_MEM = 256
EPS = 1e-6
NEG = -1e30
BIG = 1e30

POOL_WINDOWS = (2, 4, 8, 16)
POOL_WIDTH = 768
POOL_GROUP = 192
POOL_HALO = 16

RET_HEADS = 4
RET_DK = 128
RET_DV = 192
RET_CHUNK = 128
ROPE_BASE = 10000.0

MEM_HEADS = 4
MEM_DH = 128
MEM_WIDTH = 512

NSA_HEADS = 12
NSA_KV = 2
NSA_HPG = 6
NSA_DH = 128
CMP_BLOCK = 32
CMP_STRIDE = 16
CMP_HIDDEN = 256
SLC_BLOCK = 64
SLC_TOPK = 8
WINDOW = 512
REL_BUCKETS = 32
REL_MAX_DIST = 128

LANES = 128
NSA_TQ = 256
CMP_NEAR = 32
CMP_PAD = 16
LOG2E = math.log2(math.e)
V_ROWS = NSA_DH + 16
LOWEST = -3.0e38
KNOCKED = -3.3e38
BIAS_MASKED = 4

VMEM_LIMIT = 56 * 1024 * 1024


def _mm(a, b):
    return jnp.dot(a, b, preferred_element_type=F32)


def _mm_nt(a, b):
    return lax.dot_general(a, b, (((1,), (1,)), ((), ())), preferred_element_type=F32)


def _mm_tn(a, b):
    return lax.dot_general(a, b, (((0,), (0,)), ((), ())), preferred_element_type=F32)


def _rms(x, g):
    return x * lax.rsqrt(jnp.mean(x * x, axis=-1, keepdims=True) + EPS) * g


def _silu(x):
    return x * jax.nn.sigmoid(x)


def _const_spec(shape):
    nd = len(shape)
    return pl.BlockSpec(shape, lambda *_: (0,) * nd, pipeline_mode=pl.Buffered(1))


def _params(sem):
    return pltpu.CompilerParams(dimension_semantics=sem, vmem_limit_bytes=VMEM_LIMIT)


def _bucket_starts():
    n = np.arange(REL_MAX_DIST + 1)
    max_exact = REL_BUCKETS // 2
    nf = np.maximum(n, 1).astype(np.float32)
    large = max_exact + (np.log(nf / np.float32(max_exact)) / np.float32(math.log(REL_MAX_DIST / max_exact))
                         * np.float32(REL_BUCKETS - max_exact)).astype(np.int32)
    bucket = np.where(n < max_exact, n, np.minimum(large, REL_BUCKETS - 1))
    assert np.all(np.diff(bucket) >= 0) and bucket[-1] == REL_BUCKETS - 1
    return tuple(int(np.argmax(bucket >= b)) for b in range(REL_BUCKETS))


def _bias_kernel(tab_ref, rel_ref, out_ref, *, boxes, starts):
    h = pl.program_id(0)

    def lookup(rel):
        far = tab_ref[REL_BUCKETS - 1, h]
        val = jnp.full(rel.shape, tab_ref[0, h] - far, F32)
        for b in range(1, REL_BUCKETS - 1):
            val = jnp.where(rel >= starts[b], tab_ref[b, h] - far, val)
        val = jnp.where(rel >= starts[REL_BUCKETS - 1], 0.0, val)
        return jnp.where(rel < 0, NEG, val * LOG2E)

    for kind, box in enumerate(boxes):
        out_ref[0, kind] = jnp.where(rel_ref[kind] < 0, NEG, 0.0)
        if box is not None:
            r0, r1, c0, c1 = box
            out_ref[0, kind, r0:r1, c0:c1] = lookup(rel_ref[kind, r0:r1, c0:c1])


def _bias_tiles(rel_bias):
    tq = NSA_TQ
    r = np.arange(tq)[:, None]
    c = np.arange(tq)[None, :]
    d0 = r - c
    d1 = tq + r - c
    dw = np.where(c > r, WINDOW + r - c, -1)
    gc = np.where(c < CMP_NEAR, r - CMP_STRIDE * c + (CMP_STRIDE * CMP_PAD - (CMP_BLOCK - 1)), -1)
    rel_np = np.stack([d0.T, d1.T, dw.T, gc.T, np.full((tq, tq), -1)]).astype(np.int32)
    boxes = []
    for tile in rel_np:
        rr, cc = np.nonzero((tile >= 0) & (tile < REL_MAX_DIST))
        boxes.append(None if rr.size == 0 else tuple(int(v) for v in (
            rr.min() // 8 * 8, -(-(rr.max() + 1) // 8) * 8, cc.min() // LANES * LANES, -(-(cc.max() + 1) // LANES) * LANES)))
    rel = jnp.asarray(rel_np)
    nt = rel.shape[0]
    return pl.pallas_call(
        functools.partial(_bias_kernel, boxes=tuple(boxes), starts=_bucket_starts()),
        out_shape=jax.ShapeDtypeStruct((NSA_KV, nt, tq, NSA_HPG * tq), F32),
        grid=(NSA_HEADS,),
        in_specs=[pl.BlockSpec(memory_space=pltpu.SMEM),
                  _const_spec((nt, tq, tq))],
        out_specs=pl.BlockSpec((1, nt, tq, tq), lambda h: (h // NSA_HPG, 0, 0, h % NSA_HPG)),
        compiler_params=_params(("arbitrary",)),
        name="bias_tiles",
    )(rel_bias.astype(F32), rel)


def _memkv_kernel(mem_ref, g_ref, w_ref, out_ref):
    y = _rms(mem_ref[0], g_ref[...]).astype(BF16)
    out_ref[0] = _mm(y, w_ref[...]).astype(BF16)


def _memkv(mem, mem_norm_g, w_all):
    b, m, d = mem.shape
    n = w_all.shape[1]
    return pl.pallas_call(
        _memkv_kernel,
        out_shape=jax.ShapeDtypeStruct((b, m, n), BF16),
        grid=(b,),
        in_specs=[pl.BlockSpec((1, m, d), lambda i: (i, 0, 0)),
                  _const_spec((1, d)),
                  _const_spec((d, n))],
        out_specs=pl.BlockSpec((1, m, n), lambda i: (i, 0, 0)),
        compiler_params=_params(("arbitrary",)),
        name="mem_kv",
    )(mem, mem_norm_g.reshape(1, d), w_all)


def _mem_attention(xq, mk, mv):
    outs = []
    for hd in range(MEM_HEADS):
        sl = slice(hd * MEM_DH, (hd + 1) * MEM_DH)
        qm = (xq[:, sl] * (MEM_DH ** -0.5)).astype(BF16)
        s = _mm_nt(qm, mk[:, sl])
        p = jnp.exp(s - jnp.max(s, axis=-1, keepdims=True))
        l = jnp.sum(p, axis=-1, keepdims=True)
        outs.append(_mm(p.astype(BF16), mv[:, sl]) / l)
    return outs


EV_ZA = 0
EV_RQ = EV_ZA + POOL_WIDTH
EV_RK = EV_RQ + RET_HEADS * RET_DK
EV_RV = EV_RK + RET_HEADS * RET_DK
EV_XQ = EV_RV + RET_HEADS * RET_DV
EV_GA = EV_XQ + MEM_WIDTH
EV_GR = EV_GA + POOL_WIDTH
EV_GM = EV_GR + RET_HEADS * RET_DV
EV_COLS = EV_GM + MEM_WIDTH
EV_YA = 0
EV_YR = POOL_WIDTH
EV_YM = EV_YR + RET_HEADS * RET_DV
EV_YCOLS = EV_YM + MEM_WIDTH
EV_SUB = 256


def _even_kernel(gch_ref, h_ref, g_ref, win_ref, wbd_ref, pscale_ref, cos_ref, sin_ref, decay_ref, xi_ref,
                 zeta_ref, mk_ref, mv_ref, wout_ref, o_ref, ext_ref, state_ref, y_ref):
    si = pl.program_id(1)
    ts = h_ref.shape[1]

    @pl.when(si == 0)
    def _():
        ext_ref[0:POOL_HALO, :] = jnp.zeros((POOL_HALO, POOL_WIDTH), F32)
        state_ref[...] = jnp.zeros(state_ref.shape, F32)

    for r0 in range(0, ts, EV_SUB):
        _even_subtile(si * ts + r0, slice(r0, r0 + EV_SUB), gch_ref, h_ref, g_ref, win_ref, wbd_ref, pscale_ref,
                      cos_ref, sin_ref, decay_ref, xi_ref, zeta_ref, mk_ref, mv_ref, wout_ref, o_ref, ext_ref,
                      state_ref, y_ref)


def _even_subtile(t0, tile, gch_ref, h_ref, g_ref, win_ref, wbd_ref, pscale_ref, cos_ref, sin_ref, decay_ref, xi_ref,
                  zeta_ref, mk_ref, mv_ref, wout_ref, o_ref, ext_ref, state_ref, y_ref):
    ts = EV_SUB
    r0 = tile.start
    h = h_ref[0, tile]
    u = _rms(h, g_ref[...]).astype(BF16)

    def proj(start, width):
        return _mm(u, win_ref[:, start:start + width])

    ext_ref[POOL_HALO:, :] = proj(EV_ZA, POOL_WIDTH)
    e = ext_ref[...]
    s2 = e + pltpu.roll(e, 1, 0)
    s4 = s2 + pltpu.roll(s2, 2, 0)
    s8 = s4 + pltpu.roll(s4, 4, 0)
    s16 = s8 + pltpu.roll(s8, 8, 0)
    lane = lax.broadcasted_iota(jnp.int32, e.shape, 1)
    row = lax.broadcasted_iota(jnp.int32, e.shape, 0)
    tpos = t0 + row - POOL_HALO
    g0, g1, g2 = lane < POOL_GROUP, lane < 2 * POOL_GROUP, lane < 3 * POOL_GROUP
    wsum = jnp.where(g0, s2, jnp.where(g1, s4, jnp.where(g2, s8, s16)))
    wlen = jnp.where(g0, POOL_WINDOWS[0], jnp.where(g1, POOL_WINDOWS[1],
                                                    jnp.where(g2, POOL_WINDOWS[2], POOL_WINDOWS[3])))
    cnt = jnp.maximum(jnp.minimum(tpos + 1, wlen), 1).astype(F32)
    pooled = (wsum / cnt - e)[POOL_HALO:]
    ext_ref[0:POOL_HALO, :] = e[ts:ts + POOL_HALO]
    a = _mm(pooled.astype(BF16), wbd_ref[...]) * pscale_ref[...]
    y_ref[tile, EV_YA:EV_YA + POOL_WIDTH] = (a * _silu(proj(EV_GA, POOL_WIDTH))).astype(BF16)

    cos = cos_ref[tile]
    sin = sin_ref[tile]
    zv = proj(EV_RV, RET_HEADS * RET_DV).astype(BF16)
    gate_r = _silu(proj(EV_GR, RET_HEADS * RET_DV))
    q_rot, k_rot = [], []
    for hd in range(RET_HEADS):
        qh = proj(EV_RQ + hd * RET_DK, RET_DK)
        kh = proj(EV_RK + hd * RET_DK, RET_DK)
        q_rot.append((qh * cos + pltpu.roll(qh, RET_DK // 2, 1) * sin) * (RET_DK ** -0.5))
        k_rot.append(kh * cos + pltpu.roll(kh, RET_DK // 2, 1) * sin)
    for c in range(ts // RET_CHUNK):
        rows = slice(c * RET_CHUNK, (c + 1) * RET_CHUNK)
        normed = []
        for hd in range(RET_HEADS):
            qc, kc = q_rot[hd][rows], k_rot[hd][rows]
            vc = zv[rows, hd * RET_DV:(hd + 1) * RET_DV]
            att = _mm_nt(qc.astype(BF16), kc.astype(BF16)) * decay_ref[hd]
            state = state_ref[hd]
            o = _mm(att.astype(BF16), vc) + _mm((qc * xi_ref[hd]).astype(BF16), state.astype(BF16))
            kv = _mm_tn((kc * zeta_ref[hd]).astype(BF16), vc)
            state_ref[hd] = state * gch_ref[hd] + kv
            dlt = o - jnp.mean(o, axis=-1, keepdims=True)
            normed.append(dlt * lax.rsqrt(jnp.mean(dlt * dlt, axis=-1, keepdims=True) + EPS))
        y_ref[r0 + c * RET_CHUNK:r0 + (c + 1) * RET_CHUNK, EV_YR:EV_YM] = (
            jnp.concatenate(normed, axis=-1) * gate_r[rows]).astype(BF16)

    xq = proj(EV_XQ, MEM_WIDTH)
    gm = _silu(proj(EV_GM, MEM_WIDTH))
    for hd, om in enumerate(_mem_attention(xq, mk_ref[0], mv_ref[0])):
        sl = slice(hd * MEM_DH, (hd + 1) * MEM_DH)
        y_ref[tile, EV_YM + hd * MEM_DH:EV_YM + (hd + 1) * MEM_DH] = (om * gm[:, sl]).astype(BF16)

    o_ref[0, tile] = h + _mm(y_ref[tile], wout_ref[...])


def _retention_tables(s):
    half = RET_DK // 2
    inv = ROPE_BASE ** (-jnp.arange(half, dtype=F32) / half)
    ang = jnp.arange(s, dtype=F32)[:, None] * inv[None, :]
    cos, sin = jnp.cos(ang), jnp.sin(ang)
    cos_t = jnp.concatenate([cos, cos], axis=-1)
    sin_t = jnp.concatenate([-sin, sin], axis=-1)
    c = RET_CHUNK
    log_g = jnp.log(1.0 - jnp.exp2(-5.0 - jnp.arange(RET_HEADS, dtype=F32)))
    n = jnp.arange(c, dtype=F32)
    diff = n[:, None] - n[None, :]
    decay = jnp.where(diff >= 0, jnp.exp(log_g[:, None, None] * jnp.maximum(diff, 0.0)), 0.0)
    xi = jnp.exp(log_g[:, None] * (n + 1.0))
    zeta = jnp.exp(log_g[:, None] * (c - 1.0 - n))
    g_chunk = jnp.exp(log_g * c)
    xi_t = jnp.broadcast_to(xi[:, :, None], (RET_HEADS, c, RET_DK))
    zeta_t = jnp.broadcast_to(zeta[:, :, None], (RET_HEADS, c, RET_DK))
    return cos_t, sin_t, decay, xi_t, zeta_t, g_chunk


def _even_layer(h, memkv, layer, g, w_in, pool_w, pool_scale, w_out, ts=2 * EV_SUB):
    b, s, d = h.shape
    assert w_in.shape == (d, EV_COLS) and w_out.shape == (EV_YCOLS, d)
    win = w_in.astype(BF16)
    wout = w_out.astype(BF16)
    wbd = jnp.zeros((POOL_WIDTH, POOL_WIDTH), F32)
    for gi in range(len(POOL_WINDOWS)):
        sl = slice(gi * POOL_GROUP, (gi + 1) * POOL_GROUP)
        wbd = wbd.at[sl, sl].set(pool_w[gi])
    wbd = wbd.astype(BF16)
    cos_t, sin_t, decay, xi_t, zeta_t, g_chunk = _retention_tables(s)
    kv_blk = 2 * layer
    return pl.pallas_call(
        _even_kernel,
        out_shape=jax.ShapeDtypeStruct((b, s, d), F32),
        grid=(b, s // ts),
        in_specs=[pl.BlockSpec(memory_space=pltpu.SMEM),
                  pl.BlockSpec((1, ts, d), lambda i, j: (i, j, 0)),
                  _const_spec((1, d)),
                  _const_spec((d, EV_COLS)),
                  _const_spec((POOL_WIDTH, POOL_WIDTH)),
                  _const_spec((1, POOL_WIDTH)),
                  pl.BlockSpec((ts, RET_DK), lambda i, j: (j, 0)),
                  pl.BlockSpec((ts, RET_DK), lambda i, j: (j, 0)),
                  _const_spec((RET_HEADS, RET_CHUNK, RET_CHUNK)),
                  _const_spec((RET_HEADS, RET_CHUNK, RET_DK)),
                  _const_spec((RET_HEADS, RET_CHUNK, RET_DK)),
                  pl.BlockSpec((1, N_MEM, MEM_WIDTH), lambda i, j: (i, 0, kv_blk)),
                  pl.BlockSpec((1, N_MEM, MEM_WIDTH), lambda i, j: (i, 0, kv_blk + 1)),
                  _const_spec((EV_YCOLS, d))],
        out_specs=pl.BlockSpec((1, ts, d), lambda i, j: (i, j, 0)),
        scratch_shapes=[pltpu.VMEM((POOL_HALO + EV_SUB, POOL_WIDTH), F32),
                        pltpu.VMEM((RET_HEADS, RET_DK, RET_DV), F32),
                        pltpu.VMEM((ts, EV_YCOLS), BF16)],
        compiler_params=_params(("arbitrary", "arbitrary")),
        name="even_layer",
    )(g_chunk, h, g.reshape(1, d), win, wbd, pool_scale.reshape(1, POOL_WIDTH), cos_t, sin_t, decay, xi_t, zeta_t,
      memkv, memkv, wout)


KV_W = NSA_KV * NSA_DH
OD_Q = 0
OD_KV = OD_Q + NSA_HEADS * NSA_DH
OD_A = OD_KV + 6 * KV_W
PIECE_KC, PIECE_VC, PIECE_KS, PIECE_VS, PIECE_KW, PIECE_VW = range(6)
OD_GL = OD_A
OD_XQ = OD_GL + NSA_KV * LANES
OD_GN = OD_XQ + MEM_WIDTH
OD_GM = OD_GN + NSA_HEADS * NSA_DH
OD_COLS = OD_GM + MEM_WIDTH


def _odd_in_kernel(h_ref, g_ref, wraw_ref, wb_ref, pe_ref, perm_ref, mk_ref, mv_ref,
                   q_ref, kaug_ref, vst_ref, kw_ref, vwt_ref, kca_ref, kcb_ref, vca_ref, vcb_ref, gatest_ref,
                   sg_ref, ym_ref, wa_ref):
    @pl.when((pl.program_id(0) == 0) & (pl.program_id(1) == 0))
    def _():
        for c0 in range(0, OD_A, 4 * LANES):
            wa_ref[:, c0:c0 + 4 * LANES] = wraw_ref[:, c0:c0 + 4 * LANES].astype(BF16)

    si = pl.program_id(1)
    n_sub = h_ref.shape[1] // NSA_TQ
    for sub in range(n_sub):
        _odd_in_subtile(si * n_sub + sub, sub, h_ref, g_ref, wa_ref, wb_ref, pe_ref, perm_ref, mk_ref, mv_ref,
                        q_ref, kaug_ref, vst_ref, kw_ref, vwt_ref, kca_ref, kcb_ref, vca_ref, vcb_ref, gatest_ref,
                        sg_ref, ym_ref)


def _odd_in_subtile(tile_idx, sub, h_ref, g_ref, wa_ref, wb_ref, pe_ref, perm_ref, mk_ref, mv_ref,
                    q_ref, kaug_ref, vst_ref, kw_ref, vwt_ref, kca_ref, kcb_ref, vca_ref, vcb_ref, gatest_ref,
                    sg_ref, ym_ref):
    ts = NSA_TQ
    tile = slice(sub * ts, (sub + 1) * ts)
    u = _rms(h_ref[0, tile], g_ref[...]).astype(BF16)

    def proj(start, width):
        if start < OD_A:
            return _mm(u, wa_ref[:, start:start + width])
        return _mm(u, wb_ref[:, start - OD_A:start - OD_A + width])

    zq = proj(OD_Q, NSA_HEADS * NSA_DH) * (NSA_DH ** -0.5 * LOG2E)
    zg = _silu(proj(OD_GN, NSA_HEADS * NSA_DH))
    for g in range(NSA_KV):
        for hh in range(NSA_HPG):
            sl = slice((g * NSA_HPG + hh) * NSA_DH, (g * NSA_HPG + hh + 1) * NSA_DH)
            q_ref[0, g, hh, tile] = zq[:, sl].astype(BF16)
            sg_ref[0, g, hh, tile] = zg[:, sl]

    zkv = proj(OD_KV, 6 * KV_W)
    lane = lax.broadcasted_iota(jnp.int32, (ts, LANES), 1)
    blk = (tile_idx * ts + lax.broadcasted_iota(jnp.int32, (ts, LANES), 0)) // SLC_BLOCK
    onehot = jnp.where(lane == blk, 1.0, 0.0).astype(BF16)
    ones_rows = jnp.where(lax.broadcasted_iota(jnp.int32, (V_ROWS - NSA_DH, ts), 0) == 0, 1.0, 0.0).astype(BF16)
    zgl = jax.nn.sigmoid(proj(OD_GL, NSA_KV * LANES))
    flat_in = []
    for g in range(NSA_KV):
        def piece(idx):
            off = idx * KV_W + g * NSA_DH
            return zkv[:, off:off + NSA_DH]
        kaug_ref[0, g, tile, 0:NSA_DH] = piece(PIECE_KS).astype(BF16)
        kaug_ref[0, g, tile, NSA_DH:2 * NSA_DH] = onehot
        vst_ref[0, g, sub, 0:NSA_DH] = piece(PIECE_VS).T.astype(BF16)
        vst_ref[0, g, sub, NSA_DH:V_ROWS] = ones_rows
        kw_ref[0, g, tile] = piece(PIECE_KW).astype(BF16)
        vwt_ref[0, g, sub, 0:NSA_DH] = piece(PIECE_VW).T.astype(BF16)
        vwt_ref[0, g, sub, NSA_DH:V_ROWS] = ones_rows
        kc, vc = piece(PIECE_KC), piece(PIECE_VC)
        flat_in += [(kc + pe_ref[0, 0]).astype(BF16), (kc + pe_ref[0, 1]).astype(BF16),
                    (vc + pe_ref[1, 0]).astype(BF16), (vc + pe_ref[1, 1]).astype(BF16)]
        gatest_ref[0, g, :, tile] = zgl[:, g * LANES:(g + 1) * LANES].T

    nj = ts // CMP_STRIDE
    perm = _mm(perm_ref[...], jnp.concatenate(flat_in, axis=1)).astype(BF16)
    for c, out_ref in enumerate((kca_ref, kcb_ref, vca_ref, vcb_ref) * NSA_KV):
        for l in range(CMP_STRIDE):
            out_ref[0, c // 4, sub * nj:(sub + 1) * nj, l * NSA_DH:(l + 1) * NSA_DH] = (
                perm[l * nj:(l + 1) * nj, c * NSA_DH:(c + 1) * NSA_DH])

    xq = proj(OD_XQ, MEM_WIDTH)
    gm = _silu(proj(OD_GM, MEM_WIDTH))
    for hd, om in enumerate(_mem_attention(xq, mk_ref[0], mv_ref[0])):
        sl = slice(hd * MEM_DH, (hd + 1) * MEM_DH)
        ym_ref[0, tile, sl] = (om * gm[:, sl]).astype(BF16)


def _odd_in(h, memkv, layer, g, w_in, cmp_pe, ts=2 * NSA_TQ):
    b, s, d = h.shape
    assert w_in.shape[1] == OD_A + 3 * NSA_HEADS + MEM_WIDTH + D_INNER
    tail = lax.optimization_barrier(w_in[:, OD_A:])
    gl, xq, gate = jnp.split(tail, [3 * NSA_HEADS, 3 * NSA_HEADS + MEM_WIDTH], axis=1)
    gl = gl.reshape(d, 3, NSA_KV, NSA_HPG).transpose(0, 2, 1, 3).reshape(d, NSA_KV, 3 * NSA_HPG)
    gl = jnp.pad(gl, ((0, 0), (0, 0), (0, LANES - 3 * NSA_HPG))).reshape(d, NSA_KV * LANES)
    wb = jnp.concatenate([w.astype(BF16) for w in (gl, xq, gate)], axis=1)
    sub = NSA_TQ
    reps = sub // CMP_STRIDE
    pe = jnp.stack([jnp.stack([jnp.tile(cmp_pe[kv, :CMP_STRIDE], (reps, 1)),
                               jnp.tile(cmp_pe[kv, CMP_STRIDE:], (reps, 1))]) for kv in range(2)])
    kv_blk = 2 * layer
    assert ts % sub == 0
    nj = sub // CMP_STRIDE
    perm = np.zeros((sub, sub), np.float32)
    perm[np.arange(sub), (np.arange(sub) % nj) * CMP_STRIDE + np.arange(sub) // nj] = 1.0
    head_t = jax.ShapeDtypeStruct((b, NSA_KV, s, NSA_DH), BF16)
    head_spec = pl.BlockSpec((1, NSA_KV, ts, NSA_DH), lambda i, j: (i, 0, j, 0))
    headt_t = jax.ShapeDtypeStruct((b, NSA_KV, s // sub, V_ROWS, sub), BF16)
    headt_spec = pl.BlockSpec((1, NSA_KV, ts // sub, V_ROWS, sub), lambda i, j: (i, 0, j, 0, 0))
    flat_t = jax.ShapeDtypeStruct((b, NSA_KV, s // CMP_STRIDE, CMP_STRIDE * NSA_DH), BF16)
    flat_spec = pl.BlockSpec((1, NSA_KV, ts // CMP_STRIDE, CMP_STRIDE * NSA_DH), lambda i, j: (i, 0, j, 0))
    qlike_spec = pl.BlockSpec((1, NSA_KV, NSA_HPG, ts, NSA_DH), lambda i, j: (i, 0, 0, j, 0))
    return pl.pallas_call(
        _odd_in_kernel,
        out_shape=[jax.ShapeDtypeStruct((b, NSA_KV, NSA_HPG, s, NSA_DH), BF16),
                   jax.ShapeDtypeStruct((b, NSA_KV, s, 2 * NSA_DH), BF16),
                   headt_t, head_t, headt_t, flat_t, flat_t, flat_t, flat_t,
                   jax.ShapeDtypeStruct((b, NSA_KV, LANES, s), F32),
                   jax.ShapeDtypeStruct((b, NSA_KV, NSA_HPG, s, NSA_DH), F32),
                   jax.ShapeDtypeStruct((b, s, MEM_WIDTH), BF16)],
        grid=(b, s // ts),
        in_specs=[pl.BlockSpec((1, ts, d), lambda i, j: (i, j, 0)),
                  _const_spec((1, d)),
                  _const_spec((d, OD_A)),
                  _const_spec((d, OD_COLS - OD_A)),
                  _const_spec((2, 2, sub, NSA_DH)),
                  _const_spec((sub, sub)),
                  pl.BlockSpec((1, N_MEM, MEM_WIDTH), lambda i, j: (i, 0, kv_blk)),
                  pl.BlockSpec((1, N_MEM, MEM_WIDTH), lambda i, j: (i, 0, kv_blk + 1))],
        out_specs=[qlike_spec,
                   pl.BlockSpec((1, NSA_KV, ts, 2 * NSA_DH), lambda i, j: (i, 0, j, 0)),
                   headt_spec, head_spec, headt_spec, flat_spec, flat_spec, flat_spec, flat_spec,
                   pl.BlockSpec((1, NSA_KV, LANES, ts), lambda i, j: (i, 0, 0, j)),
                   qlike_spec,
                   pl.BlockSpec((1, ts, MEM_WIDTH), lambda i, j: (i, j, 0))],
        scratch_shapes=[pltpu.VMEM((d, OD_A), BF16)],
        compiler_params=_params(("arbitrary", "arbitrary")),
        name="odd_in_proj",
    )(h, g.reshape(1, d), w_in, wb, pe, jnp.asarray(perm, BF16), memkv, memkv)


def _compress_kernel(xka_ref, xkb_ref, xva_ref, xvb_ref, w1a_ref, w1b_ref, b1_ref, w2_ref, ovt_ref, kc_ref, vct_ref):
    n = xka_ref.shape[2]
    k_cols = vct_ref.shape[3]

    def block_mlp(kv, xa, xb):
        first = _mm(xa[0, 0], w1a_ref[kv])
        second = _mm(xb[0, 0], w1b_ref[kv])
        hid = first + pltpu.roll(second, n - 1, 0) + b1_ref[kv]
        return _mm(_silu(hid).astype(BF16), w2_ref[kv])

    kc_ref[0, 0, 0:CMP_PAD] = jnp.zeros((CMP_PAD, NSA_DH), BF16)
    kc_ref[0, 0, CMP_PAD:] = block_mlp(0, xka_ref, xkb_ref).astype(BF16)
    vc = jnp.concatenate([jnp.zeros((CMP_PAD, NSA_DH), F32), block_mlp(1, xva_ref, xvb_ref),
                          jnp.zeros((k_cols - CMP_PAD - n, NSA_DH), F32)], axis=0)
    vct_ref[0, 0, 0:NSA_DH] = vc.T.astype(BF16)
    vct_ref[0, 0, NSA_DH:] = ovt_ref[...]


def _compress(kca, kcb, vca, vcb, w1, b1, w2):
    b, g, n, half = kca.shape
    dh = half // CMP_STRIDE
    s = n * CMP_STRIDE
    xs = (kca, kcb, vca, vcb)
    w1 = w1.astype(BF16)
    ovt = _overlap_table(s)
    k_cols = ovt.shape[1]
    x_spec = pl.BlockSpec((1, 1, n, half), lambda i, j: (i, j, 0, 0))
    return pl.pallas_call(
        _compress_kernel,
        out_shape=[jax.ShapeDtypeStruct((b, g, n + CMP_PAD, dh), BF16),
                   jax.ShapeDtypeStruct((b, g, dh + LANES, k_cols), BF16)],
        grid=(b, g),
        in_specs=[x_spec, x_spec, x_spec, x_spec,
                  _const_spec((2, half, CMP_HIDDEN)), _const_spec((2, half, CMP_HIDDEN)),
                  _const_spec((2, 1, CMP_HIDDEN)), _const_spec((2, CMP_HIDDEN, dh)),
                  _const_spec((LANES, k_cols))],
        out_specs=[pl.BlockSpec((1, 1, n + CMP_PAD, dh), lambda i, j: (i, j, 0, 0)),
                   pl.BlockSpec((1, 1, dh + LANES, k_cols), lambda i, j: (i, j, 0, 0))],
        compiler_params=_params(("arbitrary", "arbitrary")),
        name="compress",
    )(*xs, w1[:, :half], w1[:, half:], b1.reshape(2, 1, CMP_HIDDEN), w2.astype(BF16), ovt)


def _nsa_kernel(q_ref, kaug_ref, vst_ref, kw_ref, vwt_ref, kcmp_ref, vcmpt_ref, gatest_ref, sg_ref,
                d0_ref, d1_ref, dw_ref, gc_ref, y_ref, qaug_ref, m_ref, acc_ref, s_ref, sc_ref, mw_ref, accw_ref, sw_ref,
                mix_ref, tmax_ref, tmaxw_ref):
    i = pl.program_id(2)
    tq = NSA_TQ
    rows = NSA_HPG * tq
    s_len = kw_ref.shape[2]
    n_cmp = s_len // CMP_STRIDE
    n_slc = s_len // SLC_BLOCK
    t0 = i * tq
    q = q_ref[0, 0].reshape(rows, NSA_DH)


    def stream(mx_ref, ac_ref, sc2_ref, tmax_ref):
        def reset():
            mx_ref[...] = jnp.full(mx_ref.shape, LOWEST, F32)
            ac_ref[...] = jnp.zeros(ac_ref.shape, F32)

        def s_to(slot, k_ref, query, tile, bias=None):
            k0 = pl.multiple_of(tile * tq, tq)
            st = _mm_nt(k_ref[0, 0, pl.ds(k0, tq)], query)
            if bias is not None:
                st = st + bias
            sc2_ref[slot] = st
            tmax_ref[slot] = jnp.max(st, axis=0, keepdims=True)

        def pv_from(slot, vt):
            st = sc2_ref[slot]
            m_prev = mx_ref[...]
            m_new = jnp.maximum(m_prev, tmax_ref[slot])
            pt = jnp.exp2(st - m_new).astype(BF16)
            ac_ref[...] = jnp.exp2(m_prev - m_new) * ac_ref[...] + _mm(vt, pt)
            mx_ref[...] = m_new

        def finish(cols=slice(None)):
            return ac_ref[0:NSA_DH, cols] / ac_ref[NSA_DH:NSA_DH + 1, cols]

        return reset, s_to, pv_from, finish

    def off_unless(cond):
        return jnp.where(cond, 0.0, NEG)

    prev1 = jnp.maximum(i - 1, 0)
    prev2 = jnp.maximum(i - 2, 0)
    w_reset, w_s_to, w_pv_from, w_finish = stream(mw_ref, accw_ref, sw_ref, tmaxw_ref)


    n_pad = kcmp_ref.shape[2]
    j_near = pl.multiple_of(i * (tq // CMP_STRIDE), tq // CMP_STRIDE)
    qaug_ref[:, 0:NSA_DH] = q
    qaug_ref[:, NSA_DH:] = jnp.where(lax.broadcasted_iota(jnp.int32, (rows, NSA_DH), 1) == 0, 1.0, 0.0).astype(BF16)
    krow = lax.broadcasted_iota(jnp.int32, (n_pad, NSA_DH), 0)
    key_bias = jnp.where((krow >= CMP_PAD) & (krow < j_near + CMP_NEAR), 0.0, NEG).astype(BF16)
    sc_ref[...] = _mm_nt(jnp.concatenate([kcmp_ref[0, 0], key_bias], axis=1), qaug_ref[...])
    w_reset()
    w_s_to(0, kw_ref, q, prev2, dw_ref[0, 0])
    w_s_to(1, kw_ref, q, prev1, d1_ref[0, 0])
    sc_ref[pl.ds(j_near, CMP_NEAR)] = sc_ref[pl.ds(j_near, CMP_NEAR)] + gc_ref[0, 0, 0:CMP_NEAR, :]
    s_cmp = sc_ref[...]
    p_cmp = jnp.exp2(s_cmp - jnp.max(s_cmp, axis=0, keepdims=True))
    tcol = t0 + jnp.bitwise_and(lax.broadcasted_iota(jnp.int32, (1, rows), 1), tq - 1)
    scale = jnp.where(tcol >= CMP_BLOCK - 1, 1.0, 0.0) / jnp.sum(p_cmp, axis=0, keepdims=True)
    p_cmp = p_cmp.astype(BF16)
    k_cols = vcmpt_ref.shape[3]
    if k_cols > n_pad:
        p_cmp = jnp.concatenate([p_cmp, jnp.zeros((k_cols - n_pad, rows), BF16)], axis=0)
    w_pv_from(0, vwt_ref[0, 0, prev2])
    both = _mm(vcmpt_ref[0, 0], p_cmp) * scale
    w_s_to(0, kw_ref, q, i, d0_ref[0, 0])
    w_pv_from(1, vwt_ref[0, 0, prev1])
    gtst = gatest_ref[0, 0]
    n_blk = -(-n_slc // 8) * 8
    imp = both[NSA_DH:NSA_DH + n_blk, 0:tq]
    for hh in range(NSA_HPG):
        hs = slice(hh * tq, (hh + 1) * tq)
        mix_ref[:, hs] = gtst[hh:hh + 1] * both[0:NSA_DH, hs]
        if hh:
            imp = imp + both[NSA_DH:NSA_DH + n_blk, hs]

    blk = lax.broadcasted_iota(jnp.int32, (n_blk, tq), 0)
    tq_pos = t0 + lax.broadcasted_iota(jnp.int32, (n_blk, tq), 1)
    cur = tq_pos // SLC_BLOCK
    forced = (blk == 0) | (blk == cur) | (blk == cur - 1)
    future = blk * SLC_BLOCK > tq_pos
    imp = jnp.where(forced, BIG, jnp.where(future, -BIG, imp))
    imp = jnp.where(blk < n_slc, imp, LOWEST)
    blk_f = blk.astype(F32)
    sel = jnp.zeros((n_blk, tq), jnp.bool_)
    for _ in range(SLC_TOPK):
        top = jnp.max(imp, axis=0, keepdims=True)
        first = jnp.min(jnp.where(imp == top, blk_f, float(LANES)), axis=0, keepdims=True)
        hit = blk_f == first
        sel = sel | hit
        imp = jnp.where(hit, KNOCKED, imp)
    sneg_t = jnp.where(sel, 0.0, NEG)
    if n_blk < LANES:
        sneg_t = jnp.concatenate([sneg_t, jnp.zeros((LANES - n_blk, tq), F32)], axis=0)
    sneg = sneg_t.T.astype(BF16)
    for hh in range(NSA_HPG):
        qaug_ref[hh * tq:(hh + 1) * tq, NSA_DH:2 * NSA_DH] = sneg
    w_pv_from(0, vwt_ref[0, 0, i])
    for hh in range(NSA_HPG):
        hs = slice(hh * tq, (hh + 1) * tq)
        mix_ref[:, hs] = mix_ref[:, hs] + gtst[2 * NSA_HPG + hh:2 * NSA_HPG + hh + 1] * w_finish(hs)

    reset, s_to, pv_from, finish = stream(m_ref, acc_ref, s_ref, tmax_ref)
    reset()
    n_far = prev1
    n_pairs = jnp.maximum(n_far - 1, 0) // 2
    s_to(0, kaug_ref, qaug_ref[...], 0, off_unless(n_far >= 1))

    def far_pair(t2, carry):
        s_to(1, kaug_ref, qaug_ref[...], 2 * t2 + 1)
        pv_from(0, vst_ref[0, 0, 2 * t2])
        s_to(0, kaug_ref, qaug_ref[...], 2 * t2 + 2)
        pv_from(1, vst_ref[0, 0, 2 * t2 + 1])
        return carry

    lax.fori_loop(0, n_pairs, far_pair, 0)
    c0 = 2 * n_pairs
    two_left = n_far - c0 == 2

    @pl.when(two_left)
    def _():
        s_to(1, kaug_ref, qaug_ref[...], c0 + 1)
        pv_from(0, vst_ref[0, 0, c0])
        s_to(0, kaug_ref, qaug_ref[...], prev1, d1_ref[0, 0])
        pv_from(1, vst_ref[0, 0, c0 + 1])
        s_to(1, kaug_ref, qaug_ref[...], i, d0_ref[0, 0])
        pv_from(0, vst_ref[0, 0, prev1])
        pv_from(1, vst_ref[0, 0, i])

    @pl.when(jnp.logical_not(two_left))
    def _():
        s_to(1, kaug_ref, qaug_ref[...], prev1, d1_ref[0, 0])
        pv_from(0, vst_ref[0, 0, c0])
        s_to(0, kaug_ref, qaug_ref[...], i, d0_ref[0, 0])
        pv_from(1, vst_ref[0, 0, prev1])
        pv_from(0, vst_ref[0, 0, i])

    for hh in range(NSA_HPG):
        hs = slice(hh * tq, (hh + 1) * tq)
        ot = mix_ref[:, hs] + gatest_ref[0, 0, NSA_HPG + hh:NSA_HPG + hh + 1] * finish(hs)
        y_ref[0, :, hh * NSA_DH:(hh + 1) * NSA_DH] = (ot.T * sg_ref[0, 0, hh]).astype(BF16)


def _overlap_table(s):
    n_cmp = (s - CMP_BLOCK) // CMP_STRIDE + 1
    n_slc = s // SLC_BLOCK
    cst = np.arange(n_cmp)[:, None] * CMP_STRIDE
    sst = np.arange(n_slc)[None, :] * SLC_BLOCK
    ov = np.clip(np.minimum(cst + CMP_BLOCK, sst + SLC_BLOCK) - np.maximum(cst, sst), 0, None) / CMP_STRIDE
    k_cols = -(-(CMP_PAD + s // CMP_STRIDE) // LANES) * LANES
    full = np.zeros((LANES, k_cols), np.float32)
    full[:n_slc, CMP_PAD:CMP_PAD + n_cmp] = ov.T
    return jnp.asarray(full, BF16)


def _nsa(q, kaug, vst, kw, vwt, kcmp, vcmpt, gatest, sg, bias):
    b, g, hpg, s, dh = q.shape
    tq = NSA_TQ
    assert WINDOW == 2 * tq and s % tq == 0 and s // SLC_BLOCK <= LANES and s // SLC_BLOCK >= SLC_TOPK
    n_pad = kcmp.shape[2]
    rows = hpg * tq
    seq_spec = pl.BlockSpec((1, 1, s, dh), lambda i, j, k: (i, j, 0, 0))
    seqt_spec = pl.BlockSpec((1, 1, s // tq, V_ROWS, tq), lambda i, j, k: (i, j, 0, 0, 0))
    qlike_spec = pl.BlockSpec((1, 1, hpg, tq, dh), lambda i, j, k: (i, j, 0, k, 0))

    def bias_spec(idx, needs_tiles_before=0):
        return pl.BlockSpec((1, 1, tq, rows),
                            lambda i, j, k: (j, jnp.where(k >= needs_tiles_before, idx, BIAS_MASKED), 0, 0))

    return pl.pallas_call(
        _nsa_kernel,
        out_shape=jax.ShapeDtypeStruct((b, s, g * hpg * dh), BF16),
        grid=(b, g, s // tq),
        in_specs=[qlike_spec,
                  pl.BlockSpec((1, 1, s, 2 * dh), lambda i, j, k: (i, j, 0, 0)),
                  seqt_spec, seq_spec, seqt_spec,
                  pl.BlockSpec((1, 1, n_pad, dh), lambda i, j, k: (i, j, 0, 0)),
                  pl.BlockSpec((1, 1, dh + LANES, vcmpt.shape[3]), lambda i, j, k: (i, j, 0, 0)),
                  pl.BlockSpec((1, 1, LANES, tq), lambda i, j, k: (i, j, 0, k)),
                  qlike_spec,
                  bias_spec(0), bias_spec(1, 1), bias_spec(2, 2), bias_spec(3)],
        out_specs=pl.BlockSpec((1, tq, hpg * dh), lambda i, j, k: (i, k, j)),
        scratch_shapes=[pltpu.VMEM((rows, 2 * dh), BF16),
                        pltpu.VMEM((1, rows), F32),
                        pltpu.VMEM((V_ROWS, rows), F32),
                        pltpu.VMEM((2, tq, rows), F32),
                        pltpu.VMEM((n_pad, rows), F32),
                        pltpu.VMEM((1, rows), F32),
                        pltpu.VMEM((V_ROWS, rows), F32),
                        pltpu.VMEM((2, tq, rows), F32),
                        pltpu.VMEM((dh, rows), F32),
                        pltpu.VMEM((2, 1, rows), F32),
                        pltpu.VMEM((2, 1, rows), F32)],
        compiler_params=_params(("arbitrary", "arbitrary", "arbitrary")),
        name="nsa_attention",
    )(q, kaug, vst, kw, vwt, kcmp, vcmpt, gatest, sg, bias, bias, bias, bias)


def _odd_out_kernel(h_ref, yn_ref, ym_ref, wn_ref, wm_ref, fg_ref, o_ref, *, final_norm, sub):
    for r0 in range(0, h_ref.shape[0], sub):
        rows = slice(r0, r0 + sub)
        out = h_ref[rows] + _mm(yn_ref[rows], wn_ref[...]) + _mm(ym_ref[rows], wm_ref[...])
        o_ref[rows] = _rms(out, fg_ref[...]) if final_norm else out


def _odd_out(h, yn, ym, w_out, final_g, final_norm, tm=1024, sub=256):
    b, s, d = h.shape
    t = b * s
    nw = NSA_HEADS * NSA_DH
    wout = w_out.astype(BF16)
    out = pl.pallas_call(
        functools.partial(_odd_out_kernel, final_norm=final_norm, sub=sub),
        out_shape=jax.ShapeDtypeStruct((t, d), F32),
        grid=(t // tm,),
        in_specs=[pl.BlockSpec((tm, d), lambda i: (i, 0)),
                  pl.BlockSpec((tm, nw), lambda i: (i, 0)),
                  pl.BlockSpec((tm, MEM_WIDTH), lambda i: (i, 0)),
                  _const_spec((nw, d)), _const_spec((MEM_WIDTH, d)), _const_spec((1, d))],
        out_specs=pl.BlockSpec((tm, d), lambda i: (i, 0)),
        compiler_params=_params(("arbitrary",)),
        name="odd_out_proj",
    )(h.reshape(t, d), yn.reshape(t, nw), ym.reshape(t, MEM_WIDTH), wout[:nw], wout[nw:], final_g.reshape(1, d))
    return out.reshape(b, s, d)


def _final_norm_kernel(h_ref, g_ref, o_ref):
    o_ref[...] = _rms(h_ref[...], g_ref[...])


def _final_norm(h, final_g, tm=512):
    b, s, d = h.shape
    t = b * s
    out = pl.pallas_call(
        _final_norm_kernel,
        out_shape=jax.ShapeDtypeStruct((t, d), F32),
        grid=(t // tm,),
        in_specs=[pl.BlockSpec((tm, d), lambda i: (i, 0)), _const_spec((1, d))],
        out_specs=pl.BlockSpec((tm, d), lambda i: (i, 0)),
        compiler_params=_params(("arbitrary",)),
        name="final_norm",
    )(h.reshape(t, d), final_g.reshape(1, d))
    return out.reshape(b, s, d)


def kernel(x, mem, norm_g, final_g, mem_norm_g, rel_bias, ev_w_in, ev_pool_w, ev_pool_scale, ev_w_mem_kv, ev_w_out,
           od_w_in, od_cmp_pe, od_cmp_w1, od_cmp_b1, od_cmp_w2, od_w_mem_kv, od_w_out):
    depth = norm_g.shape[0]
    w_mem = [(ev_w_mem_kv if i % 2 == 0 else od_w_mem_kv)[i // 2] for i in range(depth)]
    memkv = _memkv(mem, mem_norm_g, jnp.concatenate(w_mem, axis=1).astype(BF16))
    bias = _bias_tiles(rel_bias) if depth > 1 else None
    h = x
    for i in range(depth):
        j = i // 2
        last = i == depth - 1
        if i % 2 == 0:
            h = _even_layer(h, memkv, i, norm_g[i], ev_w_in[j], ev_pool_w[j], ev_pool_scale[j], ev_w_out[j])
            if last:
                h = _final_norm(h, final_g)
        else:
            (q, kaug, vst, kw, vwt, kca, kcb, vca, vcb, gatest, sg, ym) = _odd_in(
                h, memkv, i, norm_g[i], od_w_in[j], od_cmp_pe[j])
            kcmp, vcmp = _compress(kca, kcb, vca, vcb, od_cmp_w1[j], od_cmp_b1[j], od_cmp_w2[j])
            yn = _nsa(q, kaug, vst, kw, vwt, kcmp, vcmp, gatest, sg, bias)
            h = _odd_out(h, yn, ym, od_w_out[j], final_g, last)
    return h
```

```python
import functools
import math

import numpy as np
import jax
import jax.numpy as jnp
from jax import lax
from jax.experimental import pallas as pl
from jax.experimental.pallas import tpu as pltpu

F32 = jnp.float32
BF16 = jnp.bfloat16

D_MODEL = 1024
D_INNER = 2048
N_MEM = 256
EPS = 1e-6
NEG = -1e30
BIG = 1e30

POOL_WINDOWS = (2, 4, 8, 16)
POOL_WIDTH = 768
POOL_GROUP = 192
POOL_HALO = 16

RET_HEADS = 4
RET_DK = 128
RET_DV = 192
RET_CHUNK = 128
ROPE_BASE = 10000.0

MEM_HEADS = 4
MEM_DH = 128
MEM_WIDTH = 512

NSA_HEADS = 12
NSA_KV = 2
NSA_HPG = 6
NSA_DH = 128
CMP_BLOCK = 32
CMP_STRIDE = 16
CMP_HIDDEN = 256
SLC_BLOCK = 64
SLC_TOPK = 8
WINDOW = 512
REL_BUCKETS = 32
REL_MAX_DIST = 128

LANES = 128
NSA_TQ = 256
CMP_NEAR = 32
CMP_PAD = 16
LOG2E = math.log2(math.e)
V_ROWS = NSA_DH + 16
LOWEST = -3.0e38
KNOCKED = -3.3e38
BIAS_MASKED = 4

VMEM_LIMIT = 56 * 1024 * 1024


def _mm(a, b):
    return jnp.dot(a, b, preferred_element_type=F32)


def _mm_nt(a, b):
    return lax.dot_general(a, b, (((1,), (1,)), ((), ())), preferred_element_type=F32)


def _mm_tn(a, b):
    return lax.dot_general(a, b, (((0,), (0,)), ((), ())), preferred_element_type=F32)


def _rms(x, g):
    return x * lax.rsqrt(jnp.mean(x * x, axis=-1, keepdims=True) + EPS) * g


def _silu(x):
    return x * jax.nn.sigmoid(x)


def _const_spec(shape):
    nd = len(shape)
    return pl.BlockSpec(shape, lambda *_: (0,) * nd, pipeline_mode=pl.Buffered(1))


def _params(sem):
    return pltpu.CompilerParams(dimension_semantics=sem, vmem_limit_bytes=VMEM_LIMIT)


def _bucket_starts():
    n = np.arange(REL_MAX_DIST + 1)
    max_exact = REL_BUCKETS // 2
    nf = np.maximum(n, 1).astype(np.float32)
    large = max_exact + (np.log(nf / np.float32(max_exact)) / np.float32(math.log(REL_MAX_DIST / max_exact))
                         * np.float32(REL_BUCKETS - max_exact)).astype(np.int32)
    bucket = np.where(n < max_exact, n, np.minimum(large, REL_BUCKETS - 1))
    assert np.all(np.diff(bucket) >= 0) and bucket[-1] == REL_BUCKETS - 1
    return tuple(int(np.argmax(bucket >= b)) for b in range(REL_BUCKETS))


def _bias_kernel(tab_ref, rel_ref, out_ref, *, boxes, starts):
    h = pl.program_id(0)

    def lookup(rel):
        far = tab_ref[REL_BUCKETS - 1, h]
        val = jnp.full(rel.shape, tab_ref[0, h] - far, F32)
        for b in range(1, REL_BUCKETS - 1):
            val = jnp.where(rel >= starts[b], tab_ref[b, h] - far, val)
        val = jnp.where(rel >= starts[REL_BUCKETS - 1], 0.0, val)
        return jnp.where(rel < 0, NEG, val * LOG2E)

    for kind, box in enumerate(boxes):
        out_ref[0, kind] = jnp.where(rel_ref[kind] < 0, NEG, 0.0)
        if box is not None:
            r0, r1, c0, c1 = box
            out_ref[0, kind, r0:r1, c0:c1] = lookup(rel_ref[kind, r0:r1, c0:c1])


def _bias_tiles(rel_bias):
    tq = NSA_TQ
    r = np.arange(tq)[:, None]
    c = np.arange(tq)[None, :]
    d0 = r - c
    d1 = tq + r - c
    dw = np.where(c > r, WINDOW + r - c, -1)
    gc = np.where(c < CMP_NEAR, r - CMP_STRIDE * c + (CMP_STRIDE * CMP_PAD - (CMP_BLOCK - 1)), -1)
    rel_np = np.stack([d0.T, d1.T, dw.T, gc.T, np.full((tq, tq), -1)]).astype(np.int32)
    boxes = []
    for tile in rel_np:
        rr, cc = np.nonzero((tile >= 0) & (tile < REL_MAX_DIST))
        boxes.append(None if rr.size == 0 else tuple(int(v) for v in (
            rr.min() // 8 * 8, -(-(rr.max() + 1) // 8) * 8, cc.min() // LANES * LANES, -(-(cc.max() + 1) // LANES) * LANES)))
    rel = jnp.asarray(rel_np)
    nt = rel.shape[0]
    return pl.pallas_call(
        functools.partial(_bias_kernel, boxes=tuple(boxes), starts=_bucket_starts()),
        out_shape=jax.ShapeDtypeStruct((NSA_KV, nt, tq, NSA_HPG * tq), F32),
        grid=(NSA_HEADS,),
        in_specs=[pl.BlockSpec(memory_space=pltpu.SMEM),
                  _const_spec((nt, tq, tq))],
        out_specs=pl.BlockSpec((1, nt, tq, tq), lambda h: (h // NSA_HPG, 0, 0, h % NSA_HPG)),
        compiler_params=_params(("arbitrary",)),
        name="bias_tiles",
    )(rel_bias.astype(F32), rel)


def _memkv_kernel(mem_ref, g_ref, w_ref, out_ref):
    y = _rms(mem_ref[0], g_ref[...]).astype(BF16)
    out_ref[0] = _mm(y, w_ref[...]).astype(BF16)


def _memkv(mem, mem_norm_g, w_all):
    b, m, d = mem.shape
    n = w_all.shape[1]
    return pl.pallas_call(
        _memkv_kernel,
        out_shape=jax.ShapeDtypeStruct((b, m, n), BF16),
        grid=(b,),
        in_specs=[pl.BlockSpec((1, m, d), lambda i: (i, 0, 0)),
                  _const_spec((1, d)),
                  _const_spec((d, n))],
        out_specs=pl.BlockSpec((1, m, n), lambda i: (i, 0, 0)),
        compiler_params=_params(("arbitrary",)),
        name="mem_kv",
    )(mem, mem_norm_g.reshape(1, d), w_all)


def _mem_attention(xq, mk, mv):
    outs = []
    for hd in range(MEM_HEADS):
        sl = slice(hd * MEM_DH, (hd + 1) * MEM_DH)
        qm = (xq[:, sl] * (MEM_DH ** -0.5)).astype(BF16)
        s = _mm_nt(qm, mk[:, sl])
        p = jnp.exp(s - jnp.max(s, axis=-1, keepdims=True))
        l = jnp.sum(p, axis=-1, keepdims=True)
        outs.append(_mm(p.astype(BF16), mv[:, sl]) / l)
    return outs


EV_ZA = 0
EV_RQ = EV_ZA + POOL_WIDTH
EV_RK = EV_RQ + RET_HEADS * RET_DK
EV_RV = EV_RK + RET_HEADS * RET_DK
EV_XQ = EV_RV + RET_HEADS * RET_DV
EV_GA = EV_XQ + MEM_WIDTH
EV_GR = EV_GA + POOL_WIDTH
EV_GM = EV_GR + RET_HEADS * RET_DV
EV_COLS = EV_GM + MEM_WIDTH
EV_YA = 0
EV_YR = POOL_WIDTH
EV_YM = EV_YR + RET_HEADS * RET_DV
EV_YCOLS = EV_YM + MEM_WIDTH
EV_SUB = 256


def _even_kernel(gch_ref, h_ref, g_ref, win_ref, wbd_ref, pscale_ref, cos_ref, sin_ref, decay_ref, xi_ref,
                 zeta_ref, mk_ref, mv_ref, wout_ref, o_ref, ext_ref, state_ref, y_ref):
    si = pl.program_id(1)
    ts = h_ref.shape[1]

    @pl.when(si == 0)
    def _():
        ext_ref[0:POOL_HALO, :] = jnp.zeros((POOL_HALO, POOL_WIDTH), F32)
        state_ref[...] = jnp.zeros(state_ref.shape, F32)

    for r0 in range(0, ts, EV_SUB):
        _even_subtile(si * ts + r0, slice(r0, r0 + EV_SUB), gch_ref, h_ref, g_ref, win_ref, wbd_ref, pscale_ref,
                      cos_ref, sin_ref, decay_ref, xi_ref, zeta_ref, mk_ref, mv_ref, wout_ref, o_ref, ext_ref,
                      state_ref, y_ref)


def _even_subtile(t0, tile, gch_ref, h_ref, g_ref, win_ref, wbd_ref, pscale_ref, cos_ref, sin_ref, decay_ref, xi_ref,
                  zeta_ref, mk_ref, mv_ref, wout_ref, o_ref, ext_ref, state_ref, y_ref):
    ts = EV_SUB
    r0 = tile.start
    h = h_ref[0, tile]
    u = _rms(h, g_ref[...]).astype(BF16)

    def proj(start, width):
        return _mm(u, win_ref[:, start:start + width])

    ext_ref[POOL_HALO:, :] = proj(EV_ZA, POOL_WIDTH)
    e = ext_ref[...]
    s2 = e + pltpu.roll(e, 1, 0)
    s4 = s2 + pltpu.roll(s2, 2, 0)
    s8 = s4 + pltpu.roll(s4, 4, 0)
    s16 = s8 + pltpu.roll(s8, 8, 0)
    lane = lax.broadcasted_iota(jnp.int32, e.shape, 1)
    row = lax.broadcasted_iota(jnp.int32, e.shape, 0)
    tpos = t0 + row - POOL_HALO
    g0, g1, g2 = lane < POOL_GROUP, lane < 2 * POOL_GROUP, lane < 3 * POOL_GROUP
    wsum = jnp.where(g0, s2, jnp.where(g1, s4, jnp.where(g2, s8, s16)))
    wlen = jnp.where(g0, POOL_WINDOWS[0], jnp.where(g1, POOL_WINDOWS[1],
                                                    jnp.where(g2, POOL_WINDOWS[2], POOL_WINDOWS[3])))
    cnt = jnp.maximum(jnp.minimum(tpos + 1, wlen), 1).astype(F32)
    pooled = (wsum / cnt - e)[POOL_HALO:]
    ext_ref[0:POOL_HALO, :] = e[ts:ts + POOL_HALO]
    a = _mm(pooled.astype(BF16), wbd_ref[...]) * pscale_ref[...]
    y_ref[tile, EV_YA:EV_YA + POOL_WIDTH] = (a * _silu(proj(EV_GA, POOL_WIDTH))).astype(BF16)

    cos = cos_ref[tile]
    sin = sin_ref[tile]
    zv = proj(EV_RV, RET_HEADS * RET_DV).astype(BF16)
    gate_r = _silu(proj(EV_GR, RET_HEADS * RET_DV))
    q_rot, k_rot = [], []
    for hd in range(RET_HEADS):
        qh = proj(EV_RQ + hd * RET_DK, RET_DK)
        kh = proj(EV_RK + hd * RET_DK, RET_DK)
        q_rot.append((qh * cos + pltpu.roll(qh, RET_DK // 2, 1) * sin) * (RET_DK ** -0.5))
        k_rot.append(kh * cos + pltpu.roll(kh, RET_DK // 2, 1) * sin)
    for c in range(ts // RET_CHUNK):
        rows = slice(c * RET_CHUNK, (c + 1) * RET_CHUNK)
        normed = []
        for hd in range(RET_HEADS):
            qc, kc = q_rot[hd][rows], k_rot[hd][rows]
            vc = zv[rows, hd * RET_DV:(hd + 1) * RET_DV]
            att = _mm_nt(qc.astype(BF16), kc.astype(BF16)) * decay_ref[hd]
            state = state_ref[hd]
            o = _mm(att.astype(BF16), vc) + _mm((qc * xi_ref[hd]).astype(BF16), state.astype(BF16))
            kv = _mm_tn((kc * zeta_ref[hd]).astype(BF16), vc)
            state_ref[hd] = state * gch_ref[hd] + kv
            dlt = o - jnp.mean(o, axis=-1, keepdims=True)
            normed.append(dlt * lax.rsqrt(jnp.mean(dlt * dlt, axis=-1, keepdims=True) + EPS))
        y_ref[r0 + c * RET_CHUNK:r0 + (c + 1) * RET_CHUNK, EV_YR:EV_YM] = (
            jnp.concatenate(normed, axis=-1) * gate_r[rows]).astype(BF16)

    xq = proj(EV_XQ, MEM_WIDTH)
    gm = _silu(proj(EV_GM, MEM_WIDTH))
    for hd, om in enumerate(_mem_attention(xq, mk_ref[0], mv_ref[0])):
        sl = slice(hd * MEM_DH, (hd + 1) * MEM_DH)
        y_ref[tile, EV_YM + hd * MEM_DH:EV_YM + (hd + 1) * MEM_DH] = (om * gm[:, sl]).astype(BF16)

    o_ref[0, tile] = h + _mm(y_ref[tile], wout_ref[...])


def _retention_tables(s):
    half = RET_DK // 2
    inv = ROPE_BASE ** (-jnp.arange(half, dtype=F32) / half)
    ang = jnp.arange(s, dtype=F32)[:, None] * inv[None, :]
    cos, sin = jnp.cos(ang), jnp.sin(ang)
    cos_t = jnp.concatenate([cos, cos], axis=-1)
    sin_t = jnp.concatenate([-sin, sin], axis=-1)
    c = RET_CHUNK
    log_g = jnp.log(1.0 - jnp.exp2(-5.0 - jnp.arange(RET_HEADS, dtype=F32)))
    n = jnp.arange(c, dtype=F32)
    diff = n[:, None] - n[None, :]
    decay = jnp.where(diff >= 0, jnp.exp(log_g[:, None, None] * jnp.maximum(diff, 0.0)), 0.0)
    xi = jnp.exp(log_g[:, None] * (n + 1.0))
    zeta = jnp.exp(log_g[:, None] * (c - 1.0 - n))
    g_chunk = jnp.exp(log_g * c)
    xi_t = jnp.broadcast_to(xi[:, :, None], (RET_HEADS, c, RET_DK))
    zeta_t = jnp.broadcast_to(zeta[:, :, None], (RET_HEADS, c, RET_DK))
    return cos_t, sin_t, decay, xi_t, zeta_t, g_chunk


def _even_layer(h, memkv, layer, g, w_in, pool_w, pool_scale, w_out, ts=2 * EV_SUB):
    b, s, d = h.shape
    assert w_in.shape == (d, EV_COLS) and w_out.shape == (EV_YCOLS, d)
    win = w_in.astype(BF16)
    wout = w_out.astype(BF16)
    wbd = jnp.zeros((POOL_WIDTH, POOL_WIDTH), F32)
    for gi in range(len(POOL_WINDOWS)):
        sl = slice(gi * POOL_GROUP, (gi + 1) * POOL_GROUP)
        wbd = wbd.at[sl, sl].set(pool_w[gi])
    wbd = wbd.astype(BF16)
    cos_t, sin_t, decay, xi_t, zeta_t, g_chunk = _retention_tables(s)
    kv_blk = 2 * layer
    return pl.pallas_call(
        _even_kernel,
        out_shape=jax.ShapeDtypeStruct((b, s, d), F32),
        grid=(b, s // ts),
        in_specs=[pl.BlockSpec(memory_space=pltpu.SMEM),
                  pl.BlockSpec((1, ts, d), lambda i, j: (i, j, 0)),
                  _const_spec((1, d)),
                  _const_spec((d, EV_COLS)),
                  _const_spec((POOL_WIDTH, POOL_WIDTH)),
                  _const_spec((1, POOL_WIDTH)),
                  pl.BlockSpec((ts, RET_DK), lambda i, j: (j, 0)),
                  pl.BlockSpec((ts, RET_DK), lambda i, j: (j, 0)),
                  _const_spec((RET_HEADS, RET_CHUNK, RET_CHUNK)),
                  _const_spec((RET_HEADS, RET_CHUNK, RET_DK)),
                  _const_spec((RET_HEADS, RET_CHUNK, RET_DK)),
                  pl.BlockSpec((1, N_MEM, MEM_WIDTH), lambda i, j: (i, 0, kv_blk)),
                  pl.BlockSpec((1, N_MEM, MEM_WIDTH), lambda i, j: (i, 0, kv_blk + 1)),
                  _const_spec((EV_YCOLS, d))],
        out_specs=pl.BlockSpec((1, ts, d), lambda i, j: (i, j, 0)),
        scratch_shapes=[pltpu.VMEM((POOL_HALO + EV_SUB, POOL_WIDTH), F32),
                        pltpu.VMEM((RET_HEADS, RET_DK, RET_DV), F32),
                        pltpu.VMEM((ts, EV_YCOLS), BF16)],
        compiler_params=_params(("arbitrary", "arbitrary")),
        name="even_layer",
    )(g_chunk, h, g.reshape(1, d), win, wbd, pool_scale.reshape(1, POOL_WIDTH), cos_t, sin_t, decay, xi_t, zeta_t,
      memkv, memkv, wout)


KV_W = NSA_KV * NSA_DH
OD_Q = 0
OD_KV = OD_Q + NSA_HEADS * NSA_DH
OD_A = OD_KV + 6 * KV_W
PIECE_KC, PIECE_VC, PIECE_KS, PIECE_VS, PIECE_KW, PIECE_VW = range(6)
OD_GL = OD_A
OD_XQ = OD_GL + NSA_KV * LANES
OD_GN = OD_XQ + MEM_WIDTH
OD_GM = OD_GN + NSA_HEADS * NSA_DH
OD_COLS = OD_GM + MEM_WIDTH


def _odd_in_kernel(h_ref, g_ref, wraw_ref, wb_ref, pe_ref, perm_ref, mk_ref, mv_ref,
                   q_ref, kaug_ref, vst_ref, kw_ref, vwt_ref, kca_ref, kcb_ref, vca_ref, vcb_ref, gatest_ref,
                   sg_ref, ym_ref, wa_ref):
    @pl.when((pl.program_id(0) == 0) & (pl.program_id(1) == 0))
    def _():
        for c0 in range(0, OD_A, 4 * LANES):
            wa_ref[:, c0:c0 + 4 * LANES] = wraw_ref[:, c0:c0 + 4 * LANES].astype(BF16)

    si = pl.program_id(1)
    n_sub = h_ref.shape[1] // NSA_TQ
    for sub in range(n_sub):
        _odd_in_subtile(si * n_sub + sub, sub, h_ref, g_ref, wa_ref, wb_ref, pe_ref, perm_ref, mk_ref, mv_ref,
                        q_ref, kaug_ref, vst_ref, kw_ref, vwt_ref, kca_ref, kcb_ref, vca_ref, vcb_ref, gatest_ref,
                        sg_ref, ym_ref)


def _odd_in_subtile(tile_idx, sub, h_ref, g_ref, wa_ref, wb_ref, pe_ref, perm_ref, mk_ref, mv_ref,
                    q_ref, kaug_ref, vst_ref, kw_ref, vwt_ref, kca_ref, kcb_ref, vca_ref, vcb_ref, gatest_ref,
                    sg_ref, ym_ref):
    ts = NSA_TQ
    tile = slice(sub * ts, (sub + 1) * ts)
    u = _rms(h_ref[0, tile], g_ref[...]).astype(BF16)

    def proj(start, width):
        if start < OD_A:
            return _mm(u, wa_ref[:, start:start + width])
        return _mm(u, wb_ref[:, start - OD_A:start - OD_A + width])

    zq = proj(OD_Q, NSA_HEADS * NSA_DH) * (NSA_DH ** -0.5 * LOG2E)
    zg = _silu(proj(OD_GN, NSA_HEADS * NSA_DH))
    for g in range(NSA_KV):
        for hh in range(NSA_HPG):
            sl = slice((g * NSA_HPG + hh) * NSA_DH, (g * NSA_HPG + hh + 1) * NSA_DH)
            q_ref[0, g, hh, tile] = zq[:, sl].astype(BF16)
            sg_ref[0, g, hh, tile] = zg[:, sl]

    zkv = proj(OD_KV, 6 * KV_W)
    lane = lax.broadcasted_iota(jnp.int32, (ts, LANES), 1)
    blk = (tile_idx * ts + lax.broadcasted_iota(jnp.int32, (ts, LANES), 0)) // SLC_BLOCK
    onehot = jnp.where(lane == blk, 1.0, 0.0).astype(BF16)
    ones_rows = jnp.where(lax.broadcasted_iota(jnp.int32, (V_ROWS - NSA_DH, ts), 0) == 0, 1.0, 0.0).astype(BF16)
    zgl = jax.nn.sigmoid(proj(OD_GL, NSA_KV * LANES))
    flat_in = []
    for g in range(NSA_KV):
        def piece(idx):
            off = idx * KV_W + g * NSA_DH
            return zkv[:, off:off + NSA_DH]
        kaug_ref[0, g, tile, 0:NSA_DH] = piece(PIECE_KS).astype(BF16)
        kaug_ref[0, g, tile, NSA_DH:2 * NSA_DH] = onehot
        vst_ref[0, g, sub, 0:NSA_DH] = piece(PIECE_VS).T.astype(BF16)
        vst_ref[0, g, sub, NSA_DH:V_ROWS] = ones_rows
        kw_ref[0, g, tile] = piece(PIECE_KW).astype(BF16)
        vwt_ref[0, g, sub, 0:NSA_DH] = piece(PIECE_VW).T.astype(BF16)
        vwt_ref[0, g, sub, NSA_DH:V_ROWS] = ones_rows
        kc, vc = piece(PIECE_KC), piece(PIECE_VC)
        flat_in += [(kc + pe_ref[0, 0]).astype(BF16), (kc + pe_ref[0, 1]).astype(BF16),
                    (vc + pe_ref[1, 0]).astype(BF16), (vc + pe_ref[1, 1]).astype(BF16)]
        gatest_ref[0, g, :, tile] = zgl[:, g * LANES:(g + 1) * LANES].T

    nj = ts // CMP_STRIDE
    perm = _mm(perm_ref[...], jnp.concatenate(flat_in, axis=1)).astype(BF16)
    for c, out_ref in enumerate((kca_ref, kcb_ref, vca_ref, vcb_ref) * NSA_KV):
        for l in range(CMP_STRIDE):
            out_ref[0, c // 4, sub * nj:(sub + 1) * nj, l * NSA_DH:(l + 1) * NSA_DH] = (
                perm[l * nj:(l + 1) * nj, c * NSA_DH:(c + 1) * NSA_DH])

    xq = proj(OD_XQ, MEM_WIDTH)
    gm = _silu(proj(OD_GM, MEM_WIDTH))
    for hd, om in enumerate(_mem_attention(xq, mk_ref[0], mv_ref[0])):
        sl = slice(hd * MEM_DH, (hd + 1) * MEM_DH)
        ym_ref[0, tile, sl] = (om * gm[:, sl]).astype(BF16)


def _odd_in(h, memkv, layer, g, w_in, cmp_pe, ts=2 * NSA_TQ):
    b, s, d = h.shape
    assert w_in.shape[1] == OD_A + 3 * NSA_HEADS + MEM_WIDTH + D_INNER
    gl, xq, gate = jnp.split(w_in[:, OD_A:], [3 * NSA_HEADS, 3 * NSA_HEADS + MEM_WIDTH], axis=1)
    gl = gl.reshape(d, 3, NSA_KV, NSA_HPG).transpose(0, 2, 1, 3).reshape(d, NSA_KV, 3 * NSA_HPG)
    gl = jnp.pad(gl, ((0, 0), (0, 0), (0, LANES - 3 * NSA_HPG))).reshape(d, NSA_KV * LANES)
    wb = jnp.concatenate([w.astype(BF16) for w in (gl, xq, gate)], axis=1)
    sub = NSA_TQ
    reps = sub // CMP_STRIDE
    pe = jnp.stack([jnp.stack([jnp.tile(cmp_pe[kv, :CMP_STRIDE], (reps, 1)),
                               jnp.tile(cmp_pe[kv, CMP_STRIDE:], (reps, 1))]) for kv in range(2)])
    kv_blk = 2 * layer
    assert ts % sub == 0
    nj = sub // CMP_STRIDE
    perm = np.zeros((sub, sub), np.float32)
    perm[np.arange(sub), (np.arange(sub) % nj) * CMP_STRIDE + np.arange(sub) // nj] = 1.0
    head_t = jax.ShapeDtypeStruct((b, NSA_KV, s, NSA_DH), BF16)
    head_spec = pl.BlockSpec((1, NSA_KV, ts, NSA_DH), lambda i, j: (i, 0, j, 0))
    headt_t = jax.ShapeDtypeStruct((b, NSA_KV, s // sub, V_ROWS, sub), BF16)
    headt_spec = pl.BlockSpec((1, NSA_KV, ts // sub, V_ROWS, sub), lambda i, j: (i, 0, j, 0, 0))
    flat_t = jax.ShapeDtypeStruct((b, NSA_KV, s // CMP_STRIDE, CMP_STRIDE * NSA_DH), BF16)
    flat_spec = pl.BlockSpec((1, NSA_KV, ts // CMP_STRIDE, CMP_STRIDE * NSA_DH), lambda i, j: (i, 0, j, 0))
    qlike_spec = pl.BlockSpec((1, NSA_KV, NSA_HPG, ts, NSA_DH), lambda i, j: (i, 0, 0, j, 0))
    return pl.pallas_call(
        _odd_in_kernel,
        out_shape=[jax.ShapeDtypeStruct((b, NSA_KV, NSA_HPG, s, NSA_DH), BF16),
                   jax.ShapeDtypeStruct((b, NSA_KV, s, 2 * NSA_DH), BF16),
                   headt_t, head_t, headt_t, flat_t, flat_t, flat_t, flat_t,
                   jax.ShapeDtypeStruct((b, NSA_KV, LANES, s), F32),
                   jax.ShapeDtypeStruct((b, NSA_KV, NSA_HPG, s, NSA_DH), F32),
                   jax.ShapeDtypeStruct((b, s, MEM_WIDTH), BF16)],
        grid=(b, s // ts),
        in_specs=[pl.BlockSpec((1, ts, d), lambda i, j: (i, j, 0)),
                  _const_spec((1, d)),
                  _const_spec((d, OD_A)),
                  _const_spec((d, OD_COLS - OD_A)),
                  _const_spec((2, 2, sub, NSA_DH)),
                  _const_spec((sub, sub)),
                  pl.BlockSpec((1, N_MEM, MEM_WIDTH), lambda i, j: (i, 0, kv_blk)),
                  pl.BlockSpec((1, N_MEM, MEM_WIDTH), lambda i, j: (i, 0, kv_blk + 1))],
        out_specs=[qlike_spec,
                   pl.BlockSpec((1, NSA_KV, ts, 2 * NSA_DH), lambda i, j: (i, 0, j, 0)),
                   headt_spec, head_spec, headt_spec, flat_spec, flat_spec, flat_spec, flat_spec,
                   pl.BlockSpec((1, NSA_KV, LANES, ts), lambda i, j: (i, 0, 0, j)),
                   qlike_spec,
                   pl.BlockSpec((1, ts, MEM_WIDTH), lambda i, j: (i, j, 0))],
        scratch_shapes=[pltpu.VMEM((d, OD_A), BF16)],
        compiler_params=_params(("arbitrary", "arbitrary")),
        name="odd_in_proj",
    )(h, g.reshape(1, d), w_in, wb, pe, jnp.asarray(perm, BF16), memkv, memkv)


def _compress_kernel(xka_ref, xkb_ref, xva_ref, xvb_ref, w1a_ref, w1b_ref, b1_ref, w2_ref, ovt_ref, kc_ref, vct_ref):
    n = xka_ref.shape[2]
    k_cols = vct_ref.shape[3]

    def block_mlp(kv, xa, xb):
        first = _mm(xa[0, 0], w1a_ref[kv])
        second = _mm(xb[0, 0], w1b_ref[kv])
        hid = first + pltpu.roll(second, n - 1, 0) + b1_ref[kv]
        return _mm(_silu(hid).astype(BF16), w2_ref[kv])

    kc_ref[0, 0, 0:CMP_PAD] = jnp.zeros((CMP_PAD, NSA_DH), BF16)
    kc_ref[0, 0, CMP_PAD:] = block_mlp(0, xka_ref, xkb_ref).astype(BF16)
    vc = jnp.concatenate([jnp.zeros((CMP_PAD, NSA_DH), F32), block_mlp(1, xva_ref, xvb_ref),
                          jnp.zeros((k_cols - CMP_PAD - n, NSA_DH), F32)], axis=0)
    vct_ref[0, 0, 0:NSA_DH] = vc.T.astype(BF16)
    vct_ref[0, 0, NSA_DH:] = ovt_ref[...]


def _compress(kca, kcb, vca, vcb, w1, b1, w2):
    b, g, n, half = kca.shape
    dh = half // CMP_STRIDE
    s = n * CMP_STRIDE
    xs = (kca, kcb, vca, vcb)
    w1 = w1.astype(BF16)
    ovt = _overlap_table(s)
    k_cols = ovt.shape[1]
    x_spec = pl.BlockSpec((1, 1, n, half), lambda i, j: (i, j, 0, 0))
    return pl.pallas_call(
        _compress_kernel,
        out_shape=[jax.ShapeDtypeStruct((b, g, n + CMP_PAD, dh), BF16),
                   jax.ShapeDtypeStruct((b, g, dh + LANES, k_cols), BF16)],
        grid=(b, g),
        in_specs=[x_spec, x_spec, x_spec, x_spec,
                  _const_spec((2, half, CMP_HIDDEN)), _const_spec((2, half, CMP_HIDDEN)),
                  _const_spec((2, 1, CMP_HIDDEN)), _const_spec((2, CMP_HIDDEN, dh)),
                  _const_spec((LANES, k_cols))],
        out_specs=[pl.BlockSpec((1, 1, n + CMP_PAD, dh), lambda i, j: (i, j, 0, 0)),
                   pl.BlockSpec((1, 1, dh + LANES, k_cols), lambda i, j: (i, j, 0, 0))],
        compiler_params=_params(("arbitrary", "arbitrary")),
        name="compress",
    )(*xs, w1[:, :half], w1[:, half:], b1.reshape(2, 1, CMP_HIDDEN), w2.astype(BF16), ovt)


def _nsa_kernel(q_ref, kaug_ref, vst_ref, kw_ref, vwt_ref, kcmp_ref, vcmpt_ref, gatest_ref, sg_ref,
                d0_ref, d1_ref, dw_ref, gc_ref, y_ref, qaug_ref, m_ref, acc_ref, s_ref, sc_ref, mw_ref, accw_ref, sw_ref,
                mix_ref, tmax_ref, tmaxw_ref):
    i = pl.program_id(2)
    tq = NSA_TQ
    rows = NSA_HPG * tq
    s_len = kw_ref.shape[2]
    n_cmp = s_len // CMP_STRIDE
    n_slc = s_len // SLC_BLOCK
    t0 = i * tq
    q = q_ref[0, 0].reshape(rows, NSA_DH)


    def stream(mx_ref, ac_ref, sc2_ref, tmax_ref):
        def reset():
            mx_ref[...] = jnp.full(mx_ref.shape, LOWEST, F32)
            ac_ref[...] = jnp.zeros(ac_ref.shape, F32)

        def s_to(slot, k_ref, query, tile, bias=None):
            k0 = pl.multiple_of(tile * tq, tq)
            st = _mm_nt(k_ref[0, 0, pl.ds(k0, tq)], query)
            if bias is not None:
                st = st + bias
            sc2_ref[slot] = st
            tmax_ref[slot] = jnp.max(st, axis=0, keepdims=True)

        def pv_from(slot, vt):
            st = sc2_ref[slot]
            m_prev = mx_ref[...]
            m_new = jnp.maximum(m_prev, tmax_ref[slot])
            pt = jnp.exp2(st - m_new).astype(BF16)
            ac_ref[...] = jnp.exp2(m_prev - m_new) * ac_ref[...] + _mm(vt, pt)
            mx_ref[...] = m_new

        def finish(cols=slice(None)):
            return ac_ref[0:NSA_DH, cols] / ac_ref[NSA_DH:NSA_DH + 1, cols]

        return reset, s_to, pv_from, finish

    def off_unless(cond):
        return jnp.where(cond, 0.0, NEG)

    prev1 = jnp.maximum(i - 1, 0)
    prev2 = jnp.maximum(i - 2, 0)
    w_reset, w_s_to, w_pv_from, w_finish = stream(mw_ref, accw_ref, sw_ref, tmaxw_ref)


    n_pad = kcmp_ref.shape[2]
    j_near = pl.multiple_of(i * (tq // CMP_STRIDE), tq // CMP_STRIDE)
    qaug_ref[:, 0:NSA_DH] = q
    qaug_ref[:, NSA_DH:] = jnp.where(lax.broadcasted_iota(jnp.int32, (rows, NSA_DH), 1) == 0, 1.0, 0.0).astype(BF16)
    krow = lax.broadcasted_iota(jnp.int32, (n_pad, NSA_DH), 0)
    key_bias = jnp.where((krow >= CMP_PAD) & (krow < j_near + CMP_NEAR), 0.0, NEG).astype(BF16)
    sc_ref[...] = _mm_nt(jnp.concatenate([kcmp_ref[0, 0], key_bias], axis=1), qaug_ref[...])
    w_reset()
    w_s_to(0, kw_ref, q, prev2, dw_ref[0, 0])
    w_s_to(1, kw_ref, q, prev1, d1_ref[0, 0])
    sc_ref[pl.ds(j_near, CMP_NEAR)] = sc_ref[pl.ds(j_near, CMP_NEAR)] + gc_ref[0, 0, 0:CMP_NEAR, :]
    s_cmp = sc_ref[...]
    p_cmp = jnp.exp2(s_cmp - jnp.max(s_cmp, axis=0, keepdims=True))
    tcol = t0 + jnp.bitwise_and(lax.broadcasted_iota(jnp.int32, (1, rows), 1), tq - 1)
    scale = jnp.where(tcol >= CMP_BLOCK - 1, 1.0, 0.0) / jnp.sum(p_cmp, axis=0, keepdims=True)
    p_cmp = p_cmp.astype(BF16)
    k_cols = vcmpt_ref.shape[3]
    if k_cols > n_pad:
        p_cmp = jnp.concatenate([p_cmp, jnp.zeros((k_cols - n_pad, rows), BF16)], axis=0)
    w_pv_from(0, vwt_ref[0, 0, prev2])
    both = _mm(vcmpt_ref[0, 0], p_cmp) * scale
    w_s_to(0, kw_ref, q, i, d0_ref[0, 0])
    w_pv_from(1, vwt_ref[0, 0, prev1])
    gtst = gatest_ref[0, 0]
    n_blk = -(-n_slc // 8) * 8
    imp = both[NSA_DH:NSA_DH + n_blk, 0:tq]
    for hh in range(NSA_HPG):
        hs = slice(hh * tq, (hh + 1) * tq)
        mix_ref[:, hs] = gtst[hh:hh + 1] * both[0:NSA_DH, hs]
        if hh:
            imp = imp + both[NSA_DH:NSA_DH + n_blk, hs]

    blk = lax.broadcasted_iota(jnp.int32, (n_blk, tq), 0)
    tq_pos = t0 + lax.broadcasted_iota(jnp.int32, (n_blk, tq), 1)
    cur = tq_pos // SLC_BLOCK
    forced = (blk == 0) | (blk == cur) | (blk == cur - 1)
    future = blk * SLC_BLOCK > tq_pos
    imp = jnp.where(forced, BIG, jnp.where(future, -BIG, imp))
    imp = jnp.where(blk < n_slc, imp, LOWEST)
    blk_f = blk.astype(F32)
    sel = jnp.zeros((n_blk, tq), jnp.bool_)
    for _ in range(SLC_TOPK):
        top = jnp.max(imp, axis=0, keepdims=True)
        first = jnp.min(jnp.where(imp == top, blk_f, float(LANES)), axis=0, keepdims=True)
        hit = blk_f == first
        sel = sel | hit
        imp = jnp.where(hit, KNOCKED, imp)
    sneg_t = jnp.where(sel, 0.0, NEG)
    if n_blk < LANES:
        sneg_t = jnp.concatenate([sneg_t, jnp.zeros((LANES - n_blk, tq), F32)], axis=0)
    sneg = sneg_t.T.astype(BF16)
    for hh in range(NSA_HPG):
        qaug_ref[hh * tq:(hh + 1) * tq, NSA_DH:2 * NSA_DH] = sneg
    w_pv_from(0, vwt_ref[0, 0, i])
    for hh in range(NSA_HPG):
        hs = slice(hh * tq, (hh + 1) * tq)
        mix_ref[:, hs] = mix_ref[:, hs] + gtst[2 * NSA_HPG + hh:2 * NSA_HPG + hh + 1] * w_finish(hs)

    reset, s_to, pv_from, finish = stream(m_ref, acc_ref, s_ref, tmax_ref)
    reset()
    n_far = prev1
    n_pairs = jnp.maximum(n_far - 1, 0) // 2
    s_to(0, kaug_ref, qaug_ref[...], 0, off_unless(n_far >= 1))

    def far_pair(t2):
        s_to(1, kaug_ref, qaug_ref[...], 2 * t2 + 1)
        pv_from(0, vst_ref[0, 0, 2 * t2])
        s_to(0, kaug_ref, qaug_ref[...], 2 * t2 + 2)
        pv_from(1, vst_ref[0, 0, 2 * t2 + 1])

    def far_quad(t4, carry):
        far_pair(2 * t4)
        far_pair(2 * t4 + 1)
        return carry

    lax.fori_loop(0, n_pairs // 2, far_quad, 0)

    @pl.when(n_pairs % 2 == 1)
    def _():
        far_pair(n_pairs - 1)
    c0 = 2 * n_pairs
    two_left = n_far - c0 == 2

    @pl.when(two_left)
    def _():
        s_to(1, kaug_ref, qaug_ref[...], c0 + 1)
        pv_from(0, vst_ref[0, 0, c0])
        s_to(0, kaug_ref, qaug_ref[...], prev1, d1_ref[0, 0])
        pv_from(1, vst_ref[0, 0, c0 + 1])
        s_to(1, kaug_ref, qaug_ref[...], i, d0_ref[0, 0])
        pv_from(0, vst_ref[0, 0, prev1])
        pv_from(1, vst_ref[0, 0, i])

    @pl.when(jnp.logical_not(two_left))
    def _():
        s_to(1, kaug_ref, qaug_ref[...], prev1, d1_ref[0, 0])
        pv_from(0, vst_ref[0, 0, c0])
        s_to(0, kaug_ref, qaug_ref[...], i, d0_ref[0, 0])
        pv_from(1, vst_ref[0, 0, prev1])
        pv_from(0, vst_ref[0, 0, i])

    for hh in range(NSA_HPG):
        hs = slice(hh * tq, (hh + 1) * tq)
        ot = mix_ref[:, hs] + gatest_ref[0, 0, NSA_HPG + hh:NSA_HPG + hh + 1] * finish(hs)
        y_ref[0, :, hh * NSA_DH:(hh + 1) * NSA_DH] = (ot.T * sg_ref[0, 0, hh]).astype(BF16)


def _overlap_table(s):
    n_cmp = (s - CMP_BLOCK) // CMP_STRIDE + 1
    n_slc = s // SLC_BLOCK
    cst = np.arange(n_cmp)[:, None] * CMP_STRIDE
    sst = np.arange(n_slc)[None, :] * SLC_BLOCK
    ov = np.clip(np.minimum(cst + CMP_BLOCK, sst + SLC_BLOCK) - np.maximum(cst, sst), 0, None) / CMP_STRIDE
    k_cols = -(-(CMP_PAD + s // CMP_STRIDE) // LANES) * LANES
    full = np.zeros((LANES, k_cols), np.float32)
    full[:n_slc, CMP_PAD:CMP_PAD + n_cmp] = ov.T
    return jnp.asarray(full, BF16)


def _nsa(q, kaug, vst, kw, vwt, kcmp, vcmpt, gatest, sg, bias):
    b, g, hpg, s, dh = q.shape
    tq = NSA_TQ
    assert WINDOW == 2 * tq and s % tq == 0 and s // SLC_BLOCK <= LANES and s // SLC_BLOCK >= SLC_TOPK
    n_pad = kcmp.shape[2]
    rows = hpg * tq
    seq_spec = pl.BlockSpec((1, 1, s, dh), lambda i, j, k: (i, j, 0, 0))
    seqt_spec = pl.BlockSpec((1, 1, s // tq, V_ROWS, tq), lambda i, j, k: (i, j, 0, 0, 0))
    qlike_spec = pl.BlockSpec((1, 1, hpg, tq, dh), lambda i, j, k: (i, j, 0, k, 0))

    def bias_spec(idx, needs_tiles_before=0):
        return pl.BlockSpec((1, 1, tq, rows),
                            lambda i, j, k: (j, jnp.where(k >= needs_tiles_before, idx, BIAS_MASKED), 0, 0))

    return pl.pallas_call(
        _nsa_kernel,
        out_shape=jax.ShapeDtypeStruct((b, s, g * hpg * dh), BF16),
        grid=(b, g, s // tq),
        in_specs=[qlike_spec,
                  pl.BlockSpec((1, 1, s, 2 * dh), lambda i, j, k: (i, j, 0, 0)),
                  seqt_spec, seq_spec, seqt_spec,
                  pl.BlockSpec((1, 1, n_pad, dh), lambda i, j, k: (i, j, 0, 0)),
                  pl.BlockSpec((1, 1, dh + LANES, vcmpt.shape[3]), lambda i, j, k: (i, j, 0, 0)),
                  pl.BlockSpec((1, 1, LANES, tq), lambda i, j, k: (i, j, 0, k)),
                  qlike_spec,
                  bias_spec(0), bias_spec(1, 1), bias_spec(2, 2), bias_spec(3)],
        out_specs=pl.BlockSpec((1, tq, hpg * dh), lambda i, j, k: (i, k, j)),
        scratch_shapes=[pltpu.VMEM((rows, 2 * dh), BF16),
                        pltpu.VMEM((1, rows), F32),
                        pltpu.VMEM((V_ROWS, rows), F32),
                        pltpu.VMEM((2, tq, rows), F32),
                        pltpu.VMEM((n_pad, rows), F32),
                        pltpu.VMEM((1, rows), F32),
                        pltpu.VMEM((V_ROWS, rows), F32),
                        pltpu.VMEM((2, tq, rows), F32),
                        pltpu.VMEM((dh, rows), F32),
                        pltpu.VMEM((2, 1, rows), F32),
                        pltpu.VMEM((2, 1, rows), F32)],
        compiler_params=_params(("arbitrary", "arbitrary", "arbitrary")),
        name="nsa_attention",
    )(q, kaug, vst, kw, vwt, kcmp, vcmpt, gatest, sg, bias, bias, bias, bias)


def _odd_out_kernel(h_ref, yn_ref, ym_ref, wn_ref, wm_ref, fg_ref, o_ref, *, final_norm, sub):
    for r0 in range(0, h_ref.shape[0], sub):
        rows = slice(r0, r0 + sub)
        out = h_ref[rows] + _mm(yn_ref[rows], wn_ref[...]) + _mm(ym_ref[rows], wm_ref[...])
        o_ref[rows] = _rms(out, fg_ref[...]) if final_norm else out


def _odd_out(h, yn, ym, w_out, final_g, final_norm, tm=1024, sub=256):
    b, s, d = h.shape
    t = b * s
    nw = NSA_HEADS * NSA_DH
    wout = w_out.astype(BF16)
    out = pl.pallas_call(
        functools.partial(_odd_out_kernel, final_norm=final_norm, sub=sub),
        out_shape=jax.ShapeDtypeStruct((t, d), F32),
        grid=(t // tm,),
        in_specs=[pl.BlockSpec((tm, d), lambda i: (i, 0)),
                  pl.BlockSpec((tm, nw), lambda i: (i, 0)),
                  pl.BlockSpec((tm, MEM_WIDTH), lambda i: (i, 0)),
                  _const_spec((nw, d)), _const_spec((MEM_WIDTH, d)), _const_spec((1, d))],
        out_specs=pl.BlockSpec((tm, d), lambda i: (i, 0)),
        compiler_params=_params(("arbitrary",)),
        name="odd_out_proj",
    )(h.reshape(t, d), yn.reshape(t, nw), ym.reshape(t, MEM_WIDTH), wout[:nw], wout[nw:], final_g.reshape(1, d))
    return out.reshape(b, s, d)


def _final_norm_kernel(h_ref, g_ref, o_ref):
    o_ref[...] = _rms(h_ref[...], g_ref[...])


def _final_norm(h, final_g, tm=512):
    b, s, d = h.shape
    t = b * s
    out = pl.pallas_call(
        _final_norm_kernel,
        out_shape=jax.ShapeDtypeStruct((t, d), F32),
        grid=(t // tm,),
        in_specs=[pl.BlockSpec((tm, d), lambda i: (i, 0)), _const_spec((1, d))],
        out_specs=pl.BlockSpec((tm, d), lambda i: (i, 0)),
        compiler_params=_params(("arbitrary",)),
        name="final_norm",
    )(h.reshape(t, d), final_g.reshape(1, d))
    return out.reshape(b, s, d)


def kernel(x, mem, norm_g, final_g, mem_norm_g, rel_bias, ev_w_in, ev_pool_w, ev_pool_scale, ev_w_mem_kv, ev_w_out,
           od_w_in, od_cmp_pe, od_cmp_w1, od_cmp_b1, od_cmp_w2, od_w_mem_kv, od_w_out):
    depth = norm_g.shape[0]
    w_mem = [(ev_w_mem_kv if i % 2 == 0 else od_w_mem_kv)[i // 2] for i in range(depth)]
    memkv = _memkv(mem, mem_norm_g, jnp.concatenate(w_mem, axis=1).astype(BF16))
    bias = _bias_tiles(rel_bias) if depth > 1 else None
    h = x
    for i in range(depth):
        j = i // 2
        last = i == depth - 1
        if i % 2 == 0:
            h = _even_layer(h, memkv, i, norm_g[i], ev_w_in[j], ev_pool_w[j], ev_pool_scale[j], ev_w_out[j])
            if last:
                h = _final_norm(h, final_g)
        else:
            (q, kaug, vst, kw, vwt, kca, kcb, vca, vcb, gatest, sg, ym) = _odd_in(
                h, memkv, i, norm_g[i], od_w_in[j], od_cmp_pe[j])
            kcmp, vcmp = _compress(kca, kcb, vca, vcb, od_cmp_w1[j], od_cmp_b1[j], od_cmp_w2[j])
            yn = _nsa(q, kaug, vst, kw, vwt, kcmp, vcmp, gatest, sg, bias)
            h = _odd_out(h, yn, ym, od_w_out[j], final_g, last)
    return h
```

```python
import functools
import math

import numpy as np
import jax
import jax.numpy as jnp
from jax import lax
from jax.experimental import pallas as pl
from jax.experimental.pallas import tpu as pltpu

F32 = jnp.float32
BF16 = jnp.bfloat16

D_MODEL = 1024
D_INNER = 2048
N_MEM = 256
EPS = 1e-6
NEG = -1e30
BIG = 1e30

POOL_WINDOWS = (2, 4, 8, 16)
POOL_WIDTH = 768
POOL_GROUP = 192
POOL_HALO = 16

RET_HEADS = 4
RET_DK = 128
RET_DV = 192
RET_CHUNK = 128
ROPE_BASE = 10000.0

MEM_HEADS = 4
MEM_DH = 128
MEM_WIDTH = 512

NSA_HEADS = 12
NSA_KV = 2
NSA_HPG = 6
NSA_DH = 128
CMP_BLOCK = 32
CMP_STRIDE = 16
CMP_HIDDEN = 256
SLC_BLOCK = 64
SLC_TOPK = 8
WINDOW = 512
REL_BUCKETS = 32
REL_MAX_DIST = 128

LANES = 128
NSA_TQ = 256
CMP_NEAR = 32
CMP_PAD = 16
LOG2E = math.log2(math.e)
V_ROWS = NSA_DH + 16
LOWEST = -3.0e38
KNOCKED = -3.3e38
BIAS_MASKED = 3

VMEM_LIMIT = 56 * 1024 * 1024


def _mm(a, b):
    return jnp.dot(a, b, preferred_element_type=F32)


def _mm_nt(a, b):
    return lax.dot_general(a, b, (((1,), (1,)), ((), ())), preferred_element_type=F32)


def _mm_tn(a, b):
    return lax.dot_general(a, b, (((0,), (0,)), ((), ())), preferred_element_type=F32)


def _rms(x, g):
    return x * lax.rsqrt(jnp.mean(x * x, axis=-1, keepdims=True) + EPS) * g


def _silu(x):
    return x * jax.nn.sigmoid(x)


def _const_spec(shape):
    nd = len(shape)
    return pl.BlockSpec(shape, lambda *_: (0,) * nd, pipeline_mode=pl.Buffered(1))


def _params(sem):
    return pltpu.CompilerParams(dimension_semantics=sem, vmem_limit_bytes=VMEM_LIMIT)


def _bucket_starts():
    n = np.arange(REL_MAX_DIST + 1)
    max_exact = REL_BUCKETS // 2
    nf = np.maximum(n, 1).astype(np.float32)
    large = max_exact + (np.log(nf / np.float32(max_exact)) / np.float32(math.log(REL_MAX_DIST / max_exact))
                         * np.float32(REL_BUCKETS - max_exact)).astype(np.int32)
    bucket = np.where(n < max_exact, n, np.minimum(large, REL_BUCKETS - 1))
    assert np.all(np.diff(bucket) >= 0) and bucket[-1] == REL_BUCKETS - 1
    return tuple(int(np.argmax(bucket >= b)) for b in range(REL_BUCKETS))


def _bias_kernel(tab_ref, rel_ref, relq_ref, out_ref, cmpq_ref, *, boxes, starts):
    h = pl.program_id(0)

    def lookup(rel):
        far = tab_ref[REL_BUCKETS - 1, h]
        val = jnp.full(rel.shape, tab_ref[0, h] - far, F32)
        for b in range(1, REL_BUCKETS - 1):
            val = jnp.where(rel >= starts[b], tab_ref[b, h] - far, val)
        val = jnp.where(rel >= starts[REL_BUCKETS - 1], 0.0, val)
        return jnp.where(rel < 0, NEG, val * LOG2E)

    for kind, box in enumerate(boxes):
        out_ref[0, kind] = jnp.where(rel_ref[kind] < 0, NEG, 0.0)
        if box is not None:
            r0, r1, c0, c1 = box
            out_ref[0, kind, r0:r1, c0:c1] = lookup(rel_ref[kind, r0:r1, c0:c1])

    relq = relq_ref[...]
    lane = lax.broadcasted_iota(jnp.int32, relq.shape, 1)
    bias = jnp.where(lane < CMP_NEAR, lookup(relq), 0.0)
    hi = bias.astype(BF16)
    lo = (bias - hi.astype(F32)).astype(BF16).astype(F32)
    one = jnp.where(lane == 2 * CMP_NEAR, 1.0, 0.0)
    cmpq_ref[0] = (hi.astype(F32) + pltpu.roll(lo, CMP_NEAR, 1) + one).astype(BF16)


def _bias_tiles(rel_bias):
    tq = NSA_TQ
    r = np.arange(tq)[:, None]
    c = np.arange(tq)[None, :]
    d0 = r - c
    d1 = tq + r - c
    dw = np.where(c > r, WINDOW + r - c, -1)
    gc = np.where(c < CMP_NEAR, r - CMP_STRIDE * c + (CMP_STRIDE * CMP_PAD - (CMP_BLOCK - 1)), -1)
    rel_np = np.stack([d0.T, d1.T, dw.T, np.full((tq, tq), -1)]).astype(np.int32)
    boxes = []
    for tile in rel_np:
        rr, cc = np.nonzero((tile >= 0) & (tile < REL_MAX_DIST))
        boxes.append(None if rr.size == 0 else tuple(int(v) for v in (
            rr.min() // 8 * 8, -(-(rr.max() + 1) // 8) * 8, cc.min() // LANES * LANES, -(-(cc.max() + 1) // LANES) * LANES)))
    rel = jnp.asarray(rel_np)
    nt = rel.shape[0]
    return pl.pallas_call(
        functools.partial(_bias_kernel, boxes=tuple(boxes), starts=_bucket_starts()),
        out_shape=[jax.ShapeDtypeStruct((NSA_KV, nt, tq, NSA_HPG * tq), F32),
                   jax.ShapeDtypeStruct((NSA_KV, NSA_HPG * tq, LANES), BF16)],
        grid=(NSA_HEADS,),
        in_specs=[pl.BlockSpec(memory_space=pltpu.SMEM),
                  _const_spec((nt, tq, tq)),
                  _const_spec((tq, LANES))],
        out_specs=[pl.BlockSpec((1, nt, tq, tq), lambda h: (h // NSA_HPG, 0, 0, h % NSA_HPG)),
                   pl.BlockSpec((1, tq, LANES), lambda h: (h // NSA_HPG, h % NSA_HPG, 0))],
        compiler_params=_params(("arbitrary",)),
        name="bias_tiles",
    )(rel_bias.astype(F32), rel, jnp.asarray(gc[:, :LANES].astype(np.int32)))


def _memkv_kernel(mem_ref, g_ref, w_ref, out_ref):
    y = _rms(mem_ref[0], g_ref[...]).astype(BF16)
    out_ref[0] = _mm(y, w_ref[...]).astype(BF16)


def _memkv(mem, mem_norm_g, w_all):
    b, m, d = mem.shape
    n = w_all.shape[1]
    return pl.pallas_call(
        _memkv_kernel,
        out_shape=jax.ShapeDtypeStruct((b, m, n), BF16),
        grid=(b,),
        in_specs=[pl.BlockSpec((1, m, d), lambda i: (i, 0, 0)),
                  _const_spec((1, d)),
                  _const_spec((d, n))],
        out_specs=pl.BlockSpec((1, m, n), lambda i: (i, 0, 0)),
        compiler_params=_params(("arbitrary",)),
        name="mem_kv",
    )(mem, mem_norm_g.reshape(1, d), w_all)


def _mem_attention(xq, mk, mv):
    outs = []
    for hd in range(MEM_HEADS):
        sl = slice(hd * MEM_DH, (hd + 1) * MEM_DH)
        qm = (xq[:, sl] * (MEM_DH ** -0.5)).astype(BF16)
        s = _mm_nt(qm, mk[:, sl])
        p = jnp.exp(s - jnp.max(s, axis=-1, keepdims=True))
        l = jnp.sum(p, axis=-1, keepdims=True)
        outs.append(_mm(p.astype(BF16), mv[:, sl]) / l)
    return outs


EV_ZA = 0
EV_RQ = EV_ZA + POOL_WIDTH
EV_RK = EV_RQ + RET_HEADS * RET_DK
EV_RV = EV_RK + RET_HEADS * RET_DK
EV_XQ = EV_RV + RET_HEADS * RET_DV
EV_GA = EV_XQ + MEM_WIDTH
EV_GR = EV_GA + POOL_WIDTH
EV_GM = EV_GR + RET_HEADS * RET_DV
EV_COLS = EV_GM + MEM_WIDTH
EV_YA = 0
EV_YR = POOL_WIDTH
EV_YM = EV_YR + RET_HEADS * RET_DV
EV_YCOLS = EV_YM + MEM_WIDTH
EV_SUB = 256


def _even_kernel(gch_ref, h_ref, g_ref, win_ref, wbd_ref, pscale_ref, cos_ref, sin_ref, decay_ref, xi_ref,
                 zeta_ref, mk_ref, mv_ref, wout_ref, o_ref, ext_ref, state_ref, y_ref):
    si = pl.program_id(1)
    ts = h_ref.shape[1]

    @pl.when(si == 0)
    def _():
        ext_ref[0:POOL_HALO, :] = jnp.zeros((POOL_HALO, POOL_WIDTH), F32)
        state_ref[...] = jnp.zeros(state_ref.shape, F32)

    for r0 in range(0, ts, EV_SUB):
        _even_subtile(si * ts + r0, slice(r0, r0 + EV_SUB), gch_ref, h_ref, g_ref, win_ref, wbd_ref, pscale_ref,
                      cos_ref, sin_ref, decay_ref, xi_ref, zeta_ref, mk_ref, mv_ref, wout_ref, o_ref, ext_ref,
                      state_ref, y_ref)


def _even_subtile(t0, tile, gch_ref, h_ref, g_ref, win_ref, wbd_ref, pscale_ref, cos_ref, sin_ref, decay_ref, xi_ref,
                  zeta_ref, mk_ref, mv_ref, wout_ref, o_ref, ext_ref, state_ref, y_ref):
    ts = EV_SUB
    r0 = tile.start
    h = h_ref[0, tile]
    u = _rms(h, g_ref[...]).astype(BF16)

    def proj(start, width):
        return _mm(u, win_ref[:, start:start + width])

    ext_ref[POOL_HALO:, :] = proj(EV_ZA, POOL_WIDTH)
    e = ext_ref[...]
    s2 = e + pltpu.roll(e, 1, 0)
    s4 = s2 + pltpu.roll(s2, 2, 0)
    s8 = s4 + pltpu.roll(s4, 4, 0)
    s16 = s8 + pltpu.roll(s8, 8, 0)
    lane = lax.broadcasted_iota(jnp.int32, e.shape, 1)
    row = lax.broadcasted_iota(jnp.int32, e.shape, 0)
    tpos = t0 + row - POOL_HALO
    g0, g1, g2 = lane < POOL_GROUP, lane < 2 * POOL_GROUP, lane < 3 * POOL_GROUP
    wsum = jnp.where(g0, s2, jnp.where(g1, s4, jnp.where(g2, s8, s16)))
    wlen = jnp.where(g0, POOL_WINDOWS[0], jnp.where(g1, POOL_WINDOWS[1],
                                                    jnp.where(g2, POOL_WINDOWS[2], POOL_WINDOWS[3])))
    cnt = jnp.maximum(jnp.minimum(tpos + 1, wlen), 1).astype(F32)
    pooled = (wsum / cnt - e)[POOL_HALO:]
    ext_ref[0:POOL_HALO, :] = e[ts:ts + POOL_HALO]
    a = _mm(pooled.astype(BF16), wbd_ref[...]) * pscale_ref[...]
    y_ref[tile, EV_YA:EV_YA + POOL_WIDTH] = (a * _silu(proj(EV_GA, POOL_WIDTH))).astype(BF16)

    cos = cos_ref[tile]
    sin = sin_ref[tile]
    zv = proj(EV_RV, RET_HEADS * RET_DV).astype(BF16)
    gate_r = _silu(proj(EV_GR, RET_HEADS * RET_DV))
    q_rot, k_rot = [], []
    for hd in range(RET_HEADS):
        qh = proj(EV_RQ + hd * RET_DK, RET_DK)
        kh = proj(EV_RK + hd * RET_DK, RET_DK)
        q_rot.append((qh * cos + pltpu.roll(qh, RET_DK // 2, 1) * sin) * (RET_DK ** -0.5))
        k_rot.append(kh * cos + pltpu.roll(kh, RET_DK // 2, 1) * sin)
    for c in range(ts // RET_CHUNK):
        rows = slice(c * RET_CHUNK, (c + 1) * RET_CHUNK)
        normed = []
        for hd in range(RET_HEADS):
            qc, kc = q_rot[hd][rows], k_rot[hd][rows]
            vc = zv[rows, hd * RET_DV:(hd + 1) * RET_DV]
            att = _mm_nt(qc.astype(BF16), kc.astype(BF16)) * decay_ref[hd]
            state = state_ref[hd]
            o = _mm(att.astype(BF16), vc) + _mm((qc * xi_ref[hd]).astype(BF16), state.astype(BF16))
            kv = _mm_tn((kc * zeta_ref[hd]).astype(BF16), vc)
            state_ref[hd] = state * gch_ref[hd] + kv
            dlt = o - jnp.mean(o, axis=-1, keepdims=True)
            normed.append(dlt * lax.rsqrt(jnp.mean(dlt * dlt, axis=-1, keepdims=True) + EPS))
        y_ref[r0 + c * RET_CHUNK:r0 + (c + 1) * RET_CHUNK, EV_YR:EV_YM] = (
            jnp.concatenate(normed, axis=-1) * gate_r[rows]).astype(BF16)

    xq = proj(EV_XQ, MEM_WIDTH)
    gm = _silu(proj(EV_GM, MEM_WIDTH))
    for hd, om in enumerate(_mem_attention(xq, mk_ref[0], mv_ref[0])):
        sl = slice(hd * MEM_DH, (hd + 1) * MEM_DH)
        y_ref[tile, EV_YM + hd * MEM_DH:EV_YM + (hd + 1) * MEM_DH] = (om * gm[:, sl]).astype(BF16)

    o_ref[0, tile] = h + _mm(y_ref[tile], wout_ref[...])


def _retention_tables(s):
    half = RET_DK // 2
    inv = ROPE_BASE ** (-jnp.arange(half, dtype=F32) / half)
    ang = jnp.arange(s, dtype=F32)[:, None] * inv[None, :]
    cos, sin = jnp.cos(ang), jnp.sin(ang)
    cos_t = jnp.concatenate([cos, cos], axis=-1)
    sin_t = jnp.concatenate([-sin, sin], axis=-1)
    c = RET_CHUNK
    log_g = jnp.log(1.0 - jnp.exp2(-5.0 - jnp.arange(RET_HEADS, dtype=F32)))
    n = jnp.arange(c, dtype=F32)
    diff = n[:, None] - n[None, :]
    decay = jnp.where(diff >= 0, jnp.exp(log_g[:, None, None] * jnp.maximum(diff, 0.0)), 0.0)
    xi = jnp.exp(log_g[:, None] * (n + 1.0))
    zeta = jnp.exp(log_g[:, None] * (c - 1.0 - n))
    g_chunk = jnp.exp(log_g * c)
    xi_t = jnp.broadcast_to(xi[:, :, None], (RET_HEADS, c, RET_DK))
    zeta_t = jnp.broadcast_to(zeta[:, :, None], (RET_HEADS, c, RET_DK))
    return cos_t, sin_t, decay, xi_t, zeta_t, g_chunk


def _even_layer(h, memkv, layer, g, w_in, pool_w, pool_scale, w_out, ts=2 * EV_SUB):
    b, s, d = h.shape
    assert w_in.shape == (d, EV_COLS) and w_out.shape == (EV_YCOLS, d)
    win = w_in.astype(BF16)
    wout = w_out.astype(BF16)
    wbd = jnp.zeros((POOL_WIDTH, POOL_WIDTH), F32)
    for gi in range(len(POOL_WINDOWS)):
        sl = slice(gi * POOL_GROUP, (gi + 1) * POOL_GROUP)
        wbd = wbd.at[sl, sl].set(pool_w[gi])
    wbd = wbd.astype(BF16)
    cos_t, sin_t, decay, xi_t, zeta_t, g_chunk = _retention_tables(s)
    kv_blk = 2 * layer
    return pl.pallas_call(
        _even_kernel,
        out_shape=jax.ShapeDtypeStruct((b, s, d), F32),
        grid=(b, s // ts),
        in_specs=[pl.BlockSpec(memory_space=pltpu.SMEM),
                  pl.BlockSpec((1, ts, d), lambda i, j: (i, j, 0)),
                  _const_spec((1, d)),
                  _const_spec((d, EV_COLS)),
                  _const_spec((POOL_WIDTH, POOL_WIDTH)),
                  _const_spec((1, POOL_WIDTH)),
                  pl.BlockSpec((ts, RET_DK), lambda i, j: (j, 0)),
                  pl.BlockSpec((ts, RET_DK), lambda i, j: (j, 0)),
                  _const_spec((RET_HEADS, RET_CHUNK, RET_CHUNK)),
                  _const_spec((RET_HEADS, RET_CHUNK, RET_DK)),
                  _const_spec((RET_HEADS, RET_CHUNK, RET_DK)),
                  pl.BlockSpec((1, N_MEM, MEM_WIDTH), lambda i, j: (i, 0, kv_blk)),
                  pl.BlockSpec((1, N_MEM, MEM_WIDTH), lambda i, j: (i, 0, kv_blk + 1)),
                  _const_spec((EV_YCOLS, d))],
        out_specs=pl.BlockSpec((1, ts, d), lambda i, j: (i, j, 0)),
        scratch_shapes=[pltpu.VMEM((POOL_HALO + EV_SUB, POOL_WIDTH), F32),
                        pltpu.VMEM((RET_HEADS, RET_DK, RET_DV), F32),
                        pltpu.VMEM((ts, EV_YCOLS), BF16)],
        compiler_params=_params(("arbitrary", "arbitrary")),
        name="even_layer",
    )(g_chunk, h, g.reshape(1, d), win, wbd, pool_scale.reshape(1, POOL_WIDTH), cos_t, sin_t, decay, xi_t, zeta_t,
      memkv, memkv, wout)


KV_W = NSA_KV * NSA_DH
OD_Q = 0
OD_KV = OD_Q + NSA_HEADS * NSA_DH
OD_A = OD_KV + 6 * KV_W
PIECE_KC, PIECE_VC, PIECE_KS, PIECE_VS, PIECE_KW, PIECE_VW = range(6)
OD_GL = OD_A
OD_XQ = OD_GL + NSA_KV * LANES
OD_GN = OD_XQ + MEM_WIDTH
OD_GM = OD_GN + NSA_HEADS * NSA_DH
OD_COLS = OD_GM + MEM_WIDTH


def _odd_in_kernel(h_ref, g_ref, wraw_ref, wb_ref, pe_ref, perm_ref, mk_ref, mv_ref,
                   q_ref, kaug_ref, vst_ref, kw_ref, vwt_ref, kca_ref, kcb_ref, vca_ref, vcb_ref, gatest_ref,
                   sg_ref, ym_ref, wa_ref):
    @pl.when((pl.program_id(0) == 0) & (pl.program_id(1) == 0))
    def _():
        for c0 in range(0, OD_A, 4 * LANES):
            wa_ref[:, c0:c0 + 4 * LANES] = wraw_ref[:, c0:c0 + 4 * LANES].astype(BF16)

    si = pl.program_id(1)
    n_sub = h_ref.shape[1] // NSA_TQ
    for sub in range(n_sub):
        _odd_in_subtile(si * n_sub + sub, sub, h_ref, g_ref, wa_ref, wb_ref, pe_ref, perm_ref, mk_ref, mv_ref,
                        q_ref, kaug_ref, vst_ref, kw_ref, vwt_ref, kca_ref, kcb_ref, vca_ref, vcb_ref, gatest_ref,
                        sg_ref, ym_ref)


def _odd_in_subtile(tile_idx, sub, h_ref, g_ref, wa_ref, wb_ref, pe_ref, perm_ref, mk_ref, mv_ref,
                    q_ref, kaug_ref, vst_ref, kw_ref, vwt_ref, kca_ref, kcb_ref, vca_ref, vcb_ref, gatest_ref,
                    sg_ref, ym_ref):
    ts = NSA_TQ
    tile = slice(sub * ts, (sub + 1) * ts)
    u = _rms(h_ref[0, tile], g_ref[...]).astype(BF16)

    def proj(start, width):
        if start < OD_A:
            return _mm(u, wa_ref[:, start:start + width])
        return _mm(u, wb_ref[:, start - OD_A:start - OD_A + width])

    zq = proj(OD_Q, NSA_HEADS * NSA_DH) * (NSA_DH ** -0.5 * LOG2E)
    zg = _silu(proj(OD_GN, NSA_HEADS * NSA_DH))
    for g in range(NSA_KV):
        for hh in range(NSA_HPG):
            sl = slice((g * NSA_HPG + hh) * NSA_DH, (g * NSA_HPG + hh + 1) * NSA_DH)
            q_ref[0, g, hh, tile] = zq[:, sl].astype(BF16)
            sg_ref[0, g, hh, tile] = zg[:, sl]

    zkv = proj(OD_KV, 6 * KV_W)
    lane = lax.broadcasted_iota(jnp.int32, (ts, LANES), 1)
    blk = (tile_idx * ts + lax.broadcasted_iota(jnp.int32, (ts, LANES), 0)) // SLC_BLOCK
    onehot = jnp.where(lane == blk, 1.0, 0.0).astype(BF16)
    ones_rows = jnp.where(lax.broadcasted_iota(jnp.int32, (V_ROWS - NSA_DH, ts), 0) == 0, 1.0, 0.0).astype(BF16)
    zgl = jax.nn.sigmoid(proj(OD_GL, NSA_KV * LANES))
    flat_in = []
    for g in range(NSA_KV):
        def piece(idx):
            off = idx * KV_W + g * NSA_DH
            return zkv[:, off:off + NSA_DH]
        kaug_ref[0, g, tile, 0:NSA_DH] = piece(PIECE_KS).astype(BF16)
        kaug_ref[0, g, tile, NSA_DH:2 * NSA_DH] = onehot
        vst_ref[0, g, sub, 0:NSA_DH] = piece(PIECE_VS).T.astype(BF16)
        vst_ref[0, g, sub, NSA_DH:V_ROWS] = ones_rows
        kw_ref[0, g, tile] = piece(PIECE_KW).astype(BF16)
        vwt_ref[0, g, sub, 0:NSA_DH] = piece(PIECE_VW).T.astype(BF16)
        vwt_ref[0, g, sub, NSA_DH:V_ROWS] = ones_rows
        kc, vc = piece(PIECE_KC), piece(PIECE_VC)
        flat_in += [(kc + pe_ref[0, 0]).astype(BF16), (kc + pe_ref[0, 1]).astype(BF16),
                    (vc + pe_ref[1, 0]).astype(BF16), (vc + pe_ref[1, 1]).astype(BF16)]
        gatest_ref[0, g, :, tile] = zgl[:, g * LANES:(g + 1) * LANES].T

    nj = ts // CMP_STRIDE
    perm = _mm(perm_ref[...], jnp.concatenate(flat_in, axis=1)).astype(BF16)
    for c, out_ref in enumerate((kca_ref, kcb_ref, vca_ref, vcb_ref) * NSA_KV):
        for l in range(CMP_STRIDE):
            out_ref[0, c // 4, sub * nj:(sub + 1) * nj, l * NSA_DH:(l + 1) * NSA_DH] = (
                perm[l * nj:(l + 1) * nj, c * NSA_DH:(c + 1) * NSA_DH])

    xq = proj(OD_XQ, MEM_WIDTH)
    gm = _silu(proj(OD_GM, MEM_WIDTH))
    for hd, om in enumerate(_mem_attention(xq, mk_ref[0], mv_ref[0])):
        sl = slice(hd * MEM_DH, (hd + 1) * MEM_DH)
        ym_ref[0, tile, sl] = (om * gm[:, sl]).astype(BF16)


def _odd_in(h, memkv, layer, g, w_in, cmp_pe, ts=2 * NSA_TQ):
    b, s, d = h.shape
    assert w_in.shape[1] == OD_A + 3 * NSA_HEADS + MEM_WIDTH + D_INNER
    gl, xq, gate = jnp.split(w_in[:, OD_A:], [3 * NSA_HEADS, 3 * NSA_HEADS + MEM_WIDTH], axis=1)
    gl = gl.reshape(d, 3, NSA_KV, NSA_HPG).transpose(0, 2, 1, 3).reshape(d, NSA_KV, 3 * NSA_HPG)
    gl = jnp.pad(gl, ((0, 0), (0, 0), (0, LANES - 3 * NSA_HPG))).reshape(d, NSA_KV * LANES)
    wb = jnp.concatenate([w.astype(BF16) for w in (gl, xq, gate)], axis=1)
    sub = NSA_TQ
    reps = sub // CMP_STRIDE
    pe = jnp.stack([jnp.stack([jnp.tile(cmp_pe[kv, :CMP_STRIDE], (reps, 1)),
                               jnp.tile(cmp_pe[kv, CMP_STRIDE:], (reps, 1))]) for kv in range(2)])
    kv_blk = 2 * layer
    assert ts % sub == 0
    nj = sub // CMP_STRIDE
    perm = np.zeros((sub, sub), np.float32)
    perm[np.arange(sub), (np.arange(sub) % nj) * CMP_STRIDE + np.arange(sub) // nj] = 1.0
    head_t = jax.ShapeDtypeStruct((b, NSA_KV, s, NSA_DH), BF16)
    head_spec = pl.BlockSpec((1, NSA_KV, ts, NSA_DH), lambda i, j: (i, 0, j, 0))
    headt_t = jax.ShapeDtypeStruct((b, NSA_KV, s // sub, V_ROWS, sub), BF16)
    headt_spec = pl.BlockSpec((1, NSA_KV, ts // sub, V_ROWS, sub), lambda i, j: (i, 0, j, 0, 0))
    flat_t = jax.ShapeDtypeStruct((b, NSA_KV, s // CMP_STRIDE, CMP_STRIDE * NSA_DH), BF16)
    flat_spec = pl.BlockSpec((1, NSA_KV, ts // CMP_STRIDE, CMP_STRIDE * NSA_DH), lambda i, j: (i, 0, j, 0))
    qlike_spec = pl.BlockSpec((1, NSA_KV, NSA_HPG, ts, NSA_DH), lambda i, j: (i, 0, 0, j, 0))
    return pl.pallas_call(
        _odd_in_kernel,
        out_shape=[jax.ShapeDtypeStruct((b, NSA_KV, NSA_HPG, s, NSA_DH), BF16),
                   jax.ShapeDtypeStruct((b, NSA_KV, s, 2 * NSA_DH), BF16),
                   headt_t, head_t, headt_t, flat_t, flat_t, flat_t, flat_t,
                   jax.ShapeDtypeStruct((b, NSA_KV, LANES, s), F32),
                   jax.ShapeDtypeStruct((b, NSA_KV, NSA_HPG, s, NSA_DH), F32),
                   jax.ShapeDtypeStruct((b, s, MEM_WIDTH), BF16)],
        grid=(b, s // ts),
        in_specs=[pl.BlockSpec((1, ts, d), lambda i, j: (i, j, 0)),
                  _const_spec((1, d)),
                  _const_spec((d, OD_A)),
                  _const_spec((d, OD_COLS - OD_A)),
                  _const_spec((2, 2, sub, NSA_DH)),
                  _const_spec((sub, sub)),
                  pl.BlockSpec((1, N_MEM, MEM_WIDTH), lambda i, j: (i, 0, kv_blk)),
                  pl.BlockSpec((1, N_MEM, MEM_WIDTH), lambda i, j: (i, 0, kv_blk + 1))],
        out_specs=[qlike_spec,
                   pl.BlockSpec((1, NSA_KV, ts, 2 * NSA_DH), lambda i, j: (i, 0, j, 0)),
                   headt_spec, head_spec, headt_spec, flat_spec, flat_spec, flat_spec, flat_spec,
                   pl.BlockSpec((1, NSA_KV, LANES, ts), lambda i, j: (i, 0, 0, j)),
                   qlike_spec,
                   pl.BlockSpec((1, ts, MEM_WIDTH), lambda i, j: (i, j, 0))],
        scratch_shapes=[pltpu.VMEM((d, OD_A), BF16)],
        compiler_params=_params(("arbitrary", "arbitrary")),
        name="odd_in_proj",
    )(h, g.reshape(1, d), w_in, wb, pe, jnp.asarray(perm, BF16), memkv, memkv)


def _compress_kernel(xka_ref, xkb_ref, xva_ref, xvb_ref, w1a_ref, w1b_ref, b1_ref, w2_ref, ovt_ref, kc_ref, vct_ref):
    n = xka_ref.shape[2]
    k_cols = vct_ref.shape[3]

    def block_mlp(kv, xa, xb):
        first = _mm(xa[0, 0], w1a_ref[kv])
        second = _mm(xb[0, 0], w1b_ref[kv])
        hid = first + pltpu.roll(second, n - 1, 0) + b1_ref[kv]
        return _mm(_silu(hid).astype(BF16), w2_ref[kv])

    kc_ref[0, 0, 0:CMP_PAD] = jnp.zeros((CMP_PAD, NSA_DH), BF16)
    kc_ref[0, 0, CMP_PAD:] = block_mlp(0, xka_ref, xkb_ref).astype(BF16)
    vc = jnp.concatenate([jnp.zeros((CMP_PAD, NSA_DH), F32), block_mlp(1, xva_ref, xvb_ref),
                          jnp.zeros((k_cols - CMP_PAD - n, NSA_DH), F32)], axis=0)
    vct_ref[0, 0, 0:NSA_DH] = vc.T.astype(BF16)
    vct_ref[0, 0, NSA_DH:] = ovt_ref[...]


def _compress(kca, kcb, vca, vcb, w1, b1, w2):
    b, g, n, half = kca.shape
    dh = half // CMP_STRIDE
    s = n * CMP_STRIDE
    xs = (kca, kcb, vca, vcb)
    w1 = w1.astype(BF16)
    ovt = _overlap_table(s)
    k_cols = ovt.shape[1]
    x_spec = pl.BlockSpec((1, 1, n, half), lambda i, j: (i, j, 0, 0))
    return pl.pallas_call(
        _compress_kernel,
        out_shape=[jax.ShapeDtypeStruct((b, g, n + CMP_PAD, dh), BF16),
                   jax.ShapeDtypeStruct((b, g, dh + LANES, k_cols), BF16)],
        grid=(b, g),
        in_specs=[x_spec, x_spec, x_spec, x_spec,
                  _const_spec((2, half, CMP_HIDDEN)), _const_spec((2, half, CMP_HIDDEN)),
                  _const_spec((2, 1, CMP_HIDDEN)), _const_spec((2, CMP_HIDDEN, dh)),
                  _const_spec((LANES, k_cols))],
        out_specs=[pl.BlockSpec((1, 1, n + CMP_PAD, dh), lambda i, j: (i, j, 0, 0)),
                   pl.BlockSpec((1, 1, dh + LANES, k_cols), lambda i, j: (i, j, 0, 0))],
        compiler_params=_params(("arbitrary", "arbitrary")),
        name="compress",
    )(*xs, w1[:, :half], w1[:, half:], b1.reshape(2, 1, CMP_HIDDEN), w2.astype(BF16), ovt)


def _nsa_kernel(q_ref, kaug_ref, vst_ref, kw_ref, vwt_ref, kcmp_ref, vcmpt_ref, gatest_ref, sg_ref,
                d0_ref, d1_ref, dw_ref, cmpq_ref, y_ref, qaug_ref, m_ref, acc_ref, s_ref, mw_ref, accw_ref, sw_ref,
                mix_ref, tmax_ref, tmaxw_ref):
    i = pl.program_id(2)
    tq = NSA_TQ
    rows = NSA_HPG * tq
    s_len = kw_ref.shape[2]
    n_cmp = s_len // CMP_STRIDE
    n_slc = s_len // SLC_BLOCK
    t0 = i * tq
    q = q_ref[0, 0].reshape(rows, NSA_DH)


    def stream(mx_ref, ac_ref, sc2_ref, tmax_ref):
        def reset():
            mx_ref[...] = jnp.full(mx_ref.shape, LOWEST, F32)
            ac_ref[...] = jnp.zeros(ac_ref.shape, F32)

        def s_to(slot, k_ref, query, tile, bias=None):
            k0 = pl.multiple_of(tile * tq, tq)
            st = _mm_nt(k_ref[0, 0, pl.ds(k0, tq)], query)
            if bias is not None:
                st = st + bias
            sc2_ref[slot] = st
            tmax_ref[slot] = jnp.max(st, axis=0, keepdims=True)

        def pv_from(slot, vt):
            st = sc2_ref[slot]
            m_prev = mx_ref[...]
            m_new = jnp.maximum(m_prev, tmax_ref[slot])
            pt = jnp.exp2(st - m_new).astype(BF16)
            ac_ref[...] = jnp.exp2(m_prev - m_new) * ac_ref[...] + _mm(vt, pt)
            mx_ref[...] = m_new

        def finish(cols=slice(None)):
            return ac_ref[0:NSA_DH, cols] / ac_ref[NSA_DH:NSA_DH + 1, cols]

        return reset, s_to, pv_from, finish

    def off_unless(cond):
        return jnp.where(cond, 0.0, NEG)

    prev1 = jnp.maximum(i - 1, 0)
    prev2 = jnp.maximum(i - 2, 0)
    w_reset, w_s_to, w_pv_from, w_finish = stream(mw_ref, accw_ref, sw_ref, tmaxw_ref)


    n_pad = kcmp_ref.shape[2]
    j_near = pl.multiple_of(i * (tq // CMP_STRIDE), tq // CMP_STRIDE)
    qaug_ref[:, 0:NSA_DH] = q
    qaug_ref[:, NSA_DH:] = cmpq_ref[0]
    krow = lax.broadcasted_iota(jnp.int32, (n_pad, NSA_DH), 0)
    klane = lax.broadcasted_iota(jnp.int32, (n_pad, NSA_DH), 1)
    slot = krow - j_near
    in_window = (klane < 2 * CMP_NEAR) & (jnp.bitwise_and(klane, CMP_NEAR - 1) == slot)
    visible = (krow >= CMP_PAD) & (slot < CMP_NEAR)
    key_cols = jnp.where(klane == 2 * CMP_NEAR, jnp.where(visible, 0.0, NEG), jnp.where(in_window, 1.0, 0.0))
    s_cmp = _mm_nt(jnp.concatenate([kcmp_ref[0, 0], key_cols.astype(BF16)], axis=1), qaug_ref[...])
    w_reset()
    w_s_to(0, kw_ref, q, prev2, dw_ref[0, 0])
    w_s_to(1, kw_ref, q, prev1, d1_ref[0, 0])
    p_cmp = jnp.exp2(s_cmp - jnp.max(s_cmp, axis=0, keepdims=True))
    tcol = t0 + jnp.bitwise_and(lax.broadcasted_iota(jnp.int32, (1, rows), 1), tq - 1)
    scale = jnp.where(tcol >= CMP_BLOCK - 1, 1.0, 0.0) / jnp.sum(p_cmp, axis=0, keepdims=True)
    p_cmp = p_cmp.astype(BF16)
    k_cols = vcmpt_ref.shape[3]
    if k_cols > n_pad:
        p_cmp = jnp.concatenate([p_cmp, jnp.zeros((k_cols - n_pad, rows), BF16)], axis=0)
    w_pv_from(0, vwt_ref[0, 0, prev2])
    both = _mm(vcmpt_ref[0, 0], p_cmp) * scale
    w_s_to(0, kw_ref, q, i, d0_ref[0, 0])
    w_pv_from(1, vwt_ref[0, 0, prev1])
    gtst = gatest_ref[0, 0]
    n_blk = -(-n_slc // 8) * 8
    imp = both[NSA_DH:NSA_DH + n_blk, 0:tq]
    for hh in range(NSA_HPG):
        hs = slice(hh * tq, (hh + 1) * tq)
        mix_ref[:, hs] = gtst[hh:hh + 1] * both[0:NSA_DH, hs]
        if hh:
            imp = imp + both[NSA_DH:NSA_DH + n_blk, hs]

    blk = lax.broadcasted_iota(jnp.int32, (n_blk, tq), 0)
    tq_pos = t0 + lax.broadcasted_iota(jnp.int32, (n_blk, tq), 1)
    cur = tq_pos // SLC_BLOCK
    forced = (blk == 0) | (blk == cur) | (blk == cur - 1)
    future = blk * SLC_BLOCK > tq_pos
    imp = jnp.where(forced, BIG, jnp.where(future, -BIG, imp))
    imp = jnp.where(blk < n_slc, imp, LOWEST)
    blk_f = blk.astype(F32)
    sel = jnp.zeros((n_blk, tq), jnp.bool_)
    for _ in range(SLC_TOPK):
        top = jnp.max(imp, axis=0, keepdims=True)
        first = jnp.min(jnp.where(imp == top, blk_f, float(LANES)), axis=0, keepdims=True)
        hit = blk_f == first
        sel = sel | hit
        imp = jnp.where(hit, KNOCKED, imp)
    sneg_t = jnp.where(sel, 0.0, NEG)
    if n_blk < LANES:
        sneg_t = jnp.concatenate([sneg_t, jnp.zeros((LANES - n_blk, tq), F32)], axis=0)
    sneg = sneg_t.T.astype(BF16)
    for hh in range(NSA_HPG):
        qaug_ref[hh * tq:(hh + 1) * tq, NSA_DH:2 * NSA_DH] = sneg
    w_pv_from(0, vwt_ref[0, 0, i])
    for hh in range(NSA_HPG):
        hs = slice(hh * tq, (hh + 1) * tq)
        mix_ref[:, hs] = mix_ref[:, hs] + gtst[2 * NSA_HPG + hh:2 * NSA_HPG + hh + 1] * w_finish(hs)

    reset, s_to, pv_from, finish = stream(m_ref, acc_ref, s_ref, tmax_ref)
    reset()
    n_far = prev1
    n_pairs = jnp.maximum(n_far - 1, 0) // 2
    s_to(0, kaug_ref, qaug_ref[...], 0, off_unless(n_far >= 1))

    def far_pair(t2):
        s_to(1, kaug_ref, qaug_ref[...], 2 * t2 + 1)
        pv_from(0, vst_ref[0, 0, 2 * t2])
        s_to(0, kaug_ref, qaug_ref[...], 2 * t2 + 2)
        pv_from(1, vst_ref[0, 0, 2 * t2 + 1])

    def far_quad(t4, carry):
        far_pair(2 * t4)
        far_pair(2 * t4 + 1)
        return carry

    lax.fori_loop(0, n_pairs // 2, far_quad, 0)

    @pl.when(n_pairs % 2 == 1)
    def _():
        far_pair(n_pairs - 1)
    c0 = 2 * n_pairs
    two_left = n_far - c0 == 2

    @pl.when(two_left)
    def _():
        s_to(1, kaug_ref, qaug_ref[...], c0 + 1)
        pv_from(0, vst_ref[0, 0, c0])
        s_to(0, kaug_ref, qaug_ref[...], prev1, d1_ref[0, 0])
        pv_from(1, vst_ref[0, 0, c0 + 1])
        s_to(1, kaug_ref, qaug_ref[...], i, d0_ref[0, 0])
        pv_from(0, vst_ref[0, 0, prev1])
        pv_from(1, vst_ref[0, 0, i])

    @pl.when(jnp.logical_not(two_left))
    def _():
        s_to(1, kaug_ref, qaug_ref[...], prev1, d1_ref[0, 0])
        pv_from(0, vst_ref[0, 0, c0])
        s_to(0, kaug_ref, qaug_ref[...], i, d0_ref[0, 0])
        pv_from(1, vst_ref[0, 0, prev1])
        pv_from(0, vst_ref[0, 0, i])

    for hh in range(NSA_HPG):
        hs = slice(hh * tq, (hh + 1) * tq)
        ot = mix_ref[:, hs] + gatest_ref[0, 0, NSA_HPG + hh:NSA_HPG + hh + 1] * finish(hs)
        y_ref[0, :, hh * NSA_DH:(hh + 1) * NSA_DH] = (ot.T * sg_ref[0, 0, hh]).astype(BF16)


def _overlap_table(s):
    n_cmp = (s - CMP_BLOCK) // CMP_STRIDE + 1
    n_slc = s // SLC_BLOCK
    cst = np.arange(n_cmp)[:, None] * CMP_STRIDE
    sst = np.arange(n_slc)[None, :] * SLC_BLOCK
    ov = np.clip(np.minimum(cst + CMP_BLOCK, sst + SLC_BLOCK) - np.maximum(cst, sst), 0, None) / CMP_STRIDE
    k_cols = -(-(CMP_PAD + s // CMP_STRIDE) // LANES) * LANES
    full = np.zeros((LANES, k_cols), np.float32)
    full[:n_slc, CMP_PAD:CMP_PAD + n_cmp] = ov.T
    return jnp.asarray(full, BF16)


def _nsa(q, kaug, vst, kw, vwt, kcmp, vcmpt, gatest, sg, bias, cmpq):
    b, g, hpg, s, dh = q.shape
    tq = NSA_TQ
    assert WINDOW == 2 * tq and s % tq == 0 and s // SLC_BLOCK <= LANES and s // SLC_BLOCK >= SLC_TOPK
    n_pad = kcmp.shape[2]
    rows = hpg * tq
    seq_spec = pl.BlockSpec((1, 1, s, dh), lambda i, j, k: (i, j, 0, 0))
    seqt_spec = pl.BlockSpec((1, 1, s // tq, V_ROWS, tq), lambda i, j, k: (i, j, 0, 0, 0))
    qlike_spec = pl.BlockSpec((1, 1, hpg, tq, dh), lambda i, j, k: (i, j, 0, k, 0))

    def bias_spec(idx, needs_tiles_before=0):
        return pl.BlockSpec((1, 1, tq, rows),
                            lambda i, j, k: (j, jnp.where(k >= needs_tiles_before, idx, BIAS_MASKED), 0, 0))

    return pl.pallas_call(
        _nsa_kernel,
        out_shape=jax.ShapeDtypeStruct((b, s, g * hpg * dh), BF16),
        grid=(b, g, s // tq),
        in_specs=[qlike_spec,
                  pl.BlockSpec((1, 1, s, 2 * dh), lambda i, j, k: (i, j, 0, 0)),
                  seqt_spec, seq_spec, seqt_spec,
                  pl.BlockSpec((1, 1, n_pad, dh), lambda i, j, k: (i, j, 0, 0)),
                  pl.BlockSpec((1, 1, dh + LANES, vcmpt.shape[3]), lambda i, j, k: (i, j, 0, 0)),
                  pl.BlockSpec((1, 1, LANES, tq), lambda i, j, k: (i, j, 0, k)),
                  qlike_spec,
                  bias_spec(0), bias_spec(1, 1), bias_spec(2, 2),
                  pl.BlockSpec((1, rows, LANES), lambda i, j, k: (j, 0, 0))],
        out_specs=pl.BlockSpec((1, tq, hpg * dh), lambda i, j, k: (i, k, j)),
        scratch_shapes=[pltpu.VMEM((rows, 2 * dh), BF16),
                        pltpu.VMEM((1, rows), F32),
                        pltpu.VMEM((V_ROWS, rows), F32),
                        pltpu.VMEM((2, tq, rows), F32),
                        pltpu.VMEM((1, rows), F32),
                        pltpu.VMEM((V_ROWS, rows), F32),
                        pltpu.VMEM((2, tq, rows), F32),
                        pltpu.VMEM((dh, rows), F32),
                        pltpu.VMEM((2, 1, rows), F32),
                        pltpu.VMEM((2, 1, rows), F32)],
        compiler_params=_params(("arbitrary", "arbitrary", "arbitrary")),
        name="nsa_attention",
    )(q, kaug, vst, kw, vwt, kcmp, vcmpt, gatest, sg, bias, bias, bias, cmpq)


def _odd_out_kernel(h_ref, yn_ref, ym_ref, wn_ref, wm_ref, fg_ref, o_ref, *, final_norm, sub):
    for r0 in range(0, h_ref.shape[0], sub):
        rows = slice(r0, r0 + sub)
        out = h_ref[rows] + _mm(yn_ref[rows], wn_ref[...]) + _mm(ym_ref[rows], wm_ref[...])
        o_ref[rows] = _rms(out, fg_ref[...]) if final_norm else out


def _odd_out(h, yn, ym, w_out, final_g, final_norm, tm=1024, sub=256):
    b, s, d = h.shape
    t = b * s
    nw = NSA_HEADS * NSA_DH
    wout = w_out.astype(BF16)
    out = pl.pallas_call(
        functools.partial(_odd_out_kernel, final_norm=final_norm, sub=sub),
        out_shape=jax.ShapeDtypeStruct((t, d), F32),
        grid=(t // tm,),
        in_specs=[pl.BlockSpec((tm, d), lambda i: (i, 0)),
                  pl.BlockSpec((tm, nw), lambda i: (i, 0)),
                  pl.BlockSpec((tm, MEM_WIDTH), lambda i: (i, 0)),
                  _const_spec((nw, d)), _const_spec((MEM_WIDTH, d)), _const_spec((1, d))],
        out_specs=pl.BlockSpec((tm, d), lambda i: (i, 0)),
        compiler_params=_params(("arbitrary",)),
        name="odd_out_proj",
    )(h.reshape(t, d), yn.reshape(t, nw), ym.reshape(t, MEM_WIDTH), wout[:nw], wout[nw:], final_g.reshape(1, d))
    return out.reshape(b, s, d)


def _final_norm_kernel(h_ref, g_ref, o_ref):
    o_ref[...] = _rms(h_ref[...], g_ref[...])


def _final_norm(h, final_g, tm=512):
    b, s, d = h.shape
    t = b * s
    out = pl.pallas_call(
        _final_norm_kernel,
        out_shape=jax.ShapeDtypeStruct((t, d), F32),
        grid=(t // tm,),
        in_specs=[pl.BlockSpec((tm, d), lambda i: (i, 0)), _const_spec((1, d))],
        out_specs=pl.BlockSpec((tm, d), lambda i: (i, 0)),
        compiler_params=_params(("arbitrary",)),
        name="final_norm",
    )(h.reshape(t, d), final_g.reshape(1, d))
    return out.reshape(b, s, d)


def kernel(x, mem, norm_g, final_g, mem_norm_g, rel_bias, ev_w_in, ev_pool_w, ev_pool_scale, ev_w_mem_kv, ev_w_out,
           od_w_in, od_cmp_pe, od_cmp_w1, od_cmp_b1, od_cmp_w2, od_w_mem_kv, od_w_out):
    depth = norm_g.shape[0]
    w_mem = [(ev_w_mem_kv if i % 2 == 0 else od_w_mem_kv)[i // 2] for i in range(depth)]
    memkv = _memkv(mem, mem_norm_g, jnp.concatenate(w_mem, axis=1).astype(BF16))
    bias, cmpq = _bias_tiles(rel_bias) if depth > 1 else (None, None)
    h = x
    for i in range(depth):
        j = i // 2
        last = i == depth - 1
        if i % 2 == 0:
            h = _even_layer(h, memkv, i, norm_g[i], ev_w_in[j], ev_pool_w[j], ev_pool_scale[j], ev_w_out[j])
            if last:
                h = _final_norm(h, final_g)
        else:
            (q, kaug, vst, kw, vwt, kca, kcb, vca, vcb, gatest, sg, ym) = _odd_in(
                h, memkv, i, norm_g[i], od_w_in[j], od_cmp_pe[j])
            kcmp, vcmp = _compress(kca, kcb, vca, vcb, od_cmp_w1[j], od_cmp_b1[j], od_cmp_w2[j])
            yn = _nsa(q, kaug, vst, kw, vwt, kcmp, vcmp, gatest, sg, bias, cmpq)
            h = _odd_out(h, yn, ym, od_w_out[j], final_g, last)
    return h
```

```python
import functools
import math

import numpy as np
import jax
import jax.numpy as jnp
from jax import lax
from jax.experimental import pallas as pl
from jax.experimental.pallas import tpu as pltpu

F32 = jnp.float32
BF16 = jnp.bfloat16

D_MODEL = 1024
D_INNER = 2048
N_MEM = 256
EPS = 1e-6
NEG = -1e30
BIG = 1e30

POOL_WINDOWS = (2, 4, 8, 16)
POOL_WIDTH = 768
POOL_GROUP = 192
POOL_HALO = 16

RET_HEADS = 4
RET_DK = 128
RET_DV = 192
RET_CHUNK = 128
ROPE_BASE = 10000.0

MEM_HEADS = 4
MEM_DH = 128
MEM_WIDTH = 512

NSA_HEADS = 12
NSA_KV = 2
NSA_HPG = 6
NSA_DH = 128
CMP_BLOCK = 32
CMP_STRIDE = 16
CMP_HIDDEN = 256
SLC_BLOCK = 64
SLC_TOPK = 8
WINDOW = 512
REL_BUCKETS = 32
REL_MAX_DIST = 128

LANES = 128
NSA_TQ = 256
CMP_NEAR = 32
CMP_PAD = 16
LOG2E = math.log2(math.e)
V_ROWS = NSA_DH + 16
LOWEST = -3.0e38
KNOCKED = -3.3e38
BIAS_MASKED = 3

VMEM_LIMIT = 56 * 1024 * 1024


def _mm(a, b):
    return jnp.dot(a, b, preferred_element_type=F32)


def _mm_nt(a, b):
    return lax.dot_general(a, b, (((1,), (1,)), ((), ())), preferred_element_type=F32)


def _mm_tn(a, b):
    return lax.dot_general(a, b, (((0,), (0,)), ((), ())), preferred_element_type=F32)


def _rms(x, g):
    return x * lax.rsqrt(jnp.mean(x * x, axis=-1, keepdims=True) + EPS) * g


def _silu(x):
    return x * jax.nn.sigmoid(x)


def _const_spec(shape):
    nd = len(shape)
    return pl.BlockSpec(shape, lambda *_: (0,) * nd, pipeline_mode=pl.Buffered(1))


def _params(sem):
    return pltpu.CompilerParams(dimension_semantics=sem, vmem_limit_bytes=VMEM_LIMIT)


def _bucket_starts():
    n = np.arange(REL_MAX_DIST + 1)
    max_exact = REL_BUCKETS // 2
    nf = np.maximum(n, 1).astype(np.float32)
    large = max_exact + (np.log(nf / np.float32(max_exact)) / np.float32(math.log(REL_MAX_DIST / max_exact))
                         * np.float32(REL_BUCKETS - max_exact)).astype(np.int32)
    bucket = np.where(n < max_exact, n, np.minimum(large, REL_BUCKETS - 1))
    assert np.all(np.diff(bucket) >= 0) and bucket[-1] == REL_BUCKETS - 1
    return tuple(int(np.argmax(bucket >= b)) for b in range(REL_BUCKETS))


def _bias_kernel(tab_ref, rel_ref, relq_ref, out_ref, cmpq_ref, *, boxes, starts):
    h = pl.program_id(0)

    def lookup(rel):
        far = tab_ref[REL_BUCKETS - 1, h]
        val = jnp.full(rel.shape, tab_ref[0, h] - far, F32)
        for b in range(1, REL_BUCKETS - 1):
            val = jnp.where(rel >= starts[b], tab_ref[b, h] - far, val)
        val = jnp.where(rel >= starts[REL_BUCKETS - 1], 0.0, val)
        return jnp.where(rel < 0, NEG, val * LOG2E)

    for kind, box in enumerate(boxes):
        out_ref[0, kind] = jnp.where(rel_ref[kind] < 0, NEG, 0.0)
        if box is not None:
            r0, r1, c0, c1 = box
            out_ref[0, kind, r0:r1, c0:c1] = lookup(rel_ref[kind, r0:r1, c0:c1])

    relq = relq_ref[...]
    lane = lax.broadcasted_iota(jnp.int32, relq.shape, 1)
    bias = jnp.where(lane < CMP_NEAR, lookup(relq), 0.0)
    hi = bias.astype(BF16)
    lo = (bias - hi.astype(F32)).astype(BF16).astype(F32)
    one = jnp.where(lane == 2 * CMP_NEAR, 1.0, 0.0)
    cmpq_ref[0] = (hi.astype(F32) + pltpu.roll(lo, CMP_NEAR, 1) + one).astype(BF16)


def _bias_tiles(rel_bias):
    tq = NSA_TQ
    r = np.arange(tq)[:, None]
    c = np.arange(tq)[None, :]
    d0 = r - c
    d1 = tq + r - c
    dw = np.where(c > r, WINDOW + r - c, -1)
    gc = np.where(c < CMP_NEAR, r - CMP_STRIDE * c + (CMP_STRIDE * CMP_PAD - (CMP_BLOCK - 1)), -1)
    rel_np = np.stack([d0.T, d1.T, dw.T, np.full((tq, tq), -1)]).astype(np.int32)
    boxes = []
    for tile in rel_np:
        rr, cc = np.nonzero((tile >= 0) & (tile < REL_MAX_DIST))
        boxes.append(None if rr.size == 0 else tuple(int(v) for v in (
            rr.min() // 8 * 8, -(-(rr.max() + 1) // 8) * 8, cc.min() // LANES * LANES, -(-(cc.max() + 1) // LANES) * LANES)))
    rel = jnp.asarray(rel_np)
    nt = rel.shape[0]
    return pl.pallas_call(
        functools.partial(_bias_kernel, boxes=tuple(boxes), starts=_bucket_starts()),
        out_shape=[jax.ShapeDtypeStruct((NSA_KV, nt, tq, NSA_HPG * tq), F32),
                   jax.ShapeDtypeStruct((NSA_KV, NSA_HPG * tq, LANES), BF16)],
        grid=(NSA_HEADS,),
        in_specs=[pl.BlockSpec(memory_space=pltpu.SMEM),
                  _const_spec((nt, tq, tq)),
                  _const_spec((tq, LANES))],
        out_specs=[pl.BlockSpec((1, nt, tq, tq), lambda h: (h // NSA_HPG, 0, 0, h % NSA_HPG)),
                   pl.BlockSpec((1, tq, LANES), lambda h: (h // NSA_HPG, h % NSA_HPG, 0))],
        compiler_params=_params(("arbitrary",)),
        name="bias_tiles",
    )(rel_bias.astype(F32), rel, jnp.asarray(gc[:, :LANES].astype(np.int32)))


def _memkv_kernel(mem_ref, g_ref, w_ref, out_ref):
    y = _rms(mem_ref[0], g_ref[...]).astype(BF16)
    out_ref[0] = _mm(y, w_ref[...]).astype(BF16)


def _memkv(mem, mem_norm_g, w_all):
    b, m, d = mem.shape
    n = w_all.shape[1]
    return pl.pallas_call(
        _memkv_kernel,
        out_shape=jax.ShapeDtypeStruct((b, m, n), BF16),
        grid=(b,),
        in_specs=[pl.BlockSpec((1, m, d), lambda i: (i, 0, 0)),
                  _const_spec((1, d)),
                  _const_spec((d, n))],
        out_specs=pl.BlockSpec((1, m, n), lambda i: (i, 0, 0)),
        compiler_params=_params(("arbitrary",)),
        name="mem_kv",
    )(mem, mem_norm_g.reshape(1, d), w_all)


def _mem_attention(xq, mk, mv):
    outs = []
    for hd in range(MEM_HEADS):
        sl = slice(hd * MEM_DH, (hd + 1) * MEM_DH)
        qm = (xq[:, sl] * (MEM_DH ** -0.5)).astype(BF16)
        s = _mm_nt(qm, mk[:, sl])
        p = jnp.exp(s - jnp.max(s, axis=-1, keepdims=True))
        l = jnp.sum(p, axis=-1, keepdims=True)
        outs.append(_mm(p.astype(BF16), mv[:, sl]) / l)
    return outs


EV_ZA = 0
EV_RQ = EV_ZA + POOL_WIDTH
EV_RK = EV_RQ + RET_HEADS * RET_DK
EV_RV = EV_RK + RET_HEADS * RET_DK
EV_XQ = EV_RV + RET_HEADS * RET_DV
EV_GA = EV_XQ + MEM_WIDTH
EV_GR = EV_GA + POOL_WIDTH
EV_GM = EV_GR + RET_HEADS * RET_DV
EV_COLS = EV_GM + MEM_WIDTH
EV_YA = 0
EV_YR = POOL_WIDTH
EV_YM = EV_YR + RET_HEADS * RET_DV
EV_YCOLS = EV_YM + MEM_WIDTH
EV_SUB = 256


def _even_kernel(gch_ref, h_ref, g_ref, win_ref, wbd_ref, pscale_ref, cos_ref, sin_ref, decay_ref, xi_ref,
                 zeta_ref, mk_ref, mv_ref, wout_ref, o_ref, ext_ref, state_ref, y_ref):
    si = pl.program_id(1)
    ts = h_ref.shape[1]

    @pl.when(si == 0)
    def _():
        ext_ref[0:POOL_HALO, :] = jnp.zeros((POOL_HALO, POOL_WIDTH), F32)
        state_ref[...] = jnp.zeros(state_ref.shape, F32)

    for r0 in range(0, ts, EV_SUB):
        _even_subtile(si * ts + r0, slice(r0, r0 + EV_SUB), gch_ref, h_ref, g_ref, win_ref, wbd_ref, pscale_ref,
                      cos_ref, sin_ref, decay_ref, xi_ref, zeta_ref, mk_ref, mv_ref, wout_ref, o_ref, ext_ref,
                      state_ref, y_ref)


def _even_subtile(t0, tile, gch_ref, h_ref, g_ref, win_ref, wbd_ref, pscale_ref, cos_ref, sin_ref, decay_ref, xi_ref,
                  zeta_ref, mk_ref, mv_ref, wout_ref, o_ref, ext_ref, state_ref, y_ref):
    ts = EV_SUB
    r0 = tile.start
    h = h_ref[0, tile]
    u = _rms(h, g_ref[...]).astype(BF16)

    def proj(start, width):
        return _mm(u, win_ref[:, start:start + width])

    ext_ref[POOL_HALO:, :] = proj(EV_ZA, POOL_WIDTH)
    e = ext_ref[...]
    s2 = e + pltpu.roll(e, 1, 0)
    s4 = s2 + pltpu.roll(s2, 2, 0)
    s8 = s4 + pltpu.roll(s4, 4, 0)
    s16 = s8 + pltpu.roll(s8, 8, 0)
    lane = lax.broadcasted_iota(jnp.int32, e.shape, 1)
    row = lax.broadcasted_iota(jnp.int32, e.shape, 0)
    tpos = t0 + row - POOL_HALO
    g0, g1, g2 = lane < POOL_GROUP, lane < 2 * POOL_GROUP, lane < 3 * POOL_GROUP
    wsum = jnp.where(g0, s2, jnp.where(g1, s4, jnp.where(g2, s8, s16)))
    wlen = jnp.where(g0, POOL_WINDOWS[0], jnp.where(g1, POOL_WINDOWS[1],
                                                    jnp.where(g2, POOL_WINDOWS[2], POOL_WINDOWS[3])))
    cnt = jnp.maximum(jnp.minimum(tpos + 1, wlen), 1).astype(F32)
    pooled = (wsum / cnt - e)[POOL_HALO:]
    ext_ref[0:POOL_HALO, :] = e[ts:ts + POOL_HALO]
    a = _mm(pooled.astype(BF16), wbd_ref[...]) * pscale_ref[...]
    y_ref[tile, EV_YA:EV_YA + POOL_WIDTH] = (a * _silu(proj(EV_GA, POOL_WIDTH))).astype(BF16)

    cos = cos_ref[tile]
    sin = sin_ref[tile]
    zv = proj(EV_RV, RET_HEADS * RET_DV).astype(BF16)
    gate_r = _silu(proj(EV_GR, RET_HEADS * RET_DV))
    q_rot, k_rot = [], []
    for hd in range(RET_HEADS):
        qh = proj(EV_RQ + hd * RET_DK, RET_DK)
        kh = proj(EV_RK + hd * RET_DK, RET_DK)
        q_rot.append((qh * cos + pltpu.roll(qh, RET_DK // 2, 1) * sin) * (RET_DK ** -0.5))
        k_rot.append(kh * cos + pltpu.roll(kh, RET_DK // 2, 1) * sin)
    for c in range(ts // RET_CHUNK):
        rows = slice(c * RET_CHUNK, (c + 1) * RET_CHUNK)
        normed = []
        for hd in range(RET_HEADS):
            qc, kc = q_rot[hd][rows], k_rot[hd][rows]
            vc = zv[rows, hd * RET_DV:(hd + 1) * RET_DV]
            att = _mm_nt(qc.astype(BF16), kc.astype(BF16)) * decay_ref[hd]
            state = state_ref[hd]
            o = _mm(att.astype(BF16), vc) + _mm((qc * xi_ref[hd]).astype(BF16), state.astype(BF16))
            kv = _mm_tn((kc * zeta_ref[hd]).astype(BF16), vc)
            state_ref[hd] = state * gch_ref[hd] + kv
            dlt = o - jnp.mean(o, axis=-1, keepdims=True)
            normed.append(dlt * lax.rsqrt(jnp.mean(dlt * dlt, axis=-1, keepdims=True) + EPS))
        y_ref[r0 + c * RET_CHUNK:r0 + (c + 1) * RET_CHUNK, EV_YR:EV_YM] = (
            jnp.concatenate(normed, axis=-1) * gate_r[rows]).astype(BF16)

    xq = proj(EV_XQ, MEM_WIDTH)
    gm = _silu(proj(EV_GM, MEM_WIDTH))
    for hd, om in enumerate(_mem_attention(xq, mk_ref[0], mv_ref[0])):
        sl = slice(hd * MEM_DH, (hd + 1) * MEM_DH)
        y_ref[tile, EV_YM + hd * MEM_DH:EV_YM + (hd + 1) * MEM_DH] = (om * gm[:, sl]).astype(BF16)

    o_ref[0, tile] = h + _mm(y_ref[tile], wout_ref[...])


def _retention_tables(s):
    half = RET_DK // 2
    inv = ROPE_BASE ** (-jnp.arange(half, dtype=F32) / half)
    ang = jnp.arange(s, dtype=F32)[:, None] * inv[None, :]
    cos, sin = jnp.cos(ang), jnp.sin(ang)
    cos_t = jnp.concatenate([cos, cos], axis=-1)
    sin_t = jnp.concatenate([-sin, sin], axis=-1)
    c = RET_CHUNK
    log_g = jnp.log(1.0 - jnp.exp2(-5.0 - jnp.arange(RET_HEADS, dtype=F32)))
    n = jnp.arange(c, dtype=F32)
    diff = n[:, None] - n[None, :]
    decay = jnp.where(diff >= 0, jnp.exp(log_g[:, None, None] * jnp.maximum(diff, 0.0)), 0.0)
    xi = jnp.exp(log_g[:, None] * (n + 1.0))
    zeta = jnp.exp(log_g[:, None] * (c - 1.0 - n))
    g_chunk = jnp.exp(log_g * c)
    xi_t = jnp.broadcast_to(xi[:, :, None], (RET_HEADS, c, RET_DK))
    zeta_t = jnp.broadcast_to(zeta[:, :, None], (RET_HEADS, c, RET_DK))
    return cos_t, sin_t, decay, xi_t, zeta_t, g_chunk


def _even_layer(h, memkv, layer, g, w_in, pool_w, pool_scale, w_out, ts=4 * EV_SUB):
    b, s, d = h.shape
    assert w_in.shape == (d, EV_COLS) and w_out.shape == (EV_YCOLS, d)
    win = w_in.astype(BF16)
    wout = w_out.astype(BF16)
    wbd = jnp.zeros((POOL_WIDTH, POOL_WIDTH), F32)
    for gi in range(len(POOL_WINDOWS)):
        sl = slice(gi * POOL_GROUP, (gi + 1) * POOL_GROUP)
        wbd = wbd.at[sl, sl].set(pool_w[gi])
    wbd = wbd.astype(BF16)
    cos_t, sin_t, decay, xi_t, zeta_t, g_chunk = _retention_tables(s)
    kv_blk = 2 * layer
    return pl.pallas_call(
        _even_kernel,
        out_shape=jax.ShapeDtypeStruct((b, s, d), F32),
        grid=(b, s // ts),
        in_specs=[pl.BlockSpec(memory_space=pltpu.SMEM),
                  pl.BlockSpec((1, ts, d), lambda i, j: (i, j, 0)),
                  _const_spec((1, d)),
                  _const_spec((d, EV_COLS)),
                  _const_spec((POOL_WIDTH, POOL_WIDTH)),
                  _const_spec((1, POOL_WIDTH)),
                  pl.BlockSpec((ts, RET_DK), lambda i, j: (j, 0)),
                  pl.BlockSpec((ts, RET_DK), lambda i, j: (j, 0)),
                  _const_spec((RET_HEADS, RET_CHUNK, RET_CHUNK)),
                  _const_spec((RET_HEADS, RET_CHUNK, RET_DK)),
                  _const_spec((RET_HEADS, RET_CHUNK, RET_DK)),
                  pl.BlockSpec((1, N_MEM, MEM_WIDTH), lambda i, j: (i, 0, kv_blk)),
                  pl.BlockSpec((1, N_MEM, MEM_WIDTH), lambda i, j: (i, 0, kv_blk + 1)),
                  _const_spec((EV_YCOLS, d))],
        out_specs=pl.BlockSpec((1, ts, d), lambda i, j: (i, j, 0)),
        scratch_shapes=[pltpu.VMEM((POOL_HALO + EV_SUB, POOL_WIDTH), F32),
                        pltpu.VMEM((RET_HEADS, RET_DK, RET_DV), F32),
                        pltpu.VMEM((ts, EV_YCOLS), BF16)],
        compiler_params=_params(("arbitrary", "arbitrary")),
        name="even_layer",
    )(g_chunk, h, g.reshape(1, d), win, wbd, pool_scale.reshape(1, POOL_WIDTH), cos_t, sin_t, decay, xi_t, zeta_t,
      memkv, memkv, wout)


KV_W = NSA_KV * NSA_DH
OD_Q = 0
OD_KV = OD_Q + NSA_HEADS * NSA_DH
OD_A = OD_KV + 6 * KV_W
PIECE_KC, PIECE_VC, PIECE_KS, PIECE_VS, PIECE_KW, PIECE_VW = range(6)
OD_GL = OD_A
OD_XQ = OD_GL + NSA_KV * LANES
OD_GN = OD_XQ + MEM_WIDTH
OD_GM = OD_GN + NSA_HEADS * NSA_DH
OD_COLS = OD_GM + MEM_WIDTH


def _odd_in_kernel(h_ref, g_ref, wraw_ref, wb_ref, pe_ref, perm_ref, mk_ref, mv_ref,
                   q_ref, kaug_ref, vst_ref, kw_ref, vwt_ref, kca_ref, kcb_ref, vca_ref, vcb_ref, gatest_ref,
                   sg_ref, ym_ref, wa_ref):
    @pl.when((pl.program_id(0) == 0) & (pl.program_id(1) == 0))
    def _():
        for c0 in range(0, OD_A, 4 * LANES):
            wa_ref[:, c0:c0 + 4 * LANES] = wraw_ref[:, c0:c0 + 4 * LANES].astype(BF16)

    si = pl.program_id(1)
    n_sub = h_ref.shape[1] // NSA_TQ
    for sub in range(n_sub):
        _odd_in_subtile(si * n_sub + sub, sub, h_ref, g_ref, wa_ref, wb_ref, pe_ref, perm_ref, mk_ref, mv_ref,
                        q_ref, kaug_ref, vst_ref, kw_ref, vwt_ref, kca_ref, kcb_ref, vca_ref, vcb_ref, gatest_ref,
                        sg_ref, ym_ref)


def _odd_in_subtile(tile_idx, sub, h_ref, g_ref, wa_ref, wb_ref, pe_ref, perm_ref, mk_ref, mv_ref,
                    q_ref, kaug_ref, vst_ref, kw_ref, vwt_ref, kca_ref, kcb_ref, vca_ref, vcb_ref, gatest_ref,
                    sg_ref, ym_ref):
    ts = NSA_TQ
    tile = slice(sub * ts, (sub + 1) * ts)
    u = _rms(h_ref[0, tile], g_ref[...]).astype(BF16)

    def proj(start, width):
        if start < OD_A:
            return _mm(u, wa_ref[:, start:start + width])
        return _mm(u, wb_ref[:, start - OD_A:start - OD_A + width])

    zq = proj(OD_Q, NSA_HEADS * NSA_DH) * (NSA_DH ** -0.5 * LOG2E)
    zg = _silu(proj(OD_GN, NSA_HEADS * NSA_DH))
    for g in range(NSA_KV):
        for hh in range(NSA_HPG):
            sl = slice((g * NSA_HPG + hh) * NSA_DH, (g * NSA_HPG + hh + 1) * NSA_DH)
            q_ref[0, g, hh, tile] = zq[:, sl].astype(BF16)
            sg_ref[0, g, hh, tile] = zg[:, sl]

    zkv = proj(OD_KV, 6 * KV_W)
    lane = lax.broadcasted_iota(jnp.int32, (ts, LANES), 1)
    blk = (tile_idx * ts + lax.broadcasted_iota(jnp.int32, (ts, LANES), 0)) // SLC_BLOCK
    onehot = jnp.where(lane == blk, 1.0, 0.0).astype(BF16)
    ones_rows = jnp.where(lax.broadcasted_iota(jnp.int32, (V_ROWS - NSA_DH, ts), 0) == 0, 1.0, 0.0).astype(BF16)
    zgl = jax.nn.sigmoid(proj(OD_GL, NSA_KV * LANES))
    flat_in = []
    for g in range(NSA_KV):
        def piece(idx):
            off = idx * KV_W + g * NSA_DH
            return zkv[:, off:off + NSA_DH]
        kaug_ref[0, g, tile, 0:NSA_DH] = piece(PIECE_KS).astype(BF16)
        kaug_ref[0, g, tile, NSA_DH:2 * NSA_DH] = onehot
        vst_ref[0, g, sub, 0:NSA_DH] = piece(PIECE_VS).T.astype(BF16)
        vst_ref[0, g, sub, NSA_DH:V_ROWS] = ones_rows
        kw_ref[0, g, tile] = piece(PIECE_KW).astype(BF16)
        vwt_ref[0, g, sub, 0:NSA_DH] = piece(PIECE_VW).T.astype(BF16)
        vwt_ref[0, g, sub, NSA_DH:V_ROWS] = ones_rows
        kc, vc = piece(PIECE_KC), piece(PIECE_VC)
        flat_in += [(kc + pe_ref[0, 0]).astype(BF16), (kc + pe_ref[0, 1]).astype(BF16),
                    (vc + pe_ref[1, 0]).astype(BF16), (vc + pe_ref[1, 1]).astype(BF16)]
        gatest_ref[0, g, :, tile] = zgl[:, g * LANES:(g + 1) * LANES].T

    nj = ts // CMP_STRIDE
    perm = _mm(perm_ref[...], jnp.concatenate(flat_in, axis=1)).astype(BF16)
    for c, out_ref in enumerate((kca_ref, kcb_ref, vca_ref, vcb_ref) * NSA_KV):
        for l in range(CMP_STRIDE):
            out_ref[0, c // 4, sub * nj:(sub + 1) * nj, l * NSA_DH:(l + 1) * NSA_DH] = (
                perm[l * nj:(l + 1) * nj, c * NSA_DH:(c + 1) * NSA_DH])

    xq = proj(OD_XQ, MEM_WIDTH)
    gm = _silu(proj(OD_GM, MEM_WIDTH))
    for hd, om in enumerate(_mem_attention(xq, mk_ref[0], mv_ref[0])):
        sl = slice(hd * MEM_DH, (hd + 1) * MEM_DH)
        ym_ref[0, tile, sl] = (om * gm[:, sl]).astype(BF16)


def _odd_in(h, memkv, layer, g, w_in, cmp_pe, ts=2 * NSA_TQ):
    b, s, d = h.shape
    assert w_in.shape[1] == OD_A + 3 * NSA_HEADS + MEM_WIDTH + D_INNER
    gl, xq, gate = jnp.split(w_in[:, OD_A:], [3 * NSA_HEADS, 3 * NSA_HEADS + MEM_WIDTH], axis=1)
    gl = gl.reshape(d, 3, NSA_KV, NSA_HPG).transpose(0, 2, 1, 3).reshape(d, NSA_KV, 3 * NSA_HPG)
    gl = jnp.pad(gl, ((0, 0), (0, 0), (0, LANES - 3 * NSA_HPG))).reshape(d, NSA_KV * LANES)
    wb = jnp.concatenate([w.astype(BF16) for w in (gl, xq, gate)], axis=1)
    sub = NSA_TQ
    reps = sub // CMP_STRIDE
    pe = jnp.stack([jnp.stack([jnp.tile(cmp_pe[kv, :CMP_STRIDE], (reps, 1)),
                               jnp.tile(cmp_pe[kv, CMP_STRIDE:], (reps, 1))]) for kv in range(2)])
    kv_blk = 2 * layer
    assert ts % sub == 0
    nj = sub // CMP_STRIDE
    perm = np.zeros((sub, sub), np.float32)
    perm[np.arange(sub), (np.arange(sub) % nj) * CMP_STRIDE + np.arange(sub) // nj] = 1.0
    head_t = jax.ShapeDtypeStruct((b, NSA_KV, s, NSA_DH), BF16)
    head_spec = pl.BlockSpec((1, NSA_KV, ts, NSA_DH), lambda i, j: (i, 0, j, 0))
    headt_t = jax.ShapeDtypeStruct((b, NSA_KV, s // sub, V_ROWS, sub), BF16)
    headt_spec = pl.BlockSpec((1, NSA_KV, ts // sub, V_ROWS, sub), lambda i, j: (i, 0, j, 0, 0))
    flat_t = jax.ShapeDtypeStruct((b, NSA_KV, s // CMP_STRIDE, CMP_STRIDE * NSA_DH), BF16)
    flat_spec = pl.BlockSpec((1, NSA_KV, ts // CMP_STRIDE, CMP_STRIDE * NSA_DH), lambda i, j: (i, 0, j, 0))
    qlike_spec = pl.BlockSpec((1, NSA_KV, NSA_HPG, ts, NSA_DH), lambda i, j: (i, 0, 0, j, 0))
    return pl.pallas_call(
        _odd_in_kernel,
        out_shape=[jax.ShapeDtypeStruct((b, NSA_KV, NSA_HPG, s, NSA_DH), BF16),
                   jax.ShapeDtypeStruct((b, NSA_KV, s, 2 * NSA_DH), BF16),
                   headt_t, head_t, headt_t, flat_t, flat_t, flat_t, flat_t,
                   jax.ShapeDtypeStruct((b, NSA_KV, LANES, s), F32),
                   jax.ShapeDtypeStruct((b, NSA_KV, NSA_HPG, s, NSA_DH), F32),
                   jax.ShapeDtypeStruct((b, s, MEM_WIDTH), BF16)],
        grid=(b, s // ts),
        in_specs=[pl.BlockSpec((1, ts, d), lambda i, j: (i, j, 0)),
                  _const_spec((1, d)),
                  _const_spec((d, OD_A)),
                  _const_spec((d, OD_COLS - OD_A)),
                  _const_spec((2, 2, sub, NSA_DH)),
                  _const_spec((sub, sub)),
                  pl.BlockSpec((1, N_MEM, MEM_WIDTH), lambda i, j: (i, 0, kv_blk)),
                  pl.BlockSpec((1, N_MEM, MEM_WIDTH), lambda i, j: (i, 0, kv_blk + 1))],
        out_specs=[qlike_spec,
                   pl.BlockSpec((1, NSA_KV, ts, 2 * NSA_DH), lambda i, j: (i, 0, j, 0)),
                   headt_spec, head_spec, headt_spec, flat_spec, flat_spec, flat_spec, flat_spec,
                   pl.BlockSpec((1, NSA_KV, LANES, ts), lambda i, j: (i, 0, 0, j)),
                   qlike_spec,
                   pl.BlockSpec((1, ts, MEM_WIDTH), lambda i, j: (i, j, 0))],
        scratch_shapes=[pltpu.VMEM((d, OD_A), BF16)],
        compiler_params=_params(("arbitrary", "arbitrary")),
        name="odd_in_proj",
    )(h, g.reshape(1, d), w_in, wb, pe, jnp.asarray(perm, BF16), memkv, memkv)


def _compress_kernel(xka_ref, xkb_ref, xva_ref, xvb_ref, w1a_ref, w1b_ref, b1_ref, w2_ref, ovt_ref, kc_ref, vct_ref):
    n = xka_ref.shape[2]
    k_cols = vct_ref.shape[3]

    def block_mlp(kv, xa, xb):
        first = _mm(xa[0, 0], w1a_ref[kv])
        second = _mm(xb[0, 0], w1b_ref[kv])
        hid = first + pltpu.roll(second, n - 1, 0) + b1_ref[kv]
        return _mm(_silu(hid).astype(BF16), w2_ref[kv])

    kc_ref[0, 0, 0:CMP_PAD] = jnp.zeros((CMP_PAD, NSA_DH), BF16)
    kc_ref[0, 0, CMP_PAD:] = block_mlp(0, xka_ref, xkb_ref).astype(BF16)
    vc = jnp.concatenate([jnp.zeros((CMP_PAD, NSA_DH), F32), block_mlp(1, xva_ref, xvb_ref),
                          jnp.zeros((k_cols - CMP_PAD - n, NSA_DH), F32)], axis=0)
    vct_ref[0, 0, 0:NSA_DH] = vc.T.astype(BF16)
    vct_ref[0, 0, NSA_DH:] = ovt_ref[...]


def _compress(kca, kcb, vca, vcb, w1, b1, w2):
    b, g, n, half = kca.shape
    dh = half // CMP_STRIDE
    s = n * CMP_STRIDE
    xs = (kca, kcb, vca, vcb)
    w1 = w1.astype(BF16)
    ovt = _overlap_table(s)
    k_cols = ovt.shape[1]
    x_spec = pl.BlockSpec((1, 1, n, half), lambda i, j: (i, j, 0, 0))
    return pl.pallas_call(
        _compress_kernel,
        out_shape=[jax.ShapeDtypeStruct((b, g, n + CMP_PAD, dh), BF16),
                   jax.ShapeDtypeStruct((b, g, dh + LANES, k_cols), BF16)],
        grid=(b, g),
        in_specs=[x_spec, x_spec, x_spec, x_spec,
                  _const_spec((2, half, CMP_HIDDEN)), _const_spec((2, half, CMP_HIDDEN)),
                  _const_spec((2, 1, CMP_HIDDEN)), _const_spec((2, CMP_HIDDEN, dh)),
                  _const_spec((LANES, k_cols))],
        out_specs=[pl.BlockSpec((1, 1, n + CMP_PAD, dh), lambda i, j: (i, j, 0, 0)),
                   pl.BlockSpec((1, 1, dh + LANES, k_cols), lambda i, j: (i, j, 0, 0))],
        compiler_params=_params(("arbitrary", "arbitrary")),
        name="compress",
    )(*xs, w1[:, :half], w1[:, half:], b1.reshape(2, 1, CMP_HIDDEN), w2.astype(BF16), ovt)


def _nsa_kernel(q_ref, kaug_ref, vst_ref, kw_ref, vwt_ref, kcmp_ref, vcmpt_ref, gatest_ref, sg_ref,
                d0_ref, d1_ref, dw_ref, cmpq_ref, y_ref, qaug_ref, m_ref, acc_ref, s_ref, mw_ref, accw_ref, sw_ref,
                mix_ref, tmax_ref, tmaxw_ref):
    i = pl.program_id(2)
    tq = NSA_TQ
    rows = NSA_HPG * tq
    s_len = kw_ref.shape[2]
    n_cmp = s_len // CMP_STRIDE
    n_slc = s_len // SLC_BLOCK
    t0 = i * tq
    q = q_ref[0, 0].reshape(rows, NSA_DH)


    def stream(mx_ref, ac_ref, sc2_ref, tmax_ref):
        def reset():
            mx_ref[...] = jnp.full(mx_ref.shape, LOWEST, F32)
            ac_ref[...] = jnp.zeros(ac_ref.shape, F32)

        def s_to(slot, k_ref, query, tile, bias=None):
            k0 = pl.multiple_of(tile * tq, tq)
            st = _mm_nt(k_ref[0, 0, pl.ds(k0, tq)], query)
            if bias is not None:
                st = st + bias
            sc2_ref[slot] = st
            tmax_ref[slot] = jnp.max(st, axis=0, keepdims=True)

        def pv_from(slot, vt):
            st = sc2_ref[slot]
            m_prev = mx_ref[...]
            m_new = jnp.maximum(m_prev, tmax_ref[slot])
            pt = jnp.exp2(st - m_new).astype(BF16)
            ac_ref[...] = jnp.exp2(m_prev - m_new) * ac_ref[...] + _mm(vt, pt)
            mx_ref[...] = m_new

        def finish(cols=slice(None)):
            return ac_ref[0:NSA_DH, cols] / ac_ref[NSA_DH:NSA_DH + 1, cols]

        return reset, s_to, pv_from, finish

    def off_unless(cond):
        return jnp.where(cond, 0.0, NEG)

    prev1 = jnp.maximum(i - 1, 0)
    prev2 = jnp.maximum(i - 2, 0)
    w_reset, w_s_to, w_pv_from, w_finish = stream(mw_ref, accw_ref, sw_ref, tmaxw_ref)


    n_pad = kcmp_ref.shape[2]
    j_near = pl.multiple_of(i * (tq // CMP_STRIDE), tq // CMP_STRIDE)
    qaug_ref[:, 0:NSA_DH] = q
    qaug_ref[:, NSA_DH:] = cmpq_ref[0]
    krow = lax.broadcasted_iota(jnp.int32, (n_pad, NSA_DH), 0)
    klane = lax.broadcasted_iota(jnp.int32, (n_pad, NSA_DH), 1)
    slot = krow - j_near
    in_window = (klane < 2 * CMP_NEAR) & (jnp.bitwise_and(klane, CMP_NEAR - 1) == slot)
    visible = (krow >= CMP_PAD) & (slot < CMP_NEAR)
    key_cols = jnp.where(klane == 2 * CMP_NEAR, jnp.where(visible, 0.0, NEG), jnp.where(in_window, 1.0, 0.0))
    s_cmp = _mm_nt(jnp.concatenate([kcmp_ref[0, 0], key_cols.astype(BF16)], axis=1), qaug_ref[...])
    w_reset()
    w_s_to(0, kw_ref, q, prev2, dw_ref[0, 0])
    w_s_to(1, kw_ref, q, prev1, d1_ref[0, 0])
    p_cmp = jnp.exp2(s_cmp - jnp.max(s_cmp, axis=0, keepdims=True))
    tcol = t0 + jnp.bitwise_and(lax.broadcasted_iota(jnp.int32, (1, rows), 1), tq - 1)
    scale = jnp.where(tcol >= CMP_BLOCK - 1, 1.0, 0.0) / jnp.sum(p_cmp, axis=0, keepdims=True)
    p_cmp = p_cmp.astype(BF16)
    k_cols = vcmpt_ref.shape[3]
    if k_cols > n_pad:
        p_cmp = jnp.concatenate([p_cmp, jnp.zeros((k_cols - n_pad, rows), BF16)], axis=0)
    w_pv_from(0, vwt_ref[0, 0, prev2])
    both = _mm(vcmpt_ref[0, 0], p_cmp) * scale
    w_s_to(0, kw_ref, q, i, d0_ref[0, 0])
    w_pv_from(1, vwt_ref[0, 0, prev1])
    gtst = gatest_ref[0, 0]
    n_blk = -(-n_slc // 8) * 8
    imp = both[NSA_DH:NSA_DH + n_blk, 0:tq]
    for hh in range(NSA_HPG):
        hs = slice(hh * tq, (hh + 1) * tq)
        mix_ref[:, hs] = gtst[hh:hh + 1] * both[0:NSA_DH, hs]
        if hh:
            imp = imp + both[NSA_DH:NSA_DH + n_blk, hs]

    blk = lax.broadcasted_iota(jnp.int32, (n_blk, tq), 0)
    tq_pos = t0 + lax.broadcasted_iota(jnp.int32, (n_blk, tq), 1)
    cur = tq_pos // SLC_BLOCK
    forced = (blk == 0) | (blk == cur) | (blk == cur - 1)
    future = blk * SLC_BLOCK > tq_pos
    imp = jnp.where(forced, BIG, jnp.where(future, -BIG, imp))
    imp = jnp.where(blk < n_slc, imp, LOWEST)
    blk_f = blk.astype(F32)
    sel = jnp.zeros((n_blk, tq), jnp.bool_)
    for _ in range(SLC_TOPK):
        top = jnp.max(imp, axis=0, keepdims=True)
        first = jnp.min(jnp.where(imp == top, blk_f, float(LANES)), axis=0, keepdims=True)
        hit = blk_f == first
        sel = sel | hit
        imp = jnp.where(hit, KNOCKED, imp)
    sneg_t = jnp.where(sel, 0.0, NEG)
    if n_blk < LANES:
        sneg_t = jnp.concatenate([sneg_t, jnp.zeros((LANES - n_blk, tq), F32)], axis=0)
    sneg = sneg_t.T.astype(BF16)
    for hh in range(NSA_HPG):
        qaug_ref[hh * tq:(hh + 1) * tq, NSA_DH:2 * NSA_DH] = sneg
    w_pv_from(0, vwt_ref[0, 0, i])
    for hh in range(NSA_HPG):
        hs = slice(hh * tq, (hh + 1) * tq)
        mix_ref[:, hs] = mix_ref[:, hs] + gtst[2 * NSA_HPG + hh:2 * NSA_HPG + hh + 1] * w_finish(hs)

    reset, s_to, pv_from, finish = stream(m_ref, acc_ref, s_ref, tmax_ref)
    reset()
    n_far = prev1
    n_pairs = jnp.maximum(n_far - 1, 0) // 2
    s_to(0, kaug_ref, qaug_ref[...], 0, off_unless(n_far >= 1))

    def far_pair(t2):
        s_to(1, kaug_ref, qaug_ref[...], 2 * t2 + 1)
        pv_from(0, vst_ref[0, 0, 2 * t2])
        s_to(0, kaug_ref, qaug_ref[...], 2 * t2 + 2)
        pv_from(1, vst_ref[0, 0, 2 * t2 + 1])

    def far_quad(t4, carry):
        far_pair(2 * t4)
        far_pair(2 * t4 + 1)
        return carry

    lax.fori_loop(0, n_pairs // 2, far_quad, 0)

    @pl.when(n_pairs % 2 == 1)
    def _():
        far_pair(n_pairs - 1)
    c0 = 2 * n_pairs
    two_left = n_far - c0 == 2

    @pl.when(two_left)
    def _():
        s_to(1, kaug_ref, qaug_ref[...], c0 + 1)
        pv_from(0, vst_ref[0, 0, c0])
        s_to(0, kaug_ref, qaug_ref[...], prev1, d1_ref[0, 0])
        pv_from(1, vst_ref[0, 0, c0 + 1])
        s_to(1, kaug_ref, qaug_ref[...], i, d0_ref[0, 0])
        pv_from(0, vst_ref[0, 0, prev1])
        pv_from(1, vst_ref[0, 0, i])

    @pl.when(jnp.logical_not(two_left))
    def _():
        s_to(1, kaug_ref, qaug_ref[...], prev1, d1_ref[0, 0])
        pv_from(0, vst_ref[0, 0, c0])
        s_to(0, kaug_ref, qaug_ref[...], i, d0_ref[0, 0])
        pv_from(1, vst_ref[0, 0, prev1])
        pv_from(0, vst_ref[0, 0, i])

    for hh in range(NSA_HPG):
        hs = slice(hh * tq, (hh + 1) * tq)
        ot = mix_ref[:, hs] + gatest_ref[0, 0, NSA_HPG + hh:NSA_HPG + hh + 1] * finish(hs)
        y_ref[0, :, hh * NSA_DH:(hh + 1) * NSA_DH] = (ot.T * sg_ref[0, 0, hh]).astype(BF16)


def _overlap_table(s):
    n_cmp = (s - CMP_BLOCK) // CMP_STRIDE + 1
    n_slc = s // SLC_BLOCK
    cst = np.arange(n_cmp)[:, None] * CMP_STRIDE
    sst = np.arange(n_slc)[None, :] * SLC_BLOCK
    ov = np.clip(np.minimum(cst + CMP_BLOCK, sst + SLC_BLOCK) - np.maximum(cst, sst), 0, None) / CMP_STRIDE
    k_cols = -(-(CMP_PAD + s // CMP_STRIDE) // LANES) * LANES
    full = np.zeros((LANES, k_cols), np.float32)
    full[:n_slc, CMP_PAD:CMP_PAD + n_cmp] = ov.T
    return jnp.asarray(full, BF16)


def _nsa(q, kaug, vst, kw, vwt, kcmp, vcmpt, gatest, sg, bias, cmpq):
    b, g, hpg, s, dh = q.shape
    tq = NSA_TQ
    assert WINDOW == 2 * tq and s % tq == 0 and s // SLC_BLOCK <= LANES and s // SLC_BLOCK >= SLC_TOPK
    n_pad = kcmp.shape[2]
    rows = hpg * tq
    seq_spec = pl.BlockSpec((1, 1, s, dh), lambda i, j, k: (i, j, 0, 0))
    seqt_spec = pl.BlockSpec((1, 1, s // tq, V_ROWS, tq), lambda i, j, k: (i, j, 0, 0, 0))
    qlike_spec = pl.BlockSpec((1, 1, hpg, tq, dh), lambda i, j, k: (i, j, 0, k, 0))

    def bias_spec(idx, needs_tiles_before=0):
        return pl.BlockSpec((1, 1, tq, rows),
                            lambda i, j, k: (j, jnp.where(k >= needs_tiles_before, idx, BIAS_MASKED), 0, 0))

    return pl.pallas_call(
        _nsa_kernel,
        out_shape=jax.ShapeDtypeStruct((b, s, g * hpg * dh), BF16),
        grid=(b, g, s // tq),
        in_specs=[qlike_spec,
                  pl.BlockSpec((1, 1, s, 2 * dh), lambda i, j, k: (i, j, 0, 0)),
                  seqt_spec, seq_spec, seqt_spec,
                  pl.BlockSpec((1, 1, n_pad, dh), lambda i, j, k: (i, j, 0, 0)),
                  pl.BlockSpec((1, 1, dh + LANES, vcmpt.shape[3]), lambda i, j, k: (i, j, 0, 0)),
                  pl.BlockSpec((1, 1, LANES, tq), lambda i, j, k: (i, j, 0, k)),
                  qlike_spec,
                  bias_spec(0), bias_spec(1, 1), bias_spec(2, 2),
                  pl.BlockSpec((1, rows, LANES), lambda i, j, k: (j, 0, 0))],
        out_specs=pl.BlockSpec((1, tq, hpg * dh), lambda i, j, k: (i, k, j)),
        scratch_shapes=[pltpu.VMEM((rows, 2 * dh), BF16),
                        pltpu.VMEM((1, rows), F32),
                        pltpu.VMEM((V_ROWS, rows), F32),
                        pltpu.VMEM((2, tq, rows), F32),
                        pltpu.VMEM((1, rows), F32),
                        pltpu.VMEM((V_ROWS, rows), F32),
                        pltpu.VMEM((2, tq, rows), F32),
                        pltpu.VMEM((dh, rows), F32),
                        pltpu.VMEM((2, 1, rows), F32),
                        pltpu.VMEM((2, 1, rows), F32)],
        compiler_params=_params(("arbitrary", "arbitrary", "arbitrary")),
        name="nsa_attention",
    )(q, kaug, vst, kw, vwt, kcmp, vcmpt, gatest, sg, bias, bias, bias, cmpq)


def _odd_out_kernel(h_ref, yn_ref, ym_ref, wn_ref, wm_ref, fg_ref, o_ref, *, final_norm, sub):
    for r0 in range(0, h_ref.shape[0], sub):
        rows = slice(r0, r0 + sub)
        out = h_ref[rows] + _mm(yn_ref[rows], wn_ref[...]) + _mm(ym_ref[rows], wm_ref[...])
        o_ref[rows] = _rms(out, fg_ref[...]) if final_norm else out


def _odd_out(h, yn, ym, w_out, final_g, final_norm, tm=1024, sub=256):
    b, s, d = h.shape
    t = b * s
    nw = NSA_HEADS * NSA_DH
    wout = w_out.astype(BF16)
    out = pl.pallas_call(
        functools.partial(_odd_out_kernel, final_norm=final_norm, sub=sub),
        out_shape=jax.ShapeDtypeStruct((t, d), F32),
        grid=(t // tm,),
        in_specs=[pl.BlockSpec((tm, d), lambda i: (i, 0)),
                  pl.BlockSpec((tm, nw), lambda i: (i, 0)),
                  pl.BlockSpec((tm, MEM_WIDTH), lambda i: (i, 0)),
                  _const_spec((nw, d)), _const_spec((MEM_WIDTH, d)), _const_spec((1, d))],
        out_specs=pl.BlockSpec((tm, d), lambda i: (i, 0)),
        compiler_params=_params(("arbitrary",)),
        name="odd_out_proj",
    )(h.reshape(t, d), yn.reshape(t, nw), ym.reshape(t, MEM_WIDTH), wout[:nw], wout[nw:], final_g.reshape(1, d))
    return out.reshape(b, s, d)


def _final_norm_kernel(h_ref, g_ref, o_ref):
    o_ref[...] = _rms(h_ref[...], g_ref[...])


def _final_norm(h, final_g, tm=512):
    b, s, d = h.shape
    t = b * s
    out = pl.pallas_call(
        _final_norm_kernel,
        out_shape=jax.ShapeDtypeStruct((t, d), F32),
        grid=(t // tm,),
        in_specs=[pl.BlockSpec((tm, d), lambda i: (i, 0)), _const_spec((1, d))],
        out_specs=pl.BlockSpec((tm, d), lambda i: (i, 0)),
        compiler_params=_params(("arbitrary",)),
        name="final_norm",
    )(h.reshape(t, d), final_g.reshape(1, d))
    return out.reshape(b, s, d)


def kernel(x, mem, norm_g, final_g, mem_norm_g, rel_bias, ev_w_in, ev_pool_w, ev_pool_scale, ev_w_mem_kv, ev_w_out,
           od_w_in, od_cmp_pe, od_cmp_w1, od_cmp_b1, od_cmp_w2, od_w_mem_kv, od_w_out):
    depth = norm_g.shape[0]
    w_mem = [(ev_w_mem_kv if i % 2 == 0 else od_w_mem_kv)[i // 2] for i in range(depth)]
    memkv = _memkv(mem, mem_norm_g, jnp.concatenate(w_mem, axis=1).astype(BF16))
    bias, cmpq = _bias_tiles(rel_bias) if depth > 1 else (None, None)
    h = x
    for i in range(depth):
        j = i // 2
        last = i == depth - 1
        if i % 2 == 0:
            h = _even_layer(h, memkv, i, norm_g[i], ev_w_in[j], ev_pool_w[j], ev_pool_scale[j], ev_w_out[j])
            if last:
                h = _final_norm(h, final_g)
        else:
            (q, kaug, vst, kw, vwt, kca, kcb, vca, vcb, gatest, sg, ym) = _odd_in(
                h, memkv, i, norm_g[i], od_w_in[j], od_cmp_pe[j])
            kcmp, vcmp = _compress(kca, kcb, vca, vcb, od_cmp_w1[j], od_cmp_b1[j], od_cmp_w2[j])
            yn = _nsa(q, kaug, vst, kw, vwt, kcmp, vcmp, gatest, sg, bias, cmpq)
            h = _odd_out(h, yn, ym, od_w_out[j], final_g, last)
    return h
```

```python
import functools
import math

import numpy as np
import jax
import jax.numpy as jnp
from jax import lax
from jax.experimental import pallas as pl
from jax.experimental.pallas import tpu as pltpu

F32 = jnp.float32
BF16 = jnp.bfloat16

D_MODEL = 1024
D_INNER = 2048
N_MEM = 256
EPS = 1e-6
NEG = -1e30
BIG = 1e30

POOL_WINDOWS = (2, 4, 8, 16)
POOL_WIDTH = 768
POOL_GROUP = 192
POOL_HALO = 16

RET_HEADS = 4
RET_DK = 128
RET_DV = 192
RET_CHUNK = 128
ROPE_BASE = 10000.0

MEM_HEADS = 4
MEM_DH = 128
MEM_WIDTH = 512

NSA_HEADS = 12
NSA_KV = 2
NSA_HPG = 6
NSA_DH = 128
CMP_BLOCK = 32
CMP_STRIDE = 16
CMP_HIDDEN = 256
SLC_BLOCK = 64
SLC_TOPK = 8
WINDOW = 512
REL_BUCKETS = 32
REL_MAX_DIST = 128

LANES = 128
NSA_TQ = 256
CMP_NEAR = 32
CMP_PAD = 16
LOG2E = math.log2(math.e)
V_ROWS = NSA_DH + 16
LOWEST = -3.0e38
KNOCKED = -3.3e38
BIAS_MASKED = 3

VMEM_LIMIT = 56 * 1024 * 1024


def _mm(a, b):
    return jnp.dot(a, b, preferred_element_type=F32)


def _mm_nt(a, b):
    return lax.dot_general(a, b, (((1,), (1,)), ((), ())), preferred_element_type=F32)


def _mm_tn(a, b):
    return lax.dot_general(a, b, (((0,), (0,)), ((), ())), preferred_element_type=F32)


def _rms(x, g):
    return x * lax.rsqrt(jnp.mean(x * x, axis=-1, keepdims=True) + EPS) * g


def _silu(x):
    return x * jax.nn.sigmoid(x)


def _const_spec(shape):
    nd = len(shape)
    return pl.BlockSpec(shape, lambda *_: (0,) * nd, pipeline_mode=pl.Buffered(1))


def _params(sem):
    return pltpu.CompilerParams(dimension_semantics=sem, vmem_limit_bytes=VMEM_LIMIT)


def _bucket_starts():
    n = np.arange(REL_MAX_DIST + 1)
    max_exact = REL_BUCKETS // 2
    nf = np.maximum(n, 1).astype(np.float32)
    large = max_exact + (np.log(nf / np.float32(max_exact)) / np.float32(math.log(REL_MAX_DIST / max_exact))
                         * np.float32(REL_BUCKETS - max_exact)).astype(np.int32)
    bucket = np.where(n < max_exact, n, np.minimum(large, REL_BUCKETS - 1))
    assert np.all(np.diff(bucket) >= 0) and bucket[-1] == REL_BUCKETS - 1
    return tuple(int(np.argmax(bucket >= b)) for b in range(REL_BUCKETS))


def _bias_kernel(tab_ref, rel_ref, relq_ref, out_ref, cmpq_ref, *, boxes, starts):
    h = pl.program_id(0)

    def lookup(rel):
        far = tab_ref[REL_BUCKETS - 1, h]
        val = jnp.full(rel.shape, tab_ref[0, h] - far, F32)
        for b in range(1, REL_BUCKETS - 1):
            val = jnp.where(rel >= starts[b], tab_ref[b, h] - far, val)
        val = jnp.where(rel >= starts[REL_BUCKETS - 1], 0.0, val)
        return jnp.where(rel < 0, NEG, val * LOG2E)

    for kind, box in enumerate(boxes):
        out_ref[0, kind] = jnp.where(rel_ref[kind] < 0, NEG, 0.0)
        if box is not None:
            r0, r1, c0, c1 = box
            out_ref[0, kind, r0:r1, c0:c1] = lookup(rel_ref[kind, r0:r1, c0:c1])

    relq = relq_ref[...]
    lane = lax.broadcasted_iota(jnp.int32, relq.shape, 1)
    bias = jnp.where(lane < CMP_NEAR, lookup(relq), 0.0)
    hi = bias.astype(BF16)
    lo = (bias - hi.astype(F32)).astype(BF16).astype(F32)
    one = jnp.where(lane == 2 * CMP_NEAR, 1.0, 0.0)
    cmpq_ref[0] = (hi.astype(F32) + pltpu.roll(lo, CMP_NEAR, 1) + one).astype(BF16)


def _bias_tiles(rel_bias):
    tq = NSA_TQ
    r = np.arange(tq)[:, None]
    c = np.arange(tq)[None, :]
    d0 = r - c
    d1 = tq + r - c
    dw = np.where(c > r, WINDOW + r - c, -1)
    gc = np.where(c < CMP_NEAR, r - CMP_STRIDE * c + (CMP_STRIDE * CMP_PAD - (CMP_BLOCK - 1)), -1)
    rel_np = np.stack([d0.T, d1.T, dw.T, np.full((tq, tq), -1)]).astype(np.int32)
    boxes = []
    for tile in rel_np:
        rr, cc = np.nonzero((tile >= 0) & (tile < REL_MAX_DIST))
        boxes.append(None if rr.size == 0 else tuple(int(v) for v in (
            rr.min() // 8 * 8, -(-(rr.max() + 1) // 8) * 8, cc.min() // LANES * LANES, -(-(cc.max() + 1) // LANES) * LANES)))
    rel = jnp.asarray(rel_np)
    nt = rel.shape[0]
    return pl.pallas_call(
        functools.partial(_bias_kernel, boxes=tuple(boxes), starts=_bucket_starts()),
        out_shape=[jax.ShapeDtypeStruct((NSA_KV, nt, tq, NSA_HPG * tq), F32),
                   jax.ShapeDtypeStruct((NSA_KV, NSA_HPG * tq, LANES), BF16)],
        grid=(NSA_HEADS,),
        in_specs=[pl.BlockSpec(memory_space=pltpu.SMEM),
                  _const_spec((nt, tq, tq)),
                  _const_spec((tq, LANES))],
        out_specs=[pl.BlockSpec((1, nt, tq, tq), lambda h: (h // NSA_HPG, 0, 0, h % NSA_HPG)),
                   pl.BlockSpec((1, tq, LANES), lambda h: (h // NSA_HPG, h % NSA_HPG, 0))],
        compiler_params=_params(("arbitrary",)),
        name="bias_tiles",
    )(rel_bias.astype(F32), rel, jnp.asarray(gc[:, :LANES].astype(np.int32)))


def _memkv_kernel(mem_ref, g_ref, w_ref, out_ref):
    y = _rms(mem_ref[0], g_ref[...]).astype(BF16)
    out_ref[0] = _mm(y, w_ref[...]).astype(BF16)


def _memkv(mem, mem_norm_g, w_all):
    b, m, d = mem.shape
    n = w_all.shape[1]
    return pl.pallas_call(
        _memkv_kernel,
        out_shape=jax.ShapeDtypeStruct((b, m, n), BF16),
        grid=(b,),
        in_specs=[pl.BlockSpec((1, m, d), lambda i: (i, 0, 0)),
                  _const_spec((1, d)),
                  _const_spec((d, n))],
        out_specs=pl.BlockSpec((1, m, n), lambda i: (i, 0, 0)),
        compiler_params=_params(("arbitrary",)),
        name="mem_kv",
    )(mem, mem_norm_g.reshape(1, d), w_all)


def _mem_attention(xq, mk, mv):
    outs = []
    for hd in range(MEM_HEADS):
        sl = slice(hd * MEM_DH, (hd + 1) * MEM_DH)
        qm = (xq[:, sl] * (MEM_DH ** -0.5)).astype(BF16)
        s = _mm_nt(qm, mk[:, sl])
        p = jnp.exp(s - jnp.max(s, axis=-1, keepdims=True))
        l = jnp.sum(p, axis=-1, keepdims=True)
        outs.append(_mm(p.astype(BF16), mv[:, sl]) / l)
    return outs


EV_ZA = 0
EV_RQ = EV_ZA + POOL_WIDTH
EV_RK = EV_RQ + RET_HEADS * RET_DK
EV_RV = EV_RK + RET_HEADS * RET_DK
EV_XQ = EV_RV + RET_HEADS * RET_DV
EV_GA = EV_XQ + MEM_WIDTH
EV_GR = EV_GA + POOL_WIDTH
EV_GM = EV_GR + RET_HEADS * RET_DV
EV_COLS = EV_GM + MEM_WIDTH
EV_YA = 0
EV_YR = POOL_WIDTH
EV_YM = EV_YR + RET_HEADS * RET_DV
EV_YCOLS = EV_YM + MEM_WIDTH
EV_SUB = 256


def _even_kernel(gch_ref, h_ref, g_ref, win_ref, wbd_ref, pscale_ref, cos_ref, sin_ref, decay_ref, xi_ref,
                 zeta_ref, mk_ref, mv_ref, wout_ref, o_ref, ext_ref, state_ref, y_ref):
    si = pl.program_id(1)
    ts = h_ref.shape[1]

    @pl.when(si == 0)
    def _():
        ext_ref[0:POOL_HALO, :] = jnp.zeros((POOL_HALO, POOL_WIDTH), F32)
        state_ref[...] = jnp.zeros(state_ref.shape, F32)

    for r0 in range(0, ts, EV_SUB):
        _even_subtile(si * ts + r0, slice(r0, r0 + EV_SUB), gch_ref, h_ref, g_ref, win_ref, wbd_ref, pscale_ref,
                      cos_ref, sin_ref, decay_ref, xi_ref, zeta_ref, mk_ref, mv_ref, wout_ref, o_ref, ext_ref,
                      state_ref, y_ref)


def _even_subtile(t0, tile, gch_ref, h_ref, g_ref, win_ref, wbd_ref, pscale_ref, cos_ref, sin_ref, decay_ref, xi_ref,
                  zeta_ref, mk_ref, mv_ref, wout_ref, o_ref, ext_ref, state_ref, y_ref):
    ts = EV_SUB
    r0 = tile.start
    h = h_ref[0, tile]
    u = _rms(h, g_ref[...]).astype(BF16)

    def proj(start, width):
        return _mm(u, win_ref[:, start:start + width])

    ext_ref[POOL_HALO:, :] = proj(EV_ZA, POOL_WIDTH)
    e = ext_ref[...]
    s2 = e + pltpu.roll(e, 1, 0)
    s4 = s2 + pltpu.roll(s2, 2, 0)
    s8 = s4 + pltpu.roll(s4, 4, 0)
    s16 = s8 + pltpu.roll(s8, 8, 0)
    lane = lax.broadcasted_iota(jnp.int32, e.shape, 1)
    row = lax.broadcasted_iota(jnp.int32, e.shape, 0)
    tpos = t0 + row - POOL_HALO
    g0, g1, g2 = lane < POOL_GROUP, lane < 2 * POOL_GROUP, lane < 3 * POOL_GROUP
    wsum = jnp.where(g0, s2, jnp.where(g1, s4, jnp.where(g2, s8, s16)))
    wlen = jnp.where(g0, POOL_WINDOWS[0], jnp.where(g1, POOL_WINDOWS[1],
                                                    jnp.where(g2, POOL_WINDOWS[2], POOL_WINDOWS[3])))
    cnt = jnp.maximum(jnp.minimum(tpos + 1, wlen), 1).astype(F32)
    pooled = (wsum / cnt - e)[POOL_HALO:]
    ext_ref[0:POOL_HALO, :] = e[ts:ts + POOL_HALO]
    a = _mm(pooled.astype(BF16), wbd_ref[...]) * pscale_ref[...]
    y_ref[tile, EV_YA:EV_YA + POOL_WIDTH] = (a * _silu(proj(EV_GA, POOL_WIDTH))).astype(BF16)

    cos = cos_ref[tile]
    sin = sin_ref[tile]
    zv = proj(EV_RV, RET_HEADS * RET_DV).astype(BF16)
    gate_r = _silu(proj(EV_GR, RET_HEADS * RET_DV))
    q_rot, k_rot = [], []
    for hd in range(RET_HEADS):
        qh = proj(EV_RQ + hd * RET_DK, RET_DK)
        kh = proj(EV_RK + hd * RET_DK, RET_DK)
        q_rot.append((qh * cos + pltpu.roll(qh, RET_DK // 2, 1) * sin) * (RET_DK ** -0.5))
        k_rot.append(kh * cos + pltpu.roll(kh, RET_DK // 2, 1) * sin)
    for c in range(ts // RET_CHUNK):
        rows = slice(c * RET_CHUNK, (c + 1) * RET_CHUNK)
        normed = []
        for hd in range(RET_HEADS):
            qc, kc = q_rot[hd][rows], k_rot[hd][rows]
            vc = zv[rows, hd * RET_DV:(hd + 1) * RET_DV]
            att = _mm_nt(qc.astype(BF16), kc.astype(BF16)) * decay_ref[hd]
            state = state_ref[hd]
            o = _mm(att.astype(BF16), vc) + _mm((qc * xi_ref[hd]).astype(BF16), state.astype(BF16))
            kv = _mm_tn((kc * zeta_ref[hd]).astype(BF16), vc)
            state_ref[hd] = state * gch_ref[hd] + kv
            dlt = o - jnp.mean(o, axis=-1, keepdims=True)
            normed.append(dlt * lax.rsqrt(jnp.mean(dlt * dlt, axis=-1, keepdims=True) + EPS))
        y_ref[r0 + c * RET_CHUNK:r0 + (c + 1) * RET_CHUNK, EV_YR:EV_YM] = (
            jnp.concatenate(normed, axis=-1) * gate_r[rows]).astype(BF16)

    xq = proj(EV_XQ, MEM_WIDTH)
    gm = _silu(proj(EV_GM, MEM_WIDTH))
    for hd, om in enumerate(_mem_attention(xq, mk_ref[0], mv_ref[0])):
        sl = slice(hd * MEM_DH, (hd + 1) * MEM_DH)
        y_ref[tile, EV_YM + hd * MEM_DH:EV_YM + (hd + 1) * MEM_DH] = (om * gm[:, sl]).astype(BF16)

    o_ref[0, tile] = h + _mm(y_ref[tile], wout_ref[...])


def _retention_tables(s):
    half = RET_DK // 2
    inv = ROPE_BASE ** (-jnp.arange(half, dtype=F32) / half)
    ang = jnp.arange(s, dtype=F32)[:, None] * inv[None, :]
    cos, sin = jnp.cos(ang), jnp.sin(ang)
    cos_t = jnp.concatenate([cos, cos], axis=-1)
    sin_t = jnp.concatenate([-sin, sin], axis=-1)
    c = RET_CHUNK
    log_g = jnp.log(1.0 - jnp.exp2(-5.0 - jnp.arange(RET_HEADS, dtype=F32)))
    n = jnp.arange(c, dtype=F32)
    diff = n[:, None] - n[None, :]
    decay = jnp.where(diff >= 0, jnp.exp(log_g[:, None, None] * jnp.maximum(diff, 0.0)), 0.0)
    xi = jnp.exp(log_g[:, None] * (n + 1.0))
    zeta = jnp.exp(log_g[:, None] * (c - 1.0 - n))
    g_chunk = jnp.exp(log_g * c)
    xi_t = jnp.broadcast_to(xi[:, :, None], (RET_HEADS, c, RET_DK))
    zeta_t = jnp.broadcast_to(zeta[:, :, None], (RET_HEADS, c, RET_DK))
    return cos_t, sin_t, decay, xi_t, zeta_t, g_chunk


def _even_layer(h, memkv, layer, g, w_in, pool_w, pool_scale, w_out, ts=4 * EV_SUB):
    b, s, d = h.shape
    assert w_in.shape == (d, EV_COLS) and w_out.shape == (EV_YCOLS, d)
    win = w_in.astype(BF16)
    wout = w_out.astype(BF16)
    wbd = jnp.zeros((POOL_WIDTH, POOL_WIDTH), F32)
    for gi in range(len(POOL_WINDOWS)):
        sl = slice(gi * POOL_GROUP, (gi + 1) * POOL_GROUP)
        wbd = wbd.at[sl, sl].set(pool_w[gi])
    wbd = wbd.astype(BF16)
    cos_t, sin_t, decay, xi_t, zeta_t, g_chunk = _retention_tables(s)
    kv_blk = 2 * layer
    return pl.pallas_call(
        _even_kernel,
        out_shape=jax.ShapeDtypeStruct((b, s, d), F32),
        grid=(b, s // ts),
        in_specs=[pl.BlockSpec(memory_space=pltpu.SMEM),
                  pl.BlockSpec((1, ts, d), lambda i, j: (i, j, 0)),
                  _const_spec((1, d)),
                  _const_spec((d, EV_COLS)),
                  _const_spec((POOL_WIDTH, POOL_WIDTH)),
                  _const_spec((1, POOL_WIDTH)),
                  pl.BlockSpec((ts, RET_DK), lambda i, j: (j, 0)),
                  pl.BlockSpec((ts, RET_DK), lambda i, j: (j, 0)),
                  _const_spec((RET_HEADS, RET_CHUNK, RET_CHUNK)),
                  _const_spec((RET_HEADS, RET_CHUNK, RET_DK)),
                  _const_spec((RET_HEADS, RET_CHUNK, RET_DK)),
                  pl.BlockSpec((1, N_MEM, MEM_WIDTH), lambda i, j: (i, 0, kv_blk)),
                  pl.BlockSpec((1, N_MEM, MEM_WIDTH), lambda i, j: (i, 0, kv_blk + 1)),
                  _const_spec((EV_YCOLS, d))],
        out_specs=pl.BlockSpec((1, ts, d), lambda i, j: (i, j, 0)),
        scratch_shapes=[pltpu.VMEM((POOL_HALO + EV_SUB, POOL_WIDTH), F32),
                        pltpu.VMEM((RET_HEADS, RET_DK, RET_DV), F32),
                        pltpu.VMEM((ts, EV_YCOLS), BF16)],
        compiler_params=_params(("arbitrary", "arbitrary")),
        name="even_layer",
    )(g_chunk, h, g.reshape(1, d), win, wbd, pool_scale.reshape(1, POOL_WIDTH), cos_t, sin_t, decay, xi_t, zeta_t,
      memkv, memkv, wout)


KV_W = NSA_KV * NSA_DH
OD_Q = 0
OD_KV = OD_Q + NSA_HEADS * NSA_DH
OD_A = OD_KV + 6 * KV_W
PIECE_KC, PIECE_VC, PIECE_KS, PIECE_VS, PIECE_KW, PIECE_VW = range(6)
OD_GL = OD_A
OD_XQ = OD_GL + NSA_KV * LANES
OD_GN = OD_XQ + MEM_WIDTH
OD_GM = OD_GN + NSA_HEADS * NSA_DH
OD_COLS = OD_GM + MEM_WIDTH


def _odd_in_kernel(h_ref, g_ref, wraw_ref, wb_ref, pe_ref, perm_ref, mk_ref, mv_ref,
                   q_ref, kaug_ref, vst_ref, kw_ref, vwt_ref, kca_ref, kcb_ref, vca_ref, vcb_ref, gatest_ref,
                   sg_ref, ym_ref, wa_ref):
    @pl.when((pl.program_id(0) == 0) & (pl.program_id(1) == 0))
    def _():
        for c0 in range(0, OD_A, 2 * LANES):
            wa_ref[:, c0:c0 + 2 * LANES] = wraw_ref[c0:c0 + 2 * LANES, :].T.astype(BF16)

    si = pl.program_id(1)
    n_sub = h_ref.shape[1] // NSA_TQ
    for sub in range(n_sub):
        _odd_in_subtile(si * n_sub + sub, sub, h_ref, g_ref, wa_ref, wb_ref, pe_ref, perm_ref, mk_ref, mv_ref,
                        q_ref, kaug_ref, vst_ref, kw_ref, vwt_ref, kca_ref, kcb_ref, vca_ref, vcb_ref, gatest_ref,
                        sg_ref, ym_ref)


def _odd_in_subtile(tile_idx, sub, h_ref, g_ref, wa_ref, wb_ref, pe_ref, perm_ref, mk_ref, mv_ref,
                    q_ref, kaug_ref, vst_ref, kw_ref, vwt_ref, kca_ref, kcb_ref, vca_ref, vcb_ref, gatest_ref,
                    sg_ref, ym_ref):
    ts = NSA_TQ
    tile = slice(sub * ts, (sub + 1) * ts)
    u = _rms(h_ref[0, tile], g_ref[...]).astype(BF16)

    def proj(start, width):
        if start < OD_A:
            return _mm(u, wa_ref[:, start:start + width])
        return _mm(u, wb_ref[:, start - OD_A:start - OD_A + width])

    zq = proj(OD_Q, NSA_HEADS * NSA_DH) * (NSA_DH ** -0.5 * LOG2E)
    zg = _silu(proj(OD_GN, NSA_HEADS * NSA_DH))
    for g in range(NSA_KV):
        for hh in range(NSA_HPG):
            sl = slice((g * NSA_HPG + hh) * NSA_DH, (g * NSA_HPG + hh + 1) * NSA_DH)
            q_ref[0, g, hh, tile] = zq[:, sl].astype(BF16)
            sg_ref[0, g, hh, tile] = zg[:, sl]

    zkv = proj(OD_KV, 6 * KV_W)
    lane = lax.broadcasted_iota(jnp.int32, (ts, LANES), 1)
    blk = (tile_idx * ts + lax.broadcasted_iota(jnp.int32, (ts, LANES), 0)) // SLC_BLOCK
    onehot = jnp.where(lane == blk, 1.0, 0.0).astype(BF16)
    ones_rows = jnp.where(lax.broadcasted_iota(jnp.int32, (V_ROWS - NSA_DH, ts), 0) == 0, 1.0, 0.0).astype(BF16)
    zgl = jax.nn.sigmoid(proj(OD_GL, NSA_KV * LANES))
    flat_in = []
    for g in range(NSA_KV):
        def piece(idx):
            off = idx * KV_W + g * NSA_DH
            return zkv[:, off:off + NSA_DH]
        kaug_ref[0, g, tile, 0:NSA_DH] = piece(PIECE_KS).astype(BF16)
        kaug_ref[0, g, tile, NSA_DH:2 * NSA_DH] = onehot
        vst_ref[0, g, sub, 0:NSA_DH] = piece(PIECE_VS).T.astype(BF16)
        vst_ref[0, g, sub, NSA_DH:V_ROWS] = ones_rows
        kw_ref[0, g, tile] = piece(PIECE_KW).astype(BF16)
        vwt_ref[0, g, sub, 0:NSA_DH] = piece(PIECE_VW).T.astype(BF16)
        vwt_ref[0, g, sub, NSA_DH:V_ROWS] = ones_rows
        kc, vc = piece(PIECE_KC), piece(PIECE_VC)
        flat_in += [(kc + pe_ref[0, 0]).astype(BF16), (kc + pe_ref[0, 1]).astype(BF16),
                    (vc + pe_ref[1, 0]).astype(BF16), (vc + pe_ref[1, 1]).astype(BF16)]
        gatest_ref[0, g, :, tile] = zgl[:, g * LANES:(g + 1) * LANES].T

    nj = ts // CMP_STRIDE
    perm = _mm(perm_ref[...], jnp.concatenate(flat_in, axis=1)).astype(BF16)
    for c, out_ref in enumerate((kca_ref, kcb_ref, vca_ref, vcb_ref) * NSA_KV):
        for l in range(CMP_STRIDE):
            out_ref[0, c // 4, sub * nj:(sub + 1) * nj, l * NSA_DH:(l + 1) * NSA_DH] = (
                perm[l * nj:(l + 1) * nj, c * NSA_DH:(c + 1) * NSA_DH])

    xq = proj(OD_XQ, MEM_WIDTH)
    gm = _silu(proj(OD_GM, MEM_WIDTH))
    for hd, om in enumerate(_mem_attention(xq, mk_ref[0], mv_ref[0])):
        sl = slice(hd * MEM_DH, (hd + 1) * MEM_DH)
        ym_ref[0, tile, sl] = (om * gm[:, sl]).astype(BF16)


def _odd_in(h, memkv, layer, g, w_in, cmp_pe, ts=2 * NSA_TQ):
    b, s, d = h.shape
    assert w_in.shape[1] == OD_A + 3 * NSA_HEADS + MEM_WIDTH + D_INNER
    w_t = jnp.swapaxes(w_in, 0, 1)
    gl, xq, gate = jnp.split(w_t[OD_A:], [3 * NSA_HEADS, 3 * NSA_HEADS + MEM_WIDTH], axis=0)
    gl = gl.reshape(3, NSA_KV, NSA_HPG, d).transpose(1, 0, 2, 3).reshape(NSA_KV, 3 * NSA_HPG, d)
    gl = jnp.pad(gl, ((0, 0), (0, LANES - 3 * NSA_HPG), (0, 0))).reshape(NSA_KV * LANES, d)
    wb = jnp.concatenate([gl, xq, gate], axis=0).T.astype(BF16)
    sub = NSA_TQ
    reps = sub // CMP_STRIDE
    pe = jnp.stack([jnp.stack([jnp.tile(cmp_pe[kv, :CMP_STRIDE], (reps, 1)),
                               jnp.tile(cmp_pe[kv, CMP_STRIDE:], (reps, 1))]) for kv in range(2)])
    kv_blk = 2 * layer
    assert ts % sub == 0
    nj = sub // CMP_STRIDE
    perm = np.zeros((sub, sub), np.float32)
    perm[np.arange(sub), (np.arange(sub) % nj) * CMP_STRIDE + np.arange(sub) // nj] = 1.0
    head_t = jax.ShapeDtypeStruct((b, NSA_KV, s, NSA_DH), BF16)
    head_spec = pl.BlockSpec((1, NSA_KV, ts, NSA_DH), lambda i, j: (i, 0, j, 0))
    headt_t = jax.ShapeDtypeStruct((b, NSA_KV, s // sub, V_ROWS, sub), BF16)
    headt_spec = pl.BlockSpec((1, NSA_KV, ts // sub, V_ROWS, sub), lambda i, j: (i, 0, j, 0, 0))
    flat_t = jax.ShapeDtypeStruct((b, NSA_KV, s // CMP_STRIDE, CMP_STRIDE * NSA_DH), BF16)
    flat_spec = pl.BlockSpec((1, NSA_KV, ts // CMP_STRIDE, CMP_STRIDE * NSA_DH), lambda i, j: (i, 0, j, 0))
    qlike_spec = pl.BlockSpec((1, NSA_KV, NSA_HPG, ts, NSA_DH), lambda i, j: (i, 0, 0, j, 0))
    return pl.pallas_call(
        _odd_in_kernel,
        out_shape=[jax.ShapeDtypeStruct((b, NSA_KV, NSA_HPG, s, NSA_DH), BF16),
                   jax.ShapeDtypeStruct((b, NSA_KV, s, 2 * NSA_DH), BF16),
                   headt_t, head_t, headt_t, flat_t, flat_t, flat_t, flat_t,
                   jax.ShapeDtypeStruct((b, NSA_KV, LANES, s), F32),
                   jax.ShapeDtypeStruct((b, NSA_KV, NSA_HPG, s, NSA_DH), F32),
                   jax.ShapeDtypeStruct((b, s, MEM_WIDTH), BF16)],
        grid=(b, s // ts),
        in_specs=[pl.BlockSpec((1, ts, d), lambda i, j: (i, j, 0)),
                  _const_spec((1, d)),
                  _const_spec((OD_A, d)),
                  _const_spec((d, OD_COLS - OD_A)),
                  _const_spec((2, 2, sub, NSA_DH)),
                  _const_spec((sub, sub)),
                  pl.BlockSpec((1, N_MEM, MEM_WIDTH), lambda i, j: (i, 0, kv_blk)),
                  pl.BlockSpec((1, N_MEM, MEM_WIDTH), lambda i, j: (i, 0, kv_blk + 1))],
        out_specs=[qlike_spec,
                   pl.BlockSpec((1, NSA_KV, ts, 2 * NSA_DH), lambda i, j: (i, 0, j, 0)),
                   headt_spec, head_spec, headt_spec, flat_spec, flat_spec, flat_spec, flat_spec,
                   pl.BlockSpec((1, NSA_KV, LANES, ts), lambda i, j: (i, 0, 0, j)),
                   qlike_spec,
                   pl.BlockSpec((1, ts, MEM_WIDTH), lambda i, j: (i, j, 0))],
        scratch_shapes=[pltpu.VMEM((d, OD_A), BF16)],
        compiler_params=_params(("arbitrary", "arbitrary")),
        name="odd_in_proj",
    )(h, g.reshape(1, d), w_t, wb, pe, jnp.asarray(perm, BF16), memkv, memkv)


def _compress_kernel(xka_ref, xkb_ref, xva_ref, xvb_ref, w1a_ref, w1b_ref, b1_ref, w2_ref, ovt_ref, kc_ref, vct_ref):
    n = xka_ref.shape[2]
    k_cols = vct_ref.shape[3]

    def block_mlp(kv, xa, xb):
        first = _mm(xa[0, 0], w1a_ref[kv])
        second = _mm(xb[0, 0], w1b_ref[kv])
        hid = first + pltpu.roll(second, n - 1, 0) + b1_ref[kv]
        return _mm(_silu(hid).astype(BF16), w2_ref[kv])

    kc_ref[0, 0, 0:CMP_PAD] = jnp.zeros((CMP_PAD, NSA_DH), BF16)
    kc_ref[0, 0, CMP_PAD:] = block_mlp(0, xka_ref, xkb_ref).astype(BF16)
    vc = jnp.concatenate([jnp.zeros((CMP_PAD, NSA_DH), F32), block_mlp(1, xva_ref, xvb_ref),
                          jnp.zeros((k_cols - CMP_PAD - n, NSA_DH), F32)], axis=0)
    vct_ref[0, 0, 0:NSA_DH] = vc.T.astype(BF16)
    vct_ref[0, 0, NSA_DH:] = ovt_ref[...]


def _compress(kca, kcb, vca, vcb, w1, b1, w2):
    b, g, n, half = kca.shape
    dh = half // CMP_STRIDE
    s = n * CMP_STRIDE
    xs = (kca, kcb, vca, vcb)
    w1 = w1.astype(BF16)
    ovt = _overlap_table(s)
    k_cols = ovt.shape[1]
    x_spec = pl.BlockSpec((1, 1, n, half), lambda i, j: (i, j, 0, 0))
    return pl.pallas_call(
        _compress_kernel,
        out_shape=[jax.ShapeDtypeStruct((b, g, n + CMP_PAD, dh), BF16),
                   jax.ShapeDtypeStruct((b, g, dh + LANES, k_cols), BF16)],
        grid=(b, g),
        in_specs=[x_spec, x_spec, x_spec, x_spec,
                  _const_spec((2, half, CMP_HIDDEN)), _const_spec((2, half, CMP_HIDDEN)),
                  _const_spec((2, 1, CMP_HIDDEN)), _const_spec((2, CMP_HIDDEN, dh)),
                  _const_spec((LANES, k_cols))],
        out_specs=[pl.BlockSpec((1, 1, n + CMP_PAD, dh), lambda i, j: (i, j, 0, 0)),
                   pl.BlockSpec((1, 1, dh + LANES, k_cols), lambda i, j: (i, j, 0, 0))],
        compiler_params=_params(("arbitrary", "arbitrary")),
        name="compress",
    )(*xs, w1[:, :half], w1[:, half:], b1.reshape(2, 1, CMP_HIDDEN), w2.astype(BF16), ovt)


def _nsa_kernel(q_ref, kaug_ref, vst_ref, kw_ref, vwt_ref, kcmp_ref, vcmpt_ref, gatest_ref, sg_ref,
                d0_ref, d1_ref, dw_ref, cmpq_ref, y_ref, qaug_ref, m_ref, acc_ref, s_ref, mw_ref, accw_ref, sw_ref,
                mix_ref, tmax_ref, tmaxw_ref):
    i = pl.program_id(2)
    tq = NSA_TQ
    rows = NSA_HPG * tq
    s_len = kw_ref.shape[2]
    n_cmp = s_len // CMP_STRIDE
    n_slc = s_len // SLC_BLOCK
    t0 = i * tq
    q = q_ref[0, 0].reshape(rows, NSA_DH)


    def stream(mx_ref, ac_ref, sc2_ref, tmax_ref):
        def reset():
            mx_ref[...] = jnp.full(mx_ref.shape, LOWEST, F32)
            ac_ref[...] = jnp.zeros(ac_ref.shape, F32)

        def s_to(slot, k_ref, query, tile, bias=None):
            k0 = pl.multiple_of(tile * tq, tq)
            st = _mm_nt(k_ref[0, 0, pl.ds(k0, tq)], query)
            if bias is not None:
                st = st + bias
            sc2_ref[slot] = st
            tmax_ref[slot] = jnp.max(st, axis=0, keepdims=True)

        def pv_from(slot, vt):
            st = sc2_ref[slot]
            m_prev = mx_ref[...]
            m_new = jnp.maximum(m_prev, tmax_ref[slot])
            pt = jnp.exp2(st - m_new).astype(BF16)
            ac_ref[...] = jnp.exp2(m_prev - m_new) * ac_ref[...] + _mm(vt, pt)
            mx_ref[...] = m_new

        def finish(cols=slice(None)):
            return ac_ref[0:NSA_DH, cols] / ac_ref[NSA_DH:NSA_DH + 1, cols]

        return reset, s_to, pv_from, finish

    def off_unless(cond):
        return jnp.where(cond, 0.0, NEG)

    prev1 = jnp.maximum(i - 1, 0)
    prev2 = jnp.maximum(i - 2, 0)
    w_reset, w_s_to, w_pv_from, w_finish = stream(mw_ref, accw_ref, sw_ref, tmaxw_ref)


    n_pad = kcmp_ref.shape[2]
    j_near = pl.multiple_of(i * (tq // CMP_STRIDE), tq // CMP_STRIDE)
    qaug_ref[:, 0:NSA_DH] = q
    qaug_ref[:, NSA_DH:] = cmpq_ref[0]
    krow = lax.broadcasted_iota(jnp.int32, (n_pad, NSA_DH), 0)
    klane = lax.broadcasted_iota(jnp.int32, (n_pad, NSA_DH), 1)
    slot = krow - j_near
    in_window = (klane < 2 * CMP_NEAR) & (jnp.bitwise_and(klane, CMP_NEAR - 1) == slot)
    visible = (krow >= CMP_PAD) & (slot < CMP_NEAR)
    key_cols = jnp.where(klane == 2 * CMP_NEAR, jnp.where(visible, 0.0, NEG), jnp.where(in_window, 1.0, 0.0))
    s_cmp = _mm_nt(jnp.concatenate([kcmp_ref[0, 0], key_cols.astype(BF16)], axis=1), qaug_ref[...])
    w_reset()
    w_s_to(0, kw_ref, q, prev2, dw_ref[0, 0])
    w_s_to(1, kw_ref, q, prev1, d1_ref[0, 0])
    p_cmp = jnp.exp2(s_cmp - jnp.max(s_cmp, axis=0, keepdims=True))
    tcol = t0 + jnp.bitwise_and(lax.broadcasted_iota(jnp.int32, (1, rows), 1), tq - 1)
    scale = jnp.where(tcol >= CMP_BLOCK - 1, 1.0, 0.0) / jnp.sum(p_cmp, axis=0, keepdims=True)
    p_cmp = p_cmp.astype(BF16)
    k_cols = vcmpt_ref.shape[3]
    if k_cols > n_pad:
        p_cmp = jnp.concatenate([p_cmp, jnp.zeros((k_cols - n_pad, rows), BF16)], axis=0)
    w_pv_from(0, vwt_ref[0, 0, prev2])
    both = _mm(vcmpt_ref[0, 0], p_cmp) * scale
    w_s_to(0, kw_ref, q, i, d0_ref[0, 0])
    w_pv_from(1, vwt_ref[0, 0, prev1])
    gtst = gatest_ref[0, 0]
    n_blk = -(-n_slc // 8) * 8
    imp = both[NSA_DH:NSA_DH + n_blk, 0:tq]
    for hh in range(NSA_HPG):
        hs = slice(hh * tq, (hh + 1) * tq)
        mix_ref[:, hs] = gtst[hh:hh + 1] * both[0:NSA_DH, hs]
        if hh:
            imp = imp + both[NSA_DH:NSA_DH + n_blk, hs]

    blk = lax.broadcasted_iota(jnp.int32, (n_blk, tq), 0)
    tq_pos = t0 + lax.broadcasted_iota(jnp.int32, (n_blk, tq), 1)
    cur = tq_pos // SLC_BLOCK
    forced = (blk == 0) | (blk == cur) | (blk == cur - 1)
    future = blk * SLC_BLOCK > tq_pos
    imp = jnp.where(forced, BIG, jnp.where(future, -BIG, imp))
    imp = jnp.where(blk < n_slc, imp, LOWEST)
    blk_f = blk.astype(F32)
    sel = jnp.zeros((n_blk, tq), jnp.bool_)
    for _ in range(SLC_TOPK):
        top = jnp.max(imp, axis=0, keepdims=True)
        first = jnp.min(jnp.where(imp == top, blk_f, float(LANES)), axis=0, keepdims=True)
        hit = blk_f == first
        sel = sel | hit
        imp = jnp.where(hit, KNOCKED, imp)
    sneg_t = jnp.where(sel, 0.0, NEG)
    if n_blk < LANES:
        sneg_t = jnp.concatenate([sneg_t, jnp.zeros((LANES - n_blk, tq), F32)], axis=0)
    sneg = sneg_t.T.astype(BF16)
    for hh in range(NSA_HPG):
        qaug_ref[hh * tq:(hh + 1) * tq, NSA_DH:2 * NSA_DH] = sneg
    w_pv_from(0, vwt_ref[0, 0, i])
    for hh in range(NSA_HPG):
        hs = slice(hh * tq, (hh + 1) * tq)
        mix_ref[:, hs] = mix_ref[:, hs] + gtst[2 * NSA_HPG + hh:2 * NSA_HPG + hh + 1] * w_finish(hs)

    reset, s_to, pv_from, finish = stream(m_ref, acc_ref, s_ref, tmax_ref)
    reset()
    n_far = prev1
    n_pairs = jnp.maximum(n_far - 1, 0) // 2
    s_to(0, kaug_ref, qaug_ref[...], 0, off_unless(n_far >= 1))

    def far_pair(t2):
        s_to(1, kaug_ref, qaug_ref[...], 2 * t2 + 1)
        pv_from(0, vst_ref[0, 0, 2 * t2])
        s_to(0, kaug_ref, qaug_ref[...], 2 * t2 + 2)
        pv_from(1, vst_ref[0, 0, 2 * t2 + 1])

    def far_quad(t4, carry):
        far_pair(2 * t4)
        far_pair(2 * t4 + 1)
        return carry

    lax.fori_loop(0, n_pairs // 2, far_quad, 0)

    @pl.when(n_pairs % 2 == 1)
    def _():
        far_pair(n_pairs - 1)
    c0 = 2 * n_pairs
    two_left = n_far - c0 == 2

    @pl.when(two_left)
    def _():
        s_to(1, kaug_ref, qaug_ref[...], c0 + 1)
        pv_from(0, vst_ref[0, 0, c0])
        s_to(0, kaug_ref, qaug_ref[...], prev1, d1_ref[0, 0])
        pv_from(1, vst_ref[0, 0, c0 + 1])
        s_to(1, kaug_ref, qaug_ref[...], i, d0_ref[0, 0])
        pv_from(0, vst_ref[0, 0, prev1])
        pv_from(1, vst_ref[0, 0, i])

    @pl.when(jnp.logical_not(two_left))
    def _():
        s_to(1, kaug_ref, qaug_ref[...], prev1, d1_ref[0, 0])
        pv_from(0, vst_ref[0, 0, c0])
        s_to(0, kaug_ref, qaug_ref[...], i, d0_ref[0, 0])
        pv_from(1, vst_ref[0, 0, prev1])
        pv_from(0, vst_ref[0, 0, i])

    for hh in range(NSA_HPG):
        hs = slice(hh * tq, (hh + 1) * tq)
        ot = mix_ref[:, hs] + gatest_ref[0, 0, NSA_HPG + hh:NSA_HPG + hh + 1] * finish(hs)
        y_ref[0, :, hh * NSA_DH:(hh + 1) * NSA_DH] = (ot.T * sg_ref[0, 0, hh]).astype(BF16)


def _overlap_table(s):
    n_cmp = (s - CMP_BLOCK) // CMP_STRIDE + 1
    n_slc = s // SLC_BLOCK
    cst = np.arange(n_cmp)[:, None] * CMP_STRIDE
    sst = np.arange(n_slc)[None, :] * SLC_BLOCK
    ov = np.clip(np.minimum(cst + CMP_BLOCK, sst + SLC_BLOCK) - np.maximum(cst, sst), 0, None) / CMP_STRIDE
    k_cols = -(-(CMP_PAD + s // CMP_STRIDE) // LANES) * LANES
    full = np.zeros((LANES, k_cols), np.float32)
    full[:n_slc, CMP_PAD:CMP_PAD + n_cmp] = ov.T
    return jnp.asarray(full, BF16)


def _nsa(q, kaug, vst, kw, vwt, kcmp, vcmpt, gatest, sg, bias, cmpq):
    b, g, hpg, s, dh = q.shape
    tq = NSA_TQ
    assert WINDOW == 2 * tq and s % tq == 0 and s // SLC_BLOCK <= LANES and s // SLC_BLOCK >= SLC_TOPK
    n_pad = kcmp.shape[2]
    rows = hpg * tq
    seq_spec = pl.BlockSpec((1, 1, s, dh), lambda i, j, k: (i, j, 0, 0))
    seqt_spec = pl.BlockSpec((1, 1, s // tq, V_ROWS, tq), lambda i, j, k: (i, j, 0, 0, 0))
    qlike_spec = pl.BlockSpec((1, 1, hpg, tq, dh), lambda i, j, k: (i, j, 0, k, 0))

    def bias_spec(idx, needs_tiles_before=0):
        return pl.BlockSpec((1, 1, tq, rows),
                            lambda i, j, k: (j, jnp.where(k >= needs_tiles_before, idx, BIAS_MASKED), 0, 0))

    return pl.pallas_call(
        _nsa_kernel,
        out_shape=jax.ShapeDtypeStruct((b, s, g * hpg * dh), BF16),
        grid=(b, g, s // tq),
        in_specs=[qlike_spec,
                  pl.BlockSpec((1, 1, s, 2 * dh), lambda i, j, k: (i, j, 0, 0)),
                  seqt_spec, seq_spec, seqt_spec,
                  pl.BlockSpec((1, 1, n_pad, dh), lambda i, j, k: (i, j, 0, 0)),
                  pl.BlockSpec((1, 1, dh + LANES, vcmpt.shape[3]), lambda i, j, k: (i, j, 0, 0)),
                  pl.BlockSpec((1, 1, LANES, tq), lambda i, j, k: (i, j, 0, k)),
                  qlike_spec,
                  bias_spec(0), bias_spec(1, 1), bias_spec(2, 2),
                  pl.BlockSpec((1, rows, LANES), lambda i, j, k: (j, 0, 0))],
        out_specs=pl.BlockSpec((1, tq, hpg * dh), lambda i, j, k: (i, k, j)),
        scratch_shapes=[pltpu.VMEM((rows, 2 * dh), BF16),
                        pltpu.VMEM((1, rows), F32),
                        pltpu.VMEM((V_ROWS, rows), F32),
                        pltpu.VMEM((2, tq, rows), F32),
                        pltpu.VMEM((1, rows), F32),
                        pltpu.VMEM((V_ROWS, rows), F32),
                        pltpu.VMEM((2, tq, rows), F32),
                        pltpu.VMEM((dh, rows), F32),
                        pltpu.VMEM((2, 1, rows), F32),
                        pltpu.VMEM((2, 1, rows), F32)],
        compiler_params=_params(("arbitrary", "arbitrary", "arbitrary")),
        name="nsa_attention",
    )(q, kaug, vst, kw, vwt, kcmp, vcmpt, gatest, sg, bias, bias, bias, cmpq)


def _odd_out_kernel(h_ref, yn_ref, ym_ref, wn_ref, wm_ref, fg_ref, o_ref, *, final_norm, sub):
    for r0 in range(0, h_ref.shape[0], sub):
        rows = slice(r0, r0 + sub)
        out = h_ref[rows] + _mm(yn_ref[rows], wn_ref[...]) + _mm(ym_ref[rows], wm_ref[...])
        o_ref[rows] = _rms(out, fg_ref[...]) if final_norm else out


def _odd_out(h, yn, ym, w_out, final_g, final_norm, tm=1024, sub=256):
    b, s, d = h.shape
    t = b * s
    nw = NSA_HEADS * NSA_DH
    wout = w_out.astype(BF16)
    out = pl.pallas_call(
        functools.partial(_odd_out_kernel, final_norm=final_norm, sub=sub),
        out_shape=jax.ShapeDtypeStruct((t, d), F32),
        grid=(t // tm,),
        in_specs=[pl.BlockSpec((tm, d), lambda i: (i, 0)),
                  pl.BlockSpec((tm, nw), lambda i: (i, 0)),
                  pl.BlockSpec((tm, MEM_WIDTH), lambda i: (i, 0)),
                  _const_spec((nw, d)), _const_spec((MEM_WIDTH, d)), _const_spec((1, d))],
        out_specs=pl.BlockSpec((tm, d), lambda i: (i, 0)),
        compiler_params=_params(("arbitrary",)),
        name="odd_out_proj",
    )(h.reshape(t, d), yn.reshape(t, nw), ym.reshape(t, MEM_WIDTH), wout[:nw], wout[nw:], final_g.reshape(1, d))
    return out.reshape(b, s, d)


def _final_norm_kernel(h_ref, g_ref, o_ref):
    o_ref[...] = _rms(h_ref[...], g_ref[...])


def _final_norm(h, final_g, tm=512):
    b, s, d = h.shape
    t = b * s
    out = pl.pallas_call(
        _final_norm_kernel,
        out_shape=jax.ShapeDtypeStruct((t, d), F32),
        grid=(t // tm,),
        in_specs=[pl.BlockSpec((tm, d), lambda i: (i, 0)), _const_spec((1, d))],
        out_specs=pl.BlockSpec((tm, d), lambda i: (i, 0)),
        compiler_params=_params(("arbitrary",)),
        name="final_norm",
    )(h.reshape(t, d), final_g.reshape(1, d))
    return out.reshape(b, s, d)


def kernel(x, mem, norm_g, final_g, mem_norm_g, rel_bias, ev_w_in, ev_pool_w, ev_pool_scale, ev_w_mem_kv, ev_w_out,
           od_w_in, od_cmp_pe, od_cmp_w1, od_cmp_b1, od_cmp_w2, od_w_mem_kv, od_w_out):
    depth = norm_g.shape[0]
    w_mem = [(ev_w_mem_kv if i % 2 == 0 else od_w_mem_kv)[i // 2] for i in range(depth)]
    memkv = _memkv(mem, mem_norm_g, jnp.concatenate(w_mem, axis=1).astype(BF16))
    bias, cmpq = _bias_tiles(rel_bias) if depth > 1 else (None, None)
    h = x
    for i in range(depth):
        j = i // 2
        last = i == depth - 1
        if i % 2 == 0:
            h = _even_layer(h, memkv, i, norm_g[i], ev_w_in[j], ev_pool_w[j], ev_pool_scale[j], ev_w_out[j])
            if last:
                h = _final_norm(h, final_g)
        else:
            (q, kaug, vst, kw, vwt, kca, kcb, vca, vcb, gatest, sg, ym) = _odd_in(
                h, memkv, i, norm_g[i], od_w_in[j], od_cmp_pe[j])
            kcmp, vcmp = _compress(kca, kcb, vca, vcb, od_cmp_w1[j], od_cmp_b1[j], od_cmp_w2[j])
            yn = _nsa(q, kaug, vst, kw, vwt, kcmp, vcmp, gatest, sg, bias, cmpq)
            h = _odd_out(h, yn, ym, od_w_out[j], final_g, last)
    return h
```

```python
import functools
import math

import numpy as np
import jax
import jax.numpy as jnp
from jax import lax
from jax.experimental import pallas as pl
from jax.experimental.pallas import tpu as pltpu

F32 = jnp.float32
BF16 = jnp.bfloat16

D_MODEL = 1024
D_INNER = 2048
N_MEM = 256
EPS = 1e-6
NEG = -1e30
BIG = 1e30

POOL_WINDOWS = (2, 4, 8, 16)
POOL_WIDTH = 768
POOL_GROUP = 192
POOL_HALO = 16

RET_HEADS = 4
RET_DK = 128
RET_DV = 192
RET_CHUNK = 128
ROPE_BASE = 10000.0

MEM_HEADS = 4
MEM_DH = 128
MEM_WIDTH = 512

NSA_HEADS = 12
NSA_KV = 2
NSA_HPG = 6
NSA_DH = 128
CMP_BLOCK = 32
CMP_STRIDE = 16
CMP_HIDDEN = 256
SLC_BLOCK = 64
SLC_TOPK = 8
WINDOW = 512
REL_BUCKETS = 32
REL_MAX_DIST = 128

LANES = 128
NSA_TQ = 256
CMP_NEAR = 32
CMP_PAD = 16
LOG2E = math.log2(math.e)
V_ROWS = NSA_DH + 16
LOWEST = -3.0e38
KNOCKED = -3.3e38
BIAS_MASKED = 3

VMEM_LIMIT = 56 * 1024 * 1024


def _mm(a, b):
    return jnp.dot(a, b, preferred_element_type=F32)


def _mm_nt(a, b):
    return lax.dot_general(a, b, (((1,), (1,)), ((), ())), preferred_element_type=F32)


def _mm_tn(a, b):
    return lax.dot_general(a, b, (((0,), (0,)), ((), ())), preferred_element_type=F32)


def _rms(x, g):
    return x * lax.rsqrt(jnp.mean(x * x, axis=-1, keepdims=True) + EPS) * g


def _silu(x):
    return x * jax.nn.sigmoid(x)


def _const_spec(shape):
    nd = len(shape)
    return pl.BlockSpec(shape, lambda *_: (0,) * nd, pipeline_mode=pl.Buffered(1))


def _params(sem):
    return pltpu.CompilerParams(dimension_semantics=sem, vmem_limit_bytes=VMEM_LIMIT)


def _bucket_starts():
    n = np.arange(REL_MAX_DIST + 1)
    max_exact = REL_BUCKETS // 2
    nf = np.maximum(n, 1).astype(np.float32)
    large = max_exact + (np.log(nf / np.float32(max_exact)) / np.float32(math.log(REL_MAX_DIST / max_exact))
                         * np.float32(REL_BUCKETS - max_exact)).astype(np.int32)
    bucket = np.where(n < max_exact, n, np.minimum(large, REL_BUCKETS - 1))
    assert np.all(np.diff(bucket) >= 0) and bucket[-1] == REL_BUCKETS - 1
    return tuple(int(np.argmax(bucket >= b)) for b in range(REL_BUCKETS))


def _bias_kernel(tab_ref, rel_ref, relq_ref, out_ref, cmpq_ref, *, boxes, starts):
    h = pl.program_id(0)

    def lookup(rel):
        far = tab_ref[REL_BUCKETS - 1, h]
        val = jnp.full(rel.shape, tab_ref[0, h] - far, F32)
        for b in range(1, REL_BUCKETS - 1):
            val = jnp.where(rel >= starts[b], tab_ref[b, h] - far, val)
        val = jnp.where(rel >= starts[REL_BUCKETS - 1], 0.0, val)
        return jnp.where(rel < 0, NEG, val * LOG2E)

    for kind, box in enumerate(boxes):
        out_ref[0, kind] = jnp.where(rel_ref[kind] < 0, NEG, 0.0)
        if box is not None:
            r0, r1, c0, c1 = box
            out_ref[0, kind, r0:r1, c0:c1] = lookup(rel_ref[kind, r0:r1, c0:c1])

    relq = relq_ref[...]
    lane = lax.broadcasted_iota(jnp.int32, relq.shape, 1)
    bias = jnp.where(lane < CMP_NEAR, lookup(relq), 0.0)
    hi = bias.astype(BF16)
    lo = (bias - hi.astype(F32)).astype(BF16).astype(F32)
    one = jnp.where(lane == 2 * CMP_NEAR, 1.0, 0.0)
    cmpq_ref[0] = (hi.astype(F32) + pltpu.roll(lo, CMP_NEAR, 1) + one).astype(BF16)


def _bias_tiles(rel_bias):
    tq = NSA_TQ
    r = np.arange(tq)[:, None]
    c = np.arange(tq)[None, :]
    d0 = r - c
    d1 = tq + r - c
    dw = np.where(c > r, WINDOW + r - c, -1)
    gc = np.where(c < CMP_NEAR, r - CMP_STRIDE * c + (CMP_STRIDE * CMP_PAD - (CMP_BLOCK - 1)), -1)
    rel_np = np.stack([d0.T, d1.T, dw.T, np.full((tq, tq), -1)]).astype(np.int32)
    boxes = []
    for tile in rel_np:
        rr, cc = np.nonzero((tile >= 0) & (tile < REL_MAX_DIST))
        boxes.append(None if rr.size == 0 else tuple(int(v) for v in (
            rr.min() // 8 * 8, -(-(rr.max() + 1) // 8) * 8, cc.min() // LANES * LANES, -(-(cc.max() + 1) // LANES) * LANES)))
    rel = jnp.asarray(rel_np)
    nt = rel.shape[0]
    return pl.pallas_call(
        functools.partial(_bias_kernel, boxes=tuple(boxes), starts=_bucket_starts()),
        out_shape=[jax.ShapeDtypeStruct((NSA_KV, nt, tq, NSA_HPG * tq), F32),
                   jax.ShapeDtypeStruct((NSA_KV, NSA_HPG * tq, LANES), BF16)],
        grid=(NSA_HEADS,),
        in_specs=[pl.BlockSpec(memory_space=pltpu.SMEM),
                  _const_spec((nt, tq, tq)),
                  _const_spec((tq, LANES))],
        out_specs=[pl.BlockSpec((1, nt, tq, tq), lambda h: (h // NSA_HPG, 0, 0, h % NSA_HPG)),
                   pl.BlockSpec((1, tq, LANES), lambda h: (h // NSA_HPG, h % NSA_HPG, 0))],
        compiler_params=_params(("arbitrary",)),
        name="bias_tiles",
    )(rel_bias.astype(F32), rel, jnp.asarray(gc[:, :LANES].astype(np.int32)))


def _memkv_kernel(mem_ref, g_ref, w_ref, out_ref):
    y = _rms(mem_ref[0], g_ref[...]).astype(BF16)
    out_ref[0] = _mm(y, w_ref[...]).astype(BF16)


def _memkv(mem, mem_norm_g, w_all):
    b, m, d = mem.shape
    n = w_all.shape[1]
    return pl.pallas_call(
        _memkv_kernel,
        out_shape=jax.ShapeDtypeStruct((b, m, n), BF16),
        grid=(b,),
        in_specs=[pl.BlockSpec((1, m, d), lambda i: (i, 0, 0)),
                  _const_spec((1, d)),
                  _const_spec((d, n))],
        out_specs=pl.BlockSpec((1, m, n), lambda i: (i, 0, 0)),
        compiler_params=_params(("arbitrary",)),
        name="mem_kv",
    )(mem, mem_norm_g.reshape(1, d), w_all)


def _mem_attention(xq, mk, mv):
    outs = []
    for hd in range(MEM_HEADS):
        sl = slice(hd * MEM_DH, (hd + 1) * MEM_DH)
        qm = (xq[:, sl] * (MEM_DH ** -0.5)).astype(BF16)
        s = _mm_nt(qm, mk[:, sl])
        p = jnp.exp(s - jnp.max(s, axis=-1, keepdims=True))
        l = jnp.sum(p, axis=-1, keepdims=True)
        outs.append(_mm(p.astype(BF16), mv[:, sl]) / l)
    return outs


EV_ZA = 0
EV_RQ = EV_ZA + POOL_WIDTH
EV_RK = EV_RQ + RET_HEADS * RET_DK
EV_RV = EV_RK + RET_HEADS * RET_DK
EV_XQ = EV_RV + RET_HEADS * RET_DV
EV_GA = EV_XQ + MEM_WIDTH
EV_GR = EV_GA + POOL_WIDTH
EV_GM = EV_GR + RET_HEADS * RET_DV
EV_COLS = EV_GM + MEM_WIDTH
EV_YA = 0
EV_YR = POOL_WIDTH
EV_YM = EV_YR + RET_HEADS * RET_DV
EV_YCOLS = EV_YM + MEM_WIDTH
EV_SUB = 256


def _even_kernel(gch_ref, h_ref, g_ref, win_ref, wbd_ref, pscale_ref, cos_ref, sin_ref, decay_ref, xi_ref,
                 zeta_ref, mk_ref, mv_ref, wout_ref, o_ref, ext_ref, state_ref, y_ref):
    si = pl.program_id(1)
    ts = h_ref.shape[1]

    @pl.when(si == 0)
    def _():
        ext_ref[0:POOL_HALO, :] = jnp.zeros((POOL_HALO, POOL_WIDTH), F32)
        state_ref[...] = jnp.zeros(state_ref.shape, F32)

    for r0 in range(0, ts, EV_SUB):
        _even_subtile(si * ts + r0, slice(r0, r0 + EV_SUB), gch_ref, h_ref, g_ref, win_ref, wbd_ref, pscale_ref,
                      cos_ref, sin_ref, decay_ref, xi_ref, zeta_ref, mk_ref, mv_ref, wout_ref, o_ref, ext_ref,
                      state_ref, y_ref)


def _even_subtile(t0, tile, gch_ref, h_ref, g_ref, win_ref, wbd_ref, pscale_ref, cos_ref, sin_ref, decay_ref, xi_ref,
                  zeta_ref, mk_ref, mv_ref, wout_ref, o_ref, ext_ref, state_ref, y_ref):
    ts = EV_SUB
    r0 = tile.start
    h = h_ref[0, tile]
    u = _rms(h, g_ref[...]).astype(BF16)

    def proj(start, width):
        return _mm(u, win_ref[:, start:start + width])

    ext_ref[POOL_HALO:, :] = proj(EV_ZA, POOL_WIDTH)
    e = ext_ref[...]
    s2 = e + pltpu.roll(e, 1, 0)
    s4 = s2 + pltpu.roll(s2, 2, 0)
    s8 = s4 + pltpu.roll(s4, 4, 0)
    s16 = s8 + pltpu.roll(s8, 8, 0)
    lane = lax.broadcasted_iota(jnp.int32, e.shape, 1)
    row = lax.broadcasted_iota(jnp.int32, e.shape, 0)
    tpos = t0 + row - POOL_HALO
    g0, g1, g2 = lane < POOL_GROUP, lane < 2 * POOL_GROUP, lane < 3 * POOL_GROUP
    wsum = jnp.where(g0, s2, jnp.where(g1, s4, jnp.where(g2, s8, s16)))
    wlen = jnp.where(g0, POOL_WINDOWS[0], jnp.where(g1, POOL_WINDOWS[1],
                                                    jnp.where(g2, POOL_WINDOWS[2], POOL_WINDOWS[3])))
    cnt = jnp.maximum(jnp.minimum(tpos + 1, wlen), 1).astype(F32)
    pooled = (wsum / cnt - e)[POOL_HALO:]
    ext_ref[0:POOL_HALO, :] = e[ts:ts + POOL_HALO]
    a = _mm(pooled.astype(BF16), wbd_ref[...]) * pscale_ref[...]
    y_ref[tile, EV_YA:EV_YA + POOL_WIDTH] = (a * _silu(proj(EV_GA, POOL_WIDTH))).astype(BF16)

    cos = cos_ref[tile]
    sin = sin_ref[tile]
    zv = proj(EV_RV, RET_HEADS * RET_DV).astype(BF16)
    gate_r = _silu(proj(EV_GR, RET_HEADS * RET_DV))
    q_rot, k_rot = [], []
    for hd in range(RET_HEADS):
        qh = proj(EV_RQ + hd * RET_DK, RET_DK)
        kh = proj(EV_RK + hd * RET_DK, RET_DK)
        q_rot.append((qh * cos + pltpu.roll(qh, RET_DK // 2, 1) * sin) * (RET_DK ** -0.5))
        k_rot.append(kh * cos + pltpu.roll(kh, RET_DK // 2, 1) * sin)
    for c in range(ts // RET_CHUNK):
        rows = slice(c * RET_CHUNK, (c + 1) * RET_CHUNK)
        normed = []
        for hd in range(RET_HEADS):
            qc, kc = q_rot[hd][rows], k_rot[hd][rows]
            vc = zv[rows, hd * RET_DV:(hd + 1) * RET_DV]
            att = _mm_nt(qc.astype(BF16), kc.astype(BF16)) * decay_ref[hd]
            state = state_ref[hd]
            o = _mm(att.astype(BF16), vc) + _mm((qc * xi_ref[hd]).astype(BF16), state.astype(BF16))
            kv = _mm_tn((kc * zeta_ref[hd]).astype(BF16), vc)
            state_ref[hd] = state * gch_ref[hd] + kv
            dlt = o - jnp.mean(o, axis=-1, keepdims=True)
            normed.append(dlt * lax.rsqrt(jnp.mean(dlt * dlt, axis=-1, keepdims=True) + EPS))
        y_ref[r0 + c * RET_CHUNK:r0 + (c + 1) * RET_CHUNK, EV_YR:EV_YM] = (
            jnp.concatenate(normed, axis=-1) * gate_r[rows]).astype(BF16)

    xq = proj(EV_XQ, MEM_WIDTH)
    gm = _silu(proj(EV_GM, MEM_WIDTH))
    for hd, om in enumerate(_mem_attention(xq, mk_ref[0], mv_ref[0])):
        sl = slice(hd * MEM_DH, (hd + 1) * MEM_DH)
        y_ref[tile, EV_YM + hd * MEM_DH:EV_YM + (hd + 1) * MEM_DH] = (om * gm[:, sl]).astype(BF16)

    o_ref[0, tile] = h + _mm(y_ref[tile], wout_ref[...])


def _retention_tables(s):
    half = RET_DK // 2
    inv = ROPE_BASE ** (-jnp.arange(half, dtype=F32) / half)
    ang = jnp.arange(s, dtype=F32)[:, None] * inv[None, :]
    cos, sin = jnp.cos(ang), jnp.sin(ang)
    cos_t = jnp.concatenate([cos, cos], axis=-1)
    sin_t = jnp.concatenate([-sin, sin], axis=-1)
    c = RET_CHUNK
    log_g = jnp.log(1.0 - jnp.exp2(-5.0 - jnp.arange(RET_HEADS, dtype=F32)))
    n = jnp.arange(c, dtype=F32)
    diff = n[:, None] - n[None, :]
    decay = jnp.where(diff >= 0, jnp.exp(log_g[:, None, None] * jnp.maximum(diff, 0.0)), 0.0)
    xi = jnp.exp(log_g[:, None] * (n + 1.0))
    zeta = jnp.exp(log_g[:, None] * (c - 1.0 - n))
    g_chunk = jnp.exp(log_g * c)
    xi_t = jnp.broadcast_to(xi[:, :, None], (RET_HEADS, c, RET_DK))
    zeta_t = jnp.broadcast_to(zeta[:, :, None], (RET_HEADS, c, RET_DK))
    return cos_t, sin_t, decay, xi_t, zeta_t, g_chunk


def _even_layer(h, memkv, layer, g, w_in, pool_w, pool_scale, w_out, ts=4 * EV_SUB):
    b, s, d = h.shape
    assert w_in.shape == (d, EV_COLS) and w_out.shape == (EV_YCOLS, d)
    win = w_in.astype(BF16)
    wout = w_out.astype(BF16)
    wbd = jnp.zeros((POOL_WIDTH, POOL_WIDTH), F32)
    for gi in range(len(POOL_WINDOWS)):
        sl = slice(gi * POOL_GROUP, (gi + 1) * POOL_GROUP)
        wbd = wbd.at[sl, sl].set(pool_w[gi])
    wbd = wbd.astype(BF16)
    cos_t, sin_t, decay, xi_t, zeta_t, g_chunk = _retention_tables(s)
    kv_blk = 2 * layer
    return pl.pallas_call(
        _even_kernel,
        out_shape=jax.ShapeDtypeStruct((b, s, d), F32),
        grid=(b, s // ts),
        in_specs=[pl.BlockSpec(memory_space=pltpu.SMEM),
                  pl.BlockSpec((1, ts, d), lambda i, j: (i, j, 0)),
                  _const_spec((1, d)),
                  _const_spec((d, EV_COLS)),
                  _const_spec((POOL_WIDTH, POOL_WIDTH)),
                  _const_spec((1, POOL_WIDTH)),
                  pl.BlockSpec((ts, RET_DK), lambda i, j: (j, 0)),
                  pl.BlockSpec((ts, RET_DK), lambda i, j: (j, 0)),
                  _const_spec((RET_HEADS, RET_CHUNK, RET_CHUNK)),
                  _const_spec((RET_HEADS, RET_CHUNK, RET_DK)),
                  _const_spec((RET_HEADS, RET_CHUNK, RET_DK)),
                  pl.BlockSpec((1, N_MEM, MEM_WIDTH), lambda i, j: (i, 0, kv_blk)),
                  pl.BlockSpec((1, N_MEM, MEM_WIDTH), lambda i, j: (i, 0, kv_blk + 1)),
                  _const_spec((EV_YCOLS, d))],
        out_specs=pl.BlockSpec((1, ts, d), lambda i, j: (i, j, 0)),
        scratch_shapes=[pltpu.VMEM((POOL_HALO + EV_SUB, POOL_WIDTH), F32),
                        pltpu.VMEM((RET_HEADS, RET_DK, RET_DV), F32),
                        pltpu.VMEM((ts, EV_YCOLS), BF16)],
        compiler_params=_params(("arbitrary", "arbitrary")),
        name="even_layer",
    )(g_chunk, h, g.reshape(1, d), win, wbd, pool_scale.reshape(1, POOL_WIDTH), cos_t, sin_t, decay, xi_t, zeta_t,
      memkv, memkv, wout)


KV_W = NSA_KV * NSA_DH
OD_Q = 0
OD_KV = OD_Q + NSA_HEADS * NSA_DH
OD_A = OD_KV + 6 * KV_W
PIECE_KC, PIECE_VC, PIECE_KS, PIECE_VS, PIECE_KW, PIECE_VW = range(6)
OD_GL = OD_A
OD_XQ = OD_GL + NSA_KV * LANES
OD_GN = OD_XQ + MEM_WIDTH
OD_GM = OD_GN + NSA_HEADS * NSA_DH
OD_COLS = OD_GM + MEM_WIDTH


def _odd_in_kernel(h_ref, g_ref, wraw_ref, wb_ref, pe_ref, perm_ref, mk_ref, mv_ref,
                   q_ref, kaug_ref, vst_ref, kw_ref, vwt_ref, kca_ref, kcb_ref, vca_ref, vcb_ref, gatest_ref,
                   sg_ref, ym_ref, wa_ref):
    @pl.when((pl.program_id(0) == 0) & (pl.program_id(1) == 0))
    def _():
        for c0 in range(0, OD_A, 2 * LANES):
            wa_ref[:, c0:c0 + 2 * LANES] = wraw_ref[c0:c0 + 2 * LANES, :].T.astype(BF16)

    si = pl.program_id(1)
    n_sub = h_ref.shape[1] // NSA_TQ
    for sub in range(n_sub):
        _odd_in_subtile(si * n_sub + sub, sub, h_ref, g_ref, wa_ref, wb_ref, pe_ref, perm_ref, mk_ref, mv_ref,
                        q_ref, kaug_ref, vst_ref, kw_ref, vwt_ref, kca_ref, kcb_ref, vca_ref, vcb_ref, gatest_ref,
                        sg_ref, ym_ref)


def _odd_in_subtile(tile_idx, sub, h_ref, g_ref, wa_ref, wb_ref, pe_ref, perm_ref, mk_ref, mv_ref,
                    q_ref, kaug_ref, vst_ref, kw_ref, vwt_ref, kca_ref, kcb_ref, vca_ref, vcb_ref, gatest_ref,
                    sg_ref, ym_ref):
    ts = NSA_TQ
    tile = slice(sub * ts, (sub + 1) * ts)
    u = _rms(h_ref[0, tile], g_ref[...]).astype(BF16)

    def proj(start, width):
        if start < OD_A:
            return _mm(u, wa_ref[:, start:start + width])
        return _mm(u, wb_ref[:, start - OD_A:start - OD_A + width])

    zq = proj(OD_Q, NSA_HEADS * NSA_DH) * (NSA_DH ** -0.5 * LOG2E)
    zg = _silu(proj(OD_GN, NSA_HEADS * NSA_DH))
    for g in range(NSA_KV):
        for hh in range(NSA_HPG):
            sl = slice((g * NSA_HPG + hh) * NSA_DH, (g * NSA_HPG + hh + 1) * NSA_DH)
            q_ref[0, g, hh, tile] = zq[:, sl].astype(BF16)
            sg_ref[0, g, hh, tile] = zg[:, sl]

    zkv = proj(OD_KV, 6 * KV_W)
    lane = lax.broadcasted_iota(jnp.int32, (ts, LANES), 1)
    blk = (tile_idx * ts + lax.broadcasted_iota(jnp.int32, (ts, LANES), 0)) // SLC_BLOCK
    onehot = jnp.where(lane == blk, 1.0, 0.0).astype(BF16)
    ones_rows = jnp.where(lax.broadcasted_iota(jnp.int32, (V_ROWS - NSA_DH, ts), 0) == 0, 1.0, 0.0).astype(BF16)
    zgl = jax.nn.sigmoid(proj(OD_GL, NSA_KV * LANES))
    flat_in = []
    for g in range(NSA_KV):
        def piece(idx):
            off = idx * KV_W + g * NSA_DH
            return zkv[:, off:off + NSA_DH]
        kaug_ref[0, g, tile, 0:NSA_DH] = piece(PIECE_KS).astype(BF16)
        kaug_ref[0, g, tile, NSA_DH:2 * NSA_DH] = onehot
        vst_ref[0, g, sub, 0:NSA_DH] = piece(PIECE_VS).T.astype(BF16)
        vst_ref[0, g, sub, NSA_DH:V_ROWS] = ones_rows
        kw_ref[0, g, tile] = piece(PIECE_KW).astype(BF16)
        vwt_ref[0, g, sub, 0:NSA_DH] = piece(PIECE_VW).T.astype(BF16)
        vwt_ref[0, g, sub, NSA_DH:V_ROWS] = ones_rows
        kc, vc = piece(PIECE_KC), piece(PIECE_VC)
        flat_in += [(kc + pe_ref[0, 0]).astype(BF16), (kc + pe_ref[0, 1]).astype(BF16),
                    (vc + pe_ref[1, 0]).astype(BF16), (vc + pe_ref[1, 1]).astype(BF16)]
        gatest_ref[0, g, :, tile] = zgl[:, g * LANES:(g + 1) * LANES].T

    nj = ts // CMP_STRIDE
    perm = _mm(perm_ref[...], jnp.concatenate(flat_in, axis=1)).astype(BF16)
    for c, out_ref in enumerate((kca_ref, kcb_ref, vca_ref, vcb_ref) * NSA_KV):
        for l in range(CMP_STRIDE):
            out_ref[0, c // 4, sub * nj:(sub + 1) * nj, l * NSA_DH:(l + 1) * NSA_DH] = (
                perm[l * nj:(l + 1) * nj, c * NSA_DH:(c + 1) * NSA_DH])

    xq = proj(OD_XQ, MEM_WIDTH)
    gm = _silu(proj(OD_GM, MEM_WIDTH))
    for hd, om in enumerate(_mem_attention(xq, mk_ref[0], mv_ref[0])):
        sl = slice(hd * MEM_DH, (hd + 1) * MEM_DH)
        ym_ref[0, tile, sl] = (om * gm[:, sl]).astype(BF16)


def _odd_in(h, memkv, layer, g, w_in, cmp_pe, ts=2 * NSA_TQ):
    b, s, d = h.shape
    assert w_in.shape[1] == OD_A + 3 * NSA_HEADS + MEM_WIDTH + D_INNER
    w_t = lax.optimization_barrier(jnp.swapaxes(w_in, 0, 1))
    gl, xq, gate = jnp.split(w_t[OD_A:], [3 * NSA_HEADS, 3 * NSA_HEADS + MEM_WIDTH], axis=0)
    gl = gl.reshape(3, NSA_KV, NSA_HPG, d).transpose(1, 0, 2, 3).reshape(NSA_KV, 3 * NSA_HPG, d)
    gl = jnp.pad(gl, ((0, 0), (0, LANES - 3 * NSA_HPG), (0, 0))).reshape(NSA_KV * LANES, d)
    wb = jnp.concatenate([gl, xq, gate], axis=0).T.astype(BF16)
    sub = NSA_TQ
    reps = sub // CMP_STRIDE
    pe = jnp.stack([jnp.stack([jnp.tile(cmp_pe[kv, :CMP_STRIDE], (reps, 1)),
                               jnp.tile(cmp_pe[kv, CMP_STRIDE:], (reps, 1))]) for kv in range(2)])
    kv_blk = 2 * layer
    assert ts % sub == 0
    nj = sub // CMP_STRIDE
    perm = np.zeros((sub, sub), np.float32)
    perm[np.arange(sub), (np.arange(sub) % nj) * CMP_STRIDE + np.arange(sub) // nj] = 1.0
    head_t = jax.ShapeDtypeStruct((b, NSA_KV, s, NSA_DH), BF16)
    head_spec = pl.BlockSpec((1, NSA_KV, ts, NSA_DH), lambda i, j: (i, 0, j, 0))
    headt_t = jax.ShapeDtypeStruct((b, NSA_KV, s // sub, V_ROWS, sub), BF16)
    headt_spec = pl.BlockSpec((1, NSA_KV, ts // sub, V_ROWS, sub), lambda i, j: (i, 0, j, 0, 0))
    flat_t = jax.ShapeDtypeStruct((b, NSA_KV, s // CMP_STRIDE, CMP_STRIDE * NSA_DH), BF16)
    flat_spec = pl.BlockSpec((1, NSA_KV, ts // CMP_STRIDE, CMP_STRIDE * NSA_DH), lambda i, j: (i, 0, j, 0))
    qlike_spec = pl.BlockSpec((1, NSA_KV, NSA_HPG, ts, NSA_DH), lambda i, j: (i, 0, 0, j, 0))
    return pl.pallas_call(
        _odd_in_kernel,
        out_shape=[jax.ShapeDtypeStruct((b, NSA_KV, NSA_HPG, s, NSA_DH), BF16),
                   jax.ShapeDtypeStruct((b, NSA_KV, s, 2 * NSA_DH), BF16),
                   headt_t, head_t, headt_t, flat_t, flat_t, flat_t, flat_t,
                   jax.ShapeDtypeStruct((b, NSA_KV, LANES, s), F32),
                   jax.ShapeDtypeStruct((b, NSA_KV, NSA_HPG, s, NSA_DH), F32),
                   jax.ShapeDtypeStruct((b, s, MEM_WIDTH), BF16)],
        grid=(b, s // ts),
        in_specs=[pl.BlockSpec((1, ts, d), lambda i, j: (i, j, 0)),
                  _const_spec((1, d)),
                  _const_spec((OD_A, d)),
                  _const_spec((d, OD_COLS - OD_A)),
                  _const_spec((2, 2, sub, NSA_DH)),
                  _const_spec((sub, sub)),
                  pl.BlockSpec((1, N_MEM, MEM_WIDTH), lambda i, j: (i, 0, kv_blk)),
                  pl.BlockSpec((1, N_MEM, MEM_WIDTH), lambda i, j: (i, 0, kv_blk + 1))],
        out_specs=[qlike_spec,
                   pl.BlockSpec((1, NSA_KV, ts, 2 * NSA_DH), lambda i, j: (i, 0, j, 0)),
                   headt_spec, head_spec, headt_spec, flat_spec, flat_spec, flat_spec, flat_spec,
                   pl.BlockSpec((1, NSA_KV, LANES, ts), lambda i, j: (i, 0, 0, j)),
                   qlike_spec,
                   pl.BlockSpec((1, ts, MEM_WIDTH), lambda i, j: (i, j, 0))],
        scratch_shapes=[pltpu.VMEM((d, OD_A), BF16)],
        compiler_params=_params(("arbitrary", "arbitrary")),
        name="odd_in_proj",
    )(h, g.reshape(1, d), w_t, wb, pe, jnp.asarray(perm, BF16), memkv, memkv)


def _compress_kernel(xka_ref, xkb_ref, xva_ref, xvb_ref, w1a_ref, w1b_ref, b1_ref, w2_ref, ovt_ref, kc_ref, vct_ref):
    n = xka_ref.shape[2]
    k_cols = vct_ref.shape[3]

    def block_mlp(kv, xa, xb):
        first = _mm(xa[0, 0], w1a_ref[kv])
        second = _mm(xb[0, 0], w1b_ref[kv])
        hid = first + pltpu.roll(second, n - 1, 0) + b1_ref[kv]
        return _mm(_silu(hid).astype(BF16), w2_ref[kv])

    kc_ref[0, 0, 0:CMP_PAD] = jnp.zeros((CMP_PAD, NSA_DH), BF16)
    kc_ref[0, 0, CMP_PAD:] = block_mlp(0, xka_ref, xkb_ref).astype(BF16)
    vc = jnp.concatenate([jnp.zeros((CMP_PAD, NSA_DH), F32), block_mlp(1, xva_ref, xvb_ref),
                          jnp.zeros((k_cols - CMP_PAD - n, NSA_DH), F32)], axis=0)
    vct_ref[0, 0, 0:NSA_DH] = vc.T.astype(BF16)
    vct_ref[0, 0, NSA_DH:] = ovt_ref[...]


def _compress(kca, kcb, vca, vcb, w1, b1, w2):
    b, g, n, half = kca.shape
    dh = half // CMP_STRIDE
    s = n * CMP_STRIDE
    xs = (kca, kcb, vca, vcb)
    w1 = w1.astype(BF16)
    ovt = _overlap_table(s)
    k_cols = ovt.shape[1]
    x_spec = pl.BlockSpec((1, 1, n, half), lambda i, j: (i, j, 0, 0))
    return pl.pallas_call(
        _compress_kernel,
        out_shape=[jax.ShapeDtypeStruct((b, g, n + CMP_PAD, dh), BF16),
                   jax.ShapeDtypeStruct((b, g, dh + LANES, k_cols), BF16)],
        grid=(b, g),
        in_specs=[x_spec, x_spec, x_spec, x_spec,
                  _const_spec((2, half, CMP_HIDDEN)), _const_spec((2, half, CMP_HIDDEN)),
                  _const_spec((2, 1, CMP_HIDDEN)), _const_spec((2, CMP_HIDDEN, dh)),
                  _const_spec((LANES, k_cols))],
        out_specs=[pl.BlockSpec((1, 1, n + CMP_PAD, dh), lambda i, j: (i, j, 0, 0)),
                   pl.BlockSpec((1, 1, dh + LANES, k_cols), lambda i, j: (i, j, 0, 0))],
        compiler_params=_params(("arbitrary", "arbitrary")),
        name="compress",
    )(*xs, w1[:, :half], w1[:, half:], b1.reshape(2, 1, CMP_HIDDEN), w2.astype(BF16), ovt)


def _nsa_kernel(q_ref, kaug_ref, vst_ref, kw_ref, vwt_ref, kcmp_ref, vcmpt_ref, gatest_ref, sg_ref,
                d0_ref, d1_ref, dw_ref, cmpq_ref, y_ref, qaug_ref, m_ref, acc_ref, s_ref, mw_ref, accw_ref, sw_ref,
                mix_ref, tmax_ref, tmaxw_ref):
    i = pl.program_id(2)
    tq = NSA_TQ
    rows = NSA_HPG * tq
    s_len = kw_ref.shape[2]
    n_cmp = s_len // CMP_STRIDE
    n_slc = s_len // SLC_BLOCK
    t0 = i * tq
    q = q_ref[0, 0].reshape(rows, NSA_DH)


    def stream(mx_ref, ac_ref, sc2_ref, tmax_ref):
        def reset():
            mx_ref[...] = jnp.full(mx_ref.shape, LOWEST, F32)
            ac_ref[...] = jnp.zeros(ac_ref.shape, F32)

        def s_to(slot, k_ref, query, tile, bias=None):
            k0 = pl.multiple_of(tile * tq, tq)
            st = _mm_nt(k_ref[0, 0, pl.ds(k0, tq)], query)
            if bias is not None:
                st = st + bias
            sc2_ref[slot] = st
            tmax_ref[slot] = jnp.max(st, axis=0, keepdims=True)

        def pv_from(slot, vt):
            st = sc2_ref[slot]
            m_prev = mx_ref[...]
            m_new = jnp.maximum(m_prev, tmax_ref[slot])
            pt = jnp.exp2(st - m_new).astype(BF16)
            ac_ref[...] = jnp.exp2(m_prev - m_new) * ac_ref[...] + _mm(vt, pt)
            mx_ref[...] = m_new

        def finish(cols=slice(None)):
            return ac_ref[0:NSA_DH, cols] / ac_ref[NSA_DH:NSA_DH + 1, cols]

        return reset, s_to, pv_from, finish

    def off_unless(cond):
        return jnp.where(cond, 0.0, NEG)

    prev1 = jnp.maximum(i - 1, 0)
    prev2 = jnp.maximum(i - 2, 0)
    w_reset, w_s_to, w_pv_from, w_finish = stream(mw_ref, accw_ref, sw_ref, tmaxw_ref)


    n_pad = kcmp_ref.shape[2]
    j_near = pl.multiple_of(i * (tq // CMP_STRIDE), tq // CMP_STRIDE)
    qaug_ref[:, 0:NSA_DH] = q
    qaug_ref[:, NSA_DH:] = cmpq_ref[0]
    krow = lax.broadcasted_iota(jnp.int32, (n_pad, NSA_DH), 0)
    klane = lax.broadcasted_iota(jnp.int32, (n_pad, NSA_DH), 1)
    slot = krow - j_near
    in_window = (klane < 2 * CMP_NEAR) & (jnp.bitwise_and(klane, CMP_NEAR - 1) == slot)
    visible = (krow >= CMP_PAD) & (slot < CMP_NEAR)
    key_cols = jnp.where(klane == 2 * CMP_NEAR, jnp.where(visible, 0.0, NEG), jnp.where(in_window, 1.0, 0.0))
    s_cmp = _mm_nt(jnp.concatenate([kcmp_ref[0, 0], key_cols.astype(BF16)], axis=1), qaug_ref[...])
    w_reset()
    w_s_to(0, kw_ref, q, prev2, dw_ref[0, 0])
    w_s_to(1, kw_ref, q, prev1, d1_ref[0, 0])
    p_cmp = jnp.exp2(s_cmp - jnp.max(s_cmp, axis=0, keepdims=True))
    tcol = t0 + jnp.bitwise_and(lax.broadcasted_iota(jnp.int32, (1, rows), 1), tq - 1)
    scale = jnp.where(tcol >= CMP_BLOCK - 1, 1.0, 0.0) / jnp.sum(p_cmp, axis=0, keepdims=True)
    p_cmp = p_cmp.astype(BF16)
    k_cols = vcmpt_ref.shape[3]
    if k_cols > n_pad:
        p_cmp = jnp.concatenate([p_cmp, jnp.zeros((k_cols - n_pad, rows), BF16)], axis=0)
    w_pv_from(0, vwt_ref[0, 0, prev2])
    both = _mm(vcmpt_ref[0, 0], p_cmp) * scale
    w_s_to(0, kw_ref, q, i, d0_ref[0, 0])
    w_pv_from(1, vwt_ref[0, 0, prev1])
    gtst = gatest_ref[0, 0]
    n_blk = -(-n_slc // 8) * 8
    imp = both[NSA_DH:NSA_DH + n_blk, 0:tq]
    for hh in range(NSA_HPG):
        hs = slice(hh * tq, (hh + 1) * tq)
        mix_ref[:, hs] = gtst[hh:hh + 1] * both[0:NSA_DH, hs]
        if hh:
            imp = imp + both[NSA_DH:NSA_DH + n_blk, hs]

    blk = lax.broadcasted_iota(jnp.int32, (n_blk, tq), 0)
    tq_pos = t0 + lax.broadcasted_iota(jnp.int32, (n_blk, tq), 1)
    cur = tq_pos // SLC_BLOCK
    forced = (blk == 0) | (blk == cur) | (blk == cur - 1)
    future = blk * SLC_BLOCK > tq_pos
    imp = jnp.where(forced, BIG, jnp.where(future, -BIG, imp))
    imp = jnp.where(blk < n_slc, imp, LOWEST)
    blk_f = blk.astype(F32)
    sel = jnp.zeros((n_blk, tq), jnp.bool_)
    for _ in range(SLC_TOPK):
        top = jnp.max(imp, axis=0, keepdims=True)
        first = jnp.min(jnp.where(imp == top, blk_f, float(LANES)), axis=0, keepdims=True)
        hit = blk_f == first
        sel = sel | hit
        imp = jnp.where(hit, KNOCKED, imp)
    sneg_t = jnp.where(sel, 0.0, NEG)
    if n_blk < LANES:
        sneg_t = jnp.concatenate([sneg_t, jnp.zeros((LANES - n_blk, tq), F32)], axis=0)
    sneg = sneg_t.T.astype(BF16)
    for hh in range(NSA_HPG):
        qaug_ref[hh * tq:(hh + 1) * tq, NSA_DH:2 * NSA_DH] = sneg
    w_pv_from(0, vwt_ref[0, 0, i])
    for hh in range(NSA_HPG):
        hs = slice(hh * tq, (hh + 1) * tq)
        mix_ref[:, hs] = mix_ref[:, hs] + gtst[2 * NSA_HPG + hh:2 * NSA_HPG + hh + 1] * w_finish(hs)

    reset, s_to, pv_from, finish = stream(m_ref, acc_ref, s_ref, tmax_ref)
    reset()
    n_far = prev1
    n_pairs = jnp.maximum(n_far - 1, 0) // 2
    s_to(0, kaug_ref, qaug_ref[...], 0, off_unless(n_far >= 1))

    def far_pair(t2):
        s_to(1, kaug_ref, qaug_ref[...], 2 * t2 + 1)
        pv_from(0, vst_ref[0, 0, 2 * t2])
        s_to(0, kaug_ref, qaug_ref[...], 2 * t2 + 2)
        pv_from(1, vst_ref[0, 0, 2 * t2 + 1])

    def far_quad(t4, carry):
        far_pair(2 * t4)
        far_pair(2 * t4 + 1)
        return carry

    lax.fori_loop(0, n_pairs // 2, far_quad, 0)

    @pl.when(n_pairs % 2 == 1)
    def _():
        far_pair(n_pairs - 1)
    c0 = 2 * n_pairs
    two_left = n_far - c0 == 2

    @pl.when(two_left)
    def _():
        s_to(1, kaug_ref, qaug_ref[...], c0 + 1)
        pv_from(0, vst_ref[0, 0, c0])
        s_to(0, kaug_ref, qaug_ref[...], prev1, d1_ref[0, 0])
        pv_from(1, vst_ref[0, 0, c0 + 1])
        s_to(1, kaug_ref, qaug_ref[...], i, d0_ref[0, 0])
        pv_from(0, vst_ref[0, 0, prev1])
        pv_from(1, vst_ref[0, 0, i])

    @pl.when(jnp.logical_not(two_left))
    def _():
        s_to(1, kaug_ref, qaug_ref[...], prev1, d1_ref[0, 0])
        pv_from(0, vst_ref[0, 0, c0])
        s_to(0, kaug_ref, qaug_ref[...], i, d0_ref[0, 0])
        pv_from(1, vst_ref[0, 0, prev1])
        pv_from(0, vst_ref[0, 0, i])

    for hh in range(NSA_HPG):
        hs = slice(hh * tq, (hh + 1) * tq)
        ot = mix_ref[:, hs] + gatest_ref[0, 0, NSA_HPG + hh:NSA_HPG + hh + 1] * finish(hs)
        y_ref[0, :, hh * NSA_DH:(hh + 1) * NSA_DH] = (ot.T * sg_ref[0, 0, hh]).astype(BF16)


def _overlap_table(s):
    n_cmp = (s - CMP_BLOCK) // CMP_STRIDE + 1
    n_slc = s // SLC_BLOCK
    cst = np.arange(n_cmp)[:, None] * CMP_STRIDE
    sst = np.arange(n_slc)[None, :] * SLC_BLOCK
    ov = np.clip(np.minimum(cst + CMP_BLOCK, sst + SLC_BLOCK) - np.maximum(cst, sst), 0, None) / CMP_STRIDE
    k_cols = -(-(CMP_PAD + s // CMP_STRIDE) // LANES) * LANES
    full = np.zeros((LANES, k_cols), np.float32)
    full[:n_slc, CMP_PAD:CMP_PAD + n_cmp] = ov.T
    return jnp.asarray(full, BF16)


def _nsa(q, kaug, vst, kw, vwt, kcmp, vcmpt, gatest, sg, bias, cmpq):
    b, g, hpg, s, dh = q.shape
    tq = NSA_TQ
    assert WINDOW == 2 * tq and s % tq == 0 and s // SLC_BLOCK <= LANES and s // SLC_BLOCK >= SLC_TOPK
    n_pad = kcmp.shape[2]
    rows = hpg * tq
    seq_spec = pl.BlockSpec((1, 1, s, dh), lambda i, j, k: (i, j, 0, 0))
    seqt_spec = pl.BlockSpec((1, 1, s // tq, V_ROWS, tq), lambda i, j, k: (i, j, 0, 0, 0))
    qlike_spec = pl.BlockSpec((1, 1, hpg, tq, dh), lambda i, j, k: (i, j, 0, k, 0))

    def bias_spec(idx, needs_tiles_before=0):
        return pl.BlockSpec((1, 1, tq, rows),
                            lambda i, j, k: (j, jnp.where(k >= needs_tiles_before, idx, BIAS_MASKED), 0, 0))

    return pl.pallas_call(
        _nsa_kernel,
        out_shape=jax.ShapeDtypeStruct((b, s, g * hpg * dh), BF16),
        grid=(b, g, s // tq),
        in_specs=[qlike_spec,
                  pl.BlockSpec((1, 1, s, 2 * dh), lambda i, j, k: (i, j, 0, 0)),
                  seqt_spec, seq_spec, seqt_spec,
                  pl.BlockSpec((1, 1, n_pad, dh), lambda i, j, k: (i, j, 0, 0)),
                  pl.BlockSpec((1, 1, dh + LANES, vcmpt.shape[3]), lambda i, j, k: (i, j, 0, 0)),
                  pl.BlockSpec((1, 1, LANES, tq), lambda i, j, k: (i, j, 0, k)),
                  qlike_spec,
                  bias_spec(0), bias_spec(1, 1), bias_spec(2, 2),
                  pl.BlockSpec((1, rows, LANES), lambda i, j, k: (j, 0, 0))],
        out_specs=pl.BlockSpec((1, tq, hpg * dh), lambda i, j, k: (i, k, j)),
        scratch_shapes=[pltpu.VMEM((rows, 2 * dh), BF16),
                        pltpu.VMEM((1, rows), F32),
                        pltpu.VMEM((V_ROWS, rows), F32),
                        pltpu.VMEM((2, tq, rows), F32),
                        pltpu.VMEM((1, rows), F32),
                        pltpu.VMEM((V_ROWS, rows), F32),
                        pltpu.VMEM((2, tq, rows), F32),
                        pltpu.VMEM((dh, rows), F32),
                        pltpu.VMEM((2, 1, rows), F32),
                        pltpu.VMEM((2, 1, rows), F32)],
        compiler_params=_params(("arbitrary", "arbitrary", "arbitrary")),
        name="nsa_attention",
    )(q, kaug, vst, kw, vwt, kcmp, vcmpt, gatest, sg, bias, bias, bias, cmpq)


def _odd_out_kernel(h_ref, yn_ref, ym_ref, wn_ref, wm_ref, fg_ref, o_ref, *, final_norm, sub):
    for r0 in range(0, h_ref.shape[0], sub):
        rows = slice(r0, r0 + sub)
        out = h_ref[rows] + _mm(yn_ref[rows], wn_ref[...]) + _mm(ym_ref[rows], wm_ref[...])
        o_ref[rows] = _rms(out, fg_ref[...]) if final_norm else out


def _odd_out(h, yn, ym, w_out, final_g, final_norm, tm=1024, sub=256):
    b, s, d = h.shape
    t = b * s
    nw = NSA_HEADS * NSA_DH
    wout = w_out.astype(BF16)
    out = pl.pallas_call(
        functools.partial(_odd_out_kernel, final_norm=final_norm, sub=sub),
        out_shape=jax.ShapeDtypeStruct((t, d), F32),
        grid=(t // tm,),
        in_specs=[pl.BlockSpec((tm, d), lambda i: (i, 0)),
                  pl.BlockSpec((tm, nw), lambda i: (i, 0)),
                  pl.BlockSpec((tm, MEM_WIDTH), lambda i: (i, 0)),
                  _const_spec((nw, d)), _const_spec((MEM_WIDTH, d)), _const_spec((1, d))],
        out_specs=pl.BlockSpec((tm, d), lambda i: (i, 0)),
        compiler_params=_params(("arbitrary",)),
        name="odd_out_proj",
    )(h.reshape(t, d), yn.reshape(t, nw), ym.reshape(t, MEM_WIDTH), wout[:nw], wout[nw:], final_g.reshape(1, d))
    return out.reshape(b, s, d)


def _final_norm_kernel(h_ref, g_ref, o_ref):
    o_ref[...] = _rms(h_ref[...], g_ref[...])


def _final_norm(h, final_g, tm=512):
    b, s, d = h.shape
    t = b * s
    out = pl.pallas_call(
        _final_norm_kernel,
        out_shape=jax.ShapeDtypeStruct((t, d), F32),
        grid=(t // tm,),
        in_specs=[pl.BlockSpec((tm, d), lambda i: (i, 0)), _const_spec((1, d))],
        out_specs=pl.BlockSpec((tm, d), lambda i: (i, 0)),
        compiler_params=_params(("arbitrary",)),
        name="final_norm",
    )(h.reshape(t, d), final_g.reshape(1, d))
    return out.reshape(b, s, d)


def kernel(x, mem, norm_g, final_g, mem_norm_g, rel_bias, ev_w_in, ev_pool_w, ev_pool_scale, ev_w_mem_kv, ev_w_out,
           od_w_in, od_cmp_pe, od_cmp_w1, od_cmp_b1, od_cmp_w2, od_w_mem_kv, od_w_out):
    depth = norm_g.shape[0]
    w_mem = [(ev_w_mem_kv if i % 2 == 0 else od_w_mem_kv)[i // 2] for i in range(depth)]
    memkv = _memkv(mem, mem_norm_g, jnp.concatenate(w_mem, axis=1).astype(BF16))
    bias, cmpq = _bias_tiles(rel_bias) if depth > 1 else (None, None)
    h = x
    for i in range(depth):
        j = i // 2
        last = i == depth - 1
        if i % 2 == 0:
            h = _even_layer(h, memkv, i, norm_g[i], ev_w_in[j], ev_pool_w[j], ev_pool_scale[j], ev_w_out[j])
            if last:
                h = _final_norm(h, final_g)
        else:
            (q, kaug, vst, kw, vwt, kca, kcb, vca, vcb, gatest, sg, ym) = _odd_in(
                h, memkv, i, norm_g[i], od_w_in[j], od_cmp_pe[j])
            kcmp, vcmp = _compress(kca, kcb, vca, vcb, od_cmp_w1[j], od_cmp_b1[j], od_cmp_w2[j])
            yn = _nsa(q, kaug, vst, kw, vwt, kcmp, vcmp, gatest, sg, bias, cmpq)
            h = _odd_out(h, yn, ym, od_w_out[j], final_g, last)
    return h
```

```python
import functools
import math

import numpy as np
import jax
import jax.numpy as jnp
from jax import lax
from jax.experimental import pallas as pl
from jax.experimental.pallas import tpu as pltpu

F32 = jnp.float32
BF16 = jnp.bfloat16

D_INNER = 2048
N_MEM = 256
EPS = 1e-6
NEG = -1e30
BIG = 1e30

POOL_WINDOWS = (2, 4, 8, 16)
POOL_WIDTH = 768
POOL_GROUP = 192
POOL_HALO = 16

RET_HEADS = 4
RET_DK = 128
RET_DV = 192
RET_CHUNK = 128
ROPE_BASE = 10000.0

MEM_HEADS = 4
MEM_DH = 128
MEM_WIDTH = 512

NSA_HEADS = 12
NSA_KV = 2
NSA_HPG = 6
NSA_DH = 128
CMP_BLOCK = 32
CMP_STRIDE = 16
CMP_HIDDEN = 256
SLC_BLOCK = 64
SLC_TOPK = 8
WINDOW = 512
REL_BUCKETS = 32
REL_MAX_DIST = 128

LANES = 128
SUBLANES = 8
NSA_TQ = 256
CMP_NEAR = 32
CMP_PAD = 16
LOG2E = math.log2(math.e)
V_ROWS = NSA_DH + 16
LOWEST = -3.0e38
KNOCKED = -3.3e38
BIAS_MASKED = 3

VMEM_LIMIT = 56 * 1024 * 1024


def _mm(a, b):
    return jnp.dot(a, b, preferred_element_type=F32)


def _mm_nt(a, b):
    return lax.dot_general(a, b, (((1,), (1,)), ((), ())), preferred_element_type=F32)


def _mm_tn(a, b):
    return lax.dot_general(a, b, (((0,), (0,)), ((), ())), preferred_element_type=F32)


def _rms(x, g):
    return x * lax.rsqrt(jnp.mean(x * x, axis=-1, keepdims=True) + EPS) * g


def _silu(x):
    return x * jax.nn.sigmoid(x)


def _const_spec(shape):
    nd = len(shape)
    return pl.BlockSpec(shape, lambda *_: (0,) * nd, pipeline_mode=pl.Buffered(1))


def _params(sem):
    return pltpu.CompilerParams(dimension_semantics=sem, vmem_limit_bytes=VMEM_LIMIT)


def _bucket_starts():
    n = np.arange(REL_MAX_DIST + 1)
    max_exact = REL_BUCKETS // 2
    nf = np.maximum(n, 1).astype(np.float32)
    large = max_exact + (np.log(nf / np.float32(max_exact)) / np.float32(math.log(REL_MAX_DIST / max_exact))
                         * np.float32(REL_BUCKETS - max_exact)).astype(np.int32)
    bucket = np.where(n < max_exact, n, np.minimum(large, REL_BUCKETS - 1))
    assert np.all(np.diff(bucket) >= 0) and bucket[-1] == REL_BUCKETS - 1
    return tuple(int(np.argmax(bucket >= b)) for b in range(REL_BUCKETS))


def _bias_kernel(tab_ref, rel_ref, relq_ref, out_ref, cmpq_ref, *, boxes, starts):
    h = pl.program_id(0)

    def lookup(rel):
        far = tab_ref[REL_BUCKETS - 1, h]
        val = jnp.full(rel.shape, tab_ref[0, h] - far, F32)
        for b in range(1, REL_BUCKETS - 1):
            val = jnp.where(rel >= starts[b], tab_ref[b, h] - far, val)
        val = jnp.where(rel >= starts[REL_BUCKETS - 1], 0.0, val)
        return jnp.where(rel < 0, NEG, val * LOG2E)

    for kind, box in enumerate(boxes):
        out_ref[0, kind] = jnp.where(rel_ref[kind] < 0, NEG, 0.0)
        if box is not None:
            r0, r1, c0, c1 = box
            out_ref[0, kind, r0:r1, c0:c1] = lookup(rel_ref[kind, r0:r1, c0:c1])

    relq = relq_ref[...]
    lane = lax.broadcasted_iota(jnp.int32, relq.shape, 1)
    bias = jnp.where(lane < CMP_NEAR, lookup(relq), 0.0)
    hi = bias.astype(BF16)
    lo = (bias - hi.astype(F32)).astype(BF16).astype(F32)
    one = jnp.where(lane == 2 * CMP_NEAR, 1.0, 0.0)
    cmpq_ref[0] = (hi.astype(F32) + pltpu.roll(lo, CMP_NEAR, 1) + one).astype(BF16)


def _bias_tiles(rel_bias):
    tq = NSA_TQ
    r = np.arange(tq)[:, None]
    c = np.arange(tq)[None, :]
    d0 = r - c
    d1 = tq + r - c
    dw = np.where(c > r, WINDOW + r - c, -1)
    gc = np.where(c < CMP_NEAR, r - CMP_STRIDE * c + (CMP_STRIDE * CMP_PAD - (CMP_BLOCK - 1)), -1)
    rel_np = np.stack([d0.T, d1.T, dw.T, np.full((tq, tq), -1)]).astype(np.int32)
    boxes = []
    for tile in rel_np:
        rr, cc = np.nonzero((tile >= 0) & (tile < REL_MAX_DIST))
        boxes.append(None if rr.size == 0 else tuple(int(v) for v in (
            rr.min() // SUBLANES * SUBLANES, -(-(rr.max() + 1) // SUBLANES) * SUBLANES,
            cc.min() // LANES * LANES, -(-(cc.max() + 1) // LANES) * LANES)))
    rel = jnp.asarray(rel_np)
    nt = rel.shape[0]
    return pl.pallas_call(
        functools.partial(_bias_kernel, boxes=tuple(boxes), starts=_bucket_starts()),
        out_shape=[jax.ShapeDtypeStruct((NSA_KV, nt, tq, NSA_HPG * tq), F32),
                   jax.ShapeDtypeStruct((NSA_KV, NSA_HPG * tq, LANES), BF16)],
        grid=(NSA_HEADS,),
        in_specs=[pl.BlockSpec(memory_space=pltpu.SMEM),
                  _const_spec((nt, tq, tq)),
                  _const_spec((tq, LANES))],
        out_specs=[pl.BlockSpec((1, nt, tq, tq), lambda h: (h // NSA_HPG, 0, 0, h % NSA_HPG)),
                   pl.BlockSpec((1, tq, LANES), lambda h: (h // NSA_HPG, h % NSA_HPG, 0))],
        compiler_params=_params(("arbitrary",)),
        name="bias_tiles",
    )(rel_bias.astype(F32), rel, jnp.asarray(gc[:, :LANES].astype(np.int32)))


def _memkv_kernel(mem_ref, g_ref, w_ref, out_ref):
    y = _rms(mem_ref[0], g_ref[...]).astype(BF16)
    out_ref[0] = _mm(y, w_ref[...]).astype(BF16)


def _memkv(mem, mem_norm_g, w_all):
    b, m, d = mem.shape
    n = w_all.shape[1]
    return pl.pallas_call(
        _memkv_kernel,
        out_shape=jax.ShapeDtypeStruct((b, m, n), BF16),
        grid=(b,),
        in_specs=[pl.BlockSpec((1, m, d), lambda i: (i, 0, 0)),
                  _const_spec((1, d)),
                  _const_spec((d, n))],
        out_specs=pl.BlockSpec((1, m, n), lambda i: (i, 0, 0)),
        compiler_params=_params(("arbitrary",)),
        name="mem_kv",
    )(mem, mem_norm_g.reshape(1, d), w_all)


def _mem_attention(xq, mk, mv):
    outs = []
    for hd in range(MEM_HEADS):
        sl = slice(hd * MEM_DH, (hd + 1) * MEM_DH)
        qm = (xq[:, sl] * (MEM_DH ** -0.5)).astype(BF16)
        s = _mm_nt(qm, mk[:, sl])
        p = jnp.exp(s - jnp.max(s, axis=-1, keepdims=True))
        l = jnp.sum(p, axis=-1, keepdims=True)
        outs.append(_mm(p.astype(BF16), mv[:, sl]) / l)
    return outs


EV_ZA = 0
EV_RQ = EV_ZA + POOL_WIDTH
EV_RK = EV_RQ + RET_HEADS * RET_DK
EV_RV = EV_RK + RET_HEADS * RET_DK
EV_XQ = EV_RV + RET_HEADS * RET_DV
EV_GA = EV_XQ + MEM_WIDTH
EV_GR = EV_GA + POOL_WIDTH
EV_GM = EV_GR + RET_HEADS * RET_DV
EV_COLS = EV_GM + MEM_WIDTH
EV_YA = 0
EV_YR = POOL_WIDTH
EV_YM = EV_YR + RET_HEADS * RET_DV
EV_YCOLS = EV_YM + MEM_WIDTH
EV_SUB = 256


def _even_kernel(gch_ref, h_ref, g_ref, win_ref, wbd_ref, pscale_ref, cos_ref, sin_ref, decay_ref, xi_ref,
                 zeta_ref, mk_ref, mv_ref, wout_ref, o_ref, ext_ref, state_ref, y_ref):
    si = pl.program_id(1)
    ts = h_ref.shape[1]

    @pl.when(si == 0)
    def _():
        ext_ref[0:POOL_HALO, :] = jnp.zeros((POOL_HALO, POOL_WIDTH), F32)
        state_ref[...] = jnp.zeros(state_ref.shape, F32)

    for r0 in range(0, ts, EV_SUB):
        _even_subtile(si * ts + r0, slice(r0, r0 + EV_SUB), gch_ref, h_ref, g_ref, win_ref, wbd_ref, pscale_ref,
                      cos_ref, sin_ref, decay_ref, xi_ref, zeta_ref, mk_ref, mv_ref, wout_ref, o_ref, ext_ref,
                      state_ref, y_ref)


def _even_subtile(t0, tile, gch_ref, h_ref, g_ref, win_ref, wbd_ref, pscale_ref, cos_ref, sin_ref, decay_ref, xi_ref,
                  zeta_ref, mk_ref, mv_ref, wout_ref, o_ref, ext_ref, state_ref, y_ref):
    ts = EV_SUB
    r0 = tile.start
    h = h_ref[0, tile]
    u = _rms(h, g_ref[...]).astype(BF16)

    def proj(start, width):
        return _mm(u, win_ref[:, start:start + width])

    ext_ref[POOL_HALO:, :] = proj(EV_ZA, POOL_WIDTH)
    e = ext_ref[...]
    s2 = e + pltpu.roll(e, 1, 0)
    s4 = s2 + pltpu.roll(s2, 2, 0)
    s8 = s4 + pltpu.roll(s4, 4, 0)
    s16 = s8 + pltpu.roll(s8, 8, 0)
    lane = lax.broadcasted_iota(jnp.int32, e.shape, 1)
    row = lax.broadcasted_iota(jnp.int32, e.shape, 0)
    tpos = t0 + row - POOL_HALO
    g0, g1, g2 = lane < POOL_GROUP, lane < 2 * POOL_GROUP, lane < 3 * POOL_GROUP
    wsum = jnp.where(g0, s2, jnp.where(g1, s4, jnp.where(g2, s8, s16)))
    wlen = jnp.where(g0, POOL_WINDOWS[0], jnp.where(g1, POOL_WINDOWS[1],
                                                    jnp.where(g2, POOL_WINDOWS[2], POOL_WINDOWS[3])))
    cnt = jnp.maximum(jnp.minimum(tpos + 1, wlen), 1).astype(F32)
    pooled = (wsum / cnt - e)[POOL_HALO:]
    ext_ref[0:POOL_HALO, :] = e[ts:ts + POOL_HALO]
    a = _mm(pooled.astype(BF16), wbd_ref[...]) * pscale_ref[...]
    y_ref[tile, EV_YA:EV_YA + POOL_WIDTH] = (a * _silu(proj(EV_GA, POOL_WIDTH))).astype(BF16)

    cos = cos_ref[tile]
    sin = sin_ref[tile]
    zv = proj(EV_RV, RET_HEADS * RET_DV).astype(BF16)
    gate_r = _silu(proj(EV_GR, RET_HEADS * RET_DV))
    q_rot, k_rot = [], []
    for hd in range(RET_HEADS):
        qh = proj(EV_RQ + hd * RET_DK, RET_DK)
        kh = proj(EV_RK + hd * RET_DK, RET_DK)
        q_rot.append((qh * cos + pltpu.roll(qh, RET_DK // 2, 1) * sin) * (RET_DK ** -0.5))
        k_rot.append(kh * cos + pltpu.roll(kh, RET_DK // 2, 1) * sin)
    for c in range(ts // RET_CHUNK):
        rows = slice(c * RET_CHUNK, (c + 1) * RET_CHUNK)
        normed = []
        for hd in range(RET_HEADS):
            qc, kc = q_rot[hd][rows], k_rot[hd][rows]
            vc = zv[rows, hd * RET_DV:(hd + 1) * RET_DV]
            att = _mm_nt(qc.astype(BF16), kc.astype(BF16)) * decay_ref[hd]
            state = state_ref[hd]
            o = _mm(att.astype(BF16), vc) + _mm((qc * xi_ref[hd]).astype(BF16), state.astype(BF16))
            kv = _mm_tn((kc * zeta_ref[hd]).astype(BF16), vc)
            state_ref[hd] = state * gch_ref[hd] + kv
            dlt = o - jnp.mean(o, axis=-1, keepdims=True)
            normed.append(dlt * lax.rsqrt(jnp.mean(dlt * dlt, axis=-1, keepdims=True) + EPS))
        y_ref[r0 + c * RET_CHUNK:r0 + (c + 1) * RET_CHUNK, EV_YR:EV_YM] = (
            jnp.concatenate(normed, axis=-1) * gate_r[rows]).astype(BF16)

    xq = proj(EV_XQ, MEM_WIDTH)
    gm = _silu(proj(EV_GM, MEM_WIDTH))
    for hd, om in enumerate(_mem_attention(xq, mk_ref[0], mv_ref[0])):
        sl = slice(hd * MEM_DH, (hd + 1) * MEM_DH)
        y_ref[tile, EV_YM + hd * MEM_DH:EV_YM + (hd + 1) * MEM_DH] = (om * gm[:, sl]).astype(BF16)

    o_ref[0, tile] = h + _mm(y_ref[tile], wout_ref[...])


def _retention_tables(s):
    half = RET_DK // 2
    inv = ROPE_BASE ** (-jnp.arange(half, dtype=F32) / half)
    ang = jnp.arange(s, dtype=F32)[:, None] * inv[None, :]
    cos, sin = jnp.cos(ang), jnp.sin(ang)
    cos_t = jnp.concatenate([cos, cos], axis=-1)
    sin_t = jnp.concatenate([-sin, sin], axis=-1)
    c = RET_CHUNK
    log_g = jnp.log(1.0 - jnp.exp2(-5.0 - jnp.arange(RET_HEADS, dtype=F32)))
    n = jnp.arange(c, dtype=F32)
    diff = n[:, None] - n[None, :]
    decay = jnp.where(diff >= 0, jnp.exp(log_g[:, None, None] * jnp.maximum(diff, 0.0)), 0.0)
    xi = jnp.exp(log_g[:, None] * (n + 1.0))
    zeta = jnp.exp(log_g[:, None] * (c - 1.0 - n))
    g_chunk = jnp.exp(log_g * c)
    xi_t = jnp.broadcast_to(xi[:, :, None], (RET_HEADS, c, RET_DK))
    zeta_t = jnp.broadcast_to(zeta[:, :, None], (RET_HEADS, c, RET_DK))
    return cos_t, sin_t, decay, xi_t, zeta_t, g_chunk


def _even_layer(h, memkv, layer, g, w_in, pool_w, pool_scale, w_out, ts=4 * EV_SUB):
    b, s, d = h.shape
    assert w_in.shape == (d, EV_COLS) and w_out.shape == (EV_YCOLS, d)
    win = w_in.astype(BF16)
    wout = w_out.astype(BF16)
    wbd = jnp.zeros((POOL_WIDTH, POOL_WIDTH), F32)
    for gi in range(len(POOL_WINDOWS)):
        sl = slice(gi * POOL_GROUP, (gi + 1) * POOL_GROUP)
        wbd = wbd.at[sl, sl].set(pool_w[gi])
    wbd = wbd.astype(BF16)
    cos_t, sin_t, decay, xi_t, zeta_t, g_chunk = _retention_tables(s)
    kv_blk = 2 * layer
    return pl.pallas_call(
        _even_kernel,
        out_shape=jax.ShapeDtypeStruct((b, s, d), F32),
        grid=(b, s // ts),
        in_specs=[pl.BlockSpec(memory_space=pltpu.SMEM),
                  pl.BlockSpec((1, ts, d), lambda i, j: (i, j, 0)),
                  _const_spec((1, d)),
                  _const_spec((d, EV_COLS)),
                  _const_spec((POOL_WIDTH, POOL_WIDTH)),
                  _const_spec((1, POOL_WIDTH)),
                  pl.BlockSpec((ts, RET_DK), lambda i, j: (j, 0)),
                  pl.BlockSpec((ts, RET_DK), lambda i, j: (j, 0)),
                  _const_spec((RET_HEADS, RET_CHUNK, RET_CHUNK)),
                  _const_spec((RET_HEADS, RET_CHUNK, RET_DK)),
                  _const_spec((RET_HEADS, RET_CHUNK, RET_DK)),
                  pl.BlockSpec((1, N_MEM, MEM_WIDTH), lambda i, j: (i, 0, kv_blk)),
                  pl.BlockSpec((1, N_MEM, MEM_WIDTH), lambda i, j: (i, 0, kv_blk + 1)),
                  _const_spec((EV_YCOLS, d))],
        out_specs=pl.BlockSpec((1, ts, d), lambda i, j: (i, j, 0)),
        scratch_shapes=[pltpu.VMEM((POOL_HALO + EV_SUB, POOL_WIDTH), F32),
                        pltpu.VMEM((RET_HEADS, RET_DK, RET_DV), F32),
                        pltpu.VMEM((ts, EV_YCOLS), BF16)],
        compiler_params=_params(("arbitrary", "arbitrary")),
        name="even_layer",
    )(g_chunk, h, g.reshape(1, d), win, wbd, pool_scale.reshape(1, POOL_WIDTH), cos_t, sin_t, decay, xi_t, zeta_t,
      memkv, memkv, wout)


KV_W = NSA_KV * NSA_DH
OD_Q = 0
OD_KV = OD_Q + NSA_HEADS * NSA_DH
OD_A = OD_KV + 6 * KV_W
PIECE_KC, PIECE_VC, PIECE_KS, PIECE_VS, PIECE_KW, PIECE_VW = range(6)
OD_GL = OD_A
OD_XQ = OD_GL + NSA_KV * LANES
OD_GN = OD_XQ + MEM_WIDTH
OD_GM = OD_GN + NSA_HEADS * NSA_DH
OD_COLS = OD_GM + MEM_WIDTH


def _odd_in_kernel(h_ref, g_ref, wraw_ref, wb_ref, pe_ref, perm_ref, mk_ref, mv_ref,
                   q_ref, kaug_ref, vst_ref, kw_ref, vwt_ref, kca_ref, kcb_ref, vca_ref, vcb_ref, gatest_ref,
                   sg_ref, ym_ref, wa_ref):
    @pl.when((pl.program_id(0) == 0) & (pl.program_id(1) == 0))
    def _():
        for c0 in range(0, OD_A, 2 * LANES):
            wa_ref[:, c0:c0 + 2 * LANES] = wraw_ref[c0:c0 + 2 * LANES, :].T.astype(BF16)

    si = pl.program_id(1)
    n_sub = h_ref.shape[1] // NSA_TQ
    for sub in range(n_sub):
        _odd_in_subtile(si * n_sub + sub, sub, h_ref, g_ref, wa_ref, wb_ref, pe_ref, perm_ref, mk_ref, mv_ref,
                        q_ref, kaug_ref, vst_ref, kw_ref, vwt_ref, kca_ref, kcb_ref, vca_ref, vcb_ref, gatest_ref,
                        sg_ref, ym_ref)


def _odd_in_subtile(tile_idx, sub, h_ref, g_ref, wa_ref, wb_ref, pe_ref, perm_ref, mk_ref, mv_ref,
                    q_ref, kaug_ref, vst_ref, kw_ref, vwt_ref, kca_ref, kcb_ref, vca_ref, vcb_ref, gatest_ref,
                    sg_ref, ym_ref):
    ts = NSA_TQ
    tile = slice(sub * ts, (sub + 1) * ts)
    u = _rms(h_ref[0, tile], g_ref[...]).astype(BF16)

    def proj(start, width):
        if start < OD_A:
            return _mm(u, wa_ref[:, start:start + width])
        return _mm(u, wb_ref[:, start - OD_A:start - OD_A + width])

    zq = proj(OD_Q, NSA_HEADS * NSA_DH) * (NSA_DH ** -0.5 * LOG2E)
    zg = _silu(proj(OD_GN, NSA_HEADS * NSA_DH))
    for g in range(NSA_KV):
        for hh in range(NSA_HPG):
            sl = slice((g * NSA_HPG + hh) * NSA_DH, (g * NSA_HPG + hh + 1) * NSA_DH)
            q_ref[0, g, hh, tile] = zq[:, sl].astype(BF16)
            sg_ref[0, g, hh, tile] = zg[:, sl]

    zkv = proj(OD_KV, 6 * KV_W)
    lane = lax.broadcasted_iota(jnp.int32, (ts, LANES), 1)
    blk = (tile_idx * ts + lax.broadcasted_iota(jnp.int32, (ts, LANES), 0)) // SLC_BLOCK
    onehot = jnp.where(lane == blk, 1.0, 0.0).astype(BF16)
    ones_rows = jnp.where(lax.broadcasted_iota(jnp.int32, (V_ROWS - NSA_DH, ts), 0) == 0, 1.0, 0.0).astype(BF16)
    zgl = jax.nn.sigmoid(proj(OD_GL, NSA_KV * LANES))
    flat_in = []
    for g in range(NSA_KV):
        def piece(idx):
            off = idx * KV_W + g * NSA_DH
            return zkv[:, off:off + NSA_DH]
        kaug_ref[0, g, tile, 0:NSA_DH] = piece(PIECE_KS).astype(BF16)
        kaug_ref[0, g, tile, NSA_DH:2 * NSA_DH] = onehot
        vst_ref[0, g, sub, 0:NSA_DH] = piece(PIECE_VS).T.astype(BF16)
        vst_ref[0, g, sub, NSA_DH:V_ROWS] = ones_rows
        kw_ref[0, g, tile] = piece(PIECE_KW).astype(BF16)
        vwt_ref[0, g, sub, 0:NSA_DH] = piece(PIECE_VW).T.astype(BF16)
        vwt_ref[0, g, sub, NSA_DH:V_ROWS] = ones_rows
        kc, vc = piece(PIECE_KC), piece(PIECE_VC)
        flat_in += [(kc + pe_ref[0, 0]).astype(BF16), (kc + pe_ref[0, 1]).astype(BF16),
                    (vc + pe_ref[1, 0]).astype(BF16), (vc + pe_ref[1, 1]).astype(BF16)]
        gatest_ref[0, g, :, tile] = zgl[:, g * LANES:(g + 1) * LANES].T

    nj = ts // CMP_STRIDE
    perm = _mm(perm_ref[...], jnp.concatenate(flat_in, axis=1)).astype(BF16)
    for c, out_ref in enumerate((kca_ref, kcb_ref, vca_ref, vcb_ref) * NSA_KV):
        for l in range(CMP_STRIDE):
            out_ref[0, c // 4, sub * nj:(sub + 1) * nj, l * NSA_DH:(l + 1) * NSA_DH] = (
                perm[l * nj:(l + 1) * nj, c * NSA_DH:(c + 1) * NSA_DH])

    xq = proj(OD_XQ, MEM_WIDTH)
    gm = _silu(proj(OD_GM, MEM_WIDTH))
    for hd, om in enumerate(_mem_attention(xq, mk_ref[0], mv_ref[0])):
        sl = slice(hd * MEM_DH, (hd + 1) * MEM_DH)
        ym_ref[0, tile, sl] = (om * gm[:, sl]).astype(BF16)


def _odd_in(h, memkv, layer, g, w_in, cmp_pe, ts=2 * NSA_TQ):
    b, s, d = h.shape
    assert w_in.shape[1] == OD_A + 3 * NSA_HEADS + MEM_WIDTH + D_INNER
    w_t = jnp.swapaxes(w_in, 0, 1)
    gl, xq, gate = jnp.split(w_t[OD_A:], [3 * NSA_HEADS, 3 * NSA_HEADS + MEM_WIDTH], axis=0)
    gl = gl.reshape(3, NSA_KV, NSA_HPG, d).transpose(1, 0, 2, 3).reshape(NSA_KV, 3 * NSA_HPG, d)
    gl = jnp.pad(gl, ((0, 0), (0, LANES - 3 * NSA_HPG), (0, 0))).reshape(NSA_KV * LANES, d)
    wb = jnp.concatenate([gl, xq, gate], axis=0).T.astype(BF16)
    sub = NSA_TQ
    reps = sub // CMP_STRIDE
    pe = jnp.stack([jnp.stack([jnp.tile(cmp_pe[kv, :CMP_STRIDE], (reps, 1)),
                               jnp.tile(cmp_pe[kv, CMP_STRIDE:], (reps, 1))]) for kv in range(2)])
    kv_blk = 2 * layer
    assert ts % sub == 0
    nj = sub // CMP_STRIDE
    perm = np.zeros((sub, sub), np.float32)
    perm[np.arange(sub), (np.arange(sub) % nj) * CMP_STRIDE + np.arange(sub) // nj] = 1.0
    head_t = jax.ShapeDtypeStruct((b, NSA_KV, s, NSA_DH), BF16)
    head_spec = pl.BlockSpec((1, NSA_KV, ts, NSA_DH), lambda i, j: (i, 0, j, 0))
    headt_t = jax.ShapeDtypeStruct((b, NSA_KV, s // sub, V_ROWS, sub), BF16)
    headt_spec = pl.BlockSpec((1, NSA_KV, ts // sub, V_ROWS, sub), lambda i, j: (i, 0, j, 0, 0))
    flat_t = jax.ShapeDtypeStruct((b, NSA_KV, s // CMP_STRIDE, CMP_STRIDE * NSA_DH), BF16)
    flat_spec = pl.BlockSpec((1, NSA_KV, ts // CMP_STRIDE, CMP_STRIDE * NSA_DH), lambda i, j: (i, 0, j, 0))
    qlike_spec = pl.BlockSpec((1, NSA_KV, NSA_HPG, ts, NSA_DH), lambda i, j: (i, 0, 0, j, 0))
    return pl.pallas_call(
        _odd_in_kernel,
        out_shape=[jax.ShapeDtypeStruct((b, NSA_KV, NSA_HPG, s, NSA_DH), BF16),
                   jax.ShapeDtypeStruct((b, NSA_KV, s, 2 * NSA_DH), BF16),
                   headt_t, head_t, headt_t, flat_t, flat_t, flat_t, flat_t,
                   jax.ShapeDtypeStruct((b, NSA_KV, LANES, s), F32),
                   jax.ShapeDtypeStruct((b, NSA_KV, NSA_HPG, s, NSA_DH), F32),
                   jax.ShapeDtypeStruct((b, s, MEM_WIDTH), BF16)],
        grid=(b, s // ts),
        in_specs=[pl.BlockSpec((1, ts, d), lambda i, j: (i, j, 0)),
                  _const_spec((1, d)),
                  _const_spec((OD_A, d)),
                  _const_spec((d, OD_COLS - OD_A)),
                  _const_spec((2, 2, sub, NSA_DH)),
                  _const_spec((sub, sub)),
                  pl.BlockSpec((1, N_MEM, MEM_WIDTH), lambda i, j: (i, 0, kv_blk)),
                  pl.BlockSpec((1, N_MEM, MEM_WIDTH), lambda i, j: (i, 0, kv_blk + 1))],
        out_specs=[qlike_spec,
                   pl.BlockSpec((1, NSA_KV, ts, 2 * NSA_DH), lambda i, j: (i, 0, j, 0)),
                   headt_spec, head_spec, headt_spec, flat_spec, flat_spec, flat_spec, flat_spec,
                   pl.BlockSpec((1, NSA_KV, LANES, ts), lambda i, j: (i, 0, 0, j)),
                   qlike_spec,
                   pl.BlockSpec((1, ts, MEM_WIDTH), lambda i, j: (i, j, 0))],
        scratch_shapes=[pltpu.VMEM((d, OD_A), BF16)],
        compiler_params=_params(("arbitrary", "arbitrary")),
        name="odd_in_proj",
    )(h, g.reshape(1, d), w_t, wb, pe, jnp.asarray(perm, BF16), memkv, memkv)


def _compress_kernel(xka_ref, xkb_ref, xva_ref, xvb_ref, w1a_ref, w1b_ref, b1_ref, w2_ref, ovt_ref, kc_ref, vct_ref):
    n = xka_ref.shape[2]
    k_cols = vct_ref.shape[3]

    def block_mlp(kv, xa, xb):
        first = _mm(xa[0, 0], w1a_ref[kv])
        second = _mm(xb[0, 0], w1b_ref[kv])
        hid = first + pltpu.roll(second, n - 1, 0) + b1_ref[kv]
        return _mm(_silu(hid).astype(BF16), w2_ref[kv])

    kc_ref[0, 0, 0:CMP_PAD] = jnp.zeros((CMP_PAD, NSA_DH), BF16)
    kc_ref[0, 0, CMP_PAD:] = block_mlp(0, xka_ref, xkb_ref).astype(BF16)
    vc = jnp.concatenate([jnp.zeros((CMP_PAD, NSA_DH), F32), block_mlp(1, xva_ref, xvb_ref),
                          jnp.zeros((k_cols - CMP_PAD - n, NSA_DH), F32)], axis=0)
    vct_ref[0, 0, 0:NSA_DH] = vc.T.astype(BF16)
    vct_ref[0, 0, NSA_DH:] = ovt_ref[...]


def _compress(kca, kcb, vca, vcb, w1, b1, w2):
    b, g, n, half = kca.shape
    dh = half // CMP_STRIDE
    s = n * CMP_STRIDE
    xs = (kca, kcb, vca, vcb)
    w1 = w1.astype(BF16)
    ovt = _overlap_table(s)
    k_cols = ovt.shape[1]
    x_spec = pl.BlockSpec((1, 1, n, half), lambda i, j: (i, j, 0, 0))
    return pl.pallas_call(
        _compress_kernel,
        out_shape=[jax.ShapeDtypeStruct((b, g, n + CMP_PAD, dh), BF16),
                   jax.ShapeDtypeStruct((b, g, dh + LANES, k_cols), BF16)],
        grid=(b, g),
        in_specs=[x_spec, x_spec, x_spec, x_spec,
                  _const_spec((2, half, CMP_HIDDEN)), _const_spec((2, half, CMP_HIDDEN)),
                  _const_spec((2, 1, CMP_HIDDEN)), _const_spec((2, CMP_HIDDEN, dh)),
                  _const_spec((LANES, k_cols))],
        out_specs=[pl.BlockSpec((1, 1, n + CMP_PAD, dh), lambda i, j: (i, j, 0, 0)),
                   pl.BlockSpec((1, 1, dh + LANES, k_cols), lambda i, j: (i, j, 0, 0))],
        compiler_params=_params(("arbitrary", "arbitrary")),
        name="compress",
    )(*xs, w1[:, :half], w1[:, half:], b1.reshape(2, 1, CMP_HIDDEN), w2.astype(BF16), ovt)


def _nsa_kernel(q_ref, kaug_ref, vst_ref, kw_ref, vwt_ref, kcmp_ref, vcmpt_ref, gatest_ref, sg_ref,
                d0_ref, d1_ref, dw_ref, cmpq_ref, y_ref, qaug_ref, m_ref, acc_ref, s_ref, mw_ref, accw_ref, sw_ref,
                mix_ref, tmax_ref, tmaxw_ref):
    i = pl.program_id(2)
    tq = NSA_TQ
    rows = NSA_HPG * tq
    s_len = kw_ref.shape[2]
    n_cmp = s_len // CMP_STRIDE
    n_slc = s_len // SLC_BLOCK
    t0 = i * tq
    q = q_ref[0, 0].reshape(rows, NSA_DH)


    def stream(mx_ref, ac_ref, sc2_ref, tmax_ref):
        def reset():
            mx_ref[...] = jnp.full(mx_ref.shape, LOWEST, F32)
            ac_ref[...] = jnp.zeros(ac_ref.shape, F32)

        def s_to(slot, k_ref, query, tile, bias=None):
            k0 = pl.multiple_of(tile * tq, tq)
            st = _mm_nt(k_ref[0, 0, pl.ds(k0, tq)], query)
            if bias is not None:
                st = st + bias
            sc2_ref[slot] = st
            tmax_ref[slot] = jnp.max(st, axis=0, keepdims=True)

        def pv_from(slot, vt):
            st = sc2_ref[slot]
            m_prev = mx_ref[...]
            m_new = jnp.maximum(m_prev, tmax_ref[slot])
            pt = jnp.exp2(st - m_new).astype(BF16)
            ac_ref[...] = jnp.exp2(m_prev - m_new) * ac_ref[...] + _mm(vt, pt)
            mx_ref[...] = m_new

        def finish(cols=slice(None)):
            return ac_ref[0:NSA_DH, cols] / ac_ref[NSA_DH:NSA_DH + 1, cols]

        return reset, s_to, pv_from, finish

    def off_unless(cond):
        return jnp.where(cond, 0.0, NEG)

    prev1 = jnp.maximum(i - 1, 0)
    prev2 = jnp.maximum(i - 2, 0)
    w_reset, w_s_to, w_pv_from, w_finish = stream(mw_ref, accw_ref, sw_ref, tmaxw_ref)


    n_pad = kcmp_ref.shape[2]
    j_near = pl.multiple_of(i * (tq // CMP_STRIDE), tq // CMP_STRIDE)
    qaug_ref[:, 0:NSA_DH] = q
    qaug_ref[:, NSA_DH:] = cmpq_ref[0]
    krow = lax.broadcasted_iota(jnp.int32, (n_pad, NSA_DH), 0)
    klane = lax.broadcasted_iota(jnp.int32, (n_pad, NSA_DH), 1)
    slot = krow - j_near
    in_window = (klane < 2 * CMP_NEAR) & (jnp.bitwise_and(klane, CMP_NEAR - 1) == slot)
    visible = (krow >= CMP_PAD) & (slot < CMP_NEAR)
    key_cols = jnp.where(klane == 2 * CMP_NEAR, jnp.where(visible, 0.0, NEG), jnp.where(in_window, 1.0, 0.0))
    s_cmp = _mm_nt(jnp.concatenate([kcmp_ref[0, 0], key_cols.astype(BF16)], axis=1), qaug_ref[...])
    w_reset()
    w_s_to(0, kw_ref, q, prev2, dw_ref[0, 0])
    w_s_to(1, kw_ref, q, prev1, d1_ref[0, 0])
    p_cmp = jnp.exp2(s_cmp - jnp.max(s_cmp, axis=0, keepdims=True))
    tcol = t0 + jnp.bitwise_and(lax.broadcasted_iota(jnp.int32, (1, rows), 1), tq - 1)
    scale = jnp.where(tcol >= CMP_BLOCK - 1, 1.0, 0.0) / jnp.sum(p_cmp, axis=0, keepdims=True)
    p_cmp = p_cmp.astype(BF16)
    k_cols = vcmpt_ref.shape[3]
    if k_cols > n_pad:
        p_cmp = jnp.concatenate([p_cmp, jnp.zeros((k_cols - n_pad, rows), BF16)], axis=0)
    w_pv_from(0, vwt_ref[0, 0, prev2])
    both = _mm(vcmpt_ref[0, 0], p_cmp) * scale
    w_s_to(0, kw_ref, q, i, d0_ref[0, 0])
    w_pv_from(1, vwt_ref[0, 0, prev1])
    gtst = gatest_ref[0, 0]
    n_blk = -(-n_slc // SUBLANES) * SUBLANES
    imp = both[NSA_DH:NSA_DH + n_blk, 0:tq]
    for hh in range(NSA_HPG):
        hs = slice(hh * tq, (hh + 1) * tq)
        mix_ref[:, hs] = gtst[hh:hh + 1] * both[0:NSA_DH, hs]
        if hh:
            imp = imp + both[NSA_DH:NSA_DH + n_blk, hs]

    blk = lax.broadcasted_iota(jnp.int32, (n_blk, tq), 0)
    tq_pos = t0 + lax.broadcasted_iota(jnp.int32, (n_blk, tq), 1)
    cur = tq_pos // SLC_BLOCK
    forced = (blk == 0) | (blk == cur) | (blk == cur - 1)
    future = blk * SLC_BLOCK > tq_pos
    imp = jnp.where(forced, BIG, jnp.where(future, -BIG, imp))
    imp = jnp.where(blk < n_slc, imp, LOWEST)
    blk_f = blk.astype(F32)
    sel = jnp.zeros((n_blk, tq), jnp.bool_)
    for _ in range(SLC_TOPK):
        top = jnp.max(imp, axis=0, keepdims=True)
        first = jnp.min(jnp.where(imp == top, blk_f, float(LANES)), axis=0, keepdims=True)
        hit = blk_f == first
        sel = sel | hit
        imp = jnp.where(hit, KNOCKED, imp)
    sneg_t = jnp.where(sel, 0.0, NEG)
    if n_blk < LANES:
        sneg_t = jnp.concatenate([sneg_t, jnp.zeros((LANES - n_blk, tq), F32)], axis=0)
    sneg = sneg_t.T.astype(BF16)
    for hh in range(NSA_HPG):
        qaug_ref[hh * tq:(hh + 1) * tq, NSA_DH:2 * NSA_DH] = sneg
    w_pv_from(0, vwt_ref[0, 0, i])
    for hh in range(NSA_HPG):
        hs = slice(hh * tq, (hh + 1) * tq)
        mix_ref[:, hs] = mix_ref[:, hs] + gtst[2 * NSA_HPG + hh:2 * NSA_HPG + hh + 1] * w_finish(hs)

    reset, s_to, pv_from, finish = stream(m_ref, acc_ref, s_ref, tmax_ref)
    reset()
    n_far = prev1
    n_pairs = jnp.maximum(n_far - 1, 0) // 2
    s_to(0, kaug_ref, qaug_ref[...], 0, off_unless(n_far >= 1))

    def far_pair(t2):
        s_to(1, kaug_ref, qaug_ref[...], 2 * t2 + 1)
        pv_from(0, vst_ref[0, 0, 2 * t2])
        s_to(0, kaug_ref, qaug_ref[...], 2 * t2 + 2)
        pv_from(1, vst_ref[0, 0, 2 * t2 + 1])

    def far_quad(t4, carry):
        far_pair(2 * t4)
        far_pair(2 * t4 + 1)
        return carry

    lax.fori_loop(0, n_pairs // 2, far_quad, 0)

    @pl.when(n_pairs % 2 == 1)
    def _():
        far_pair(n_pairs - 1)
    c0 = 2 * n_pairs
    two_left = n_far - c0 == 2

    @pl.when(two_left)
    def _():
        s_to(1, kaug_ref, qaug_ref[...], c0 + 1)
        pv_from(0, vst_ref[0, 0, c0])
        s_to(0, kaug_ref, qaug_ref[...], prev1, d1_ref[0, 0])
        pv_from(1, vst_ref[0, 0, c0 + 1])
        s_to(1, kaug_ref, qaug_ref[...], i, d0_ref[0, 0])
        pv_from(0, vst_ref[0, 0, prev1])
        pv_from(1, vst_ref[0, 0, i])

    @pl.when(jnp.logical_not(two_left))
    def _():
        s_to(1, kaug_ref, qaug_ref[...], prev1, d1_ref[0, 0])
        pv_from(0, vst_ref[0, 0, c0])
        s_to(0, kaug_ref, qaug_ref[...], i, d0_ref[0, 0])
        pv_from(1, vst_ref[0, 0, prev1])
        pv_from(0, vst_ref[0, 0, i])

    for hh in range(NSA_HPG):
        hs = slice(hh * tq, (hh + 1) * tq)
        ot = mix_ref[:, hs] + gatest_ref[0, 0, NSA_HPG + hh:NSA_HPG + hh + 1] * finish(hs)
        y_ref[0, :, hh * NSA_DH:(hh + 1) * NSA_DH] = (ot.T * sg_ref[0, 0, hh]).astype(BF16)


def _overlap_table(s):
    n_cmp = (s - CMP_BLOCK) // CMP_STRIDE + 1
    n_slc = s // SLC_BLOCK
    cst = np.arange(n_cmp)[:, None] * CMP_STRIDE
    sst = np.arange(n_slc)[None, :] * SLC_BLOCK
    ov = np.clip(np.minimum(cst + CMP_BLOCK, sst + SLC_BLOCK) - np.maximum(cst, sst), 0, None) / CMP_STRIDE
    k_cols = -(-(CMP_PAD + s // CMP_STRIDE) // LANES) * LANES
    full = np.zeros((LANES, k_cols), np.float32)
    full[:n_slc, CMP_PAD:CMP_PAD + n_cmp] = ov.T
    return jnp.asarray(full, BF16)


def _nsa(q, kaug, vst, kw, vwt, kcmp, vcmpt, gatest, sg, bias, cmpq):
    b, g, hpg, s, dh = q.shape
    tq = NSA_TQ
    assert WINDOW == 2 * tq and s % tq == 0 and s // SLC_BLOCK <= LANES and s // SLC_BLOCK >= SLC_TOPK
    n_pad = kcmp.shape[2]
    rows = hpg * tq
    seq_spec = pl.BlockSpec((1, 1, s, dh), lambda i, j, k: (i, j, 0, 0))
    seqt_spec = pl.BlockSpec((1, 1, s // tq, V_ROWS, tq), lambda i, j, k: (i, j, 0, 0, 0))
    qlike_spec = pl.BlockSpec((1, 1, hpg, tq, dh), lambda i, j, k: (i, j, 0, k, 0))

    def bias_spec(idx, needs_tiles_before=0):
        return pl.BlockSpec((1, 1, tq, rows),
                            lambda i, j, k: (j, jnp.where(k >= needs_tiles_before, idx, BIAS_MASKED), 0, 0))

    return pl.pallas_call(
        _nsa_kernel,
        out_shape=jax.ShapeDtypeStruct((b, s, g * hpg * dh), BF16),
        grid=(b, g, s // tq),
        in_specs=[qlike_spec,
                  pl.BlockSpec((1, 1, s, 2 * dh), lambda i, j, k: (i, j, 0, 0)),
                  seqt_spec, seq_spec, seqt_spec,
                  pl.BlockSpec((1, 1, n_pad, dh), lambda i, j, k: (i, j, 0, 0)),
                  pl.BlockSpec((1, 1, dh + LANES, vcmpt.shape[3]), lambda i, j, k: (i, j, 0, 0)),
                  pl.BlockSpec((1, 1, LANES, tq), lambda i, j, k: (i, j, 0, k)),
                  qlike_spec,
                  bias_spec(0), bias_spec(1, 1), bias_spec(2, 2),
                  pl.BlockSpec((1, rows, LANES), lambda i, j, k: (j, 0, 0))],
        out_specs=pl.BlockSpec((1, tq, hpg * dh), lambda i, j, k: (i, k, j)),
        scratch_shapes=[pltpu.VMEM((rows, 2 * dh), BF16),
                        pltpu.VMEM((1, rows), F32),
                        pltpu.VMEM((V_ROWS, rows), F32),
                        pltpu.VMEM((2, tq, rows), F32),
                        pltpu.VMEM((1, rows), F32),
                        pltpu.VMEM((V_ROWS, rows), F32),
                        pltpu.VMEM((2, tq, rows), F32),
                        pltpu.VMEM((dh, rows), F32),
                        pltpu.VMEM((2, 1, rows), F32),
                        pltpu.VMEM((2, 1, rows), F32)],
        compiler_params=_params(("arbitrary", "arbitrary", "arbitrary")),
        name="nsa_attention",
    )(q, kaug, vst, kw, vwt, kcmp, vcmpt, gatest, sg, bias, bias, bias, cmpq)


def _odd_out_kernel(h_ref, yn_ref, ym_ref, wn_ref, wm_ref, fg_ref, o_ref, *, final_norm, sub):
    for r0 in range(0, h_ref.shape[0], sub):
        rows = slice(r0, r0 + sub)
        out = h_ref[rows] + _mm(yn_ref[rows], wn_ref[...]) + _mm(ym_ref[rows], wm_ref[...])
        o_ref[rows] = _rms(out, fg_ref[...]) if final_norm else out


def _odd_out(h, yn, ym, w_out, final_g, final_norm, tm=1024, sub=256):
    b, s, d = h.shape
    t = b * s
    nw = NSA_HEADS * NSA_DH
    wout = w_out.astype(BF16)
    out = pl.pallas_call(
        functools.partial(_odd_out_kernel, final_norm=final_norm, sub=sub),
        out_shape=jax.ShapeDtypeStruct((t, d), F32),
        grid=(t // tm,),
        in_specs=[pl.BlockSpec((tm, d), lambda i: (i, 0)),
                  pl.BlockSpec((tm, nw), lambda i: (i, 0)),
                  pl.BlockSpec((tm, MEM_WIDTH), lambda i: (i, 0)),
                  _const_spec((nw, d)), _const_spec((MEM_WIDTH, d)), _const_spec((1, d))],
        out_specs=pl.BlockSpec((tm, d), lambda i: (i, 0)),
        compiler_params=_params(("arbitrary",)),
        name="odd_out_proj",
    )(h.reshape(t, d), yn.reshape(t, nw), ym.reshape(t, MEM_WIDTH), wout[:nw], wout[nw:], final_g.reshape(1, d))
    return out.reshape(b, s, d)


def _final_norm_kernel(h_ref, g_ref, o_ref):
    o_ref[...] = _rms(h_ref[...], g_ref[...])


def _final_norm(h, final_g, tm=512):
    b, s, d = h.shape
    t = b * s
    out = pl.pallas_call(
        _final_norm_kernel,
        out_shape=jax.ShapeDtypeStruct((t, d), F32),
        grid=(t // tm,),
        in_specs=[pl.BlockSpec((tm, d), lambda i: (i, 0)), _const_spec((1, d))],
        out_specs=pl.BlockSpec((tm, d), lambda i: (i, 0)),
        compiler_params=_params(("arbitrary",)),
        name="final_norm",
    )(h.reshape(t, d), final_g.reshape(1, d))
    return out.reshape(b, s, d)


def kernel(x, mem, norm_g, final_g, mem_norm_g, rel_bias, ev_w_in, ev_pool_w, ev_pool_scale, ev_w_mem_kv, ev_w_out,
           od_w_in, od_cmp_pe, od_cmp_w1, od_cmp_b1, od_cmp_w2, od_w_mem_kv, od_w_out):
    depth = norm_g.shape[0]
    w_mem = [(ev_w_mem_kv if i % 2 == 0 else od_w_mem_kv)[i // 2] for i in range(depth)]
    memkv = _memkv(mem, mem_norm_g, jnp.concatenate(w_mem, axis=1).astype(BF16))
    bias, cmpq = _bias_tiles(rel_bias) if depth > 1 else (None, None)
    h = x
    for i in range(depth):
        j = i // 2
        last = i == depth - 1
        if i % 2 == 0:
            h = _even_layer(h, memkv, i, norm_g[i], ev_w_in[j], ev_pool_w[j], ev_pool_scale[j], ev_w_out[j])
            if last:
                h = _final_norm(h, final_g)
        else:
            (q, kaug, vst, kw, vwt, kca, kcb, vca, vcb, gatest, sg, ym) = _odd_in(
                h, memkv, i, norm_g[i], od_w_in[j], od_cmp_pe[j])
            kcmp, vcmp = _compress(kca, kcb, vca, vcb, od_cmp_w1[j], od_cmp_b1[j], od_cmp_w2[j])
            yn = _nsa(q, kaug, vst, kw, vwt, kcmp, vcmp, gatest, sg, bias, cmpq)
            h = _odd_out(h, yn, ym, od_w_out[j], final_g, last)
    return h
```

```python
import functools
import math

import numpy as np
import jax
import jax.numpy as jnp
from jax import lax
from jax.experimental import pallas as pl
from jax.experimental.pallas import tpu as pltpu

F32 = jnp.float32
BF16 = jnp.bfloat16

D_INNER = 2048
N_MEM = 256
EPS = 1e-6
NEG = -1e30
BIG = 1e30

POOL_WINDOWS = (2, 4, 8, 16)
POOL_WIDTH = 768
POOL_GROUP = 192
POOL_HALO = 16

RET_HEADS = 4
RET_DK = 128
RET_DV = 192
RET_CHUNK = 128
ROPE_BASE = 10000.0

MEM_HEADS = 4
MEM_DH = 128
MEM_WIDTH = 512

NSA_HEADS = 12
NSA_KV = 2
NSA_HPG = 6
NSA_DH = 128
CMP_BLOCK = 32
CMP_STRIDE = 16
CMP_HIDDEN = 256
SLC_BLOCK = 64
SLC_TOPK = 8
WINDOW = 512
REL_BUCKETS = 32
REL_MAX_DIST = 128

LANES = 128
SUBLANES = 8
NSA_TQ = 256
CMP_NEAR = 32
CMP_PAD = 16
LOG2E = math.log2(math.e)
V_ROWS = NSA_DH + 16
LOWEST = -3.0e38
KNOCKED = -3.3e38
BIAS_MASKED = 3

VMEM_LIMIT = 56 * 1024 * 1024


def _mm(a, b):
    return jnp.dot(a, b, preferred_element_type=F32)


def _mm_nt(a, b):
    return lax.dot_general(a, b, (((1,), (1,)), ((), ())), preferred_element_type=F32)


def _mm_tn(a, b):
    return lax.dot_general(a, b, (((0,), (0,)), ((), ())), preferred_element_type=F32)


def _rms(x, g):
    return x * lax.rsqrt(jnp.mean(x * x, axis=-1, keepdims=True) + EPS) * g


def _silu(x):
    return x * jax.nn.sigmoid(x)


def _const_spec(shape):
    nd = len(shape)
    return pl.BlockSpec(shape, lambda *_: (0,) * nd, pipeline_mode=pl.Buffered(1))


def _params(sem):
    return pltpu.CompilerParams(dimension_semantics=sem, vmem_limit_bytes=VMEM_LIMIT)


def _bucket_starts():
    n = np.arange(REL_MAX_DIST + 1)
    max_exact = REL_BUCKETS // 2
    nf = np.maximum(n, 1).astype(np.float32)
    large = max_exact + (np.log(nf / np.float32(max_exact)) / np.float32(math.log(REL_MAX_DIST / max_exact))
                         * np.float32(REL_BUCKETS - max_exact)).astype(np.int32)
    bucket = np.where(n < max_exact, n, np.minimum(large, REL_BUCKETS - 1))
    assert np.all(np.diff(bucket) >= 0) and bucket[-1] == REL_BUCKETS - 1
    return tuple(int(np.argmax(bucket >= b)) for b in range(REL_BUCKETS))


def _bias_kernel(tab_ref, rel_ref, relq_ref, out_ref, cmpq_ref, *, boxes, starts):
    h = pl.program_id(0)

    def lookup(rel):
        far = tab_ref[REL_BUCKETS - 1, h]
        val = jnp.full(rel.shape, tab_ref[0, h] - far, F32)
        for b in range(1, REL_BUCKETS - 1):
            val = jnp.where(rel >= starts[b], tab_ref[b, h] - far, val)
        val = jnp.where(rel >= starts[REL_BUCKETS - 1], 0.0, val)
        return jnp.where(rel < 0, NEG, val * LOG2E)

    for kind, box in enumerate(boxes):
        out_ref[0, kind] = jnp.where(rel_ref[kind] < 0, NEG, 0.0)
        if box is not None:
            r0, r1, c0, c1 = box
            out_ref[0, kind, r0:r1, c0:c1] = lookup(rel_ref[kind, r0:r1, c0:c1])

    relq = relq_ref[...]
    lane = lax.broadcasted_iota(jnp.int32, relq.shape, 1)
    bias = jnp.where(lane < CMP_NEAR, lookup(relq), 0.0)
    hi = bias.astype(BF16)
    lo = (bias - hi.astype(F32)).astype(BF16).astype(F32)
    one = jnp.where(lane == 2 * CMP_NEAR, 1.0, 0.0)
    cmpq_ref[0] = (hi.astype(F32) + pltpu.roll(lo, CMP_NEAR, 1) + one).astype(BF16)


def _bias_tiles(rel_bias):
    tq = NSA_TQ
    r = np.arange(tq)[:, None]
    c = np.arange(tq)[None, :]
    d0 = r - c
    d1 = tq + r - c
    dw = np.where(c > r, WINDOW + r - c, -1)
    gc = np.where(c < CMP_NEAR, r - CMP_STRIDE * c + (CMP_STRIDE * CMP_PAD - (CMP_BLOCK - 1)), -1)
    rel_np = np.stack([d0.T, d1.T, dw.T, np.full((tq, tq), -1)]).astype(np.int32)
    boxes = []
    for tile in rel_np:
        rr, cc = np.nonzero((tile >= 0) & (tile < REL_MAX_DIST))
        boxes.append(None if rr.size == 0 else tuple(int(v) for v in (
            rr.min() // SUBLANES * SUBLANES, -(-(rr.max() + 1) // SUBLANES) * SUBLANES,
            cc.min() // LANES * LANES, -(-(cc.max() + 1) // LANES) * LANES)))
    rel = jnp.asarray(rel_np)
    nt = rel.shape[0]
    return pl.pallas_call(
        functools.partial(_bias_kernel, boxes=tuple(boxes), starts=_bucket_starts()),
        out_shape=[jax.ShapeDtypeStruct((NSA_KV, nt, tq, NSA_HPG * tq), F32),
                   jax.ShapeDtypeStruct((NSA_KV, NSA_HPG * tq, LANES), BF16)],
        grid=(NSA_HEADS,),
        in_specs=[pl.BlockSpec(memory_space=pltpu.SMEM),
                  _const_spec((nt, tq, tq)),
                  _const_spec((tq, LANES))],
        out_specs=[pl.BlockSpec((1, nt, tq, tq), lambda h: (h // NSA_HPG, 0, 0, h % NSA_HPG)),
                   pl.BlockSpec((1, tq, LANES), lambda h: (h // NSA_HPG, h % NSA_HPG, 0))],
        compiler_params=_params(("arbitrary",)),
        name="bias_tiles",
    )(rel_bias.astype(F32), rel, jnp.asarray(gc[:, :LANES].astype(np.int32)))


def _memkv_kernel(mem_ref, g_ref, w_ref, out_ref):
    y = _rms(mem_ref[0], g_ref[...]).astype(BF16)
    out_ref[0] = _mm(y, w_ref[...]).astype(BF16)


def _memkv(mem, mem_norm_g, w_all):
    b, m, d = mem.shape
    n = w_all.shape[1]
    return pl.pallas_call(
        _memkv_kernel,
        out_shape=jax.ShapeDtypeStruct((b, m, n), BF16),
        grid=(b,),
        in_specs=[pl.BlockSpec((1, m, d), lambda i: (i, 0, 0)),
                  _const_spec((1, d)),
                  _const_spec((d, n))],
        out_specs=pl.BlockSpec((1, m, n), lambda i: (i, 0, 0)),
        compiler_params=_params(("arbitrary",)),
        name="mem_kv",
    )(mem, mem_norm_g.reshape(1, d), w_all)


def _mem_attention(xq, mk, mv):
    outs = []
    for hd in range(MEM_HEADS):
        sl = slice(hd * MEM_DH, (hd + 1) * MEM_DH)
        qm = (xq[:, sl] * (MEM_DH ** -0.5)).astype(BF16)
        s = _mm_nt(qm, mk[:, sl])
        p = jnp.exp(s - jnp.max(s, axis=-1, keepdims=True))
        l = jnp.sum(p, axis=-1, keepdims=True)
        outs.append(_mm(p.astype(BF16), mv[:, sl]) / l)
    return outs


EV_ZA = 0
EV_RQ = EV_ZA + POOL_WIDTH
EV_RK = EV_RQ + RET_HEADS * RET_DK
EV_RV = EV_RK + RET_HEADS * RET_DK
EV_XQ = EV_RV + RET_HEADS * RET_DV
EV_GA = EV_XQ + MEM_WIDTH
EV_GR = EV_GA + POOL_WIDTH
EV_GM = EV_GR + RET_HEADS * RET_DV
EV_COLS = EV_GM + MEM_WIDTH
EV_YA = 0
EV_YR = POOL_WIDTH
EV_YM = EV_YR + RET_HEADS * RET_DV
EV_YCOLS = EV_YM + MEM_WIDTH
EV_SUB = 256


def _even_kernel(gch_ref, h_ref, g_ref, win_ref, wbd_ref, pscale_ref, cos_ref, sin_ref, decay_ref, xi_ref,
                 zeta_ref, mk_ref, mv_ref, wout_ref, o_ref, ext_ref, state_ref, y_ref):
    si = pl.program_id(1)
    ts = h_ref.shape[1]

    @pl.when(si == 0)
    def _():
        ext_ref[0:POOL_HALO, :] = jnp.zeros((POOL_HALO, POOL_WIDTH), F32)
        state_ref[...] = jnp.zeros(state_ref.shape, F32)

    for r0 in range(0, ts, EV_SUB):
        _even_subtile(si * ts + r0, slice(r0, r0 + EV_SUB), gch_ref, h_ref, g_ref, win_ref, wbd_ref, pscale_ref,
                      cos_ref, sin_ref, decay_ref, xi_ref, zeta_ref, mk_ref, mv_ref, wout_ref, o_ref, ext_ref,
                      state_ref, y_ref)


def _even_subtile(t0, tile, gch_ref, h_ref, g_ref, win_ref, wbd_ref, pscale_ref, cos_ref, sin_ref, decay_ref, xi_ref,
                  zeta_ref, mk_ref, mv_ref, wout_ref, o_ref, ext_ref, state_ref, y_ref):
    ts = EV_SUB
    r0 = tile.start
    h = h_ref[0, tile]
    u = _rms(h, g_ref[...]).astype(BF16)

    def proj(start, width):
        return _mm(u, win_ref[:, start:start + width])

    ext_ref[POOL_HALO:, :] = proj(EV_ZA, POOL_WIDTH)
    e = ext_ref[...]
    s2 = e + pltpu.roll(e, 1, 0)
    s4 = s2 + pltpu.roll(s2, 2, 0)
    s8 = s4 + pltpu.roll(s4, 4, 0)
    s16 = s8 + pltpu.roll(s8, 8, 0)
    lane = lax.broadcasted_iota(jnp.int32, e.shape, 1)
    row = lax.broadcasted_iota(jnp.int32, e.shape, 0)
    tpos = t0 + row - POOL_HALO
    g0, g1, g2 = lane < POOL_GROUP, lane < 2 * POOL_GROUP, lane < 3 * POOL_GROUP
    wsum = jnp.where(g0, s2, jnp.where(g1, s4, jnp.where(g2, s8, s16)))
    wlen = jnp.where(g0, POOL_WINDOWS[0], jnp.where(g1, POOL_WINDOWS[1],
                                                    jnp.where(g2, POOL_WINDOWS[2], POOL_WINDOWS[3])))
    cnt = jnp.maximum(jnp.minimum(tpos + 1, wlen), 1).astype(F32)
    pooled = (wsum / cnt - e)[POOL_HALO:]
    ext_ref[0:POOL_HALO, :] = e[ts:ts + POOL_HALO]
    a = _mm(pooled.astype(BF16), wbd_ref[...]) * pscale_ref[...]
    y_ref[tile, EV_YA:EV_YA + POOL_WIDTH] = (a * _silu(proj(EV_GA, POOL_WIDTH))).astype(BF16)

    cos = cos_ref[tile]
    sin = sin_ref[tile]
    zv = proj(EV_RV, RET_HEADS * RET_DV).astype(BF16)
    gate_r = _silu(proj(EV_GR, RET_HEADS * RET_DV))
    q_rot, k_rot = [], []
    for hd in range(RET_HEADS):
        qh = proj(EV_RQ + hd * RET_DK, RET_DK)
        kh = proj(EV_RK + hd * RET_DK, RET_DK)
        q_rot.append((qh * cos + pltpu.roll(qh, RET_DK // 2, 1) * sin) * (RET_DK ** -0.5))
        k_rot.append(kh * cos + pltpu.roll(kh, RET_DK // 2, 1) * sin)
    for c in range(ts // RET_CHUNK):
        rows = slice(c * RET_CHUNK, (c + 1) * RET_CHUNK)
        normed = []
        for hd in range(RET_HEADS):
            qc, kc = q_rot[hd][rows], k_rot[hd][rows]
            vc = zv[rows, hd * RET_DV:(hd + 1) * RET_DV]
            att = _mm_nt(qc.astype(BF16), kc.astype(BF16)) * decay_ref[hd]
            state = state_ref[hd]
            o = _mm(att.astype(BF16), vc) + _mm((qc * xi_ref[hd]).astype(BF16), state.astype(BF16))
            kv = _mm_tn((kc * zeta_ref[hd]).astype(BF16), vc)
            state_ref[hd] = state * gch_ref[hd] + kv
            dlt = o - jnp.mean(o, axis=-1, keepdims=True)
            normed.append(dlt * lax.rsqrt(jnp.mean(dlt * dlt, axis=-1, keepdims=True) + EPS))
        y_ref[r0 + c * RET_CHUNK:r0 + (c + 1) * RET_CHUNK, EV_YR:EV_YM] = (
            jnp.concatenate(normed, axis=-1) * gate_r[rows]).astype(BF16)

    xq = proj(EV_XQ, MEM_WIDTH)
    gm = _silu(proj(EV_GM, MEM_WIDTH))
    for hd, om in enumerate(_mem_attention(xq, mk_ref[0], mv_ref[0])):
        sl = slice(hd * MEM_DH, (hd + 1) * MEM_DH)
        y_ref[tile, EV_YM + hd * MEM_DH:EV_YM + (hd + 1) * MEM_DH] = (om * gm[:, sl]).astype(BF16)

    o_ref[0, tile] = h + _mm(y_ref[tile], wout_ref[...])


def _retention_tables(s):
    half = RET_DK // 2
    inv = ROPE_BASE ** (-jnp.arange(half, dtype=F32) / half)
    ang = jnp.arange(s, dtype=F32)[:, None] * inv[None, :]
    cos, sin = jnp.cos(ang), jnp.sin(ang)
    cos_t = jnp.concatenate([cos, cos], axis=-1)
    sin_t = jnp.concatenate([-sin, sin], axis=-1)
    c = RET_CHUNK
    log_g = jnp.log(1.0 - jnp.exp2(-5.0 - jnp.arange(RET_HEADS, dtype=F32)))
    n = jnp.arange(c, dtype=F32)
    diff = n[:, None] - n[None, :]
    decay = jnp.where(diff >= 0, jnp.exp(log_g[:, None, None] * jnp.maximum(diff, 0.0)), 0.0)
    xi = jnp.exp(log_g[:, None] * (n + 1.0))
    zeta = jnp.exp(log_g[:, None] * (c - 1.0 - n))
    g_chunk = jnp.exp(log_g * c)
    xi_t = jnp.broadcast_to(xi[:, :, None], (RET_HEADS, c, RET_DK))
    zeta_t = jnp.broadcast_to(zeta[:, :, None], (RET_HEADS, c, RET_DK))
    return cos_t, sin_t, decay, xi_t, zeta_t, g_chunk


def _even_layer(h, memkv, layer, g, w_in, pool_w, pool_scale, w_out, ts=4 * EV_SUB):
    b, s, d = h.shape
    assert w_in.shape == (d, EV_COLS) and w_out.shape == (EV_YCOLS, d)
    win = w_in.astype(BF16)
    wout = w_out.astype(BF16)
    n_grp = len(POOL_WINDOWS)
    wbd = jnp.concatenate([jnp.pad(pool_w[gi], ((0, 0), (gi * POOL_GROUP, (n_grp - 1 - gi) * POOL_GROUP)))
                           for gi in range(n_grp)], axis=0).astype(BF16)
    cos_t, sin_t, decay, xi_t, zeta_t, g_chunk = _retention_tables(s)
    kv_blk = 2 * layer
    return pl.pallas_call(
        _even_kernel,
        out_shape=jax.ShapeDtypeStruct((b, s, d), F32),
        grid=(b, s // ts),
        in_specs=[pl.BlockSpec(memory_space=pltpu.SMEM),
                  pl.BlockSpec((1, ts, d), lambda i, j: (i, j, 0)),
                  _const_spec((1, d)),
                  _const_spec((d, EV_COLS)),
                  _const_spec((POOL_WIDTH, POOL_WIDTH)),
                  _const_spec((1, POOL_WIDTH)),
                  pl.BlockSpec((ts, RET_DK), lambda i, j: (j, 0)),
                  pl.BlockSpec((ts, RET_DK), lambda i, j: (j, 0)),
                  _const_spec((RET_HEADS, RET_CHUNK, RET_CHUNK)),
                  _const_spec((RET_HEADS, RET_CHUNK, RET_DK)),
                  _const_spec((RET_HEADS, RET_CHUNK, RET_DK)),
                  pl.BlockSpec((1, N_MEM, MEM_WIDTH), lambda i, j: (i, 0, kv_blk)),
                  pl.BlockSpec((1, N_MEM, MEM_WIDTH), lambda i, j: (i, 0, kv_blk + 1)),
                  _const_spec((EV_YCOLS, d))],
        out_specs=pl.BlockSpec((1, ts, d), lambda i, j: (i, j, 0)),
        scratch_shapes=[pltpu.VMEM((POOL_HALO + EV_SUB, POOL_WIDTH), F32),
                        pltpu.VMEM((RET_HEADS, RET_DK, RET_DV), F32),
                        pltpu.VMEM((ts, EV_YCOLS), BF16)],
        compiler_params=_params(("arbitrary", "arbitrary")),
        name="even_layer",
    )(g_chunk, h, g.reshape(1, d), win, wbd, pool_scale.reshape(1, POOL_WIDTH), cos_t, sin_t, decay, xi_t, zeta_t,
      memkv, memkv, wout)


KV_W = NSA_KV * NSA_DH
OD_Q = 0
OD_KV = OD_Q + NSA_HEADS * NSA_DH
OD_A = OD_KV + 6 * KV_W
PIECE_KC, PIECE_VC, PIECE_KS, PIECE_VS, PIECE_KW, PIECE_VW = range(6)
OD_GL = OD_A
OD_XQ = OD_GL + NSA_KV * LANES
OD_GN = OD_XQ + MEM_WIDTH
OD_GM = OD_GN + NSA_HEADS * NSA_DH
OD_COLS = OD_GM + MEM_WIDTH


def _odd_in_kernel(h_ref, g_ref, wraw_ref, wb_ref, pe_ref, perm_ref, mk_ref, mv_ref,
                   q_ref, kaug_ref, vst_ref, kw_ref, vwt_ref, kca_ref, kcb_ref, vca_ref, vcb_ref, gatest_ref,
                   sg_ref, ym_ref, wa_ref):
    @pl.when((pl.program_id(0) == 0) & (pl.program_id(1) == 0))
    def _():
        for c0 in range(0, OD_A, 2 * LANES):
            wa_ref[:, c0:c0 + 2 * LANES] = wraw_ref[c0:c0 + 2 * LANES, :].T.astype(BF16)

    si = pl.program_id(1)
    n_sub = h_ref.shape[1] // NSA_TQ
    for sub in range(n_sub):
        _odd_in_subtile(si * n_sub + sub, sub, h_ref, g_ref, wa_ref, wb_ref, pe_ref, perm_ref, mk_ref, mv_ref,
                        q_ref, kaug_ref, vst_ref, kw_ref, vwt_ref, kca_ref, kcb_ref, vca_ref, vcb_ref, gatest_ref,
                        sg_ref, ym_ref)


def _odd_in_subtile(tile_idx, sub, h_ref, g_ref, wa_ref, wb_ref, pe_ref, perm_ref, mk_ref, mv_ref,
                    q_ref, kaug_ref, vst_ref, kw_ref, vwt_ref, kca_ref, kcb_ref, vca_ref, vcb_ref, gatest_ref,
                    sg_ref, ym_ref):
    ts = NSA_TQ
    tile = slice(sub * ts, (sub + 1) * ts)
    u = _rms(h_ref[0, tile], g_ref[...]).astype(BF16)

    def proj(start, width):
        if start < OD_A:
            return _mm(u, wa_ref[:, start:start + width])
        return _mm(u, wb_ref[:, start - OD_A:start - OD_A + width])

    zq = proj(OD_Q, NSA_HEADS * NSA_DH) * (NSA_DH ** -0.5 * LOG2E)
    zg = _silu(proj(OD_GN, NSA_HEADS * NSA_DH))
    for g in range(NSA_KV):
        for hh in range(NSA_HPG):
            sl = slice((g * NSA_HPG + hh) * NSA_DH, (g * NSA_HPG + hh + 1) * NSA_DH)
            q_ref[0, g, hh, tile] = zq[:, sl].astype(BF16)
            sg_ref[0, g, hh, tile] = zg[:, sl]

    zkv = proj(OD_KV, 6 * KV_W)
    lane = lax.broadcasted_iota(jnp.int32, (ts, LANES), 1)
    blk = (tile_idx * ts + lax.broadcasted_iota(jnp.int32, (ts, LANES), 0)) // SLC_BLOCK
    onehot = jnp.where(lane == blk, 1.0, 0.0).astype(BF16)
    ones_rows = jnp.where(lax.broadcasted_iota(jnp.int32, (V_ROWS - NSA_DH, ts), 0) == 0, 1.0, 0.0).astype(BF16)
    zgl = jax.nn.sigmoid(proj(OD_GL, NSA_KV * LANES))
    flat_in = []
    for g in range(NSA_KV):
        def piece(idx):
            off = idx * KV_W + g * NSA_DH
            return zkv[:, off:off + NSA_DH]
        kaug_ref[0, g, tile, 0:NSA_DH] = piece(PIECE_KS).astype(BF16)
        kaug_ref[0, g, tile, NSA_DH:2 * NSA_DH] = onehot
        vst_ref[0, g, sub, 0:NSA_DH] = piece(PIECE_VS).T.astype(BF16)
        vst_ref[0, g, sub, NSA_DH:V_ROWS] = ones_rows
        kw_ref[0, g, tile] = piece(PIECE_KW).astype(BF16)
        vwt_ref[0, g, sub, 0:NSA_DH] = piece(PIECE_VW).T.astype(BF16)
        vwt_ref[0, g, sub, NSA_DH:V_ROWS] = ones_rows
        kc, vc = piece(PIECE_KC), piece(PIECE_VC)
        flat_in += [(kc + pe_ref[0, 0]).astype(BF16), (kc + pe_ref[0, 1]).astype(BF16),
                    (vc + pe_ref[1, 0]).astype(BF16), (vc + pe_ref[1, 1]).astype(BF16)]
        gatest_ref[0, g, :, tile] = zgl[:, g * LANES:(g + 1) * LANES].T

    nj = ts // CMP_STRIDE
    perm = _mm(perm_ref[...], jnp.concatenate(flat_in, axis=1)).astype(BF16)
    for c, out_ref in enumerate((kca_ref, kcb_ref, vca_ref, vcb_ref) * NSA_KV):
        for l in range(CMP_STRIDE):
            out_ref[0, c // 4, sub * nj:(sub + 1) * nj, l * NSA_DH:(l + 1) * NSA_DH] = (
                perm[l * nj:(l + 1) * nj, c * NSA_DH:(c + 1) * NSA_DH])

    xq = proj(OD_XQ, MEM_WIDTH)
    gm = _silu(proj(OD_GM, MEM_WIDTH))
    for hd, om in enumerate(_mem_attention(xq, mk_ref[0], mv_ref[0])):
        sl = slice(hd * MEM_DH, (hd + 1) * MEM_DH)
        ym_ref[0, tile, sl] = (om * gm[:, sl]).astype(BF16)


def _odd_in(h, memkv, layer, g, w_in, cmp_pe, ts=2 * NSA_TQ):
    b, s, d = h.shape
    assert w_in.shape[1] == OD_A + 3 * NSA_HEADS + MEM_WIDTH + D_INNER
    w_t = jnp.swapaxes(w_in, 0, 1)
    gl, xq, gate = jnp.split(w_t[OD_A:], [3 * NSA_HEADS, 3 * NSA_HEADS + MEM_WIDTH], axis=0)
    gl = gl.reshape(3, NSA_KV, NSA_HPG, d).transpose(1, 0, 2, 3).reshape(NSA_KV, 3 * NSA_HPG, d)
    gl = jnp.pad(gl, ((0, 0), (0, LANES - 3 * NSA_HPG), (0, 0))).reshape(NSA_KV * LANES, d)
    wb = jnp.concatenate([gl, xq, gate], axis=0).T.astype(BF16)
    sub = NSA_TQ
    reps = sub // CMP_STRIDE
    pe = jnp.stack([jnp.stack([jnp.tile(cmp_pe[kv, :CMP_STRIDE], (reps, 1)),
                               jnp.tile(cmp_pe[kv, CMP_STRIDE:], (reps, 1))]) for kv in range(2)])
    kv_blk = 2 * layer
    assert ts % sub == 0
    nj = sub // CMP_STRIDE
    perm = np.zeros((sub, sub), np.float32)
    perm[np.arange(sub), (np.arange(sub) % nj) * CMP_STRIDE + np.arange(sub) // nj] = 1.0
    head_t = jax.ShapeDtypeStruct((b, NSA_KV, s, NSA_DH), BF16)
    head_spec = pl.BlockSpec((1, NSA_KV, ts, NSA_DH), lambda i, j: (i, 0, j, 0))
    headt_t = jax.ShapeDtypeStruct((b, NSA_KV, s // sub, V_ROWS, sub), BF16)
    headt_spec = pl.BlockSpec((1, NSA_KV, ts // sub, V_ROWS, sub), lambda i, j: (i, 0, j, 0, 0))
    flat_t = jax.ShapeDtypeStruct((b, NSA_KV, s // CMP_STRIDE, CMP_STRIDE * NSA_DH), BF16)
    flat_spec = pl.BlockSpec((1, NSA_KV, ts // CMP_STRIDE, CMP_STRIDE * NSA_DH), lambda i, j: (i, 0, j, 0))
    qlike_spec = pl.BlockSpec((1, NSA_KV, NSA_HPG, ts, NSA_DH), lambda i, j: (i, 0, 0, j, 0))
    return pl.pallas_call(
        _odd_in_kernel,
        out_shape=[jax.ShapeDtypeStruct((b, NSA_KV, NSA_HPG, s, NSA_DH), BF16),
                   jax.ShapeDtypeStruct((b, NSA_KV, s, 2 * NSA_DH), BF16),
                   headt_t, head_t, headt_t, flat_t, flat_t, flat_t, flat_t,
                   jax.ShapeDtypeStruct((b, NSA_KV, LANES, s), F32),
                   jax.ShapeDtypeStruct((b, NSA_KV, NSA_HPG, s, NSA_DH), F32),
                   jax.ShapeDtypeStruct((b, s, MEM_WIDTH), BF16)],
        grid=(b, s // ts),
        in_specs=[pl.BlockSpec((1, ts, d), lambda i, j: (i, j, 0)),
                  _const_spec((1, d)),
                  _const_spec((OD_A, d)),
                  _const_spec((d, OD_COLS - OD_A)),
                  _const_spec((2, 2, sub, NSA_DH)),
                  _const_spec((sub, sub)),
                  pl.BlockSpec((1, N_MEM, MEM_WIDTH), lambda i, j: (i, 0, kv_blk)),
                  pl.BlockSpec((1, N_MEM, MEM_WIDTH), lambda i, j: (i, 0, kv_blk + 1))],
        out_specs=[qlike_spec,
                   pl.BlockSpec((1, NSA_KV, ts, 2 * NSA_DH), lambda i, j: (i, 0, j, 0)),
                   headt_spec, head_spec, headt_spec, flat_spec, flat_spec, flat_spec, flat_spec,
                   pl.BlockSpec((1, NSA_KV, LANES, ts), lambda i, j: (i, 0, 0, j)),
                   qlike_spec,
                   pl.BlockSpec((1, ts, MEM_WIDTH), lambda i, j: (i, j, 0))],
        scratch_shapes=[pltpu.VMEM((d, OD_A), BF16)],
        compiler_params=_params(("arbitrary", "arbitrary")),
        name="odd_in_proj",
    )(h, g.reshape(1, d), w_t, wb, pe, jnp.asarray(perm, BF16), memkv, memkv)


def _compress_kernel(xka_ref, xkb_ref, xva_ref, xvb_ref, w1a_ref, w1b_ref, b1_ref, w2_ref, ovt_ref, kc_ref, vct_ref):
    n = xka_ref.shape[2]
    k_cols = vct_ref.shape[3]

    def block_mlp(kv, xa, xb):
        first = _mm(xa[0, 0], w1a_ref[kv])
        second = _mm(xb[0, 0], w1b_ref[kv])
        hid = first + pltpu.roll(second, n - 1, 0) + b1_ref[kv]
        return _mm(_silu(hid).astype(BF16), w2_ref[kv])

    kc_ref[0, 0, 0:CMP_PAD] = jnp.zeros((CMP_PAD, NSA_DH), BF16)
    kc_ref[0, 0, CMP_PAD:] = block_mlp(0, xka_ref, xkb_ref).astype(BF16)
    vc = jnp.concatenate([jnp.zeros((CMP_PAD, NSA_DH), F32), block_mlp(1, xva_ref, xvb_ref),
                          jnp.zeros((k_cols - CMP_PAD - n, NSA_DH), F32)], axis=0)
    vct_ref[0, 0, 0:NSA_DH] = vc.T.astype(BF16)
    vct_ref[0, 0, NSA_DH:] = ovt_ref[...]


def _compress(kca, kcb, vca, vcb, w1, b1, w2):
    b, g, n, half = kca.shape
    dh = half // CMP_STRIDE
    s = n * CMP_STRIDE
    xs = (kca, kcb, vca, vcb)
    w1 = w1.astype(BF16)
    ovt = _overlap_table(s)
    k_cols = ovt.shape[1]
    x_spec = pl.BlockSpec((1, 1, n, half), lambda i, j: (i, j, 0, 0))
    return pl.pallas_call(
        _compress_kernel,
        out_shape=[jax.ShapeDtypeStruct((b, g, n + CMP_PAD, dh), BF16),
                   jax.ShapeDtypeStruct((b, g, dh + LANES, k_cols), BF16)],
        grid=(b, g),
        in_specs=[x_spec, x_spec, x_spec, x_spec,
                  _const_spec((2, half, CMP_HIDDEN)), _const_spec((2, half, CMP_HIDDEN)),
                  _const_spec((2, 1, CMP_HIDDEN)), _const_spec((2, CMP_HIDDEN, dh)),
                  _const_spec((LANES, k_cols))],
        out_specs=[pl.BlockSpec((1, 1, n + CMP_PAD, dh), lambda i, j: (i, j, 0, 0)),
                   pl.BlockSpec((1, 1, dh + LANES, k_cols), lambda i, j: (i, j, 0, 0))],
        compiler_params=_params(("arbitrary", "arbitrary")),
        name="compress",
    )(*xs, w1[:, :half], w1[:, half:], b1.reshape(2, 1, CMP_HIDDEN), w2.astype(BF16), ovt)


def _nsa_kernel(q_ref, kaug_ref, vst_ref, kw_ref, vwt_ref, kcmp_ref, vcmpt_ref, gatest_ref, sg_ref,
                d0_ref, d1_ref, dw_ref, cmpq_ref, y_ref, qaug_ref, m_ref, acc_ref, s_ref, mw_ref, accw_ref, sw_ref,
                mix_ref, tmax_ref, tmaxw_ref):
    i = pl.program_id(2)
    tq = NSA_TQ
    rows = NSA_HPG * tq
    s_len = kw_ref.shape[2]
    n_cmp = s_len // CMP_STRIDE
    n_slc = s_len // SLC_BLOCK
    t0 = i * tq
    q = q_ref[0, 0].reshape(rows, NSA_DH)


    def stream(mx_ref, ac_ref, sc2_ref, tmax_ref):
        def reset():
            mx_ref[...] = jnp.full(mx_ref.shape, LOWEST, F32)
            ac_ref[...] = jnp.zeros(ac_ref.shape, F32)

        def s_to(slot, k_ref, query, tile, bias=None):
            k0 = pl.multiple_of(tile * tq, tq)
            st = _mm_nt(k_ref[0, 0, pl.ds(k0, tq)], query)
            if bias is not None:
                st = st + bias
            sc2_ref[slot] = st
            tmax_ref[slot] = jnp.max(st, axis=0, keepdims=True)

        def pv_from(slot, vt):
            st = sc2_ref[slot]
            m_prev = mx_ref[...]
            m_new = jnp.maximum(m_prev, tmax_ref[slot])
            pt = jnp.exp2(st - m_new).astype(BF16)
            ac_ref[...] = jnp.exp2(m_prev - m_new) * ac_ref[...] + _mm(vt, pt)
            mx_ref[...] = m_new

        def finish(cols=slice(None)):
            return ac_ref[0:NSA_DH, cols] / ac_ref[NSA_DH:NSA_DH + 1, cols]

        return reset, s_to, pv_from, finish

    def off_unless(cond):
        return jnp.where(cond, 0.0, NEG)

    prev1 = jnp.maximum(i - 1, 0)
    prev2 = jnp.maximum(i - 2, 0)
    w_reset, w_s_to, w_pv_from, w_finish = stream(mw_ref, accw_ref, sw_ref, tmaxw_ref)


    n_pad = kcmp_ref.shape[2]
    j_near = pl.multiple_of(i * (tq // CMP_STRIDE), tq // CMP_STRIDE)
    qaug_ref[:, 0:NSA_DH] = q
    qaug_ref[:, NSA_DH:] = cmpq_ref[0]
    krow = lax.broadcasted_iota(jnp.int32, (n_pad, NSA_DH), 0)
    klane = lax.broadcasted_iota(jnp.int32, (n_pad, NSA_DH), 1)
    slot = krow - j_near
    in_window = (klane < 2 * CMP_NEAR) & (jnp.bitwise_and(klane, CMP_NEAR - 1) == slot)
    visible = (krow >= CMP_PAD) & (slot < CMP_NEAR)
    key_cols = jnp.where(klane == 2 * CMP_NEAR, jnp.where(visible, 0.0, NEG), jnp.where(in_window, 1.0, 0.0))
    s_cmp = _mm_nt(jnp.concatenate([kcmp_ref[0, 0], key_cols.astype(BF16)], axis=1), qaug_ref[...])
    w_reset()
    w_s_to(0, kw_ref, q, prev2, dw_ref[0, 0])
    w_s_to(1, kw_ref, q, prev1, d1_ref[0, 0])
    p_cmp = jnp.exp2(s_cmp - jnp.max(s_cmp, axis=0, keepdims=True))
    tcol = t0 + jnp.bitwise_and(lax.broadcasted_iota(jnp.int32, (1, rows), 1), tq - 1)
    scale = jnp.where(tcol >= CMP_BLOCK - 1, 1.0, 0.0) / jnp.sum(p_cmp, axis=0, keepdims=True)
    p_cmp = p_cmp.astype(BF16)
    k_cols = vcmpt_ref.shape[3]
    if k_cols > n_pad:
        p_cmp = jnp.concatenate([p_cmp, jnp.zeros((k_cols - n_pad, rows), BF16)], axis=0)
    w_pv_from(0, vwt_ref[0, 0, prev2])
    both = _mm(vcmpt_ref[0, 0], p_cmp) * scale
    w_s_to(0, kw_ref, q, i, d0_ref[0, 0])
    w_pv_from(1, vwt_ref[0, 0, prev1])
    gtst = gatest_ref[0, 0]
    n_blk = -(-n_slc // SUBLANES) * SUBLANES
    imp = both[NSA_DH:NSA_DH + n_blk, 0:tq]
    for hh in range(NSA_HPG):
        hs = slice(hh * tq, (hh + 1) * tq)
        mix_ref[:, hs] = gtst[hh:hh + 1] * both[0:NSA_DH, hs]
        if hh:
            imp = imp + both[NSA_DH:NSA_DH + n_blk, hs]

    blk = lax.broadcasted_iota(jnp.int32, (n_blk, tq), 0)
    tq_pos = t0 + lax.broadcasted_iota(jnp.int32, (n_blk, tq), 1)
    cur = tq_pos // SLC_BLOCK
    forced = (blk == 0) | (blk == cur) | (blk == cur - 1)
    future = blk * SLC_BLOCK > tq_pos
    imp = jnp.where(forced, BIG, jnp.where(future, -BIG, imp))
    imp = jnp.where(blk < n_slc, imp, LOWEST)
    blk_f = blk.astype(F32)
    sel = jnp.zeros((n_blk, tq), jnp.bool_)
    for _ in range(SLC_TOPK):
        top = jnp.max(imp, axis=0, keepdims=True)
        first = jnp.min(jnp.where(imp == top, blk_f, float(LANES)), axis=0, keepdims=True)
        hit = blk_f == first
        sel = sel | hit
        imp = jnp.where(hit, KNOCKED, imp)
    sneg_t = jnp.where(sel, 0.0, NEG)
    if n_blk < LANES:
        sneg_t = jnp.concatenate([sneg_t, jnp.zeros((LANES - n_blk, tq), F32)], axis=0)
    sneg = sneg_t.T.astype(BF16)
    for hh in range(NSA_HPG):
        qaug_ref[hh * tq:(hh + 1) * tq, NSA_DH:2 * NSA_DH] = sneg
    w_pv_from(0, vwt_ref[0, 0, i])
    for hh in range(NSA_HPG):
        hs = slice(hh * tq, (hh + 1) * tq)
        mix_ref[:, hs] = mix_ref[:, hs] + gtst[2 * NSA_HPG + hh:2 * NSA_HPG + hh + 1] * w_finish(hs)

    reset, s_to, pv_from, finish = stream(m_ref, acc_ref, s_ref, tmax_ref)
    reset()
    n_far = prev1
    n_pairs = jnp.maximum(n_far - 1, 0) // 2
    s_to(0, kaug_ref, qaug_ref[...], 0, off_unless(n_far >= 1))

    def far_pair(t2):
        s_to(1, kaug_ref, qaug_ref[...], 2 * t2 + 1)
        pv_from(0, vst_ref[0, 0, 2 * t2])
        s_to(0, kaug_ref, qaug_ref[...], 2 * t2 + 2)
        pv_from(1, vst_ref[0, 0, 2 * t2 + 1])

    def far_quad(t4, carry):
        far_pair(2 * t4)
        far_pair(2 * t4 + 1)
        return carry

    lax.fori_loop(0, n_pairs // 2, far_quad, 0)

    @pl.when(n_pairs % 2 == 1)
    def _():
        far_pair(n_pairs - 1)
    c0 = 2 * n_pairs
    two_left = n_far - c0 == 2

    @pl.when(two_left)
    def _():
        s_to(1, kaug_ref, qaug_ref[...], c0 + 1)
        pv_from(0, vst_ref[0, 0, c0])
        s_to(0, kaug_ref, qaug_ref[...], prev1, d1_ref[0, 0])
        pv_from(1, vst_ref[0, 0, c0 + 1])
        s_to(1, kaug_ref, qaug_ref[...], i, d0_ref[0, 0])
        pv_from(0, vst_ref[0, 0, prev1])
        pv_from(1, vst_ref[0, 0, i])

    @pl.when(jnp.logical_not(two_left))
    def _():
        s_to(1, kaug_ref, qaug_ref[...], prev1, d1_ref[0, 0])
        pv_from(0, vst_ref[0, 0, c0])
        s_to(0, kaug_ref, qaug_ref[...], i, d0_ref[0, 0])
        pv_from(1, vst_ref[0, 0, prev1])
        pv_from(0, vst_ref[0, 0, i])

    for hh in range(NSA_HPG):
        hs = slice(hh * tq, (hh + 1) * tq)
        ot = mix_ref[:, hs] + gatest_ref[0, 0, NSA_HPG + hh:NSA_HPG + hh + 1] * finish(hs)
        y_ref[0, :, hh * NSA_DH:(hh + 1) * NSA_DH] = (ot.T * sg_ref[0, 0, hh]).astype(BF16)


def _overlap_table(s):
    n_cmp = (s - CMP_BLOCK) // CMP_STRIDE + 1
    n_slc = s // SLC_BLOCK
    cst = np.arange(n_cmp)[:, None] * CMP_STRIDE
    sst = np.arange(n_slc)[None, :] * SLC_BLOCK
    ov = np.clip(np.minimum(cst + CMP_BLOCK, sst + SLC_BLOCK) - np.maximum(cst, sst), 0, None) / CMP_STRIDE
    k_cols = -(-(CMP_PAD + s // CMP_STRIDE) // LANES) * LANES
    full = np.zeros((LANES, k_cols), np.float32)
    full[:n_slc, CMP_PAD:CMP_PAD + n_cmp] = ov.T
    return jnp.asarray(full, BF16)


def _nsa(q, kaug, vst, kw, vwt, kcmp, vcmpt, gatest, sg, bias, cmpq):
    b, g, hpg, s, dh = q.shape
    tq = NSA_TQ
    assert WINDOW == 2 * tq and s % tq == 0 and s // SLC_BLOCK <= LANES and s // SLC_BLOCK >= SLC_TOPK
    n_pad = kcmp.shape[2]
    rows = hpg * tq
    seq_spec = pl.BlockSpec((1, 1, s, dh), lambda i, j, k: (i, j, 0, 0))
    seqt_spec = pl.BlockSpec((1, 1, s // tq, V_ROWS, tq), lambda i, j, k: (i, j, 0, 0, 0))
    qlike_spec = pl.BlockSpec((1, 1, hpg, tq, dh), lambda i, j, k: (i, j, 0, k, 0))

    def bias_spec(idx, needs_tiles_before=0):
        return pl.BlockSpec((1, 1, tq, rows),
                            lambda i, j, k: (j, jnp.where(k >= needs_tiles_before, idx, BIAS_MASKED), 0, 0))

    return pl.pallas_call(
        _nsa_kernel,
        out_shape=jax.ShapeDtypeStruct((b, s, g * hpg * dh), BF16),
        grid=(b, g, s // tq),
        in_specs=[qlike_spec,
                  pl.BlockSpec((1, 1, s, 2 * dh), lambda i, j, k: (i, j, 0, 0)),
                  seqt_spec, seq_spec, seqt_spec,
                  pl.BlockSpec((1, 1, n_pad, dh), lambda i, j, k: (i, j, 0, 0)),
                  pl.BlockSpec((1, 1, dh + LANES, vcmpt.shape[3]), lambda i, j, k: (i, j, 0, 0)),
                  pl.BlockSpec((1, 1, LANES, tq), lambda i, j, k: (i, j, 0, k)),
                  qlike_spec,
                  bias_spec(0), bias_spec(1, 1), bias_spec(2, 2),
                  pl.BlockSpec((1, rows, LANES), lambda i, j, k: (j, 0, 0))],
        out_specs=pl.BlockSpec((1, tq, hpg * dh), lambda i, j, k: (i, k, j)),
        scratch_shapes=[pltpu.VMEM((rows, 2 * dh), BF16),
                        pltpu.VMEM((1, rows), F32),
                        pltpu.VMEM((V_ROWS, rows), F32),
                        pltpu.VMEM((2, tq, rows), F32),
                        pltpu.VMEM((1, rows), F32),
                        pltpu.VMEM((V_ROWS, rows), F32),
                        pltpu.VMEM((2, tq, rows), F32),
                        pltpu.VMEM((dh, rows), F32),
                        pltpu.VMEM((2, 1, rows), F32),
                        pltpu.VMEM((2, 1, rows), F32)],
        compiler_params=_params(("arbitrary", "arbitrary", "arbitrary")),
        name="nsa_attention",
    )(q, kaug, vst, kw, vwt, kcmp, vcmpt, gatest, sg, bias, bias, bias, cmpq)


def _odd_out_kernel(h_ref, yn_ref, ym_ref, wn_ref, wm_ref, fg_ref, o_ref, *, final_norm, sub):
    for r0 in range(0, h_ref.shape[0], sub):
        rows = slice(r0, r0 + sub)
        out = h_ref[rows] + _mm(yn_ref[rows], wn_ref[...]) + _mm(ym_ref[rows], wm_ref[...])
        o_ref[rows] = _rms(out, fg_ref[...]) if final_norm else out


def _odd_out(h, yn, ym, w_out, final_g, final_norm, tm=1024, sub=256):
    b, s, d = h.shape
    t = b * s
    nw = NSA_HEADS * NSA_DH
    wout = w_out.astype(BF16)
    out = pl.pallas_call(
        functools.partial(_odd_out_kernel, final_norm=final_norm, sub=sub),
        out_shape=jax.ShapeDtypeStruct((t, d), F32),
        grid=(t // tm,),
        in_specs=[pl.BlockSpec((tm, d), lambda i: (i, 0)),
                  pl.BlockSpec((tm, nw), lambda i: (i, 0)),
                  pl.BlockSpec((tm, MEM_WIDTH), lambda i: (i, 0)),
                  _const_spec((nw, d)), _const_spec((MEM_WIDTH, d)), _const_spec((1, d))],
        out_specs=pl.BlockSpec((tm, d), lambda i: (i, 0)),
        compiler_params=_params(("arbitrary",)),
        name="odd_out_proj",
    )(h.reshape(t, d), yn.reshape(t, nw), ym.reshape(t, MEM_WIDTH), wout[:nw], wout[nw:], final_g.reshape(1, d))
    return out.reshape(b, s, d)


def _final_norm_kernel(h_ref, g_ref, o_ref):
    o_ref[...] = _rms(h_ref[...], g_ref[...])


def _final_norm(h, final_g, tm=512):
    b, s, d = h.shape
    t = b * s
    out = pl.pallas_call(
        _final_norm_kernel,
        out_shape=jax.ShapeDtypeStruct((t, d), F32),
        grid=(t // tm,),
        in_specs=[pl.BlockSpec((tm, d), lambda i: (i, 0)), _const_spec((1, d))],
        out_specs=pl.BlockSpec((tm, d), lambda i: (i, 0)),
        compiler_params=_params(("arbitrary",)),
        name="final_norm",
    )(h.reshape(t, d), final_g.reshape(1, d))
    return out.reshape(b, s, d)


def kernel(x, mem, norm_g, final_g, mem_norm_g, rel_bias, ev_w_in, ev_pool_w, ev_pool_scale, ev_w_mem_kv, ev_w_out,
           od_w_in, od_cmp_pe, od_cmp_w1, od_cmp_b1, od_cmp_w2, od_w_mem_kv, od_w_out):
    depth = norm_g.shape[0]
    w_mem = [(ev_w_mem_kv if i % 2 == 0 else od_w_mem_kv)[i // 2] for i in range(depth)]
    memkv = _memkv(mem, mem_norm_g, jnp.concatenate(w_mem, axis=1).astype(BF16))
    bias, cmpq = _bias_tiles(rel_bias) if depth > 1 else (None, None)
    h = x
    for i in range(depth):
        j = i // 2
        last = i == depth - 1
        if i % 2 == 0:
            h = _even_layer(h, memkv, i, norm_g[i], ev_w_in[j], ev_pool_w[j], ev_pool_scale[j], ev_w_out[j])
            if last:
                h = _final_norm(h, final_g)
        else:
            (q, kaug, vst, kw, vwt, kca, kcb, vca, vcb, gatest, sg, ym) = _odd_in(
                h, memkv, i, norm_g[i], od_w_in[j], od_cmp_pe[j])
            kcmp, vcmp = _compress(kca, kcb, vca, vcb, od_cmp_w1[j], od_cmp_b1[j], od_cmp_w2[j])
            yn = _nsa(q, kaug, vst, kw, vwt, kcmp, vcmp, gatest, sg, bias, cmpq)
            h = _odd_out(h, yn, ym, od_w_out[j], final_g, last)
    return h
```
